```python
import math
import jax, jax.numpy as jnp
from jax import lax
import numpy as np

D_MODEL = 2048
BATCH = 4
SEQ = 2048
DEPTH = 1
DEC_BATCH = 128
DEC_SEQ = 8
PAST_LEN = 16384
PAGE_SIZE = 128

N_MEM = 256
GLA_HEADS = 4
GLA_DK = D_MODEL // 2 // GLA_HEADS
GLA_DV = D_MODEL // GLA_HEADS
GLA_RANK = 16
GLA_GATE_NORM = 16.0
HGRN_DK = 128
HGRN_HEADS = D_MODEL // HGRN_DK
HGRN_DV = D_MODEL // HGRN_HEADS
MEM_HEADS = 4
MEM_DH = D_MODEL // MEM_HEADS
D_FF = 5504
CONV_W = 3
CHUNK = 16
EPS = 1e-6

IN_SPLITS = (
    GLA_HEADS * GLA_DK,
    GLA_HEADS * GLA_DK,
    GLA_HEADS * GLA_DV,
    GLA_HEADS * GLA_DV,
    GLA_RANK,
    HGRN_HEADS * HGRN_DK,
    HGRN_HEADS * HGRN_DK,
    HGRN_HEADS * HGRN_DV,
    HGRN_HEADS * HGRN_DV,
    D_MODEL,
    D_MODEL,
)
D_IN = sum(IN_SPLITS)

kernel_name = "gla_hgrn2_gated_merge_memxattn_convffn_step"


def rmsnorm(x, g):
    xf = x.astype(jnp.float32)
    y = xf * lax.rsqrt(jnp.mean(xf * xf, axis=-1, keepdims=True) + EPS)
    return (y * g.astype(jnp.float32)).astype(x.dtype)


def chunked_gated_linear_attention(q, k, v, log_a, s0):
    B, L, H, K = q.shape
    V = v.shape[-1]
    C = math.gcd(CHUNK, L)
    N = L // C

    def to_chunks(t):
        return t.astype(jnp.float32).reshape(B, N, C, H, t.shape[-1]).transpose(1, 0, 3, 2, 4)

    qc, kc, vc, ac = to_chunks(q), to_chunks(k), to_chunks(v), to_chunks(log_a)
    causal = jnp.tril(jnp.ones((C, C), dtype=bool))

    def step(S, xs):
        qn, kn, vn, an = xs
        b = jnp.cumsum(an, axis=-2)
        b_last = b[..., -1:, :]
        q_dec = qn * jnp.exp(b)
        k_inv = kn * jnp.exp(-b)
        k_end = kn * jnp.exp(b_last - b)
        scores = jnp.where(causal, jnp.einsum('bhck,bhsk->bhcs', q_dec, k_inv), 0.0)
        o = jnp.einsum('bhcs,bhsv->bhcv', scores, vn) + jnp.einsum('bhck,bhkv->bhcv', q_dec, S)
        S = jnp.exp(b_last[..., 0, :])[..., None] * S + jnp.einsum('bhsk,bhsv->bhkv', k_end, vn)
        return S, o

    s_final, o = lax.scan(step, s0.astype(jnp.float32), (qc, kc, vc, ac))
    o = o.transpose(1, 0, 3, 2, 4).reshape(B, L, H, V)
    return o.astype(v.dtype), s_final.astype(s0.dtype)


def head_gated_norm(o, g_norm, gate):
    B, L, H, V = o.shape
    return rmsnorm(o, g_norm.reshape(H, V)).reshape(B, L, H * V) * jax.nn.silu(gate)


def memory_kv(mem, norm_mem_g, w_mem_k, w_mem_v):
    B, M, _ = mem.shape
    hm = rmsnorm(mem, norm_mem_g)
    return (hm @ w_mem_k).reshape(B, M, MEM_HEADS, MEM_DH), (hm @ w_mem_v).reshape(B, M, MEM_HEADS, MEM_DH)


def hybrid_layer(x, mem_k, mem_v, s_gla, s_hgrn, conv_buf, lb, p):
    B, L, _ = x.shape
    h = rmsnorm(x, p['norm_mix_g'])
    z = h @ p['w_in']
    offsets = np.cumsum(IN_SPLITS)[:-1].tolist()
    gq, gk, gv, gg, ga, hq, hf, hi, hg, mix_a, mix_b = jnp.split(z, offsets, axis=-1)
    gla_q = gq.reshape(B, L, GLA_HEADS, GLA_DK) * (GLA_DK ** -0.5)
    gla_k = gk.reshape(B, L, GLA_HEADS, GLA_DK)
    gla_v = gv.reshape(B, L, GLA_HEADS, GLA_DV)
    a_logit = (ga @ p['gla_w_a_up'] + p['gla_b_a']).astype(jnp.float32)
    gla_log_a = (jax.nn.log_sigmoid(a_logit) / GLA_GATE_NORM).reshape(B, L, GLA_HEADS, GLA_DK)
    o_gla, s_gla_new = chunked_gated_linear_attention(gla_q, gla_k, gla_v, gla_log_a, s_gla)
    o_gla = head_gated_norm(o_gla, p['gla_norm_g'], gg)
    f = lb + (1.0 - lb) * jax.nn.sigmoid(hf.astype(jnp.float32))
    h_k = (1.0 - f).reshape(B, L, HGRN_HEADS, HGRN_DK)
    h_log_f = jnp.log(f).reshape(B, L, HGRN_HEADS, HGRN_DK)
    h_q = jax.nn.silu(hq).reshape(B, L, HGRN_HEADS, HGRN_DK) * (HGRN_DK ** -0.5)
    h_i = hi.reshape(B, L, HGRN_HEADS, HGRN_DV)
    o_hgrn, s_hgrn_new = chunked_gated_linear_attention(h_q, h_k, h_i, h_log_f, s_hgrn)
    o_hgrn = head_gated_norm(o_hgrn, p['hgrn_norm_g'], hg)
    o_mix = jax.nn.sigmoid(mix_a) * o_gla + jax.nn.sigmoid(mix_b) * o_hgrn
    x = x + (o_mix @ p['w_out']).astype(x.dtype)
    hx = rmsnorm(x, p['norm_xattn_g'])
    mq = (hx @ p['w_mem_q']).reshape(B, L, MEM_HEADS, MEM_DH)
    sc = jnp.einsum('blhd,bmhd->bhlm', mq, mem_k).astype(jnp.float32) * (MEM_DH ** -0.5)
    pr = jax.nn.softmax(sc, axis=-1).astype(mem_v.dtype)
    mo = jnp.einsum('bhlm,bmhd->blhd', pr, mem_v).reshape(B, L, D_MODEL)
    x = x + (mo @ p['w_mem_o']).astype(x.dtype)
    hf_ = rmsnorm(x, p['norm_ffn_g'])
    u = hf_ @ p['w_ffn_up']
    ug, uv = u[..., :D_FF], u[..., D_FF:]
    seq = jnp.concatenate([conv_buf.astype(ug.dtype), ug], axis=1)
    w = p['ffn_conv_w']
    conv = sum(w[j] * seq[:, j:j + L] for j in range(CONV_W)) + p['ffn_conv_b']
    act = jax.nn.silu(conv) * uv
    x = x + (act @ p['w_ffn_down']).astype(x.dtype)
    conv_new = seq[:, -(CONV_W - 1):].astype(conv_buf.dtype)
    return x, s_gla_new, s_hgrn_new, conv_new


def setup_inputs(seed: int = 0) -> dict:
    key = jax.random.key(seed)
    ks = jax.random.split(key, 32)
    f32 = jnp.float32

    def nrm(k, shape, scale):
        return jax.random.normal(k, shape, f32) * scale

    def gain(k, shape):
        return 1.0 + 0.02 * jax.random.normal(k, shape, f32)

    return {
        "x_prompt": nrm(ks[0], (BATCH, SEQ, D_MODEL), 1.0),
        "x_sample": nrm(ks[1], (DEC_BATCH, DEC_SEQ, D_MODEL), 1.0),
        "mem_prompt": nrm(ks[2], (BATCH, N_MEM, D_MODEL), 1.0),
        "state_gla": nrm(ks[3], (DEPTH, DEC_BATCH, GLA_HEADS, GLA_DK, GLA_DV), 0.3),
        "state_hgrn": nrm(ks[4], (DEPTH, DEC_BATCH, HGRN_HEADS, HGRN_DK, HGRN_DV), 0.3),
        "state_ffn_conv": nrm(ks[5], (DEPTH, DEC_BATCH, CONV_W - 1, D_FF), 1.0),
        "cache_mem_k": nrm(ks[6], (DEPTH, DEC_BATCH, N_MEM, MEM_HEADS, MEM_DH), 1.0),
        "cache_mem_v": nrm(ks[7], (DEPTH, DEC_BATCH, N_MEM, MEM_HEADS, MEM_DH), 1.0),
        "norm_mix_g": gain(ks[8], (DEPTH, D_MODEL)),
        "w_in": nrm(ks[9], (DEPTH, D_MODEL, D_IN), D_MODEL ** -0.5),
        "gla_w_a_up": nrm(ks[10], (DEPTH, GLA_RANK, GLA_HEADS * GLA_DK), GLA_RANK ** -0.5),
        "gla_b_a": nrm(ks[11], (DEPTH, GLA_HEADS * GLA_DK), 0.1),
        "gla_norm_g": gain(ks[12], (DEPTH, GLA_HEADS * GLA_DV)),
        "hgrn_lower_bound": nrm(ks[13], (DEPTH + 1, HGRN_HEADS * HGRN_DK), 0.1),
        "hgrn_norm_g": gain(ks[14], (DEPTH, HGRN_HEADS * HGRN_DV)),
        "w_out": nrm(ks[15], (DEPTH, D_MODEL, D_MODEL), D_MODEL ** -0.5),
        "norm_xattn_g": gain(ks[16], (DEPTH, D_MODEL)),
        "norm_mem_g": gain(ks[17], (DEPTH, D_MODEL)),
        "w_mem_q": nrm(ks[18], (DEPTH, D_MODEL, D_MODEL), D_MODEL ** -0.5),
        "w_mem_k": nrm(ks[19], (DEPTH, D_MODEL, D_MODEL), D_MODEL ** -0.5),
        "w_mem_v": nrm(ks[20], (DEPTH, D_MODEL, D_MODEL), D_MODEL ** -0.5),
        "w_mem_o": nrm(ks[21], (DEPTH, D_MODEL, D_MODEL), D_MODEL ** -0.5),
        "norm_ffn_g": gain(ks[22], (DEPTH, D_MODEL)),
        "w_ffn_up": nrm(ks[23], (DEPTH, D_MODEL, 2 * D_FF), D_MODEL ** -0.5),
        "ffn_conv_w": nrm(ks[24], (DEPTH, CONV_W, D_FF), CONV_W ** -0.5),
        "ffn_conv_b": nrm(ks[25], (DEPTH, D_FF), 0.02),
        "w_ffn_down": nrm(ks[26], (DEPTH, D_FF, D_MODEL), D_FF ** -0.5),
        "norm_final_g": gain(ks[27], (D_MODEL,)),
    }


def reference(x_prompt, x_sample, mem_prompt, state_gla, state_hgrn, state_ffn_conv, cache_mem_k, cache_mem_v,
              norm_mix_g, w_in, gla_w_a_up, gla_b_a, gla_norm_g, hgrn_lower_bound, hgrn_norm_g, w_out,
              norm_xattn_g, norm_mem_g, w_mem_q, w_mem_k, w_mem_v, w_mem_o,
              norm_ffn_g, w_ffn_up, ffn_conv_w, ffn_conv_b, w_ffn_down, norm_final_g):
    lb_all = jnp.cumsum(jax.nn.softmax(hgrn_lower_bound.astype(jnp.float32), axis=0), axis=0)
    xp, xs = x_prompt, x_sample
    Bp = x_prompt.shape[0]
    gla_p, hgrn_p, conv_p, mk_p, mv_p = [], [], [], [], []
    gla_s, hgrn_s, conv_s = [], [], []
    for l in range(DEPTH):
        p = {
            'norm_mix_g': norm_mix_g[l], 'w_in': w_in[l], 'gla_w_a_up': gla_w_a_up[l], 'gla_b_a': gla_b_a[l],
            'gla_norm_g': gla_norm_g[l], 'hgrn_norm_g': hgrn_norm_g[l], 'w_out': w_out[l],
            'norm_xattn_g': norm_xattn_g[l], 'w_mem_q': w_mem_q[l], 'w_mem_o': w_mem_o[l],
            'norm_ffn_g': norm_ffn_g[l], 'w_ffn_up': w_ffn_up[l], 'ffn_conv_w': ffn_conv_w[l],
            'ffn_conv_b': ffn_conv_b[l], 'w_ffn_down': w_ffn_down[l],
        }
        lb = lb_all[l]
        mk, mv = memory_kv(mem_prompt, norm_mem_g[l], w_mem_k[l], w_mem_v[l])
        s0_gla = jnp.zeros((Bp, GLA_HEADS, GLA_DK, GLA_DV), x_prompt.dtype)
        s0_hgrn = jnp.zeros((Bp, HGRN_HEADS, HGRN_DK, HGRN_DV), x_prompt.dtype)
        c0 = jnp.zeros((Bp, CONV_W - 1, D_FF), x_prompt.dtype)
        xp, sg, sh, cb = hybrid_layer(xp, mk, mv, s0_gla, s0_hgrn, c0, lb, p)
        gla_p.append(sg); hgrn_p.append(sh); conv_p.append(cb); mk_p.append(mk); mv_p.append(mv)
        xs, sg2, sh2, cb2 = hybrid_layer(xs, cache_mem_k[l], cache_mem_v[l], state_gla[l], state_hgrn[l],
                                         state_ffn_conv[l], lb, p)
        gla_s.append(sg2); hgrn_s.append(sh2); conv_s.append(cb2)
    y_prompt = rmsnorm(xp, norm_final_g)
    y_sample = rmsnorm(xs, norm_final_g)
    new_gla_prompt = jnp.stack(gla_p)
    new_hgrn_prompt = jnp.stack(hgrn_p)
    new_conv_prompt = jnp.stack(conv_p)
    new_mem_k_prompt = jnp.stack(mk_p)
    new_mem_v_prompt = jnp.stack(mv_p)
    new_gla_sample = jnp.stack(gla_s)
    new_hgrn_sample = jnp.stack(hgrn_s)
    new_conv_sample = jnp.stack(conv_s)
    return (y_prompt, y_sample, new_gla_prompt, new_hgrn_prompt, new_conv_prompt, new_mem_k_prompt,
            new_mem_v_prompt, new_gla_sample, new_hgrn_sample, new_conv_sample)
```

```python
import functools
import math

import jax
import jax.numpy as jnp
from jax import lax
from jax.experimental import pallas as pl
from jax.experimental.pallas import tpu as pltpu

F32 = jnp.float32
BF16 = jnp.bfloat16

LANE = 128
BF16_ROWS = 16
V7X_VMEM_LIMIT = 56 * 1024 * 1024

EPS = 1e-6
GLA_HEADS = 4
GLA_GATE_NORM = 16.0
HGRN_DK = 128
MEM_HEADS = 4
CONV_W = 3
REF_CHUNK = 16
PROMPT_BLOCK = 64
PROMPT_STEP = 256
SAMPLE_SEQS = 2


def _cparams(n_axes):
    return pltpu.CompilerParams(dimension_semantics=("arbitrary",) * n_axes,
                                vmem_limit_bytes=V7X_VMEM_LIMIT)


def _sigmoid(x):
    return 1.0 / (1.0 + jnp.exp(-x))


def _pick_tile(n, candidates):
    for c in candidates:
        if n % c == 0:
            return c
    return n


def _dense_body(*refs, prologue, has_res, row_chunk):
    it = iter(refs)
    x_ref = next(it)
    g_ref = next(it) if prologue == "norm" else None
    w_ref = next(it)
    r_ref = next(it) if has_res else None
    o_ref = next(it)
    h_ref = next(it) if prologue != "plain" else None

    if prologue != "plain":
        @pl.when(pl.program_id(1) == 0)
        def _():
            def body(c, carry):
                rows = pl.ds(pl.multiple_of(c * row_chunk, row_chunk), row_chunk)
                xf = x_ref[rows, :].astype(F32)
                if prologue == "norm":
                    ms = jnp.mean(xf * xf, axis=-1, keepdims=True)
                    xf = xf * lax.rsqrt(ms + EPS) * g_ref[...]
                h_ref[rows, :] = xf.astype(BF16)
                return carry
            lax.fori_loop(0, x_ref.shape[0] // row_chunk, body, 0)
        lhs = h_ref[...]
    else:
        lhs = x_ref[...]
    acc = jnp.dot(lhs, w_ref[...], preferred_element_type=F32)
    if has_res:
        acc = acc + r_ref[...]
    o_ref[...] = acc.astype(o_ref.dtype)


def _dense(x, w, *, gain=None, res=None, prologue, out_dtype, name):
    m, k = x.shape
    n = w.shape[1]
    tm = _pick_tile(m, (1024, 512, 256, 128, 64, 32, 16))
    tn = _pick_tile(n, (1024, 512, 256, 128) if k <= 2048 else (512, 256, 128))
    row_chunk = min(tm, 128)
    in_specs = [pl.BlockSpec((tm, k), lambda i, j: (i, 0))]
    args = [x]
    if prologue == "norm":
        in_specs.append(pl.BlockSpec((1, k), lambda i, j: (0, 0)))
        args.append(gain)
    in_specs.append(pl.BlockSpec((k, tn), lambda i, j: (0, j)))
    args.append(w)
    if res is not None:
        in_specs.append(pl.BlockSpec((tm, tn), lambda i, j: (i, j)))
        args.append(res)
    scratch = [] if prologue == "plain" else [pltpu.VMEM((tm, k), BF16)]
    return pl.pallas_call(
        functools.partial(_dense_body, prologue=prologue, has_res=res is not None, row_chunk=row_chunk),
        grid=(m // tm, n // tn),
        in_specs=in_specs,
        out_specs=pl.BlockSpec((tm, tn), lambda i, j: (i, j)),
        out_shape=jax.ShapeDtypeStruct((m, n), out_dtype),
        scratch_shapes=scratch,
        compiler_params=_cparams(2),
        name=name,
    )(*args)


def _rmsnorm_body(x_ref, g_ref, o_ref):
    xf = x_ref[...]
    ms = jnp.mean(xf * xf, axis=-1, keepdims=True)
    o_ref[...] = xf * lax.rsqrt(ms + EPS) * g_ref[...]


def _rmsnorm(x, gain, name):
    m, k = x.shape
    tm = _pick_tile(m, (256, 128, 64, 32, 16, 8))
    return pl.pallas_call(
        _rmsnorm_body,
        grid=(m // tm,),
        in_specs=[pl.BlockSpec((tm, k), lambda i: (i, 0)), pl.BlockSpec((1, k), lambda i: (0, 0))],
        out_specs=pl.BlockSpec((tm, k), lambda i: (i, 0)),
        out_shape=jax.ShapeDtypeStruct((m, k), F32),
        compiler_params=_cparams(1),
        name=name,
    )(x, gain)


def _cumsum_rows(x, group):
    rows = lax.broadcasted_iota(jnp.int32, x.shape, 0) % group
    shift = 1
    while shift < group:
        rolled = pltpu.roll(x, shift, axis=0)
        x = x + jnp.where(rows >= shift, rolled, 0.0)
        shift *= 2
    return x


def _recurrence_block(q, k, v, la, s_prev, *, sub, mm_dtype):
    bt, kd = q.shape
    vd = v.shape[1]
    ns = bt // sub
    b_loc = _cumsum_rows(la, sub)
    qd = q * jnp.exp(b_loc)
    ki = k * jnp.exp(-b_loc)
    tot = [b_loc[(i + 1) * sub - 1:(i + 1) * sub, :] for i in range(ns)]
    pre = [jnp.zeros((1, kd), F32)]
    for i in range(ns):
        pre.append(pre[-1] + tot[i])
    sl = [slice(i * sub, (i + 1) * sub) for i in range(ns)]
    vb = v.astype(mm_dtype)

    q_state = jnp.concatenate([qd[sl[i]] * jnp.exp(pre[i]) for i in range(ns)], axis=0) if ns > 1 else qd
    o_state = jnp.dot(q_state.astype(mm_dtype), s_prev.astype(mm_dtype), preferred_element_type=F32)

    ke = [ki[sl[i]] * jnp.exp(tot[i]) for i in range(ns)]
    o_rows = []
    for i in range(ns):
        parts = []
        for j in range(i):
            parts.append(ke[j] if j == i - 1 else ke[j] * jnp.exp(pre[i] - pre[j + 1]))
        parts.append(ki[sl[i]])
        kmat = jnp.concatenate(parts, axis=0) if len(parts) > 1 else parts[0]
        a = lax.dot_general(qd[sl[i]].astype(mm_dtype), kmat.astype(mm_dtype),
                            (((1,), (1,)), ((), ())), preferred_element_type=F32)
        row = lax.broadcasted_iota(jnp.int32, a.shape, 0)
        col = lax.broadcasted_iota(jnp.int32, a.shape, 1)
        a = jnp.where(col <= row + i * sub, a, 0.0)
        o_rows.append(jnp.dot(a.astype(mm_dtype), vb[:(i + 1) * sub], preferred_element_type=F32)
                      + o_state[sl[i]])
    o = jnp.concatenate(o_rows, axis=0) if ns > 1 else o_rows[0]

    k_end = jnp.concatenate([ke[i] if i == ns - 1 else ke[i] * jnp.exp(pre[ns] - pre[i + 1])
                             for i in range(ns)], axis=0) if ns > 1 else ke[0]
    ds = lax.dot_general(k_end.astype(mm_dtype), vb, (((0,), (0,)), ((), ())), preferred_element_type=F32)
    decay = jnp.broadcast_to(jnp.exp(pre[ns]), (LANE, kd)).T
    s_new = jnp.concatenate([decay * s_prev[:, c * LANE:(c + 1) * LANE] for c in range(vd // LANE)],
                            axis=1) + ds
    return o, s_new


def _head_gated_norm(o, gnorm, gate):
    ms = jnp.mean(o * o, axis=-1, keepdims=True)
    return o * lax.rsqrt(ms + EPS) * gnorm * (gate * _sigmoid(gate))


def _log_sigmoid(x):
    return jnp.minimum(x, 0.0) - jnp.log(1.0 + jnp.exp(-jnp.abs(x)))


def _gla_body(*refs, mode, heads, dk, dv, block, sub, mm_dtype):
    if mode == "prompt":
        q_ref, k_ref, v_ref, g_ref, ga_ref, wup_ref, ba_ref, gn_ref, og_ref, so_ref = refs
        si_ref = None
    else:
        q_ref, k_ref, v_ref, g_ref, ga_ref, wup_ref, ba_ref, gn_ref, si_ref, og_ref, so_ref = refs

    def process(rows, state_read, state_write):
        ga = ga_ref[rows, :].astype(BF16)
        for h in range(heads):
            ks = slice(h * dk, (h + 1) * dk)
            vs = slice(h * dv, (h + 1) * dv)
            a_logit = jnp.dot(ga, wup_ref[:, ks], preferred_element_type=F32) + ba_ref[:, ks]
            la = _log_sigmoid(a_logit) * (1.0 / GLA_GATE_NORM)
            q = q_ref[rows, ks].astype(F32) * (dk ** -0.5)
            k = k_ref[rows, ks].astype(F32)
            v = v_ref[rows, vs].astype(F32)
            o, s_new = _recurrence_block(q, k, v, la, state_read(h), sub=sub, mm_dtype=mm_dtype)
            state_write(h, s_new)
            og_ref[rows, vs] = _head_gated_norm(o, gn_ref[:, vs], g_ref[rows, vs].astype(F32)).astype(og_ref.dtype)

    if mode == "prompt":
        @pl.when(pl.program_id(1) == 0)
        def _():
            so_ref[...] = jnp.zeros(so_ref.shape, F32)

        def read(h):
            return so_ref[0, h]

        def write(h, s):
            so_ref[0, h] = s

        def body(c, carry):
            process(pl.ds(pl.multiple_of(c * block, block), block), read, write)
            return carry
        lax.fori_loop(0, q_ref.shape[0] // block, body, 0)
    else:
        for s in range(si_ref.shape[0]):
            process(slice(s * block, (s + 1) * block),
                    lambda h, s=s: si_ref[s, h],
                    lambda h, val, s=s: so_ref.__setitem__((s, h), val))


def _gla(z, zga, wup, ba, gnorm, state, *, mode, n_seq, seq_len, d_model, name):
    heads = GLA_HEADS
    dk = d_model // 2 // heads
    dv = d_model // heads
    t = z.shape[0]
    if mode == "prompt":
        step = PROMPT_STEP
        n_t = seq_len // step
        grid = (n_seq, n_t)
        row = lambda b, i: b * n_t + i
        block, sub, mm_dtype = PROMPT_BLOCK, REF_CHUNK, BF16
        st_spec = pl.BlockSpec((1, heads, dk, dv), lambda b, i: (b, 0, 0, 0))
        const = lambda b, i: (0, 0)
        zspec = lambda width, col: pl.BlockSpec((step, width), lambda b, i: (row(b, i), col))
    else:
        step = SAMPLE_SEQS * seq_len
        grid = (n_seq // SAMPLE_SEQS,)
        block = seq_len
        sub, mm_dtype = math.gcd(REF_CHUNK, seq_len), F32
        st_spec = pl.BlockSpec((SAMPLE_SEQS, heads, dk, dv), lambda i: (i, 0, 0, 0))
        const = lambda i: (0, 0)
        zspec = lambda width, col: pl.BlockSpec((step, width), lambda i: (i, col))
    kw = heads * dk
    in_specs = [zspec(kw, 0), zspec(kw, 1), zspec(d_model, 1), zspec(d_model, 2),
                zspec(LANE, 0),
                pl.BlockSpec(wup.shape, const), pl.BlockSpec(ba.shape, const), pl.BlockSpec(gnorm.shape, const)]
    args = [z, z, z, z, zga, wup, ba, gnorm]
    if mode != "prompt":
        in_specs.append(st_spec)
        args.append(state)
    og, s_out = pl.pallas_call(
        functools.partial(_gla_body, mode=mode, heads=heads, dk=dk, dv=dv, block=block, sub=sub,
                          mm_dtype=mm_dtype),
        grid=grid,
        in_specs=in_specs,
        out_specs=[zspec(d_model, 0), st_spec],
        out_shape=[jax.ShapeDtypeStruct((t, d_model), F32),
                   jax.ShapeDtypeStruct((n_seq, heads, dk, dv), F32)],
        compiler_params=_cparams(len(grid)),
        name=name,
    )(*args)
    return og, s_out


def _hgrn_body(*refs, mode, layer, heads, dk, block, sub, mm_dtype):
    if mode == "prompt":
        q_ref, f_ref, i_ref, g_ref, ma_ref, mb_ref, ogla_ref, lb_ref, gn_ref, om_ref, so_ref = refs
        si_ref = None
    else:
        q_ref, f_ref, i_ref, g_ref, ma_ref, mb_ref, ogla_ref, lb_ref, gn_ref, si_ref, om_ref, so_ref = refs

    def lower_bound(ks):
        p = lb_ref[:, ks]
        e = jnp.exp(p - jnp.max(p, axis=0, keepdims=True))
        return jnp.sum(e[:layer + 1], axis=0, keepdims=True) / jnp.sum(e, axis=0, keepdims=True)

    def process(rows, state_read, state_write):
        for h in range(heads):
            ks = slice(h * dk, (h + 1) * dk)
            lb = lower_bound(ks)
            f = lb + (1.0 - lb) * _sigmoid(f_ref[rows, ks].astype(F32))
            hq = q_ref[rows, ks].astype(F32)
            q = hq * _sigmoid(hq) * (dk ** -0.5)
            v = i_ref[rows, ks].astype(F32)
            o, s_new = _recurrence_block(q, 1.0 - f, v, jnp.log(f), state_read(h), sub=sub, mm_dtype=mm_dtype)
            state_write(h, s_new)
            o_h = _head_gated_norm(o, gn_ref[:, ks], g_ref[rows, ks].astype(F32))
            mix = (_sigmoid(ma_ref[rows, ks].astype(F32)) * ogla_ref[rows, ks].astype(F32)
                   + _sigmoid(mb_ref[rows, ks].astype(F32)) * o_h)
            om_ref[rows, ks] = mix.astype(om_ref.dtype)

    if mode == "prompt":
        @pl.when(pl.program_id(1) == 0)
        def _():
            so_ref[...] = jnp.zeros(so_ref.shape, F32)

        def read(h):
            return so_ref[0, h]

        def write(h, s):
            so_ref[0, h] = s

        def body(c, carry):
            process(pl.ds(pl.multiple_of(c * block, block), block), read, write)
            return carry
        lax.fori_loop(0, q_ref.shape[0] // block, body, 0)
    else:
        for s in range(si_ref.shape[0]):
            process(slice(s * block, (s + 1) * block),
                    lambda h, s=s: si_ref[s, h],
                    lambda h, val, s=s: so_ref.__setitem__((s, h), val))


def _hgrn(z, og_gla, lb_param, gnorm, state, *, mode, layer, n_seq, seq_len, d_model, name):
    dk = HGRN_DK
    heads = d_model // dk
    t = z.shape[0]
    if mode == "prompt":
        step = PROMPT_STEP
        n_t = seq_len // step
        grid = (n_seq, n_t)
        block, sub, mm_dtype = PROMPT_BLOCK, REF_CHUNK, BF16
        st_spec = pl.BlockSpec((1, heads, dk, dk), lambda b, i: (b, 0, 0, 0))
        const = lambda b, i: (0, 0)
        zspec = lambda col: pl.BlockSpec((step, d_model), lambda b, i: (b * n_t + i, col))
    else:
        step = SAMPLE_SEQS * seq_len
        grid = (n_seq // SAMPLE_SEQS,)
        block = seq_len
        sub, mm_dtype = math.gcd(REF_CHUNK, seq_len), F32
        st_spec = pl.BlockSpec((SAMPLE_SEQS, heads, dk, dk), lambda i: (i, 0, 0, 0))
        const = lambda i: (0, 0)
        zspec = lambda col: pl.BlockSpec((step, d_model), lambda i: (i, col))
    in_specs = [zspec(3), zspec(4), zspec(5), zspec(6), zspec(7), zspec(8), zspec(0),
                pl.BlockSpec(lb_param.shape, const), pl.BlockSpec(gnorm.shape, const)]
    args = [z, z, z, z, z, z, og_gla, lb_param, gnorm]
    if mode != "prompt":
        in_specs.append(st_spec)
        args.append(state)
    o_mix, s_out = pl.pallas_call(
        functools.partial(_hgrn_body, mode=mode, layer=layer, heads=heads, dk=dk, block=block, sub=sub,
                          mm_dtype=mm_dtype),
        grid=grid,
        in_specs=in_specs,
        out_specs=[zspec(0), st_spec],
        out_shape=[jax.ShapeDtypeStruct((t, d_model), BF16),
                   jax.ShapeDtypeStruct((n_seq, heads, dk, dk), F32)],
        compiler_params=_cparams(len(grid)),
        name=name,
    )(*args)
    return o_mix, s_out


def _xattn_body(q_ref, k_ref, v_ref, o_ref, *, heads, dh, rows_per_seq, mm_dtype):
    scale = dh ** -0.5
    for s in range(k_ref.shape[0]):
        rows = slice(s * rows_per_seq, (s + 1) * rows_per_seq)
        for h in range(heads):
            hs = slice(h * dh, (h + 1) * dh)
            q = q_ref[rows, hs].astype(mm_dtype)
            k = k_ref[s, :, hs].astype(mm_dtype)
            v = v_ref[s, :, hs].astype(mm_dtype)
            sc = lax.dot_general(q, k, (((1,), (1,)), ((), ())), preferred_element_type=F32) * scale
            p = jnp.exp(sc - jnp.max(sc, axis=-1, keepdims=True))
            denom = jnp.sum(p, axis=-1, keepdims=True)
            o = jnp.dot(p.astype(mm_dtype), v, preferred_element_type=F32) / denom
            o_ref[rows, hs] = o.astype(o_ref.dtype)


def _xattn(mq, mem_k, mem_v, *, mode, n_seq, seq_len, name):
    t, d = mq.shape
    n_mem = mem_k.shape[1]
    if mode == "prompt":
        tq = _pick_tile(seq_len, (512, 256, 128))
        n_t = seq_len // tq
        grid = (n_seq, n_t)
        q_spec = pl.BlockSpec((tq, d), lambda b, i: (b * n_t + i, 0))
        kv_spec = pl.BlockSpec((1, n_mem, d), lambda b, i: (b, 0, 0))
        rows_per_seq, mm_dtype = tq, BF16
    else:
        grid = (n_seq // SAMPLE_SEQS,)
        q_spec = pl.BlockSpec((SAMPLE_SEQS * seq_len, d), lambda i: (i, 0))
        kv_spec = pl.BlockSpec((SAMPLE_SEQS, n_mem, d), lambda i: (i, 0, 0))
        rows_per_seq, mm_dtype = seq_len, F32
    return pl.pallas_call(
        functools.partial(_xattn_body, heads=MEM_HEADS, dh=d // MEM_HEADS, rows_per_seq=rows_per_seq,
                          mm_dtype=mm_dtype),
        grid=grid,
        in_specs=[q_spec, kv_spec, kv_spec],
        out_specs=q_spec,
        out_shape=jax.ShapeDtypeStruct((t, d), BF16),
        compiler_params=_cparams(len(grid)),
        name=name,
    )(mq, mem_k, mem_v)


def _ffn_up_body(*refs, mode, seq_len, tiles_per_seq, row_chunk):
    if mode == "prompt":
        x_ref, g_ref, wg_ref, wv_ref, cw_ref, cb_ref, act_ref, h_ref, tail_ref = refs
        p1_ref = p2_ref = None
    else:
        x_ref, g_ref, wg_ref, wv_ref, cw_ref, cb_ref, p1_ref, p2_ref, act_ref, h_ref = refs
        tail_ref = None
    i = pl.program_id(0)
    j = pl.program_id(1)

    @pl.when(j == 0)
    def _():
        def body(c, carry):
            rows = pl.ds(pl.multiple_of(c * row_chunk, row_chunk), row_chunk)
            xf = x_ref[rows, :]
            ms = jnp.mean(xf * xf, axis=-1, keepdims=True)
            h_ref[rows, :] = (xf * lax.rsqrt(ms + EPS) * g_ref[...]).astype(BF16)
            return carry
        lax.fori_loop(0, x_ref.shape[0] // row_chunk, body, 0)

    h = h_ref[...]
    ug = jnp.dot(h, wg_ref[...], preferred_element_type=F32)
    uv = jnp.dot(h, wv_ref[...], preferred_element_type=F32)
    tm = ug.shape[0]
    row = lax.broadcasted_iota(jnp.int32, ug.shape, 0)
    roll1 = pltpu.roll(ug, 1, axis=0)
    roll2 = pltpu.roll(ug, 2, axis=0)
    if mode == "prompt":
        @pl.when(i % tiles_per_seq == 0)
        def _():
            tail_ref[j] = jnp.zeros(tail_ref.shape[1:], F32)
        tail = tail_ref[j]
        prev1 = tail[7:8, :]
        prev2 = tail[6:7, :]
        sh1 = jnp.where(row >= 1, roll1, prev1)
        sh2 = jnp.where(row >= 2, roll2, jnp.where(row == 0, prev2, prev1))
        tail_ref[j] = ug[tm - 8:, :]
    else:
        pos = row % seq_len
        sh1 = jnp.where(pos >= 1, roll1, p1_ref[...])
        sh2 = jnp.where(pos >= 2, roll2, p2_ref[...])
    conv = cw_ref[0:1, :] * sh2 + cw_ref[1:2, :] * sh1 + cw_ref[2:3, :] * ug + cb_ref[...]
    act_ref[...] = (conv * _sigmoid(conv) * uv).astype(act_ref.dtype)


def _ffn_up(x, gain, wg, wv, cw, cb, carry, *, mode, seq_len, name):
    m, k = x.shape
    nf = wg.shape[1]
    tm = _pick_tile(m, (1024,))
    tn = _pick_tile(nf, (512, 256, 128))
    row_chunk = 128
    xs = pl.BlockSpec((tm, k), lambda i, j: (i, 0))
    gs = pl.BlockSpec((1, k), lambda i, j: (0, 0))
    ws = pl.BlockSpec((k, tn), lambda i, j: (0, j))
    cws = pl.BlockSpec((CONV_W, tn), lambda i, j: (0, j))
    cbs = pl.BlockSpec((1, tn), lambda i, j: (0, j))
    ts = pl.BlockSpec((tm, tn), lambda i, j: (i, j))
    in_specs = [xs, gs, ws, ws, cws, cbs]
    args = [x, gain, wg, wv, cw, cb]
    scratch = [pltpu.VMEM((tm, k), BF16)]
    if mode == "prompt":
        assert seq_len % tm == 0
        scratch.append(pltpu.VMEM((nf // tn, 8, tn), F32))
    else:
        assert tm % seq_len == 0
        in_specs += [ts, ts]
        args += list(carry)
    return pl.pallas_call(
        functools.partial(_ffn_up_body, mode=mode, seq_len=seq_len, tiles_per_seq=max(seq_len // tm, 1),
                          row_chunk=row_chunk),
        grid=(m // tm, nf // tn),
        in_specs=in_specs,
        out_specs=ts,
        out_shape=jax.ShapeDtypeStruct((m, nf), BF16),
        scratch_shapes=scratch,
        compiler_params=_cparams(2),
        name=name,
    )(*args)


def _pad_cols(w, n):
    return jnp.pad(w, ((0, 0), (0, n - w.shape[1])))


def _layer(x, mem_k, mem_v, s_gla, s_hgrn, conv_buf, p, *, mode, layer, n_seq, seq_len, tag):
    t, d = x.shape
    nm = lambda s: f"{s}_{tag}"
    z = _dense(x, p["w_in"], gain=p["norm_mix_g"], prologue="norm", out_dtype=F32, name=nm("in_proj"))
    zga = _dense(x, p["w_ga"], gain=p["norm_mix_g"], prologue="norm", out_dtype=F32, name=nm("ga_proj"))
    og_gla, s_gla_new = _gla(z, zga, p["gla_w_a_up"], p["gla_b_a"], p["gla_norm_g"], s_gla,
                             mode=mode, n_seq=n_seq, seq_len=seq_len, d_model=d, name=nm("gla"))
    o_mix, s_hgrn_new = _hgrn(z, og_gla, p["hgrn_lower_bound"], p["hgrn_norm_g"], s_hgrn,
                              mode=mode, layer=layer, n_seq=n_seq, seq_len=seq_len, d_model=d, name=nm("hgrn"))
    x1 = _dense(o_mix, p["w_out"], res=x, prologue="plain", out_dtype=F32, name=nm("out_proj"))
    mq = _dense(x1, p["w_mem_q"], gain=p["norm_xattn_g"], prologue="norm",
                out_dtype=BF16 if mode == "prompt" else F32, name=nm("mem_q"))
    mo = _xattn(mq, mem_k, mem_v, mode=mode, n_seq=n_seq, seq_len=seq_len, name=nm("xattn"))
    x2 = _dense(mo, p["w_mem_o"], res=x1, prologue="plain", out_dtype=F32, name=nm("mem_o"))

    nf = p["w_up_gate"].shape[1]
    if mode == "prompt":
        carry = None
    else:
        cb = _pad_cols(conv_buf.reshape(n_seq * (CONV_W - 1), -1), nf).reshape(n_seq, CONV_W - 1, nf)
        zeros = lambda r: jnp.zeros((n_seq, r, nf), F32)
        p1 = jnp.concatenate([cb[:, 1:2], zeros(seq_len - 1)], axis=1).reshape(t, nf)
        p2 = jnp.concatenate([cb, zeros(seq_len - 2)], axis=1).reshape(t, nf)
        carry = (p1, p2)
    act = _ffn_up(x2, p["norm_ffn_g"], p["w_up_gate"], p["w_up_val"], p["ffn_conv_w"], p["ffn_conv_b"], carry,
                  mode=mode, seq_len=seq_len, name=nm("ffn_up"))
    x3 = _dense(act, p["w_down"], res=x2, prologue="plain", out_dtype=F32, name=nm("ffn_down"))

    d_ff = conv_buf.shape[-1] if conv_buf is not None else p["d_ff"]
    last = x2.reshape(n_seq, seq_len, d)[:, seq_len - (CONV_W - 1):].reshape(n_seq * (CONV_W - 1), d)
    pad_rows = (-last.shape[0]) % BF16_ROWS
    last = jnp.pad(last, ((0, pad_rows), (0, 0)))
    ug_last = _dense(last, p["w_up_gate"], gain=p["norm_ffn_g"], prologue="norm", out_dtype=F32,
                     name=nm("conv_state"))
    conv_new = ug_last[:n_seq * (CONV_W - 1), :d_ff].reshape(n_seq, CONV_W - 1, d_ff)
    return x3, s_gla_new, s_hgrn_new, conv_new


def kernel(x_prompt, x_sample, mem_prompt, state_gla, state_hgrn, state_ffn_conv, cache_mem_k, cache_mem_v,
           norm_mix_g, w_in, gla_w_a_up, gla_b_a, gla_norm_g, hgrn_lower_bound, hgrn_norm_g, w_out,
           norm_xattn_g, norm_mem_g, w_mem_q, w_mem_k, w_mem_v, w_mem_o,
           norm_ffn_g, w_ffn_up, ffn_conv_w, ffn_conv_b, w_ffn_down, norm_final_g):
    depth = w_in.shape[0]
    bp, lp, d = x_prompt.shape
    bs, ls, _ = x_sample.shape
    n_mem = mem_prompt.shape[1]
    d_ff = w_ffn_down.shape[1]
    rank = gla_w_a_up.shape[1]
    qk = gla_w_a_up.shape[2]
    ga_off = 2 * qk + 2 * d
    nf = -(-d_ff // 512) * 512

    xp = x_prompt.reshape(bp * lp, d)
    xs = x_sample.reshape(bs * ls, d)
    mem = mem_prompt.reshape(bp * n_mem, d)
    row = lambda v: v.reshape(1, -1)

    outs = {k: [] for k in ("gla_p", "hgrn_p", "conv_p", "mk_p", "mv_p", "gla_s", "hgrn_s", "conv_s")}
    for l in range(depth):
        wi = w_in[l]
        p = {
            "w_in": jnp.concatenate([wi[:, :ga_off], wi[:, ga_off + rank:]], axis=1).astype(BF16),
            "w_ga": _pad_cols(wi[:, ga_off:ga_off + rank], LANE).astype(BF16),
            "gla_w_a_up": jnp.pad(gla_w_a_up[l], ((0, LANE - rank), (0, 0))).astype(BF16),
            "gla_b_a": row(gla_b_a[l]), "gla_norm_g": row(gla_norm_g[l]),
            "hgrn_lower_bound": hgrn_lower_bound, "hgrn_norm_g": row(hgrn_norm_g[l]),
            "norm_mix_g": row(norm_mix_g[l]), "norm_xattn_g": row(norm_xattn_g[l]),
            "norm_ffn_g": row(norm_ffn_g[l]),
            "w_out": w_out[l].astype(BF16), "w_mem_q": w_mem_q[l].astype(BF16), "w_mem_o": w_mem_o[l].astype(BF16),
            "w_up_gate": _pad_cols(w_ffn_up[l][:, :d_ff], nf).astype(BF16),
            "w_up_val": _pad_cols(w_ffn_up[l][:, d_ff:], nf).astype(BF16),
            "ffn_conv_w": _pad_cols(ffn_conv_w[l], nf), "ffn_conv_b": _pad_cols(row(ffn_conv_b[l]), nf),
            "w_down": jnp.pad(w_ffn_down[l], ((0, nf - d_ff), (0, 0))).astype(BF16),
            "d_ff": d_ff,
        }
        g_mem = row(norm_mem_g[l])
        mk = _dense(mem, w_mem_k[l].astype(BF16), gain=g_mem, prologue="norm", out_dtype=F32, name=f"mem_k_{l}")
        mv = _dense(mem, w_mem_v[l].astype(BF16), gain=g_mem, prologue="norm", out_dtype=F32, name=f"mem_v_{l}")
        mk3, mv3 = mk.reshape(bp, n_mem, d), mv.reshape(bp, n_mem, d)
        xp, sg, sh, cb = _layer(xp, mk3, mv3, None, None, None, p, mode="prompt", layer=l,
                                n_seq=bp, seq_len=lp, tag=f"p{l}")
        outs["gla_p"].append(sg); outs["hgrn_p"].append(sh); outs["conv_p"].append(cb)
        outs["mk_p"].append(mk3.reshape(bp, n_mem, MEM_HEADS, d // MEM_HEADS))
        outs["mv_p"].append(mv3.reshape(bp, n_mem, MEM_HEADS, d // MEM_HEADS))
        xs, sg2, sh2, cb2 = _layer(xs, cache_mem_k[l].reshape(bs, n_mem, d), cache_mem_v[l].reshape(bs, n_mem, d),
                                   state_gla[l], state_hgrn[l], state_ffn_conv[l], p, mode="sample", layer=l,
                                   n_seq=bs, seq_len=ls, tag=f"s{l}")
        outs["gla_s"].append(sg2); outs["hgrn_s"].append(sh2); outs["conv_s"].append(cb2)

    g_fin = row(norm_final_g)
    y_prompt = _rmsnorm(xp, g_fin, "final_norm_p").reshape(bp, lp, d)
    y_sample = _rmsnorm(xs, g_fin, "final_norm_s").reshape(bs, ls, d)
    st = lambda k: jnp.stack(outs[k])
    return (y_prompt, y_sample, st("gla_p"), st("hgrn_p"), st("conv_p"), st("mk_p"), st("mv_p"),
            st("gla_s"), st("hgrn_s"), st("conv_s"))
```

```python
import functools
import math

import jax
import jax.numpy as jnp
from jax import lax
from jax.experimental import pallas as pl
from jax.experimental.pallas import tpu as pltpu

F32 = jnp.float32
BF16 = jnp.bfloat16

LANE = 128
BF16_ROWS = 16
V7X_VMEM_LIMIT = 56 * 1024 * 1024

EPS = 1e-6
GLA_HEADS = 4
GLA_GATE_NORM = 16.0
HGRN_DK = 128
MEM_HEADS = 4
CONV_W = 3
REF_CHUNK = 16
PROMPT_BLOCK = 64
PROMPT_STEP = 256
SAMPLE_SEQS = 2


def _cparams(n_axes):
    return pltpu.CompilerParams(dimension_semantics=("arbitrary",) * n_axes,
                                vmem_limit_bytes=V7X_VMEM_LIMIT)


def _sigmoid(x):
    return 1.0 / (1.0 + jnp.exp(-x))


def _pick_tile(n, candidates):
    for c in candidates:
        if n % c == 0:
            return c
    return n


def _dense_body(*refs, prologue, has_res, row_chunk):
    it = iter(refs)
    x_ref = next(it)
    g_ref = next(it) if prologue == "norm" else None
    w_ref = next(it)
    r_ref = next(it) if has_res else None
    o_ref = next(it)
    h_ref = next(it) if prologue != "plain" else None

    if prologue != "plain":
        @pl.when(pl.program_id(1) == 0)
        def _():
            def body(c, carry):
                rows = pl.ds(pl.multiple_of(c * row_chunk, row_chunk), row_chunk)
                xf = x_ref[rows, :].astype(F32)
                if prologue == "norm":
                    ms = jnp.mean(xf * xf, axis=-1, keepdims=True)
                    xf = xf * lax.rsqrt(ms + EPS) * g_ref[...]
                h_ref[rows, :] = xf.astype(BF16)
                return carry
            lax.fori_loop(0, x_ref.shape[0] // row_chunk, body, 0)
        lhs = h_ref[...]
    else:
        lhs = x_ref[...]
    acc = jnp.dot(lhs, w_ref[...].astype(BF16), preferred_element_type=F32)
    if has_res:
        acc = acc + r_ref[...]
    o_ref[...] = acc.astype(o_ref.dtype)


def _dense(x, w, *, gain=None, res=None, prologue, out_dtype, name, layer=0, col_off=0, n_cols=None):
    m, k = x.shape
    n = n_cols if n_cols is not None else w.shape[-1]
    tm = _pick_tile(m, (1024, 512, 256, 128, 64, 32, 16))
    tn = _pick_tile(n, (1024, 512, 256, 128) if k <= 2048 else (512, 256, 128))
    row_chunk = min(tm, 128)
    in_specs = [pl.BlockSpec((tm, k), lambda i, j: (i, 0))]
    args = [x]
    if prologue == "norm":
        in_specs.append(pl.BlockSpec((1, k), lambda i, j: (0, 0)))
        args.append(gain)
    if w.ndim == 3:
        assert col_off % tn == 0
        in_specs.append(pl.BlockSpec((None, k, tn), lambda i, j: (layer, 0, j + col_off // tn)))
    else:
        in_specs.append(pl.BlockSpec((k, tn), lambda i, j: (0, j)))
    args.append(w)
    if res is not None:
        in_specs.append(pl.BlockSpec((tm, tn), lambda i, j: (i, j)))
        args.append(res)
    scratch = [] if prologue == "plain" else [pltpu.VMEM((tm, k), BF16)]
    return pl.pallas_call(
        functools.partial(_dense_body, prologue=prologue, has_res=res is not None, row_chunk=row_chunk),
        grid=(m // tm, n // tn),
        in_specs=in_specs,
        out_specs=pl.BlockSpec((tm, tn), lambda i, j: (i, j)),
        out_shape=jax.ShapeDtypeStruct((m, n), out_dtype),
        scratch_shapes=scratch,
        compiler_params=_cparams(2),
        name=name,
    )(*args)


def _rmsnorm_body(x_ref, g_ref, o_ref):
    xf = x_ref[...]
    ms = jnp.mean(xf * xf, axis=-1, keepdims=True)
    o_ref[...] = xf * lax.rsqrt(ms + EPS) * g_ref[...]


def _rmsnorm(x, gain, name):
    m, k = x.shape
    tm = _pick_tile(m, (256, 128, 64, 32, 16, 8))
    return pl.pallas_call(
        _rmsnorm_body,
        grid=(m // tm,),
        in_specs=[pl.BlockSpec((tm, k), lambda i: (i, 0)), pl.BlockSpec((1, k), lambda i: (0, 0))],
        out_specs=pl.BlockSpec((tm, k), lambda i: (i, 0)),
        out_shape=jax.ShapeDtypeStruct((m, k), F32),
        compiler_params=_cparams(1),
        name=name,
    )(x, gain)


def _cumsum_rows(x, group):
    rows = lax.broadcasted_iota(jnp.int32, x.shape, 0) % group
    shift = 1
    while shift < group:
        rolled = pltpu.roll(x, shift, axis=0)
        x = x + jnp.where(rows >= shift, rolled, 0.0)
        shift *= 2
    return x


def _recurrence_block(q, k, v, la, s_prev, *, sub, mm_dtype):
    bt, kd = q.shape
    vd = v.shape[1]
    ns = bt // sub
    b_loc = _cumsum_rows(la, sub)
    qd = q * jnp.exp(b_loc)
    ki = k * jnp.exp(-b_loc)
    tot = [b_loc[(i + 1) * sub - 1:(i + 1) * sub, :] for i in range(ns)]
    pre = [jnp.zeros((1, kd), F32)]
    for i in range(ns):
        pre.append(pre[-1] + tot[i])
    sl = [slice(i * sub, (i + 1) * sub) for i in range(ns)]
    vb = v.astype(mm_dtype)
    qdm = qd.astype(mm_dtype)

    q_state = jnp.concatenate([qd[sl[i]] * jnp.exp(pre[i]) for i in range(ns)], axis=0) if ns > 1 else qd
    o_state = jnp.dot(q_state.astype(mm_dtype), s_prev.astype(mm_dtype), preferred_element_type=F32)

    ke = [ki[sl[i]] * jnp.exp(tot[i]) for i in range(ns)]
    a_rows = []
    for i in range(ns):
        parts = [ke[j] if j == i - 1 else ke[j] * jnp.exp(pre[i] - pre[j + 1]) for j in range(i)]
        parts.append(ki[sl[i]])
        parts += [jnp.zeros((sub, kd), F32)] * (ns - 1 - i)
        kmat = jnp.concatenate(parts, axis=0) if ns > 1 else parts[0]
        a_rows.append(lax.dot_general(qdm[sl[i]], kmat.astype(mm_dtype), (((1,), (1,)), ((), ())),
                                      preferred_element_type=F32))
    a = jnp.concatenate(a_rows, axis=0) if ns > 1 else a_rows[0]
    row = lax.broadcasted_iota(jnp.int32, a.shape, 0)
    col = lax.broadcasted_iota(jnp.int32, a.shape, 1)
    a = jnp.where(col <= row, a, 0.0)
    o = jnp.dot(a.astype(mm_dtype), vb, preferred_element_type=F32) + o_state

    k_end = jnp.concatenate([ke[i] if i == ns - 1 else ke[i] * jnp.exp(pre[ns] - pre[i + 1])
                             for i in range(ns)], axis=0) if ns > 1 else ke[0]
    ds = lax.dot_general(k_end.astype(mm_dtype), vb, (((0,), (0,)), ((), ())), preferred_element_type=F32)
    decay = jnp.broadcast_to(jnp.exp(pre[ns]), (LANE, kd)).T
    s_new = jnp.concatenate([decay * s_prev[:, c * LANE:(c + 1) * LANE] for c in range(vd // LANE)],
                            axis=1) + ds
    return o, s_new


def _head_gated_norm(o, gnorm, gate):
    ms = jnp.mean(o * o, axis=-1, keepdims=True)
    return o * lax.rsqrt(ms + EPS) * gnorm * (gate * _sigmoid(gate))


def _log_sigmoid(x):
    return jnp.minimum(x, 0.0) - jnp.log(1.0 + jnp.exp(-jnp.abs(x)))


def _run_blocks(head_fn, heads, width, out_ref, so_ref, si_ref, *, mode, block, n_rows):
    if mode == "prompt":
        @pl.when(pl.program_id(1) == 0)
        def _():
            so_ref[...] = jnp.zeros(so_ref.shape, F32)

        def body(c, carry):
            rows = pl.ds(pl.multiple_of(c * block, block), block)
            load = lambda ref, cols: ref[rows, cols].astype(F32)
            for h in range(heads):
                out, s_new = head_fn(h, load, so_ref[0, h])
                so_ref[0, h] = s_new
                out_ref[rows, h * width:(h + 1) * width] = out.astype(out_ref.dtype)
            return carry
        lax.fori_loop(0, n_rows // block, body, 0)
    else:
        for h in range(heads):
            outs = []
            for s in range(n_rows // block):
                load = lambda ref, cols, s=s: ref[:, cols].astype(F32)[s * block:(s + 1) * block]
                out, s_new = head_fn(h, load, si_ref[s, h])
                so_ref[s, h] = s_new
                outs.append(out)
            out_ref[:, h * width:(h + 1) * width] = jnp.concatenate(outs, axis=0).astype(out_ref.dtype)


def _gla_body(*refs, mode, heads, dk, dv, block, sub, mm_dtype):
    if mode == "prompt":
        q_ref, k_ref, v_ref, g_ref, ga_ref, wup_ref, ba_ref, gn_ref, og_ref, so_ref = refs
        si_ref = None
    else:
        q_ref, k_ref, v_ref, g_ref, ga_ref, wup_ref, ba_ref, gn_ref, si_ref, og_ref, so_ref = refs

    def head_fn(h, load, s_prev):
        ks = slice(h * dk, (h + 1) * dk)
        vs = slice(h * dv, (h + 1) * dv)
        ga = load(ga_ref, slice(None)).astype(BF16)
        a_logit = jnp.dot(ga, wup_ref[:, ks], preferred_element_type=F32) + ba_ref[:, ks]
        la = _log_sigmoid(a_logit) * (1.0 / GLA_GATE_NORM)
        q = load(q_ref, ks) * (dk ** -0.5)
        o, s_new = _recurrence_block(q, load(k_ref, ks), load(v_ref, vs), la, s_prev, sub=sub, mm_dtype=mm_dtype)
        return _head_gated_norm(o, gn_ref[:, vs], load(g_ref, vs)), s_new

    _run_blocks(head_fn, heads, dv, og_ref, so_ref, si_ref, mode=mode, block=block, n_rows=q_ref.shape[0])


def _gla(z, zga, wup, ba, gnorm, state, *, mode, layer, n_seq, seq_len, d_model, name):
    heads = GLA_HEADS
    dk = d_model // 2 // heads
    dv = d_model // heads
    t = z.shape[0]
    if mode == "prompt":
        step = PROMPT_STEP
        n_t = seq_len // step
        grid = (n_seq, n_t)
        block, sub, mm_dtype = PROMPT_BLOCK, REF_CHUNK, BF16
        st_out = pl.BlockSpec((1, heads, dk, dv), lambda b, i: (b, 0, 0, 0))
        const = lambda b, i: (0, 0)
        zspec = lambda width, col: pl.BlockSpec((step, width), lambda b, i: (b * n_t + i, col))
    else:
        step = SAMPLE_SEQS * seq_len
        grid = (n_seq // SAMPLE_SEQS,)
        block = seq_len
        sub, mm_dtype = math.gcd(REF_CHUNK, seq_len), F32
        st_out = pl.BlockSpec((SAMPLE_SEQS, heads, dk, dv), lambda i: (i, 0, 0, 0))
        const = lambda i: (0, 0)
        zspec = lambda width, col: pl.BlockSpec((step, width), lambda i: (i, col))
    kw = heads * dk
    in_specs = [zspec(kw, 0), zspec(kw, 1), zspec(d_model, 1), zspec(d_model, 2),
                zspec(LANE, 0),
                pl.BlockSpec(wup.shape, const), pl.BlockSpec(ba.shape, const), pl.BlockSpec(gnorm.shape, const)]
    args = [z, z, z, z, zga, wup, ba, gnorm]
    if mode != "prompt":
        in_specs.append(pl.BlockSpec((None, SAMPLE_SEQS, heads, dk, dv), lambda i: (layer, i, 0, 0, 0)))
        args.append(state)
    og, s_out = pl.pallas_call(
        functools.partial(_gla_body, mode=mode, heads=heads, dk=dk, dv=dv, block=block, sub=sub,
                          mm_dtype=mm_dtype),
        grid=grid,
        in_specs=in_specs,
        out_specs=[zspec(d_model, 0), st_out],
        out_shape=[jax.ShapeDtypeStruct((t, d_model), BF16),
                   jax.ShapeDtypeStruct((n_seq, heads, dk, dv), F32)],
        compiler_params=_cparams(len(grid)),
        name=name,
    )(*args)
    return og, s_out


def _hgrn_body(*refs, mode, layer, heads, dk, block, sub, mm_dtype):
    if mode == "prompt":
        q_ref, f_ref, i_ref, g_ref, ma_ref, mb_ref, ogla_ref, lb_ref, gn_ref, om_ref, so_ref = refs
        si_ref = None
    else:
        q_ref, f_ref, i_ref, g_ref, ma_ref, mb_ref, ogla_ref, lb_ref, gn_ref, si_ref, om_ref, so_ref = refs

    def head_fn(h, load, s_prev):
        ks = slice(h * dk, (h + 1) * dk)
        p = lb_ref[:, ks]
        e = jnp.exp(p - jnp.max(p, axis=0, keepdims=True))
        lb = jnp.sum(e[:layer + 1], axis=0, keepdims=True) / jnp.sum(e, axis=0, keepdims=True)
        f = lb + (1.0 - lb) * _sigmoid(load(f_ref, ks))
        hq = load(q_ref, ks)
        q = hq * _sigmoid(hq) * (dk ** -0.5)
        o, s_new = _recurrence_block(q, 1.0 - f, load(i_ref, ks), jnp.log(f), s_prev, sub=sub, mm_dtype=mm_dtype)
        o_h = _head_gated_norm(o, gn_ref[:, ks], load(g_ref, ks))
        mix = _sigmoid(load(ma_ref, ks)) * load(ogla_ref, ks) + _sigmoid(load(mb_ref, ks)) * o_h
        return mix, s_new

    _run_blocks(head_fn, heads, dk, om_ref, so_ref, si_ref, mode=mode, block=block, n_rows=q_ref.shape[0])


def _hgrn(z, og_gla, lb_param, gnorm, state, *, mode, layer, n_seq, seq_len, d_model, name):
    dk = HGRN_DK
    heads = d_model // dk
    t = z.shape[0]
    if mode == "prompt":
        step = PROMPT_STEP
        n_t = seq_len // step
        grid = (n_seq, n_t)
        block, sub, mm_dtype = PROMPT_BLOCK, REF_CHUNK, BF16
        st_out = pl.BlockSpec((1, heads, dk, dk), lambda b, i: (b, 0, 0, 0))
        const = lambda b, i: (0, 0)
        zspec = lambda col: pl.BlockSpec((step, d_model), lambda b, i: (b * n_t + i, col))
    else:
        step = SAMPLE_SEQS * seq_len
        grid = (n_seq // SAMPLE_SEQS,)
        block = seq_len
        sub, mm_dtype = math.gcd(REF_CHUNK, seq_len), F32
        st_out = pl.BlockSpec((SAMPLE_SEQS, heads, dk, dk), lambda i: (i, 0, 0, 0))
        const = lambda i: (0, 0)
        zspec = lambda col: pl.BlockSpec((step, d_model), lambda i: (i, col))
    in_specs = [zspec(0), zspec(1), zspec(2), zspec(3), zspec(4), zspec(5), zspec(0),
                pl.BlockSpec(lb_param.shape, const), pl.BlockSpec(gnorm.shape, const)]
    args = [z, z, z, z, z, z, og_gla, lb_param, gnorm]
    if mode != "prompt":
        in_specs.append(pl.BlockSpec((None, SAMPLE_SEQS, heads, dk, dk), lambda i: (layer, i, 0, 0, 0)))
        args.append(state)
    o_mix, s_out = pl.pallas_call(
        functools.partial(_hgrn_body, mode=mode, layer=layer, heads=heads, dk=dk, block=block, sub=sub,
                          mm_dtype=mm_dtype),
        grid=grid,
        in_specs=in_specs,
        out_specs=[zspec(0), st_out],
        out_shape=[jax.ShapeDtypeStruct((t, d_model), BF16),
                   jax.ShapeDtypeStruct((n_seq, heads, dk, dk), F32)],
        compiler_params=_cparams(len(grid)),
        name=name,
    )(*args)
    return o_mix, s_out


def _xattn_prompt_body(q_ref, k_ref, v_ref, o_ref, *, heads, dh):
    scale = dh ** -0.5
    for h in range(heads):
        hs = slice(h * dh, (h + 1) * dh)
        k = k_ref[0, :, hs].astype(BF16)
        v = v_ref[0, :, hs].astype(BF16)
        sc = lax.dot_general(q_ref[:, hs], k, (((1,), (1,)), ((), ())), preferred_element_type=F32) * scale
        p = jnp.exp(sc - jnp.max(sc, axis=-1, keepdims=True))
        o = jnp.dot(p.astype(BF16), v, preferred_element_type=F32) / jnp.sum(p, axis=-1, keepdims=True)
        o_ref[:, hs] = o.astype(o_ref.dtype)


def _xattn_sample_body(q_ref, k_ref, v_ref, o_ref, *, heads, dh, seq_len):
    scale = dh ** -0.5
    n_mem = k_ref.shape[1]
    for s in range(k_ref.shape[0]):
        rows = slice(s * seq_len, (s + 1) * seq_len)
        q = q_ref[rows, :]
        q2 = jnp.concatenate([q[:, h * dh:(h + 1) * dh] for h in range(heads)], axis=0)
        k2 = k_ref[s].reshape(n_mem * heads, dh)
        v2 = v_ref[s].reshape(n_mem * heads, dh)
        sc = lax.dot_general(q2, k2, (((1,), (1,)), ((), ())), preferred_element_type=F32) * scale
        q_head = lax.broadcasted_iota(jnp.int32, sc.shape, 0) // seq_len
        k_head = lax.broadcasted_iota(jnp.int32, sc.shape, 1) % heads
        sc = jnp.where(q_head == k_head, sc, -jnp.inf)
        p = jnp.exp(sc - jnp.max(sc, axis=-1, keepdims=True))
        o2 = jnp.dot(p, v2, preferred_element_type=F32) / jnp.sum(p, axis=-1, keepdims=True)
        for h in range(heads):
            o_ref[rows, h * dh:(h + 1) * dh] = o2[h * seq_len:(h + 1) * seq_len].astype(o_ref.dtype)


def _xattn(mq, mem_k, mem_v, *, mode, layer, n_seq, seq_len, name):
    t, d = mq.shape
    dh = d // MEM_HEADS
    if mode == "prompt":
        n_mem = mem_k.shape[1]
        tq = _pick_tile(seq_len, (512, 256, 128))
        n_t = seq_len // tq
        grid = (n_seq, n_t)
        q_spec = pl.BlockSpec((tq, d), lambda b, i: (b * n_t + i, 0))
        kv_spec = pl.BlockSpec((1, n_mem, d), lambda b, i: (b, 0, 0))
        body = functools.partial(_xattn_prompt_body, heads=MEM_HEADS, dh=dh)
        out_dtype = BF16
    else:
        n_mem = mem_k.shape[2]
        grid = (n_seq // SAMPLE_SEQS,)
        q_spec = pl.BlockSpec((SAMPLE_SEQS * seq_len, d), lambda i: (i, 0))
        kv_spec = pl.BlockSpec((None, SAMPLE_SEQS, n_mem, MEM_HEADS, dh), lambda i: (layer, i, 0, 0, 0))
        body = functools.partial(_xattn_sample_body, heads=MEM_HEADS, dh=dh, seq_len=seq_len)
        out_dtype = F32
    return pl.pallas_call(
        body,
        grid=grid,
        in_specs=[q_spec, kv_spec, kv_spec],
        out_specs=q_spec,
        out_shape=jax.ShapeDtypeStruct((t, d), out_dtype),
        compiler_params=_cparams(len(grid)),
        name=name,
    )(mq, mem_k, mem_v)


def _ffn_up_body(*refs, mode, seq_len, tiles_per_seq, row_chunk):
    if mode == "prompt":
        x_ref, g_ref, wg_ref, wv_ref, cw_ref, cb_ref, act_ref, h_ref, tail_ref = refs
        p1_ref = p2_ref = None
    else:
        x_ref, g_ref, wg_ref, wv_ref, cw_ref, cb_ref, p1_ref, p2_ref, act_ref, h_ref = refs
        tail_ref = None
    i = pl.program_id(0)
    j = pl.program_id(1)

    @pl.when(j == 0)
    def _():
        def body(c, carry):
            rows = pl.ds(pl.multiple_of(c * row_chunk, row_chunk), row_chunk)
            xf = x_ref[rows, :]
            ms = jnp.mean(xf * xf, axis=-1, keepdims=True)
            h_ref[rows, :] = (xf * lax.rsqrt(ms + EPS) * g_ref[...]).astype(BF16)
            return carry
        lax.fori_loop(0, x_ref.shape[0] // row_chunk, body, 0)

    h = h_ref[...]
    ug = jnp.dot(h, wg_ref[...], preferred_element_type=F32)
    uv = jnp.dot(h, wv_ref[...], preferred_element_type=F32)
    tm = ug.shape[0]
    row = lax.broadcasted_iota(jnp.int32, ug.shape, 0)
    roll1 = pltpu.roll(ug, 1, axis=0)
    roll2 = pltpu.roll(ug, 2, axis=0)
    if mode == "prompt":
        @pl.when(i % tiles_per_seq == 0)
        def _():
            tail_ref[j] = jnp.zeros(tail_ref.shape[1:], F32)
        tail = tail_ref[j]
        prev1 = tail[7:8, :]
        prev2 = tail[6:7, :]
        sh1 = jnp.where(row >= 1, roll1, prev1)
        sh2 = jnp.where(row >= 2, roll2, jnp.where(row == 0, prev2, prev1))
        tail_ref[j] = ug[tm - 8:, :]
    else:
        pos = row % seq_len
        sh1 = jnp.where(pos >= 1, roll1, p1_ref[...])
        sh2 = jnp.where(pos >= 2, roll2, p2_ref[...])
    conv = cw_ref[0:1, :] * sh2 + cw_ref[1:2, :] * sh1 + cw_ref[2:3, :] * ug + cb_ref[...]
    act_ref[...] = (conv * _sigmoid(conv) * uv).astype(act_ref.dtype)


def _ffn_up(x, gain, wg, wv, cw, cb, carry, *, mode, seq_len, name):
    m, k = x.shape
    nf = wg.shape[1]
    tm = _pick_tile(m, (1024,))
    tn = _pick_tile(nf, (512, 256, 128))
    row_chunk = 128
    xs = pl.BlockSpec((tm, k), lambda i, j: (i, 0))
    gs = pl.BlockSpec((1, k), lambda i, j: (0, 0))
    ws = pl.BlockSpec((k, tn), lambda i, j: (0, j))
    cws = pl.BlockSpec((CONV_W, tn), lambda i, j: (0, j))
    cbs = pl.BlockSpec((1, tn), lambda i, j: (0, j))
    ts = pl.BlockSpec((tm, tn), lambda i, j: (i, j))
    in_specs = [xs, gs, ws, ws, cws, cbs]
    args = [x, gain, wg, wv, cw, cb]
    scratch = [pltpu.VMEM((tm, k), BF16)]
    if mode == "prompt":
        assert seq_len % tm == 0
        scratch.append(pltpu.VMEM((nf // tn, 8, tn), F32))
    else:
        assert tm % seq_len == 0
        in_specs += [ts, ts]
        args += list(carry)
    return pl.pallas_call(
        functools.partial(_ffn_up_body, mode=mode, seq_len=seq_len, tiles_per_seq=max(seq_len // tm, 1),
                          row_chunk=row_chunk),
        grid=(m // tm, nf // tn),
        in_specs=in_specs,
        out_specs=ts,
        out_shape=jax.ShapeDtypeStruct((m, nf), BF16),
        scratch_shapes=scratch,
        compiler_params=_cparams(2),
        name=name,
    )(*args)


def _pad_cols(w, n):
    return jnp.pad(w, ((0, 0), (0, n - w.shape[1])))


def _layer(x, mem_k, mem_v, s_gla, s_hgrn, conv_buf, p, *, mode, layer, n_seq, seq_len, tag):
    t, d = x.shape
    nm = lambda s: f"{s}_{tag}"
    za = _dense(x, p["w_in"], gain=p["norm_mix_g"], prologue="norm", out_dtype=BF16, name=nm("in_proj_a"),
                layer=layer, col_off=0, n_cols=p["ga_off"])
    zb = _dense(x, p["w_in_b"], gain=p["norm_mix_g"], prologue="norm", out_dtype=BF16, name=nm("in_proj_b"))
    zga = _dense(x, p["w_ga"], gain=p["norm_mix_g"], prologue="norm", out_dtype=F32, name=nm("ga_proj"))
    og_gla, s_gla_new = _gla(za, zga, p["gla_w_a_up"], p["gla_b_a"], p["gla_norm_g"], s_gla,
                             mode=mode, layer=layer, n_seq=n_seq, seq_len=seq_len, d_model=d, name=nm("gla"))
    o_mix, s_hgrn_new = _hgrn(zb, og_gla, p["hgrn_lower_bound"], p["hgrn_norm_g"], s_hgrn,
                              mode=mode, layer=layer, n_seq=n_seq, seq_len=seq_len, d_model=d, name=nm("hgrn"))
    x1 = _dense(o_mix, p["w_out"], res=x, prologue="plain", out_dtype=F32, name=nm("out_proj"), layer=layer)
    mq = _dense(x1, p["w_mem_q"], gain=p["norm_xattn_g"], prologue="norm",
                out_dtype=BF16 if mode == "prompt" else F32, name=nm("mem_q"), layer=layer)
    mo = _xattn(mq, mem_k, mem_v, mode=mode, layer=layer, n_seq=n_seq, seq_len=seq_len, name=nm("xattn"))
    x2 = _dense(mo, p["w_mem_o"], res=x1, prologue="plain" if mode == "prompt" else "cast", out_dtype=F32,
                name=nm("mem_o"), layer=layer)

    nf = p["w_up_gate"].shape[1]
    d_ff = p["d_ff"]
    if mode == "prompt":
        carry = None
    else:
        cb = jnp.pad(conv_buf[layer], ((0, 0), (0, 0), (0, nf - d_ff)))
        zeros = lambda r: jnp.zeros((n_seq, r, nf), F32)
        p1 = jnp.concatenate([cb[:, 1:2], zeros(seq_len - 1)], axis=1).reshape(t, nf)
        p2 = jnp.concatenate([cb, zeros(seq_len - 2)], axis=1).reshape(t, nf)
        carry = (p1, p2)
    act = _ffn_up(x2, p["norm_ffn_g"], p["w_up_gate"], p["w_up_val"], p["ffn_conv_w"], p["ffn_conv_b"], carry,
                  mode=mode, seq_len=seq_len, name=nm("ffn_up"))
    x3 = _dense(act, p["w_down"], res=x2, prologue="plain", out_dtype=F32, name=nm("ffn_down"))

    last = x2.reshape(n_seq, seq_len, d)[:, seq_len - (CONV_W - 1):].reshape(n_seq * (CONV_W - 1), d)
    last = jnp.pad(last, ((0, (-last.shape[0]) % BF16_ROWS), (0, 0)))
    ug_last = _dense(last, p["w_up_gate"], gain=p["norm_ffn_g"], prologue="norm", out_dtype=F32,
                     name=nm("conv_state"))
    conv_new = ug_last[:n_seq * (CONV_W - 1), :d_ff].reshape(n_seq, CONV_W - 1, d_ff)
    return x3, s_gla_new, s_hgrn_new, conv_new


def kernel(x_prompt, x_sample, mem_prompt, state_gla, state_hgrn, state_ffn_conv, cache_mem_k, cache_mem_v,
           norm_mix_g, w_in, gla_w_a_up, gla_b_a, gla_norm_g, hgrn_lower_bound, hgrn_norm_g, w_out,
           norm_xattn_g, norm_mem_g, w_mem_q, w_mem_k, w_mem_v, w_mem_o,
           norm_ffn_g, w_ffn_up, ffn_conv_w, ffn_conv_b, w_ffn_down, norm_final_g):
    depth = w_in.shape[0]
    bp, lp, d = x_prompt.shape
    bs, ls, _ = x_sample.shape
    n_mem = mem_prompt.shape[1]
    d_ff = w_ffn_down.shape[1]
    rank = gla_w_a_up.shape[1]
    qk = gla_w_a_up.shape[2]
    ga_off = 2 * qk + 2 * d
    nf = -(-d_ff // 512) * 512

    xp = x_prompt.reshape(bp * lp, d)
    xs = x_sample.reshape(bs * ls, d)
    mem = mem_prompt.reshape(bp * n_mem, d)
    row = lambda v: v.reshape(1, -1)

    outs = {k: [] for k in ("gla_p", "hgrn_p", "conv_p", "mk_p", "mv_p", "gla_s", "hgrn_s", "conv_s")}
    for l in range(depth):
        p = {
            "w_in": w_in, "w_out": w_out, "w_mem_q": w_mem_q, "w_mem_o": w_mem_o,
            "w_in_b": w_in[l][:, ga_off + rank:].astype(BF16),
            "w_ga": _pad_cols(w_in[l][:, ga_off:ga_off + rank], LANE).astype(BF16),
            "ga_off": ga_off,
            "gla_w_a_up": jnp.pad(gla_w_a_up[l], ((0, LANE - rank), (0, 0))).astype(BF16),
            "gla_b_a": row(gla_b_a[l]), "gla_norm_g": row(gla_norm_g[l]),
            "hgrn_lower_bound": hgrn_lower_bound, "hgrn_norm_g": row(hgrn_norm_g[l]),
            "norm_mix_g": row(norm_mix_g[l]), "norm_xattn_g": row(norm_xattn_g[l]),
            "norm_ffn_g": row(norm_ffn_g[l]),
            "w_up_gate": _pad_cols(w_ffn_up[l][:, :d_ff], nf).astype(BF16),
            "w_up_val": _pad_cols(w_ffn_up[l][:, d_ff:], nf).astype(BF16),
            "ffn_conv_w": _pad_cols(ffn_conv_w[l], nf), "ffn_conv_b": _pad_cols(row(ffn_conv_b[l]), nf),
            "w_down": jnp.pad(w_ffn_down[l], ((0, nf - d_ff), (0, 0))).astype(BF16),
            "d_ff": d_ff,
        }
        g_mem = row(norm_mem_g[l])
        mk = _dense(mem, w_mem_k, gain=g_mem, prologue="norm", out_dtype=F32, name=f"mem_k_{l}", layer=l)
        mv = _dense(mem, w_mem_v, gain=g_mem, prologue="norm", out_dtype=F32, name=f"mem_v_{l}", layer=l)
        mk3, mv3 = mk.reshape(bp, n_mem, d), mv.reshape(bp, n_mem, d)
        xp, sg, sh, cb = _layer(xp, mk3, mv3, None, None, None, p, mode="prompt", layer=l,
                                n_seq=bp, seq_len=lp, tag=f"p{l}")
        outs["gla_p"].append(sg); outs["hgrn_p"].append(sh); outs["conv_p"].append(cb)
        outs["mk_p"].append(mk3.reshape(bp, n_mem, MEM_HEADS, d // MEM_HEADS))
        outs["mv_p"].append(mv3.reshape(bp, n_mem, MEM_HEADS, d // MEM_HEADS))
        xs, sg2, sh2, cb2 = _layer(xs, cache_mem_k, cache_mem_v, state_gla, state_hgrn, state_ffn_conv, p,
                                   mode="sample", layer=l, n_seq=bs, seq_len=ls, tag=f"s{l}")
        outs["gla_s"].append(sg2); outs["hgrn_s"].append(sh2); outs["conv_s"].append(cb2)

    g_fin = row(norm_final_g)
    y_prompt = _rmsnorm(xp, g_fin, "final_norm_p").reshape(bp, lp, d)
    y_sample = _rmsnorm(xs, g_fin, "final_norm_s").reshape(bs, ls, d)
    st = lambda k: jnp.stack(outs[k])
    return (y_prompt, y_sample, st("gla_p"), st("hgrn_p"), st("conv_p"), st("mk_p"), st("mv_p"),
            st("gla_s"), st("hgrn_s"), st("conv_s"))
```

```python
import functools
import math

import jax
import jax.numpy as jnp
from jax import lax
from jax.experimental import pallas as pl
from jax.experimental.pallas import tpu as pltpu

F32 = jnp.float32
BF16 = jnp.bfloat16

LANE = 128
BF16_ROWS = 16
V7X_VMEM_LIMIT = 56 * 1024 * 1024

EPS = 1e-6
GLA_HEADS = 4
GLA_GATE_NORM = 16.0
HGRN_DK = 128
MEM_HEADS = 4
CONV_W = 3
REF_CHUNK = 16
PROMPT_BLOCK = 64
PROMPT_STEP = 256
SAMPLE_SEQS = 2


def _cparams(n_axes):
    return pltpu.CompilerParams(dimension_semantics=("arbitrary",) * n_axes,
                                vmem_limit_bytes=V7X_VMEM_LIMIT)


def _sigmoid(x):
    return 0.5 * jnp.tanh(0.5 * x) + 0.5


def _pick_tile(n, candidates):
    for c in candidates:
        if n % c == 0:
            return c
    return n


def _dense_body(*refs, prologue, has_res, row_chunk, w_transposed):
    it = iter(refs)
    x_ref = next(it)
    g_ref = next(it) if prologue == "norm" else None
    w_ref = next(it)
    r_ref = next(it) if has_res else None
    o_ref = next(it)
    h_ref = next(it) if prologue != "plain" else None

    if prologue != "plain":
        @pl.when(pl.program_id(1) == 0)
        def _():
            def body(c, carry):
                rows = pl.ds(pl.multiple_of(c * row_chunk, row_chunk), row_chunk)
                xf = x_ref[rows, :].astype(F32)
                if prologue == "norm":
                    ms = jnp.mean(xf * xf, axis=-1, keepdims=True)
                    xf = xf * lax.rsqrt(ms + EPS) * g_ref[...]
                h_ref[rows, :] = xf.astype(BF16)
                return carry
            lax.fori_loop(0, x_ref.shape[0] // row_chunk, body, 0)
        lhs = h_ref[...]
    else:
        lhs = x_ref[...]
    w = w_ref[...].astype(BF16)
    if w_transposed:
        acc = lax.dot_general(lhs, w, (((1,), (1,)), ((), ())), preferred_element_type=F32)
    else:
        acc = jnp.dot(lhs, w, preferred_element_type=F32)
    if has_res:
        acc = acc + r_ref[...]
    o_ref[...] = acc.astype(o_ref.dtype)


def _dense(x, w, *, gain=None, res=None, prologue, out_dtype, name, layer=0, col_off=0, n_cols=None,
           w_transposed=False):
    m, k = x.shape
    n = n_cols if n_cols is not None else w.shape[-1]
    tm = _pick_tile(m, (1024, 512, 256, 128, 64, 32, 16))
    tn = _pick_tile(n, (1024, 512, 256, 128) if k <= 2048 else (512, 256, 128))
    row_chunk = min(tm, 128)
    in_specs = [pl.BlockSpec((tm, k), lambda i, j: (i, 0))]
    args = [x]
    if prologue == "norm":
        in_specs.append(pl.BlockSpec((1, k), lambda i, j: (0, 0)))
        args.append(gain)
    if w_transposed:
        assert col_off % BF16_ROWS == 0 and tn % BF16_ROWS == 0
        in_specs.append(pl.BlockSpec((pl.Element(tn), pl.Element(k)),
                                     lambda i, j: (pl.multiple_of(col_off + j * tn, BF16_ROWS), 0)))
    elif w.ndim == 3:
        assert col_off % tn == 0
        in_specs.append(pl.BlockSpec((None, k, tn), lambda i, j: (layer, 0, j + col_off // tn)))
    else:
        in_specs.append(pl.BlockSpec((k, tn), lambda i, j: (0, j)))
    args.append(w)
    if res is not None:
        in_specs.append(pl.BlockSpec((tm, tn), lambda i, j: (i, j)))
        args.append(res)
    scratch = [] if prologue == "plain" else [pltpu.VMEM((tm, k), BF16)]
    return pl.pallas_call(
        functools.partial(_dense_body, prologue=prologue, has_res=res is not None, row_chunk=row_chunk,
                          w_transposed=w_transposed),
        grid=(m // tm, n // tn),
        in_specs=in_specs,
        out_specs=pl.BlockSpec((tm, tn), lambda i, j: (i, j)),
        out_shape=jax.ShapeDtypeStruct((m, n), out_dtype),
        scratch_shapes=scratch,
        compiler_params=_cparams(2),
        name=name,
    )(*args)


def _rmsnorm_body(x_ref, g_ref, o_ref):
    xf = x_ref[...]
    ms = jnp.mean(xf * xf, axis=-1, keepdims=True)
    o_ref[...] = xf * lax.rsqrt(ms + EPS) * g_ref[...]


def _rmsnorm(x, gain, name):
    m, k = x.shape
    tm = _pick_tile(m, (256, 128, 64, 32, 16, 8))
    return pl.pallas_call(
        _rmsnorm_body,
        grid=(m // tm,),
        in_specs=[pl.BlockSpec((tm, k), lambda i: (i, 0)), pl.BlockSpec((1, k), lambda i: (0, 0))],
        out_specs=pl.BlockSpec((tm, k), lambda i: (i, 0)),
        out_shape=jax.ShapeDtypeStruct((m, k), F32),
        compiler_params=_cparams(1),
        name=name,
    )(x, gain)


def _cumsum_rows(x, group):
    rows = lax.broadcasted_iota(jnp.int32, x.shape, 0) % group
    shift = 1
    while shift < group:
        rolled = pltpu.roll(x, shift, axis=0)
        x = x + jnp.where(rows >= shift, rolled, 0.0)
        shift *= 2
    return x


def _lockstep(gens):
    gens = list(gens)
    results = [None] * len(gens)
    live = list(range(len(gens)))
    while live:
        still = []
        for n in live:
            try:
                next(gens[n])
                still.append(n)
            except StopIteration as stop:
                results[n] = stop.value
        live = still
    return results


def _recurrence_block(q, k, v, la, s_read, *, sub, mm_dtype):
    bt, kd = q.shape
    vd = v.shape[1]
    ns = bt // sub
    b_loc = _cumsum_rows(la, sub)
    qd = q * jnp.exp(b_loc)
    ki = k * jnp.exp(-b_loc)
    tot = [b_loc[(i + 1) * sub - 1:(i + 1) * sub, :] for i in range(ns)]
    pre = [jnp.zeros((1, kd), F32)]
    for i in range(ns):
        pre.append(pre[-1] + tot[i])
    sl = [slice(i * sub, (i + 1) * sub) for i in range(ns)]
    vb = v.astype(mm_dtype)
    qdm = qd.astype(mm_dtype)
    q_state = jnp.concatenate([qd[sl[i]] * jnp.exp(pre[i]) for i in range(ns)], axis=0) if ns > 1 else qd
    q_state = q_state.astype(mm_dtype)
    ke = [ki[sl[i]] * jnp.exp(tot[i]) for i in range(ns)]
    kmats = []
    for i in range(ns):
        parts = [ke[j] if j == i - 1 else ke[j] * jnp.exp(pre[i] - pre[j + 1]) for j in range(i)]
        parts.append(ki[sl[i]])
        parts += [jnp.zeros((sub, kd), F32)] * (ns - 1 - i)
        kmats.append((jnp.concatenate(parts, axis=0) if ns > 1 else parts[0]).astype(mm_dtype))
    k_end = jnp.concatenate([ke[i] if i == ns - 1 else ke[i] * jnp.exp(pre[ns] - pre[i + 1])
                             for i in range(ns)], axis=0) if ns > 1 else ke[0]
    k_end = k_end.astype(mm_dtype)
    decay = jnp.broadcast_to(jnp.exp(pre[ns]), (LANE, kd)).T
    yield

    s_prev = s_read()
    o_state = jnp.dot(q_state, s_prev.astype(mm_dtype), preferred_element_type=F32)
    a_rows = [lax.dot_general(qdm[sl[i]], kmats[i], (((1,), (1,)), ((), ())), preferred_element_type=F32)
              for i in range(ns)]
    ds = lax.dot_general(k_end, vb, (((0,), (0,)), ((), ())), preferred_element_type=F32)
    yield

    a = jnp.concatenate(a_rows, axis=0) if ns > 1 else a_rows[0]
    row = lax.broadcasted_iota(jnp.int32, a.shape, 0)
    col = lax.broadcasted_iota(jnp.int32, a.shape, 1)
    a = jnp.where(col <= row, a, 0.0).astype(mm_dtype)
    o = jnp.dot(a, vb, preferred_element_type=F32) + o_state
    s_new = jnp.concatenate([decay * s_prev[:, c * LANE:(c + 1) * LANE] for c in range(vd // LANE)],
                            axis=1) + ds
    yield
    return o, s_new


def _head_gated_norm(o, gnorm, gate):
    ms = jnp.mean(o * o, axis=-1, keepdims=True)
    return o * lax.rsqrt(ms + EPS) * gnorm * (gate * _sigmoid(gate))


def _log_sigmoid(x):
    return jnp.minimum(x, 0.0) - jnp.log(1.0 + jnp.exp(-jnp.abs(x)))


def _run_blocks(head_fn, heads, width, out_ref, so_ref, si_ref, *, mode, block, n_rows):
    if mode == "prompt":
        @pl.when(pl.program_id(1) == 0)
        def _():
            so_ref[...] = jnp.zeros(so_ref.shape, F32)

        def body(c, carry):
            rows = pl.ds(pl.multiple_of(c * block, block), block)
            load = lambda ref, cols: ref[rows, cols].astype(F32)
            res = _lockstep(head_fn(h, load, lambda h=h: so_ref[0, h]) for h in range(heads))
            for h, (out, s_new) in enumerate(res):
                so_ref[0, h] = s_new
                out_ref[rows, h * width:(h + 1) * width] = out.astype(out_ref.dtype)
            return carry
        lax.fori_loop(0, n_rows // block, body, 0)
    else:
        n_seq = n_rows // block
        loads = [lambda ref, cols, s=s: ref[:, cols].astype(F32)[s * block:(s + 1) * block] for s in range(n_seq)]
        res = _lockstep(head_fn(h, loads[s], lambda s=s, h=h: si_ref[s, h])
                        for h in range(heads) for s in range(n_seq))
        for h in range(heads):
            for s in range(n_seq):
                so_ref[s, h] = res[h * n_seq + s][1]
            out_ref[:, h * width:(h + 1) * width] = jnp.concatenate(
                [res[h * n_seq + s][0] for s in range(n_seq)], axis=0).astype(out_ref.dtype)


def _gla_body(*refs, mode, heads, dk, dv, block, sub, mm_dtype):
    if mode == "prompt":
        q_ref, k_ref, v_ref, g_ref, ga_ref, wup_ref, ba_ref, gn_ref, og_ref, so_ref = refs
        si_ref = None
    else:
        q_ref, k_ref, v_ref, g_ref, ga_ref, wup_ref, ba_ref, gn_ref, si_ref, og_ref, so_ref = refs

    def head_fn(h, load, s_read):
        ks = slice(h * dk, (h + 1) * dk)
        vs = slice(h * dv, (h + 1) * dv)
        ga = load(ga_ref, slice(None)).astype(BF16)
        a_logit = jnp.dot(ga, wup_ref[:, ks], preferred_element_type=F32) + ba_ref[:, ks]
        yield
        la = _log_sigmoid(a_logit) * (1.0 / GLA_GATE_NORM)
        q = load(q_ref, ks) * (dk ** -0.5)
        o, s_new = yield from _recurrence_block(q, load(k_ref, ks), load(v_ref, vs), la, s_read,
                                                sub=sub, mm_dtype=mm_dtype)
        return _head_gated_norm(o, gn_ref[:, vs], load(g_ref, vs)), s_new

    _run_blocks(head_fn, heads, dv, og_ref, so_ref, si_ref, mode=mode, block=block, n_rows=q_ref.shape[0])


def _gla(z, zga, wup, ba, gnorm, state, *, mode, layer, n_seq, seq_len, d_model, name):
    heads = GLA_HEADS
    dk = d_model // 2 // heads
    dv = d_model // heads
    t = z.shape[0]
    if mode == "prompt":
        step = PROMPT_STEP
        n_t = seq_len // step
        grid = (n_seq, n_t)
        block, sub, mm_dtype = PROMPT_BLOCK, REF_CHUNK, BF16
        st_out = pl.BlockSpec((1, heads, dk, dv), lambda b, i: (b, 0, 0, 0))
        const = lambda b, i: (0, 0)
        zspec = lambda width, col: pl.BlockSpec((step, width), lambda b, i: (b * n_t + i, col))
    else:
        step = SAMPLE_SEQS * seq_len
        grid = (n_seq // SAMPLE_SEQS,)
        block = seq_len
        sub, mm_dtype = math.gcd(REF_CHUNK, seq_len), F32
        st_out = pl.BlockSpec((SAMPLE_SEQS, heads, dk, dv), lambda i: (i, 0, 0, 0))
        const = lambda i: (0, 0)
        zspec = lambda width, col: pl.BlockSpec((step, width), lambda i: (i, col))
    kw = heads * dk
    in_specs = [zspec(kw, 0), zspec(kw, 1), zspec(d_model, 1), zspec(d_model, 2),
                zspec(zga.shape[1], 0),
                pl.BlockSpec(wup.shape, const), pl.BlockSpec(ba.shape, const), pl.BlockSpec(gnorm.shape, const)]
    args = [z, z, z, z, zga, wup, ba, gnorm]
    if mode != "prompt":
        in_specs.append(pl.BlockSpec((None, SAMPLE_SEQS, heads, dk, dv), lambda i: (layer, i, 0, 0, 0)))
        args.append(state)
    og, s_out = pl.pallas_call(
        functools.partial(_gla_body, mode=mode, heads=heads, dk=dk, dv=dv, block=block, sub=sub,
                          mm_dtype=mm_dtype),
        grid=grid,
        in_specs=in_specs,
        out_specs=[zspec(d_model, 0), st_out],
        out_shape=[jax.ShapeDtypeStruct((t, d_model), BF16),
                   jax.ShapeDtypeStruct((n_seq, heads, dk, dv), F32)],
        compiler_params=_cparams(len(grid)),
        name=name,
    )(*args)
    return og, s_out


def _hgrn_body(*refs, mode, layer, heads, dk, block, sub, mm_dtype):
    if mode == "prompt":
        q_ref, f_ref, i_ref, g_ref, ma_ref, mb_ref, ogla_ref, lb_ref, gn_ref, om_ref, so_ref = refs
        si_ref = None
    else:
        q_ref, f_ref, i_ref, g_ref, ma_ref, mb_ref, ogla_ref, lb_ref, gn_ref, si_ref, om_ref, so_ref = refs

    p = lb_ref[...]
    e = jnp.exp(p - jnp.max(p, axis=0, keepdims=True))
    lb_all = jnp.sum(e[:layer + 1], axis=0, keepdims=True) / jnp.sum(e, axis=0, keepdims=True)

    def head_fn(h, load, s_read):
        ks = slice(h * dk, (h + 1) * dk)
        lb = lb_all[:, ks]
        f = lb + (1.0 - lb) * _sigmoid(load(f_ref, ks))
        hq = load(q_ref, ks)
        q = hq * _sigmoid(hq) * (dk ** -0.5)
        o, s_new = yield from _recurrence_block(q, 1.0 - f, load(i_ref, ks), jnp.log(f), s_read,
                                                sub=sub, mm_dtype=mm_dtype)
        o_h = _head_gated_norm(o, gn_ref[:, ks], load(g_ref, ks))
        mix = _sigmoid(load(ma_ref, ks)) * load(ogla_ref, ks) + _sigmoid(load(mb_ref, ks)) * o_h
        return mix, s_new

    _run_blocks(head_fn, heads, dk, om_ref, so_ref, si_ref, mode=mode, block=block, n_rows=q_ref.shape[0])


def _hgrn(z, og_gla, lb_param, gnorm, state, *, mode, layer, n_seq, seq_len, d_model, name):
    dk = HGRN_DK
    heads = d_model // dk
    t = z.shape[0]
    if mode == "prompt":
        step = PROMPT_STEP
        n_t = seq_len // step
        grid = (n_seq, n_t)
        block, sub, mm_dtype = PROMPT_BLOCK, REF_CHUNK, BF16
        st_out = pl.BlockSpec((1, heads, dk, dk), lambda b, i: (b, 0, 0, 0))
        const = lambda b, i: (0, 0)
        zspec = lambda col: pl.BlockSpec((step, d_model), lambda b, i: (b * n_t + i, col))
    else:
        step = SAMPLE_SEQS * seq_len
        grid = (n_seq // SAMPLE_SEQS,)
        block = seq_len
        sub, mm_dtype = math.gcd(REF_CHUNK, seq_len), F32
        st_out = pl.BlockSpec((SAMPLE_SEQS, heads, dk, dk), lambda i: (i, 0, 0, 0))
        const = lambda i: (0, 0)
        zspec = lambda col: pl.BlockSpec((step, d_model), lambda i: (i, col))
    in_specs = [zspec(0), zspec(1), zspec(2), zspec(3), zspec(4), zspec(5), zspec(0),
                pl.BlockSpec(lb_param.shape, const), pl.BlockSpec(gnorm.shape, const)]
    args = [z, z, z, z, z, z, og_gla, lb_param, gnorm]
    if mode != "prompt":
        in_specs.append(pl.BlockSpec((None, SAMPLE_SEQS, heads, dk, dk), lambda i: (layer, i, 0, 0, 0)))
        args.append(state)
    o_mix, s_out = pl.pallas_call(
        functools.partial(_hgrn_body, mode=mode, layer=layer, heads=heads, dk=dk, block=block, sub=sub,
                          mm_dtype=mm_dtype),
        grid=grid,
        in_specs=in_specs,
        out_specs=[zspec(0), st_out],
        out_shape=[jax.ShapeDtypeStruct((t, d_model), BF16),
                   jax.ShapeDtypeStruct((n_seq, heads, dk, dk), F32)],
        compiler_params=_cparams(len(grid)),
        name=name,
    )(*args)
    return o_mix, s_out


def _xattn_prompt_body(q_ref, k_ref, v_ref, o_ref, *, heads, dh):
    scale = dh ** -0.5
    for h in range(heads):
        hs = slice(h * dh, (h + 1) * dh)
        k = k_ref[0, :, hs].astype(BF16)
        v = v_ref[0, :, hs].astype(BF16)
        sc = lax.dot_general(q_ref[:, hs], k, (((1,), (1,)), ((), ())), preferred_element_type=F32) * scale
        p = jnp.exp(sc - jnp.max(sc, axis=-1, keepdims=True))
        o = jnp.dot(p.astype(BF16), v, preferred_element_type=F32) / jnp.sum(p, axis=-1, keepdims=True)
        o_ref[:, hs] = o.astype(o_ref.dtype)


def _xattn_sample_body(q_ref, k_ref, v_ref, o_ref, *, heads, dh, seq_len):
    scale = dh ** -0.5
    n_mem = k_ref.shape[1]
    for s in range(k_ref.shape[0]):
        rows = slice(s * seq_len, (s + 1) * seq_len)
        q = q_ref[rows, :]
        q2 = jnp.concatenate([q[:, h * dh:(h + 1) * dh] for h in range(heads)], axis=0)
        k2 = k_ref[s].reshape(n_mem * heads, dh)
        v2 = v_ref[s].reshape(n_mem * heads, dh)
        sc = lax.dot_general(q2, k2, (((1,), (1,)), ((), ())), preferred_element_type=F32) * scale
        q_head = lax.broadcasted_iota(jnp.int32, sc.shape, 0) // seq_len
        k_head = lax.broadcasted_iota(jnp.int32, sc.shape, 1) % heads
        sc = jnp.where(q_head == k_head, sc, -jnp.inf)
        p = jnp.exp(sc - jnp.max(sc, axis=-1, keepdims=True))
        o2 = jnp.dot(p, v2, preferred_element_type=F32) / jnp.sum(p, axis=-1, keepdims=True)
        for h in range(heads):
            o_ref[rows, h * dh:(h + 1) * dh] = o2[h * seq_len:(h + 1) * seq_len].astype(o_ref.dtype)


def _xattn(mq, mem_k, mem_v, *, mode, layer, n_seq, seq_len, name):
    t, d = mq.shape
    dh = d // MEM_HEADS
    if mode == "prompt":
        n_mem = mem_k.shape[1]
        tq = _pick_tile(seq_len, (512, 256, 128))
        n_t = seq_len // tq
        grid = (n_seq, n_t)
        q_spec = pl.BlockSpec((tq, d), lambda b, i: (b * n_t + i, 0))
        kv_spec = pl.BlockSpec((1, n_mem, d), lambda b, i: (b, 0, 0))
        body = functools.partial(_xattn_prompt_body, heads=MEM_HEADS, dh=dh)
        out_dtype = BF16
    else:
        n_mem = mem_k.shape[2]
        grid = (n_seq // SAMPLE_SEQS,)
        q_spec = pl.BlockSpec((SAMPLE_SEQS * seq_len, d), lambda i: (i, 0))
        kv_spec = pl.BlockSpec((None, SAMPLE_SEQS, n_mem, MEM_HEADS, dh), lambda i: (layer, i, 0, 0, 0))
        body = functools.partial(_xattn_sample_body, heads=MEM_HEADS, dh=dh, seq_len=seq_len)
        out_dtype = F32
    return pl.pallas_call(
        body,
        grid=grid,
        in_specs=[q_spec, kv_spec, kv_spec],
        out_specs=q_spec,
        out_shape=jax.ShapeDtypeStruct((t, d), out_dtype),
        compiler_params=_cparams(len(grid)),
        name=name,
    )(mq, mem_k, mem_v)


def _ffn_up_body(*refs, mode, seq_len, tiles_per_seq, row_chunk):
    if mode == "prompt":
        x_ref, g_ref, wg_ref, wv_ref, cw_ref, cb_ref, act_ref, h_ref, tail_ref = refs
        p1_ref = p2_ref = None
    else:
        x_ref, g_ref, wg_ref, wv_ref, cw_ref, cb_ref, p1_ref, p2_ref, act_ref, h_ref = refs
        tail_ref = None
    i = pl.program_id(0)
    j = pl.program_id(1)

    @pl.when(j == 0)
    def _():
        def body(c, carry):
            rows = pl.ds(pl.multiple_of(c * row_chunk, row_chunk), row_chunk)
            xf = x_ref[rows, :]
            ms = jnp.mean(xf * xf, axis=-1, keepdims=True)
            h_ref[rows, :] = (xf * lax.rsqrt(ms + EPS) * g_ref[...]).astype(BF16)
            return carry
        lax.fori_loop(0, x_ref.shape[0] // row_chunk, body, 0)

    h = h_ref[...]
    ug = jnp.dot(h, wg_ref[...], preferred_element_type=F32)
    uv = jnp.dot(h, wv_ref[...], preferred_element_type=F32)
    tm = ug.shape[0]
    row = lax.broadcasted_iota(jnp.int32, ug.shape, 0)
    roll1 = pltpu.roll(ug, 1, axis=0)
    roll2 = pltpu.roll(ug, 2, axis=0)
    if mode == "prompt":
        @pl.when(i % tiles_per_seq == 0)
        def _():
            tail_ref[j] = jnp.zeros(tail_ref.shape[1:], F32)
        tail = tail_ref[j]
        prev1 = tail[7:8, :]
        prev2 = tail[6:7, :]
        sh1 = jnp.where(row >= 1, roll1, prev1)
        sh2 = jnp.where(row >= 2, roll2, jnp.where(row == 0, prev2, prev1))
        tail_ref[j] = ug[tm - 8:, :]
    else:
        pos = row % seq_len
        sh1 = jnp.where(pos >= 1, roll1, p1_ref[...])
        sh2 = jnp.where(pos >= 2, roll2, p2_ref[...])
    conv = cw_ref[0:1, :] * sh2 + cw_ref[1:2, :] * sh1 + cw_ref[2:3, :] * ug + cb_ref[...]
    act_ref[...] = (conv * _sigmoid(conv) * uv).astype(act_ref.dtype)


def _ffn_up(x, gain, wg, wv, cw, cb, carry, *, mode, seq_len, name):
    m, k = x.shape
    nf = wg.shape[1]
    tm = _pick_tile(m, (1024,))
    tn = _pick_tile(nf, (512, 256, 128))
    row_chunk = 128
    xs = pl.BlockSpec((tm, k), lambda i, j: (i, 0))
    gs = pl.BlockSpec((1, k), lambda i, j: (0, 0))
    ws = pl.BlockSpec((k, tn), lambda i, j: (0, j))
    cws = pl.BlockSpec((CONV_W, tn), lambda i, j: (0, j))
    cbs = pl.BlockSpec((1, tn), lambda i, j: (0, j))
    ts = pl.BlockSpec((tm, tn), lambda i, j: (i, j))
    in_specs = [xs, gs, ws, ws, cws, cbs]
    args = [x, gain, wg, wv, cw, cb]
    scratch = [pltpu.VMEM((tm, k), BF16)]
    if mode == "prompt":
        assert seq_len % tm == 0
        scratch.append(pltpu.VMEM((nf // tn, 8, tn), F32))
    else:
        assert tm % seq_len == 0
        in_specs += [ts, ts]
        args += list(carry)
    return pl.pallas_call(
        functools.partial(_ffn_up_body, mode=mode, seq_len=seq_len, tiles_per_seq=max(seq_len // tm, 1),
                          row_chunk=row_chunk),
        grid=(m // tm, nf // tn),
        in_specs=in_specs,
        out_specs=ts,
        out_shape=jax.ShapeDtypeStruct((m, nf), BF16),
        scratch_shapes=scratch,
        compiler_params=_cparams(2),
        name=name,
    )(*args)


def _pad_cols(w, n):
    return jnp.pad(w, ((0, 0), (0, n - w.shape[1])))


def _layer(x, mem_k, mem_v, s_gla, s_hgrn, conv_buf, p, *, mode, layer, n_seq, seq_len, tag):
    t, d = x.shape
    nm = lambda s: f"{s}_{tag}"
    ga_off, rank = p["ga_off"], p["rank"]
    w_in_t = p["w_in_t"]
    in_proj = functools.partial(_dense, x, w_in_t, gain=p["norm_mix_g"], prologue="norm", w_transposed=True)
    za = in_proj(out_dtype=BF16, name=nm("in_proj_a"), col_off=0, n_cols=ga_off)
    zb = in_proj(out_dtype=BF16, name=nm("in_proj_b"), col_off=ga_off + rank,
                 n_cols=w_in_t.shape[0] - ga_off - rank)
    zga = in_proj(out_dtype=F32, name=nm("ga_proj"), col_off=ga_off, n_cols=rank)
    og_gla, s_gla_new = _gla(za, zga, p["gla_w_a_up"], p["gla_b_a"], p["gla_norm_g"], s_gla,
                             mode=mode, layer=layer, n_seq=n_seq, seq_len=seq_len, d_model=d, name=nm("gla"))
    o_mix, s_hgrn_new = _hgrn(zb, og_gla, p["hgrn_lower_bound"], p["hgrn_norm_g"], s_hgrn,
                              mode=mode, layer=layer, n_seq=n_seq, seq_len=seq_len, d_model=d, name=nm("hgrn"))
    x1 = _dense(o_mix, p["w_out"], res=x, prologue="plain", out_dtype=F32, name=nm("out_proj"), layer=layer)
    mq = _dense(x1, p["w_mem_q"], gain=p["norm_xattn_g"], prologue="norm",
                out_dtype=BF16 if mode == "prompt" else F32, name=nm("mem_q"), layer=layer)
    mo = _xattn(mq, mem_k, mem_v, mode=mode, layer=layer, n_seq=n_seq, seq_len=seq_len, name=nm("xattn"))
    x2 = _dense(mo, p["w_mem_o"], res=x1, prologue="plain" if mode == "prompt" else "cast", out_dtype=F32,
                name=nm("mem_o"), layer=layer)

    nf = p["w_up_gate"].shape[1]
    d_ff = p["d_ff"]
    if mode == "prompt":
        carry = None
    else:
        cb = jnp.pad(conv_buf[layer], ((0, 0), (0, 0), (0, nf - d_ff)))
        zeros = lambda r: jnp.zeros((n_seq, r, nf), F32)
        p1 = jnp.concatenate([cb[:, 1:2], zeros(seq_len - 1)], axis=1).reshape(t, nf)
        p2 = jnp.concatenate([cb, zeros(seq_len - 2)], axis=1).reshape(t, nf)
        carry = (p1, p2)
    act = _ffn_up(x2, p["norm_ffn_g"], p["w_up_gate"], p["w_up_val"], p["ffn_conv_w"], p["ffn_conv_b"], carry,
                  mode=mode, seq_len=seq_len, name=nm("ffn_up"))
    x3 = _dense(act, p["w_down"], res=x2, prologue="plain", out_dtype=F32, name=nm("ffn_down"))

    last = x2.reshape(n_seq, seq_len, d)[:, seq_len - (CONV_W - 1):].reshape(n_seq * (CONV_W - 1), d)
    last = jnp.pad(last, ((0, (-last.shape[0]) % BF16_ROWS), (0, 0)))
    ug_last = _dense(last, p["w_up_gate"], gain=p["norm_ffn_g"], prologue="norm", out_dtype=F32,
                     name=nm("conv_state"))
    conv_new = ug_last[:n_seq * (CONV_W - 1), :d_ff].reshape(n_seq, CONV_W - 1, d_ff)
    return x3, s_gla_new, s_hgrn_new, conv_new


def kernel(x_prompt, x_sample, mem_prompt, state_gla, state_hgrn, state_ffn_conv, cache_mem_k, cache_mem_v,
           norm_mix_g, w_in, gla_w_a_up, gla_b_a, gla_norm_g, hgrn_lower_bound, hgrn_norm_g, w_out,
           norm_xattn_g, norm_mem_g, w_mem_q, w_mem_k, w_mem_v, w_mem_o,
           norm_ffn_g, w_ffn_up, ffn_conv_w, ffn_conv_b, w_ffn_down, norm_final_g):
    depth = w_in.shape[0]
    bp, lp, d = x_prompt.shape
    bs, ls, _ = x_sample.shape
    n_mem = mem_prompt.shape[1]
    d_ff = w_ffn_down.shape[1]
    rank = gla_w_a_up.shape[1]
    qk = gla_w_a_up.shape[2]
    ga_off = 2 * qk + 2 * d
    nf = -(-d_ff // 512) * 512

    xp = x_prompt.reshape(bp * lp, d)
    xs = x_sample.reshape(bs * ls, d)
    mem = mem_prompt.reshape(bp * n_mem, d)
    row = lambda v: v.reshape(1, -1)

    outs = {k: [] for k in ("gla_p", "hgrn_p", "conv_p", "mk_p", "mv_p", "gla_s", "hgrn_s", "conv_s")}
    for l in range(depth):
        p = {
            "w_out": w_out, "w_mem_q": w_mem_q, "w_mem_o": w_mem_o,
            "w_in_t": jnp.swapaxes(w_in[l], 0, 1).astype(BF16),
            "ga_off": ga_off, "rank": rank,
            "gla_w_a_up": gla_w_a_up[l].astype(BF16),
            "gla_b_a": row(gla_b_a[l]), "gla_norm_g": row(gla_norm_g[l]),
            "hgrn_lower_bound": hgrn_lower_bound, "hgrn_norm_g": row(hgrn_norm_g[l]),
            "norm_mix_g": row(norm_mix_g[l]), "norm_xattn_g": row(norm_xattn_g[l]),
            "norm_ffn_g": row(norm_ffn_g[l]),
            "w_up_gate": _pad_cols(w_ffn_up[l][:, :d_ff], nf).astype(BF16),
            "w_up_val": _pad_cols(w_ffn_up[l][:, d_ff:], nf).astype(BF16),
            "ffn_conv_w": _pad_cols(ffn_conv_w[l], nf), "ffn_conv_b": _pad_cols(row(ffn_conv_b[l]), nf),
            "w_down": jnp.pad(w_ffn_down[l], ((0, nf - d_ff), (0, 0))).astype(BF16),
            "d_ff": d_ff,
        }
        g_mem = row(norm_mem_g[l])
        mk = _dense(mem, w_mem_k, gain=g_mem, prologue="norm", out_dtype=F32, name=f"mem_k_{l}", layer=l)
        mv = _dense(mem, w_mem_v, gain=g_mem, prologue="norm", out_dtype=F32, name=f"mem_v_{l}", layer=l)
        mk3, mv3 = mk.reshape(bp, n_mem, d), mv.reshape(bp, n_mem, d)
        xp, sg, sh, cb = _layer(xp, mk3, mv3, None, None, None, p, mode="prompt", layer=l,
                                n_seq=bp, seq_len=lp, tag=f"p{l}")
        outs["gla_p"].append(sg); outs["hgrn_p"].append(sh); outs["conv_p"].append(cb)
        outs["mk_p"].append(mk3.reshape(bp, n_mem, MEM_HEADS, d // MEM_HEADS))
        outs["mv_p"].append(mv3.reshape(bp, n_mem, MEM_HEADS, d // MEM_HEADS))
        xs, sg2, sh2, cb2 = _layer(xs, cache_mem_k, cache_mem_v, state_gla, state_hgrn, state_ffn_conv, p,
                                   mode="sample", layer=l, n_seq=bs, seq_len=ls, tag=f"s{l}")
        outs["gla_s"].append(sg2); outs["hgrn_s"].append(sh2); outs["conv_s"].append(cb2)

    g_fin = row(norm_final_g)
    y_prompt = _rmsnorm(xp, g_fin, "final_norm_p").reshape(bp, lp, d)
    y_sample = _rmsnorm(xs, g_fin, "final_norm_s").reshape(bs, ls, d)
    st = lambda k: jnp.stack(outs[k])
    return (y_prompt, y_sample, st("gla_p"), st("hgrn_p"), st("conv_p"), st("mk_p"), st("mv_p"),
            st("gla_s"), st("hgrn_s"), st("conv_s"))
```

```python
import functools
import math

import jax
import jax.numpy as jnp
from jax import lax
from jax.experimental import pallas as pl
from jax.experimental.pallas import tpu as pltpu

F32 = jnp.float32
BF16 = jnp.bfloat16

LANE = 128
BF16_ROWS = 16
V7X_VMEM_LIMIT = 56 * 1024 * 1024

EPS = 1e-6
GLA_HEADS = 4
GLA_GATE_NORM = 16.0
HGRN_DK = 128
MEM_HEADS = 4
CONV_W = 3
REF_CHUNK = 16
PROMPT_BLOCK = 64
PROMPT_STEP = 256
SAMPLE_SEQS = 2


def _cparams(n_axes):
    return pltpu.CompilerParams(dimension_semantics=("arbitrary",) * n_axes,
                                vmem_limit_bytes=V7X_VMEM_LIMIT)


def _sigmoid(x):
    return 0.5 * jnp.tanh(0.5 * x) + 0.5


def _pick_tile(n, candidates):
    for c in candidates:
        if n % c == 0:
            return c
    return n


def _dense_body(*refs, prologue, has_res, has_extra, has_final, row_chunk, w_transposed):
    it = iter(refs)
    x_ref = next(it)
    g_ref = next(it) if prologue == "norm" else None
    w_ref = next(it)
    we_ref = next(it) if has_extra else None
    r_ref = next(it) if has_res else None
    fg_ref = next(it) if has_final else None
    o_ref = next(it)
    oe_ref = next(it) if has_extra else None
    h_ref = next(it) if prologue != "plain" else None
    j = pl.program_id(1)
    nt = (((1,), (1,)), ((), ()))

    if prologue != "plain":
        @pl.when(j == 0)
        def _():
            def body(c, carry):
                rows = pl.ds(pl.multiple_of(c * row_chunk, row_chunk), row_chunk)
                xf = x_ref[rows, :].astype(F32)
                if prologue == "norm":
                    ms = jnp.mean(xf * xf, axis=-1, keepdims=True)
                    xf = xf * lax.rsqrt(ms + EPS) * g_ref[...]
                h_ref[rows, :] = xf.astype(BF16)
                return carry
            lax.fori_loop(0, x_ref.shape[0] // row_chunk, body, 0)
            if has_extra:
                oe_ref[...] = lax.dot_general(h_ref[...], we_ref[...], nt, preferred_element_type=F32)
        lhs = h_ref[...]
    else:
        lhs = x_ref[...]
    w = w_ref[...].astype(BF16)
    if w_transposed:
        acc = lax.dot_general(lhs, w, nt, preferred_element_type=F32)
    else:
        acc = jnp.dot(lhs, w, preferred_element_type=F32)
    if has_res:
        acc = acc + r_ref[...]
    if not has_final:
        o_ref[...] = acc.astype(o_ref.dtype)
    else:
        tn = acc.shape[1]
        o_ref[:, pl.ds(pl.multiple_of(j * tn, tn), tn)] = acc

        @pl.when(j == pl.num_programs(1) - 1)
        def _():
            def body(c, carry):
                rows = pl.ds(pl.multiple_of(c * row_chunk, row_chunk), row_chunk)
                xf = o_ref[rows, :]
                ms = jnp.mean(xf * xf, axis=-1, keepdims=True)
                o_ref[rows, :] = xf * lax.rsqrt(ms + EPS) * fg_ref[...]
                return carry
            lax.fori_loop(0, o_ref.shape[0] // row_chunk, body, 0)


def _dense(x, w, *, gain=None, res=None, prologue, out_dtype, name, layer=0, col_off=0, n_cols=None,
           w_transposed=False, extra_cols=None, final_gain=None, tm=None):
    m, k = x.shape
    n = n_cols if n_cols is not None else w.shape[-1]
    if tm is None:
        tm = _pick_tile(m, (1024, 512, 256, 128, 64, 32, 16))
    tn = _pick_tile(n, (1024, 512, 256, 128) if k <= 2048 else (512, 256, 128))
    row_chunk = min(tm, 128)
    in_specs = [pl.BlockSpec((tm, k), lambda i, j: (i, 0))]
    args = [x]
    if prologue == "norm":
        in_specs.append(pl.BlockSpec((1, k), lambda i, j: (0, 0)))
        args.append(gain)
    if w_transposed:
        assert col_off % BF16_ROWS == 0 and tn % BF16_ROWS == 0
        in_specs.append(pl.BlockSpec((pl.Element(tn), pl.Element(k)),
                                     lambda i, j: (pl.multiple_of(col_off + j * tn, BF16_ROWS), 0)))
    elif w.ndim == 3:
        assert col_off % tn == 0
        in_specs.append(pl.BlockSpec((None, k, tn), lambda i, j: (layer, 0, j + col_off // tn)))
    else:
        in_specs.append(pl.BlockSpec((k, tn), lambda i, j: (0, j)))
    args.append(w)
    out_specs = [pl.BlockSpec((tm, tn), lambda i, j: (i, j))]
    out_shape = [jax.ShapeDtypeStruct((m, n), out_dtype)]
    if extra_cols is not None:
        e_off, e_n = extra_cols
        assert w_transposed and prologue != "plain" and e_off % BF16_ROWS == 0 and e_n % BF16_ROWS == 0
        in_specs.append(pl.BlockSpec((pl.Element(e_n), pl.Element(k)), lambda i, j: (e_off, 0)))
        args.append(w)
        out_specs.append(pl.BlockSpec((tm, e_n), lambda i, j: (i, 0)))
        out_shape.append(jax.ShapeDtypeStruct((m, e_n), F32))
    if res is not None:
        in_specs.append(pl.BlockSpec((tm, tn), lambda i, j: (i, j)))
        args.append(res)
    if final_gain is not None:
        assert out_dtype == F32
        in_specs.append(pl.BlockSpec((1, n), lambda i, j: (0, 0)))
        args.append(final_gain)
        out_specs[0] = pl.BlockSpec((tm, n), lambda i, j: (i, 0))
    scratch = [] if prologue == "plain" else [pltpu.VMEM((tm, k), BF16)]
    outs = pl.pallas_call(
        functools.partial(_dense_body, prologue=prologue, has_res=res is not None,
                          has_extra=extra_cols is not None, has_final=final_gain is not None,
                          row_chunk=row_chunk, w_transposed=w_transposed),
        grid=(m // tm, n // tn),
        in_specs=in_specs,
        out_specs=out_specs,
        out_shape=out_shape,
        scratch_shapes=scratch,
        compiler_params=_cparams(2),
        name=name,
    )(*args)
    return outs if extra_cols is not None else outs[0]


def _cumsum_rows(x, group):
    rows = lax.broadcasted_iota(jnp.int32, x.shape, 0) % group
    shift = 1
    while shift < group:
        rolled = pltpu.roll(x, shift, axis=0)
        x = x + jnp.where(rows >= shift, rolled, 0.0)
        shift *= 2
    return x


def _lockstep(gens):
    gens = list(gens)
    results = [None] * len(gens)
    live = list(range(len(gens)))
    while live:
        still = []
        for n in live:
            try:
                next(gens[n])
                still.append(n)
            except StopIteration as stop:
                results[n] = stop.value
        live = still
    return results


def _recurrence_block(q, k, v, la, s_read, *, sub, mm_dtype):
    bt, kd = q.shape
    vd = v.shape[1]
    ns = bt // sub
    b_loc = _cumsum_rows(la, sub)
    qd = q * jnp.exp(b_loc)
    ki = k * jnp.exp(-b_loc)
    tot = [b_loc[(i + 1) * sub - 1:(i + 1) * sub, :] for i in range(ns)]
    pre = [jnp.zeros((1, kd), F32)]
    for i in range(ns):
        pre.append(pre[-1] + tot[i])
    sl = [slice(i * sub, (i + 1) * sub) for i in range(ns)]
    vb = v.astype(mm_dtype)
    qdm = qd.astype(mm_dtype)
    q_state = jnp.concatenate([qd[sl[i]] * jnp.exp(pre[i]) for i in range(ns)], axis=0) if ns > 1 else qd
    q_state = q_state.astype(mm_dtype)
    ke = [ki[sl[i]] * jnp.exp(tot[i]) for i in range(ns)]
    kmats = []
    for i in range(ns):
        parts = [ke[j] if j == i - 1 else ke[j] * jnp.exp(pre[i] - pre[j + 1]) for j in range(i)]
        parts.append(ki[sl[i]])
        parts += [jnp.zeros((sub, kd), F32)] * (ns - 1 - i)
        kmats.append((jnp.concatenate(parts, axis=0) if ns > 1 else parts[0]).astype(mm_dtype))
    k_end = jnp.concatenate([ke[i] if i == ns - 1 else ke[i] * jnp.exp(pre[ns] - pre[i + 1])
                             for i in range(ns)], axis=0) if ns > 1 else ke[0]
    k_end = k_end.astype(mm_dtype)
    decay = jnp.broadcast_to(jnp.exp(pre[ns]), (LANE, kd)).T
    yield

    s_prev = s_read()
    o_state = jnp.dot(q_state, s_prev.astype(mm_dtype), preferred_element_type=F32)
    a_rows = [lax.dot_general(qdm[sl[i]], kmats[i], (((1,), (1,)), ((), ())), preferred_element_type=F32)
              for i in range(ns)]
    ds = lax.dot_general(k_end, vb, (((0,), (0,)), ((), ())), preferred_element_type=F32)
    yield

    a = jnp.concatenate(a_rows, axis=0) if ns > 1 else a_rows[0]
    row = lax.broadcasted_iota(jnp.int32, a.shape, 0)
    col = lax.broadcasted_iota(jnp.int32, a.shape, 1)
    a = jnp.where(col <= row, a, 0.0).astype(mm_dtype)
    o = jnp.dot(a, vb, preferred_element_type=F32) + o_state
    s_new = jnp.concatenate([decay * s_prev[:, c * LANE:(c + 1) * LANE] for c in range(vd // LANE)],
                            axis=1) + ds
    yield
    return o, s_new


def _head_gated_norm(o, gnorm, gate):
    ms = jnp.mean(o * o, axis=-1, keepdims=True)
    return o * lax.rsqrt(ms + EPS) * gnorm * (gate * _sigmoid(gate))


def _log_sigmoid(x):
    return jnp.minimum(x, 0.0) - jnp.log(1.0 + jnp.exp(-jnp.abs(x)))


def _run_blocks(head_fn, heads, width, out_ref, so_ref, si_ref, *, mode, block, n_rows):
    if mode == "prompt":
        @pl.when(pl.program_id(1) == 0)
        def _():
            so_ref[...] = jnp.zeros(so_ref.shape, F32)

        def body(c, carry):
            rows = pl.ds(pl.multiple_of(c * block, block), block)
            load = lambda ref, cols: ref[rows, cols].astype(F32)
            res = _lockstep(head_fn(h, load, lambda h=h: so_ref[0, h]) for h in range(heads))
            for h, (out, s_new) in enumerate(res):
                so_ref[0, h] = s_new
                out_ref[rows, h * width:(h + 1) * width] = out.astype(out_ref.dtype)
            return carry
        lax.fori_loop(0, n_rows // block, body, 0)
    else:
        n_seq = n_rows // block
        loads = [lambda ref, cols, s=s: ref[:, cols].astype(F32)[s * block:(s + 1) * block] for s in range(n_seq)]
        res = _lockstep(head_fn(h, loads[s], lambda s=s, h=h: si_ref[s, h])
                        for h in range(heads) for s in range(n_seq))
        for h in range(heads):
            for s in range(n_seq):
                so_ref[s, h] = res[h * n_seq + s][1]
            out_ref[:, h * width:(h + 1) * width] = jnp.concatenate(
                [res[h * n_seq + s][0] for s in range(n_seq)], axis=0).astype(out_ref.dtype)


def _gla_body(*refs, mode, heads, dk, dv, block, sub, mm_dtype):
    if mode == "prompt":
        q_ref, k_ref, v_ref, g_ref, ga_ref, wup_ref, ba_ref, gn_ref, og_ref, so_ref = refs
        si_ref = None
    else:
        q_ref, k_ref, v_ref, g_ref, ga_ref, wup_ref, ba_ref, gn_ref, si_ref, og_ref, so_ref = refs

    def head_fn(h, load, s_read):
        ks = slice(h * dk, (h + 1) * dk)
        vs = slice(h * dv, (h + 1) * dv)
        ga = load(ga_ref, slice(None)).astype(BF16)
        a_logit = jnp.dot(ga, wup_ref[:, ks], preferred_element_type=F32) + ba_ref[:, ks]
        yield
        la = _log_sigmoid(a_logit) * (1.0 / GLA_GATE_NORM)
        q = load(q_ref, ks) * (dk ** -0.5)
        o, s_new = yield from _recurrence_block(q, load(k_ref, ks), load(v_ref, vs), la, s_read,
                                                sub=sub, mm_dtype=mm_dtype)
        return _head_gated_norm(o, gn_ref[:, vs], load(g_ref, vs)), s_new

    _run_blocks(head_fn, heads, dv, og_ref, so_ref, si_ref, mode=mode, block=block, n_rows=q_ref.shape[0])


def _gla(z, zga, wup, ba, gnorm, state, *, mode, layer, n_seq, seq_len, d_model, name):
    heads = GLA_HEADS
    dk = d_model // 2 // heads
    dv = d_model // heads
    t = z.shape[0]
    if mode == "prompt":
        step = PROMPT_STEP
        n_t = seq_len // step
        grid = (n_seq, n_t)
        block, sub, mm_dtype = PROMPT_BLOCK, REF_CHUNK, BF16
        st_out = pl.BlockSpec((1, heads, dk, dv), lambda b, i: (b, 0, 0, 0))
        const = lambda b, i: (0, 0)
        zspec = lambda width, col: pl.BlockSpec((step, width), lambda b, i: (b * n_t + i, col))
    else:
        step = SAMPLE_SEQS * seq_len
        grid = (n_seq // SAMPLE_SEQS,)
        block = seq_len
        sub, mm_dtype = math.gcd(REF_CHUNK, seq_len), F32
        st_out = pl.BlockSpec((SAMPLE_SEQS, heads, dk, dv), lambda i: (i, 0, 0, 0))
        const = lambda i: (0, 0)
        zspec = lambda width, col: pl.BlockSpec((step, width), lambda i: (i, col))
    kw = heads * dk
    in_specs = [zspec(kw, 0), zspec(kw, 1), zspec(d_model, 1), zspec(d_model, 2),
                zspec(zga.shape[1], 0),
                pl.BlockSpec(wup.shape, const), pl.BlockSpec(ba.shape, const), pl.BlockSpec(gnorm.shape, const)]
    args = [z, z, z, z, zga, wup, ba, gnorm]
    if mode != "prompt":
        in_specs.append(pl.BlockSpec((None, SAMPLE_SEQS, heads, dk, dv), lambda i: (layer, i, 0, 0, 0)))
        args.append(state)
    og, s_out = pl.pallas_call(
        functools.partial(_gla_body, mode=mode, heads=heads, dk=dk, dv=dv, block=block, sub=sub,
                          mm_dtype=mm_dtype),
        grid=grid,
        in_specs=in_specs,
        out_specs=[zspec(d_model, 0), st_out],
        out_shape=[jax.ShapeDtypeStruct((t, d_model), BF16),
                   jax.ShapeDtypeStruct((n_seq, heads, dk, dv), F32)],
        compiler_params=_cparams(len(grid)),
        name=name,
    )(*args)
    return og, s_out


def _hgrn_body(*refs, mode, layer, heads, dk, block, sub, mm_dtype):
    if mode == "prompt":
        q_ref, f_ref, i_ref, g_ref, ma_ref, mb_ref, ogla_ref, lb_ref, gn_ref, om_ref, so_ref = refs
        si_ref = None
    else:
        q_ref, f_ref, i_ref, g_ref, ma_ref, mb_ref, ogla_ref, lb_ref, gn_ref, si_ref, om_ref, so_ref = refs

    p = lb_ref[...]
    e = jnp.exp(p - jnp.max(p, axis=0, keepdims=True))
    lb_all = jnp.sum(e[:layer + 1], axis=0, keepdims=True) / jnp.sum(e, axis=0, keepdims=True)

    def head_fn(h, load, s_read):
        ks = slice(h * dk, (h + 1) * dk)
        lb = lb_all[:, ks]
        f = lb + (1.0 - lb) * _sigmoid(load(f_ref, ks))
        hq = load(q_ref, ks)
        q = hq * _sigmoid(hq) * (dk ** -0.5)
        o, s_new = yield from _recurrence_block(q, 1.0 - f, load(i_ref, ks), jnp.log(f), s_read,
                                                sub=sub, mm_dtype=mm_dtype)
        o_h = _head_gated_norm(o, gn_ref[:, ks], load(g_ref, ks))
        mix = _sigmoid(load(ma_ref, ks)) * load(ogla_ref, ks) + _sigmoid(load(mb_ref, ks)) * o_h
        return mix, s_new

    _run_blocks(head_fn, heads, dk, om_ref, so_ref, si_ref, mode=mode, block=block, n_rows=q_ref.shape[0])


def _hgrn(z, og_gla, lb_param, gnorm, state, *, mode, layer, n_seq, seq_len, d_model, name):
    dk = HGRN_DK
    heads = d_model // dk
    t = z.shape[0]
    if mode == "prompt":
        step = PROMPT_STEP
        n_t = seq_len // step
        grid = (n_seq, n_t)
        block, sub, mm_dtype = PROMPT_BLOCK, REF_CHUNK, BF16
        st_out = pl.BlockSpec((1, heads, dk, dk), lambda b, i: (b, 0, 0, 0))
        const = lambda b, i: (0, 0)
        zspec = lambda col: pl.BlockSpec((step, d_model), lambda b, i: (b * n_t + i, col))
    else:
        step = SAMPLE_SEQS * seq_len
        grid = (n_seq // SAMPLE_SEQS,)
        block = seq_len
        sub, mm_dtype = math.gcd(REF_CHUNK, seq_len), F32
        st_out = pl.BlockSpec((SAMPLE_SEQS, heads, dk, dk), lambda i: (i, 0, 0, 0))
        const = lambda i: (0, 0)
        zspec = lambda col: pl.BlockSpec((step, d_model), lambda i: (i, col))
    in_specs = [zspec(0), zspec(1), zspec(2), zspec(3), zspec(4), zspec(5), zspec(0),
                pl.BlockSpec(lb_param.shape, const), pl.BlockSpec(gnorm.shape, const)]
    args = [z, z, z, z, z, z, og_gla, lb_param, gnorm]
    if mode != "prompt":
        in_specs.append(pl.BlockSpec((None, SAMPLE_SEQS, heads, dk, dk), lambda i: (layer, i, 0, 0, 0)))
        args.append(state)
    o_mix, s_out = pl.pallas_call(
        functools.partial(_hgrn_body, mode=mode, layer=layer, heads=heads, dk=dk, block=block, sub=sub,
                          mm_dtype=mm_dtype),
        grid=grid,
        in_specs=in_specs,
        out_specs=[zspec(0), st_out],
        out_shape=[jax.ShapeDtypeStruct((t, d_model), BF16),
                   jax.ShapeDtypeStruct((n_seq, heads, dk, dk), F32)],
        compiler_params=_cparams(len(grid)),
        name=name,
    )(*args)
    return o_mix, s_out


def _xattn_prompt_body(q_ref, k_ref, v_ref, o_ref, *, heads, dh):
    scale = dh ** -0.5
    for h in range(heads):
        hs = slice(h * dh, (h + 1) * dh)
        k = k_ref[0, :, hs].astype(BF16)
        v = v_ref[0, :, hs].astype(BF16)
        sc = lax.dot_general(q_ref[:, hs], k, (((1,), (1,)), ((), ())), preferred_element_type=F32) * scale
        p = jnp.exp(sc - jnp.max(sc, axis=-1, keepdims=True))
        o = jnp.dot(p.astype(BF16), v, preferred_element_type=F32) / jnp.sum(p, axis=-1, keepdims=True)
        o_ref[:, hs] = o.astype(o_ref.dtype)


def _xattn_sample_body(q_ref, k_ref, v_ref, o_ref, *, heads, dh, seq_len):
    scale = dh ** -0.5
    n_mem = k_ref.shape[1]
    for s in range(k_ref.shape[0]):
        rows = slice(s * seq_len, (s + 1) * seq_len)
        q = q_ref[rows, :]
        q2 = jnp.concatenate([q[:, h * dh:(h + 1) * dh] for h in range(heads)], axis=0)
        k2 = k_ref[s].reshape(n_mem * heads, dh)
        v2 = v_ref[s].reshape(n_mem * heads, dh)
        sc = lax.dot_general(q2, k2, (((1,), (1,)), ((), ())), preferred_element_type=F32) * scale
        q_head = lax.broadcasted_iota(jnp.int32, sc.shape, 0) // seq_len
        k_head = lax.broadcasted_iota(jnp.int32, sc.shape, 1) % heads
        sc = jnp.where(q_head == k_head, sc, -jnp.inf)
        p = jnp.exp(sc - jnp.max(sc, axis=-1, keepdims=True))
        o2 = jnp.dot(p, v2, preferred_element_type=F32) / jnp.sum(p, axis=-1, keepdims=True)
        for h in range(heads):
            o_ref[rows, h * dh:(h + 1) * dh] = o2[h * seq_len:(h + 1) * seq_len].astype(o_ref.dtype)


def _xattn(mq, mem_k, mem_v, *, mode, layer, n_seq, seq_len, name):
    t, d = mq.shape
    dh = d // MEM_HEADS
    if mode == "prompt":
        n_mem = mem_k.shape[1]
        tq = _pick_tile(seq_len, (512, 256, 128))
        n_t = seq_len // tq
        grid = (n_seq, n_t)
        q_spec = pl.BlockSpec((tq, d), lambda b, i: (b * n_t + i, 0))
        kv_spec = pl.BlockSpec((1, n_mem, d), lambda b, i: (b, 0, 0))
        body = functools.partial(_xattn_prompt_body, heads=MEM_HEADS, dh=dh)
        out_dtype = BF16
    else:
        n_mem = mem_k.shape[2]
        grid = (n_seq // SAMPLE_SEQS,)
        q_spec = pl.BlockSpec((SAMPLE_SEQS * seq_len, d), lambda i: (i, 0))
        kv_spec = pl.BlockSpec((None, SAMPLE_SEQS, n_mem, MEM_HEADS, dh), lambda i: (layer, i, 0, 0, 0))
        body = functools.partial(_xattn_sample_body, heads=MEM_HEADS, dh=dh, seq_len=seq_len)
        out_dtype = F32
    return pl.pallas_call(
        body,
        grid=grid,
        in_specs=[q_spec, kv_spec, kv_spec],
        out_specs=q_spec,
        out_shape=jax.ShapeDtypeStruct((t, d), out_dtype),
        compiler_params=_cparams(len(grid)),
        name=name,
    )(mq, mem_k, mem_v)


def _ffn_up_body(*refs, mode, seq_len, tiles_per_seq, row_chunk):
    if mode == "prompt":
        x_ref, g_ref, wg_ref, wv_ref, cw_ref, cb_ref, act_ref, st_ref, h_ref, tail_ref = refs
        p1_ref = p2_ref = None
    else:
        x_ref, g_ref, wg_ref, wv_ref, cw_ref, cb_ref, p1_ref, p2_ref, act_ref, st_ref, h_ref = refs
        tail_ref = None
    i = pl.program_id(0)
    j = pl.program_id(1)

    @pl.when(j == 0)
    def _():
        def body(c, carry):
            rows = pl.ds(pl.multiple_of(c * row_chunk, row_chunk), row_chunk)
            xf = x_ref[rows, :]
            ms = jnp.mean(xf * xf, axis=-1, keepdims=True)
            h_ref[rows, :] = (xf * lax.rsqrt(ms + EPS) * g_ref[...]).astype(BF16)
            return carry
        lax.fori_loop(0, x_ref.shape[0] // row_chunk, body, 0)

    if mode == "prompt":
        @pl.when(i % tiles_per_seq == 0)
        def _():
            tail_ref[j] = jnp.zeros(tail_ref.shape[1:], F32)

    h = h_ref[...]
    ug = jnp.dot(h, wg_ref[...], preferred_element_type=F32)
    tm = ug.shape[0]
    row = lax.broadcasted_iota(jnp.int32, ug.shape, 0)
    roll1 = pltpu.roll(ug, 1, axis=0)
    roll2 = pltpu.roll(ug, 2, axis=0)
    if mode == "prompt":
        tail = tail_ref[j]
        prev1 = tail[7:8, :]
        prev2 = tail[6:7, :]
        sh1 = jnp.where(row >= 1, roll1, prev1)
        sh2 = jnp.where(row >= 2, roll2, jnp.where(row == 0, prev2, prev1))
        tail_ref[j] = ug[tm - 8:, :]
        st_ref[...] = ug[tm - 8:, :]
    else:
        pos = row % seq_len
        sh1 = jnp.where(pos >= 1, roll1, p1_ref[...])
        sh2 = jnp.where(pos >= 2, roll2, p2_ref[...])
        st_ref[...] = ug
    conv = cw_ref[0:1, :] * sh2 + cw_ref[1:2, :] * sh1 + cw_ref[2:3, :] * ug + cb_ref[...]
    gate = conv * _sigmoid(conv)
    uv = jnp.dot(h, wv_ref[...], preferred_element_type=F32)
    act_ref[...] = (gate * uv).astype(act_ref.dtype)


def _ffn_up(x, gain, wg, wv, cw, cb, carry, *, mode, seq_len, name):
    m, k = x.shape
    nf = wg.shape[1]
    tm = _pick_tile(m, (1024,))
    tn = _pick_tile(nf, (512, 256, 128))
    xs = pl.BlockSpec((tm, k), lambda i, j: (i, 0))
    gs = pl.BlockSpec((1, k), lambda i, j: (0, 0))
    ws = pl.BlockSpec((k, tn), lambda i, j: (0, j))
    cws = pl.BlockSpec((CONV_W, tn), lambda i, j: (0, j))
    cbs = pl.BlockSpec((1, tn), lambda i, j: (0, j))
    ts = pl.BlockSpec((tm, tn), lambda i, j: (i, j))
    in_specs = [xs, gs, ws, ws, cws, cbs]
    args = [x, gain, wg, wv, cw, cb]
    scratch = [pltpu.VMEM((tm, k), BF16)]
    if mode == "prompt":
        assert seq_len % tm == 0
        scratch.append(pltpu.VMEM((nf // tn, 8, tn), F32))
        st_spec = pl.BlockSpec((None, 8, tn), lambda i, j: (i, 0, j))
        st_shape = jax.ShapeDtypeStruct((m // tm, 8, nf), F32)
    else:
        assert tm % seq_len == 0
        in_specs += [ts, ts]
        args += list(carry)
        st_spec = ts
        st_shape = jax.ShapeDtypeStruct((m, nf), F32)
    return pl.pallas_call(
        functools.partial(_ffn_up_body, mode=mode, seq_len=seq_len, tiles_per_seq=max(seq_len // tm, 1),
                          row_chunk=128),
        grid=(m // tm, nf // tn),
        in_specs=in_specs,
        out_specs=[ts, st_spec],
        out_shape=[jax.ShapeDtypeStruct((m, nf), BF16), st_shape],
        scratch_shapes=scratch,
        compiler_params=_cparams(2),
        name=name,
    )(*args)


def _pad_cols(w, n):
    return jnp.pad(w, ((0, 0), (0, n - w.shape[1])))


def _layer(x, mem_k, mem_v, s_gla, s_hgrn, conv_buf, p, *, mode, layer, n_seq, seq_len, tag):
    t, d = x.shape
    nm = lambda s: f"{s}_{tag}"
    ga_off, rank = p["ga_off"], p["rank"]
    w_in_t = p["w_in_t"]
    in_proj = functools.partial(_dense, x, w_in_t, gain=p["norm_mix_g"], prologue="norm", w_transposed=True)
    za, zga = in_proj(out_dtype=BF16, name=nm("in_proj_a"), col_off=0, n_cols=ga_off, extra_cols=(ga_off, rank))
    zb = in_proj(out_dtype=BF16, name=nm("in_proj_b"), col_off=ga_off + rank,
                 n_cols=w_in_t.shape[0] - ga_off - rank)
    og_gla, s_gla_new = _gla(za, zga, p["gla_w_a_up"], p["gla_b_a"], p["gla_norm_g"], s_gla,
                             mode=mode, layer=layer, n_seq=n_seq, seq_len=seq_len, d_model=d, name=nm("gla"))
    o_mix, s_hgrn_new = _hgrn(zb, og_gla, p["hgrn_lower_bound"], p["hgrn_norm_g"], s_hgrn,
                              mode=mode, layer=layer, n_seq=n_seq, seq_len=seq_len, d_model=d, name=nm("hgrn"))
    x1 = _dense(o_mix, p["w_out"], res=x, prologue="plain", out_dtype=F32, name=nm("out_proj"), layer=layer)
    mq = _dense(x1, p["w_mem_q"], gain=p["norm_xattn_g"], prologue="norm",
                out_dtype=BF16 if mode == "prompt" else F32, name=nm("mem_q"), layer=layer)
    mo = _xattn(mq, mem_k, mem_v, mode=mode, layer=layer, n_seq=n_seq, seq_len=seq_len, name=nm("xattn"))
    x2 = _dense(mo, p["w_mem_o"], res=x1, prologue="plain" if mode == "prompt" else "cast", out_dtype=F32,
                name=nm("mem_o"), layer=layer)

    nf = p["w_up_gate"].shape[1]
    d_ff = p["d_ff"]
    if mode == "prompt":
        carry = None
    else:
        cb = jnp.pad(conv_buf[layer], ((0, 0), (0, 0), (0, nf - d_ff)))
        zeros = lambda r: jnp.zeros((n_seq, r, nf), F32)
        p1 = jnp.concatenate([cb[:, 1:2], zeros(seq_len - 1)], axis=1).reshape(t, nf)
        p2 = jnp.concatenate([cb, zeros(seq_len - 2)], axis=1).reshape(t, nf)
        carry = (p1, p2)
    act, ug_rows = _ffn_up(x2, p["norm_ffn_g"], p["w_up_gate"], p["w_up_val"], p["ffn_conv_w"], p["ffn_conv_b"],
                           carry, mode=mode, seq_len=seq_len, name=nm("ffn_up"))
    final = p["norm_final_g"] if p["is_last"] else None
    y = _dense(act, p["w_down"], res=x2, prologue="plain", out_dtype=F32, name=nm("ffn_down"),
               final_gain=final, tm=512 if final is not None else None)

    if mode == "prompt":
        tiles_per_seq = ug_rows.shape[0] // n_seq
        conv_new = ug_rows[tiles_per_seq - 1::tiles_per_seq, 8 - (CONV_W - 1):, :d_ff]
    else:
        conv_new = ug_rows.reshape(n_seq, seq_len, nf)[:, seq_len - (CONV_W - 1):, :d_ff]
    return y, s_gla_new, s_hgrn_new, conv_new


def kernel(x_prompt, x_sample, mem_prompt, state_gla, state_hgrn, state_ffn_conv, cache_mem_k, cache_mem_v,
           norm_mix_g, w_in, gla_w_a_up, gla_b_a, gla_norm_g, hgrn_lower_bound, hgrn_norm_g, w_out,
           norm_xattn_g, norm_mem_g, w_mem_q, w_mem_k, w_mem_v, w_mem_o,
           norm_ffn_g, w_ffn_up, ffn_conv_w, ffn_conv_b, w_ffn_down, norm_final_g):
    depth = w_in.shape[0]
    bp, lp, d = x_prompt.shape
    bs, ls, _ = x_sample.shape
    n_mem = mem_prompt.shape[1]
    d_ff = w_ffn_down.shape[1]
    rank = gla_w_a_up.shape[1]
    qk = gla_w_a_up.shape[2]
    ga_off = 2 * qk + 2 * d
    nf = -(-d_ff // 512) * 512

    xp = x_prompt.reshape(bp * lp, d)
    xs = x_sample.reshape(bs * ls, d)
    mem = mem_prompt.reshape(bp * n_mem, d)
    row = lambda v: v.reshape(1, -1)

    outs = {k: [] for k in ("gla_p", "hgrn_p", "conv_p", "mk_p", "mv_p", "gla_s", "hgrn_s", "conv_s")}
    for l in range(depth):
        p = {
            "w_out": w_out, "w_mem_q": w_mem_q, "w_mem_o": w_mem_o,
            "w_in_t": jnp.swapaxes(w_in[l], 0, 1).astype(BF16),
            "ga_off": ga_off, "rank": rank,
            "gla_w_a_up": gla_w_a_up[l].astype(BF16),
            "gla_b_a": row(gla_b_a[l]), "gla_norm_g": row(gla_norm_g[l]),
            "hgrn_lower_bound": hgrn_lower_bound, "hgrn_norm_g": row(hgrn_norm_g[l]),
            "norm_mix_g": row(norm_mix_g[l]), "norm_xattn_g": row(norm_xattn_g[l]),
            "norm_ffn_g": row(norm_ffn_g[l]), "norm_final_g": row(norm_final_g), "is_last": l == depth - 1,
            "w_up_gate": _pad_cols(w_ffn_up[l][:, :d_ff], nf).astype(BF16),
            "w_up_val": _pad_cols(w_ffn_up[l][:, d_ff:], nf).astype(BF16),
            "ffn_conv_w": _pad_cols(ffn_conv_w[l], nf), "ffn_conv_b": _pad_cols(row(ffn_conv_b[l]), nf),
            "w_down": jnp.pad(w_ffn_down[l], ((0, nf - d_ff), (0, 0))).astype(BF16),
            "d_ff": d_ff,
        }
        g_mem = row(norm_mem_g[l])
        mk = _dense(mem, w_mem_k, gain=g_mem, prologue="norm", out_dtype=F32, name=f"mem_k_{l}", layer=l)
        mv = _dense(mem, w_mem_v, gain=g_mem, prologue="norm", out_dtype=F32, name=f"mem_v_{l}", layer=l)
        mk3, mv3 = mk.reshape(bp, n_mem, d), mv.reshape(bp, n_mem, d)
        xp, sg, sh, cb = _layer(xp, mk3, mv3, None, None, None, p, mode="prompt", layer=l,
                                n_seq=bp, seq_len=lp, tag=f"p{l}")
        outs["gla_p"].append(sg); outs["hgrn_p"].append(sh); outs["conv_p"].append(cb)
        outs["mk_p"].append(mk3.reshape(bp, n_mem, MEM_HEADS, d // MEM_HEADS))
        outs["mv_p"].append(mv3.reshape(bp, n_mem, MEM_HEADS, d // MEM_HEADS))
        xs, sg2, sh2, cb2 = _layer(xs, cache_mem_k, cache_mem_v, state_gla, state_hgrn, state_ffn_conv, p,
                                   mode="sample", layer=l, n_seq=bs, seq_len=ls, tag=f"s{l}")
        outs["gla_s"].append(sg2); outs["hgrn_s"].append(sh2); outs["conv_s"].append(cb2)

    y_prompt = xp.reshape(bp, lp, d)
    y_sample = xs.reshape(bs, ls, d)
    st = lambda k: jnp.stack(outs[k])
    return (y_prompt, y_sample, st("gla_p"), st("hgrn_p"), st("conv_p"), st("mk_p"), st("mv_p"),
            st("gla_s"), st("hgrn_s"), st("conv_s"))
```

```python
import functools
import math

import jax
import jax.numpy as jnp
from jax import lax
from jax.experimental import pallas as pl
from jax.experimental.pallas import tpu as pltpu

F32 = jnp.float32
BF16 = jnp.bfloat16

LANE = 128
BF16_ROWS = 16
V7X_VMEM_LIMIT = 56 * 1024 * 1024

EPS = 1e-6
GLA_HEADS = 4
GLA_GATE_NORM = 16.0
HGRN_DK = 128
MEM_HEADS = 4
CONV_W = 3
REF_CHUNK = 16
PROMPT_BLOCK = 64
PROMPT_STEP = 256
SAMPLE_SEQS = 2


def _cparams(n_axes):
    return pltpu.CompilerParams(dimension_semantics=("arbitrary",) * n_axes,
                                vmem_limit_bytes=V7X_VMEM_LIMIT)


def _sigmoid(x):
    return 0.5 * jnp.tanh(0.5 * x) + 0.5


def _pick_tile(n, candidates):
    for c in candidates:
        if n % c == 0:
            return c
    return n


def _dense_body(*refs, prologue, has_res, has_extra, has_final, row_chunk, w_transposed):
    it = iter(refs)
    x_ref = next(it)
    g_ref = next(it) if prologue == "norm" else None
    w_ref = next(it)
    we_ref = next(it) if has_extra else None
    r_ref = next(it) if has_res else None
    fg_ref = next(it) if has_final else None
    o_ref = next(it)
    oe_ref = next(it) if has_extra else None
    h_ref = next(it) if prologue != "plain" else None
    j = pl.program_id(1)
    nt = (((1,), (1,)), ((), ()))

    if prologue != "plain":
        @pl.when(j == 0)
        def _():
            def body(c, carry):
                rows = pl.ds(pl.multiple_of(c * row_chunk, row_chunk), row_chunk)
                xf = x_ref[rows, :].astype(F32)
                if prologue == "norm":
                    ms = jnp.mean(xf * xf, axis=-1, keepdims=True)
                    xf = xf * lax.rsqrt(ms + EPS) * g_ref[...]
                h_ref[rows, :] = xf.astype(BF16)
                return carry
            lax.fori_loop(0, x_ref.shape[0] // row_chunk, body, 0)
            if has_extra:
                oe_ref[...] = lax.dot_general(h_ref[...], we_ref[...], nt, preferred_element_type=F32)
        lhs = h_ref[...]
    else:
        lhs = x_ref[...]
    w = w_ref[...].astype(BF16)
    if w_transposed:
        acc = lax.dot_general(lhs, w, nt, preferred_element_type=F32)
    else:
        acc = jnp.dot(lhs, w, preferred_element_type=F32)
    if has_res:
        acc = acc + r_ref[...]
    if not has_final:
        o_ref[...] = acc.astype(o_ref.dtype)
    else:
        tn = acc.shape[1]
        o_ref[:, pl.ds(pl.multiple_of(j * tn, tn), tn)] = acc

        @pl.when(j == pl.num_programs(1) - 1)
        def _():
            def body(c, carry):
                rows = pl.ds(pl.multiple_of(c * row_chunk, row_chunk), row_chunk)
                xf = o_ref[rows, :]
                ms = jnp.mean(xf * xf, axis=-1, keepdims=True)
                o_ref[rows, :] = xf * lax.rsqrt(ms + EPS) * fg_ref[...]
                return carry
            lax.fori_loop(0, o_ref.shape[0] // row_chunk, body, 0)


def _dense(x, w, *, gain=None, res=None, prologue, out_dtype, name, layer=0, col_off=0, n_cols=None,
           w_transposed=False, extra_cols=None, final_gain=None, tm=None, tn=None):
    m, k = x.shape
    n = n_cols if n_cols is not None else w.shape[-1]
    if tm is None:
        tm = _pick_tile(m, (1024, 512, 256, 128, 64, 32, 16))
    if tn is None:
        tn = _pick_tile(n, (1024, 512, 256, 128) if k <= 2048 else (512, 256, 128))
    assert m % tm == 0 and n % tn == 0
    row_chunk = min(tm, 128)
    in_specs = [pl.BlockSpec((tm, k), lambda i, j: (i, 0))]
    args = [x]
    if prologue == "norm":
        in_specs.append(pl.BlockSpec((1, k), lambda i, j: (0, 0)))
        args.append(gain)
    if w_transposed:
        assert col_off % BF16_ROWS == 0 and tn % BF16_ROWS == 0
        in_specs.append(pl.BlockSpec((pl.Element(tn), pl.Element(k)),
                                     lambda i, j: (pl.multiple_of(col_off + j * tn, BF16_ROWS), 0)))
    elif w.ndim == 3:
        assert col_off % tn == 0
        in_specs.append(pl.BlockSpec((None, k, tn), lambda i, j: (layer, 0, j + col_off // tn)))
    else:
        in_specs.append(pl.BlockSpec((k, tn), lambda i, j: (0, j)))
    args.append(w)
    out_specs = [pl.BlockSpec((tm, tn), lambda i, j: (i, j))]
    out_shape = [jax.ShapeDtypeStruct((m, n), out_dtype)]
    if extra_cols is not None:
        e_off, e_n = extra_cols
        assert w_transposed and prologue != "plain" and e_off % BF16_ROWS == 0 and e_n % BF16_ROWS == 0
        in_specs.append(pl.BlockSpec((pl.Element(e_n), pl.Element(k)), lambda i, j: (e_off, 0)))
        args.append(w)
        out_specs.append(pl.BlockSpec((tm, e_n), lambda i, j: (i, 0)))
        out_shape.append(jax.ShapeDtypeStruct((m, e_n), F32))
    if res is not None:
        in_specs.append(pl.BlockSpec((tm, tn), lambda i, j: (i, j)))
        args.append(res)
    if final_gain is not None:
        assert out_dtype == F32
        in_specs.append(pl.BlockSpec((1, n), lambda i, j: (0, 0)))
        args.append(final_gain)
        out_specs[0] = pl.BlockSpec((tm, n), lambda i, j: (i, 0))
    scratch = [] if prologue == "plain" else [pltpu.VMEM((tm, k), BF16)]
    outs = pl.pallas_call(
        functools.partial(_dense_body, prologue=prologue, has_res=res is not None,
                          has_extra=extra_cols is not None, has_final=final_gain is not None,
                          row_chunk=row_chunk, w_transposed=w_transposed),
        grid=(m // tm, n // tn),
        in_specs=in_specs,
        out_specs=out_specs,
        out_shape=out_shape,
        scratch_shapes=scratch,
        compiler_params=_cparams(2),
        name=name,
    )(*args)
    return outs if extra_cols is not None else outs[0]


def _cumsum_rows(x, group):
    rows = lax.broadcasted_iota(jnp.int32, x.shape, 0) % group
    shift = 1
    while shift < group:
        rolled = pltpu.roll(x, shift, axis=0)
        x = x + jnp.where(rows >= shift, rolled, 0.0)
        shift *= 2
    return x


def _lockstep(gens):
    gens = list(gens)
    results = [None] * len(gens)
    live = list(range(len(gens)))
    while live:
        still = []
        for n in live:
            try:
                next(gens[n])
                still.append(n)
            except StopIteration as stop:
                results[n] = stop.value
        live = still
    return results


def _recurrence_block(q, k, v, la, s_read, *, sub, mm_dtype):
    bt, kd = q.shape
    vd = v.shape[1]
    ns = bt // sub
    b_loc = _cumsum_rows(la, sub)
    qd = q * jnp.exp(b_loc)
    ki = k * jnp.exp(-b_loc)
    tot = [b_loc[(i + 1) * sub - 1:(i + 1) * sub, :] for i in range(ns)]
    pre = [jnp.zeros((1, kd), F32)]
    for i in range(ns):
        pre.append(pre[-1] + tot[i])
    sl = [slice(i * sub, (i + 1) * sub) for i in range(ns)]
    vb = v.astype(mm_dtype)
    qdm = qd.astype(mm_dtype)
    q_state = jnp.concatenate([qd[sl[i]] * jnp.exp(pre[i]) for i in range(ns)], axis=0) if ns > 1 else qd
    q_state = q_state.astype(mm_dtype)
    ke = [ki[sl[i]] * jnp.exp(tot[i]) for i in range(ns)]
    kmats = []
    for i in range(ns):
        parts = [ke[j] if j == i - 1 else ke[j] * jnp.exp(pre[i] - pre[j + 1]) for j in range(i)]
        parts.append(ki[sl[i]])
        parts += [jnp.zeros((sub, kd), F32)] * (ns - 1 - i)
        kmats.append((jnp.concatenate(parts, axis=0) if ns > 1 else parts[0]).astype(mm_dtype))
    k_end = jnp.concatenate([ke[i] if i == ns - 1 else ke[i] * jnp.exp(pre[ns] - pre[i + 1])
                             for i in range(ns)], axis=0) if ns > 1 else ke[0]
    k_end = k_end.astype(mm_dtype)
    decay = jnp.broadcast_to(jnp.exp(pre[ns]), (LANE, kd)).T
    yield

    s_prev = s_read()
    o_state = jnp.dot(q_state, s_prev.astype(mm_dtype), preferred_element_type=F32)
    a_rows = [lax.dot_general(qdm[sl[i]], kmats[i], (((1,), (1,)), ((), ())), preferred_element_type=F32)
              for i in range(ns)]
    ds = lax.dot_general(k_end, vb, (((0,), (0,)), ((), ())), preferred_element_type=F32)
    yield

    a = jnp.concatenate(a_rows, axis=0) if ns > 1 else a_rows[0]
    row = lax.broadcasted_iota(jnp.int32, a.shape, 0)
    col = lax.broadcasted_iota(jnp.int32, a.shape, 1)
    a = jnp.where(col <= row, a, 0.0).astype(mm_dtype)
    o = jnp.dot(a, vb, preferred_element_type=F32) + o_state
    s_new = jnp.concatenate([decay * s_prev[:, c * LANE:(c + 1) * LANE] for c in range(vd // LANE)],
                            axis=1) + ds
    yield
    return o, s_new


def _head_gated_norm(o, gnorm, gate):
    ms = jnp.mean(o * o, axis=-1, keepdims=True)
    return o * lax.rsqrt(ms + EPS) * gnorm * (gate * _sigmoid(gate))


def _log_sigmoid(x):
    return jnp.minimum(x, 0.0) - jnp.log(1.0 + jnp.exp(-jnp.abs(x)))


def _run_blocks(head_fn, heads, width, out_ref, so_ref, si_ref, *, mode, block, n_rows):
    if mode == "prompt":
        @pl.when(pl.program_id(1) == 0)
        def _():
            so_ref[...] = jnp.zeros(so_ref.shape, F32)

        def body(c, carry):
            rows = pl.ds(pl.multiple_of(c * block, block), block)
            load = lambda ref, cols: ref[rows, cols].astype(F32)
            res = _lockstep(head_fn(h, load, lambda h=h: so_ref[0, h]) for h in range(heads))
            for h, (out, s_new) in enumerate(res):
                so_ref[0, h] = s_new
                out_ref[rows, h * width:(h + 1) * width] = out.astype(out_ref.dtype)
            return carry
        lax.fori_loop(0, n_rows // block, body, 0)
    else:
        n_seq = n_rows // block
        loads = [lambda ref, cols, s=s: ref[:, cols].astype(F32)[s * block:(s + 1) * block] for s in range(n_seq)]
        res = _lockstep(head_fn(h, loads[s], lambda s=s, h=h: si_ref[s, h])
                        for h in range(heads) for s in range(n_seq))
        for h in range(heads):
            for s in range(n_seq):
                so_ref[s, h] = res[h * n_seq + s][1]
            out_ref[:, h * width:(h + 1) * width] = jnp.concatenate(
                [res[h * n_seq + s][0] for s in range(n_seq)], axis=0).astype(out_ref.dtype)


def _gla_body(*refs, mode, heads, dk, dv, block, sub, mm_dtype):
    if mode == "prompt":
        q_ref, k_ref, v_ref, g_ref, ga_ref, wup_ref, ba_ref, gn_ref, og_ref, so_ref = refs
        si_ref = None
    else:
        q_ref, k_ref, v_ref, g_ref, ga_ref, wup_ref, ba_ref, gn_ref, si_ref, og_ref, so_ref = refs

    def head_fn(h, load, s_read):
        ks = slice(h * dk, (h + 1) * dk)
        vs = slice(h * dv, (h + 1) * dv)
        ga = load(ga_ref, slice(None)).astype(BF16)
        a_logit = jnp.dot(ga, wup_ref[:, ks], preferred_element_type=F32) + ba_ref[:, ks]
        yield
        la = _log_sigmoid(a_logit) * (1.0 / GLA_GATE_NORM)
        q = load(q_ref, ks) * (dk ** -0.5)
        o, s_new = yield from _recurrence_block(q, load(k_ref, ks), load(v_ref, vs), la, s_read,
                                                sub=sub, mm_dtype=mm_dtype)
        return _head_gated_norm(o, gn_ref[:, vs], load(g_ref, vs)), s_new

    _run_blocks(head_fn, heads, dv, og_ref, so_ref, si_ref, mode=mode, block=block, n_rows=q_ref.shape[0])


def _gla(z, zga, wup, ba, gnorm, state, *, mode, layer, n_seq, seq_len, d_model, name):
    heads = GLA_HEADS
    dk = d_model // 2 // heads
    dv = d_model // heads
    t = z.shape[0]
    if mode == "prompt":
        step = PROMPT_STEP
        n_t = seq_len // step
        grid = (n_seq, n_t)
        block, sub, mm_dtype = PROMPT_BLOCK, REF_CHUNK, BF16
        st_out = pl.BlockSpec((1, heads, dk, dv), lambda b, i: (b, 0, 0, 0))
        const = lambda b, i: (0, 0)
        zspec = lambda width, col: pl.BlockSpec((step, width), lambda b, i: (b * n_t + i, col))
    else:
        step = SAMPLE_SEQS * seq_len
        grid = (n_seq // SAMPLE_SEQS,)
        block = seq_len
        sub, mm_dtype = math.gcd(REF_CHUNK, seq_len), F32
        st_out = pl.BlockSpec((SAMPLE_SEQS, heads, dk, dv), lambda i: (i, 0, 0, 0))
        const = lambda i: (0, 0)
        zspec = lambda width, col: pl.BlockSpec((step, width), lambda i: (i, col))
    kw = heads * dk
    in_specs = [zspec(kw, 0), zspec(kw, 1), zspec(d_model, 1), zspec(d_model, 2),
                zspec(zga.shape[1], 0),
                pl.BlockSpec(wup.shape, const), pl.BlockSpec(ba.shape, const), pl.BlockSpec(gnorm.shape, const)]
    args = [z, z, z, z, zga, wup, ba, gnorm]
    if mode != "prompt":
        in_specs.append(pl.BlockSpec((None, SAMPLE_SEQS, heads, dk, dv), lambda i: (layer, i, 0, 0, 0)))
        args.append(state)
    og, s_out = pl.pallas_call(
        functools.partial(_gla_body, mode=mode, heads=heads, dk=dk, dv=dv, block=block, sub=sub,
                          mm_dtype=mm_dtype),
        grid=grid,
        in_specs=in_specs,
        out_specs=[zspec(d_model, 0), st_out],
        out_shape=[jax.ShapeDtypeStruct((t, d_model), BF16),
                   jax.ShapeDtypeStruct((n_seq, heads, dk, dv), F32)],
        compiler_params=_cparams(len(grid)),
        name=name,
    )(*args)
    return og, s_out


def _hgrn_body(*refs, mode, layer, heads, dk, block, sub, mm_dtype):
    if mode == "prompt":
        q_ref, f_ref, i_ref, g_ref, ma_ref, mb_ref, ogla_ref, lb_ref, gn_ref, om_ref, so_ref = refs
        si_ref = None
    else:
        q_ref, f_ref, i_ref, g_ref, ma_ref, mb_ref, ogla_ref, lb_ref, gn_ref, si_ref, om_ref, so_ref = refs

    p = lb_ref[...]
    e = jnp.exp(p - jnp.max(p, axis=0, keepdims=True))
    lb_all = jnp.sum(e[:layer + 1], axis=0, keepdims=True) / jnp.sum(e, axis=0, keepdims=True)

    def head_fn(h, load, s_read):
        ks = slice(h * dk, (h + 1) * dk)
        lb = lb_all[:, ks]
        f = lb + (1.0 - lb) * _sigmoid(load(f_ref, ks))
        hq = load(q_ref, ks)
        q = hq * _sigmoid(hq) * (dk ** -0.5)
        o, s_new = yield from _recurrence_block(q, 1.0 - f, load(i_ref, ks), jnp.log(f), s_read,
                                                sub=sub, mm_dtype=mm_dtype)
        o_h = _head_gated_norm(o, gn_ref[:, ks], load(g_ref, ks))
        mix = _sigmoid(load(ma_ref, ks)) * load(ogla_ref, ks) + _sigmoid(load(mb_ref, ks)) * o_h
        return mix, s_new

    _run_blocks(head_fn, heads, dk, om_ref, so_ref, si_ref, mode=mode, block=block, n_rows=q_ref.shape[0])


def _hgrn(z, og_gla, lb_param, gnorm, state, *, mode, layer, n_seq, seq_len, d_model, name):
    dk = HGRN_DK
    heads = d_model // dk
    t = z.shape[0]
    if mode == "prompt":
        step = PROMPT_STEP
        n_t = seq_len // step
        grid = (n_seq, n_t)
        block, sub, mm_dtype = PROMPT_BLOCK, REF_CHUNK, BF16
        st_out = pl.BlockSpec((1, heads, dk, dk), lambda b, i: (b, 0, 0, 0))
        const = lambda b, i: (0, 0)
        zspec = lambda col: pl.BlockSpec((step, d_model), lambda b, i: (b * n_t + i, col))
    else:
        step = SAMPLE_SEQS * seq_len
        grid = (n_seq // SAMPLE_SEQS,)
        block = seq_len
        sub, mm_dtype = math.gcd(REF_CHUNK, seq_len), F32
        st_out = pl.BlockSpec((SAMPLE_SEQS, heads, dk, dk), lambda i: (i, 0, 0, 0))
        const = lambda i: (0, 0)
        zspec = lambda col: pl.BlockSpec((step, d_model), lambda i: (i, col))
    in_specs = [zspec(0), zspec(1), zspec(2), zspec(3), zspec(4), zspec(5), zspec(0),
                pl.BlockSpec(lb_param.shape, const), pl.BlockSpec(gnorm.shape, const)]
    args = [z, z, z, z, z, z, og_gla, lb_param, gnorm]
    if mode != "prompt":
        in_specs.append(pl.BlockSpec((None, SAMPLE_SEQS, heads, dk, dk), lambda i: (layer, i, 0, 0, 0)))
        args.append(state)
    o_mix, s_out = pl.pallas_call(
        functools.partial(_hgrn_body, mode=mode, layer=layer, heads=heads, dk=dk, block=block, sub=sub,
                          mm_dtype=mm_dtype),
        grid=grid,
        in_specs=in_specs,
        out_specs=[zspec(0), st_out],
        out_shape=[jax.ShapeDtypeStruct((t, d_model), BF16),
                   jax.ShapeDtypeStruct((n_seq, heads, dk, dk), F32)],
        compiler_params=_cparams(len(grid)),
        name=name,
    )(*args)
    return o_mix, s_out


def _xattn_prompt_body(q_ref, k_ref, v_ref, o_ref, *, heads, dh):
    scale = dh ** -0.5
    for h in range(heads):
        hs = slice(h * dh, (h + 1) * dh)
        k = k_ref[0, :, hs].astype(BF16)
        v = v_ref[0, :, hs].astype(BF16)
        sc = lax.dot_general(q_ref[:, hs], k, (((1,), (1,)), ((), ())), preferred_element_type=F32) * scale
        p = jnp.exp(sc - jnp.max(sc, axis=-1, keepdims=True))
        o = jnp.dot(p.astype(BF16), v, preferred_element_type=F32) / jnp.sum(p, axis=-1, keepdims=True)
        o_ref[:, hs] = o.astype(o_ref.dtype)


def _xattn_sample_body(q_ref, k_ref, v_ref, o_ref, *, heads, dh, seq_len):
    scale = dh ** -0.5
    n_mem = k_ref.shape[1]
    for s in range(k_ref.shape[0]):
        rows = slice(s * seq_len, (s + 1) * seq_len)
        q = q_ref[rows, :]
        q2 = jnp.concatenate([q[:, h * dh:(h + 1) * dh] for h in range(heads)], axis=0)
        k2 = k_ref[s].reshape(n_mem * heads, dh)
        v2 = v_ref[s].reshape(n_mem * heads, dh)
        sc = lax.dot_general(q2, k2, (((1,), (1,)), ((), ())), preferred_element_type=F32) * scale
        q_head = lax.broadcasted_iota(jnp.int32, sc.shape, 0) // seq_len
        k_head = lax.broadcasted_iota(jnp.int32, sc.shape, 1) % heads
        sc = jnp.where(q_head == k_head, sc, -jnp.inf)
        p = jnp.exp(sc - jnp.max(sc, axis=-1, keepdims=True))
        o2 = jnp.dot(p, v2, preferred_element_type=F32) / jnp.sum(p, axis=-1, keepdims=True)
        for h in range(heads):
            o_ref[rows, h * dh:(h + 1) * dh] = o2[h * seq_len:(h + 1) * seq_len].astype(o_ref.dtype)


def _xattn(mq, mem_k, mem_v, *, mode, layer, n_seq, seq_len, name):
    t, d = mq.shape
    dh = d // MEM_HEADS
    if mode == "prompt":
        n_mem = mem_k.shape[1]
        tq = _pick_tile(seq_len, (512, 256, 128))
        n_t = seq_len // tq
        grid = (n_seq, n_t)
        q_spec = pl.BlockSpec((tq, d), lambda b, i: (b * n_t + i, 0))
        kv_spec = pl.BlockSpec((1, n_mem, d), lambda b, i: (b, 0, 0))
        body = functools.partial(_xattn_prompt_body, heads=MEM_HEADS, dh=dh)
        out_dtype = BF16
    else:
        n_mem = mem_k.shape[2]
        grid = (n_seq // SAMPLE_SEQS,)
        q_spec = pl.BlockSpec((SAMPLE_SEQS * seq_len, d), lambda i: (i, 0))
        kv_spec = pl.BlockSpec((None, SAMPLE_SEQS, n_mem, MEM_HEADS, dh), lambda i: (layer, i, 0, 0, 0))
        body = functools.partial(_xattn_sample_body, heads=MEM_HEADS, dh=dh, seq_len=seq_len)
        out_dtype = F32
    return pl.pallas_call(
        body,
        grid=grid,
        in_specs=[q_spec, kv_spec, kv_spec],
        out_specs=q_spec,
        out_shape=jax.ShapeDtypeStruct((t, d), out_dtype),
        compiler_params=_cparams(len(grid)),
        name=name,
    )(mq, mem_k, mem_v)


def _mid_body(om_ref, x_ref, k_ref, v_ref, wo_ref, wq_ref, wm_ref, g_ref, o_ref, *, heads, dh):
    scale = dh ** -0.5
    x1 = x_ref[...] + jnp.dot(om_ref[...], wo_ref[...], preferred_element_type=F32)
    ms = jnp.mean(x1 * x1, axis=-1, keepdims=True)
    hx = (x1 * lax.rsqrt(ms + EPS) * g_ref[...]).astype(BF16)
    mq = jnp.dot(hx, wq_ref[...], preferred_element_type=F32).astype(BF16)
    heads_out = []
    for h in range(heads):
        hs = slice(h * dh, (h + 1) * dh)
        sc = lax.dot_general(mq[:, hs], k_ref[0, :, hs], (((1,), (1,)), ((), ())),
                             preferred_element_type=F32) * scale
        p = jnp.exp(sc - jnp.max(sc, axis=-1, keepdims=True))
        o = jnp.dot(p.astype(BF16), v_ref[0, :, hs], preferred_element_type=F32) / jnp.sum(p, axis=-1, keepdims=True)
        heads_out.append(o.astype(BF16))
    mo = jnp.concatenate(heads_out, axis=1)
    o_ref[...] = x1 + jnp.dot(mo, wm_ref[...], preferred_element_type=F32)


def _mid_prompt(o_mix, x, mem_k, mem_v, w_out, w_q, w_o, gain, *, n_seq, seq_len, name):
    t, d = x.shape
    n_mem = mem_k.shape[1]
    tm = 256
    n_t = seq_len // tm
    rows = lambda dtype_rows: pl.BlockSpec((tm, d), lambda b, i: (b * n_t + i, 0))
    kv = pl.BlockSpec((1, n_mem, d), lambda b, i: (b, 0, 0))
    wspec = pl.BlockSpec((d, d), lambda b, i: (0, 0), pipeline_mode=pl.Buffered(1))
    return pl.pallas_call(
        functools.partial(_mid_body, heads=MEM_HEADS, dh=d // MEM_HEADS),
        grid=(n_seq, n_t),
        in_specs=[rows(BF16), rows(F32), kv, kv, wspec, wspec, wspec, pl.BlockSpec((1, d), lambda b, i: (0, 0))],
        out_specs=rows(F32),
        out_shape=jax.ShapeDtypeStruct((t, d), F32),
        compiler_params=_cparams(2),
        name=name,
    )(o_mix, x, mem_k, mem_v, w_out, w_q, w_o, gain)


def _ffn_up_body(*refs, mode, seq_len, tiles_per_seq, row_chunk):
    if mode == "prompt":
        x_ref, g_ref, wg_ref, wv_ref, cw_ref, cb_ref, act_ref, st_ref, h_ref, tail_ref = refs
        p1_ref = p2_ref = None
    else:
        x_ref, g_ref, wg_ref, wv_ref, cw_ref, cb_ref, buf_ref, act_ref, st_ref, h_ref = refs
        tail_ref = None
    i = pl.program_id(0)
    j = pl.program_id(1)

    @pl.when(j == 0)
    def _():
        def body(c, carry):
            rows = pl.ds(pl.multiple_of(c * row_chunk, row_chunk), row_chunk)
            xf = x_ref[rows, :]
            ms = jnp.mean(xf * xf, axis=-1, keepdims=True)
            h_ref[rows, :] = (xf * lax.rsqrt(ms + EPS) * g_ref[...]).astype(BF16)
            return carry
        lax.fori_loop(0, x_ref.shape[0] // row_chunk, body, 0)

    if mode == "prompt":
        @pl.when(i % tiles_per_seq == 0)
        def _():
            tail_ref[j] = jnp.zeros(tail_ref.shape[1:], F32)

    h = h_ref[...]
    ug = jnp.dot(h, wg_ref[...], preferred_element_type=F32)
    tm = ug.shape[0]
    row = lax.broadcasted_iota(jnp.int32, ug.shape, 0)
    roll1 = pltpu.roll(ug, 1, axis=0)
    roll2 = pltpu.roll(ug, 2, axis=0)
    if mode == "prompt":
        tail = tail_ref[j]
        prev1 = tail[7:8, :]
        prev2 = tail[6:7, :]
        sh1 = jnp.where(row >= 1, roll1, prev1)
        sh2 = jnp.where(row >= 2, roll2, jnp.where(row == 0, prev2, prev1))
        tail_ref[j] = ug[tm - 8:, :]
        st_ref[...] = ug[tm - 8:, :]
    else:
        buf = buf_ref[...]
        n_seq = buf.shape[0]
        spread = lambda r: jnp.broadcast_to(buf[:, r:r + 1, :], (n_seq, seq_len, buf.shape[2])).reshape(ug.shape)
        prev2, prev1 = spread(0), spread(1)
        pos = row % seq_len
        sh1 = jnp.where(pos >= 1, roll1, prev1)
        sh2 = jnp.where(pos >= 2, roll2, jnp.where(pos == 0, prev2, prev1))
        st_ref[...] = ug.reshape(n_seq, seq_len, ug.shape[1])[:, seq_len - (CONV_W - 1):, :]
    conv = cw_ref[0:1, :] * sh2 + cw_ref[1:2, :] * sh1 + cw_ref[2:3, :] * ug + cb_ref[...]
    gate = conv * _sigmoid(conv)
    uv = jnp.dot(h, wv_ref[...], preferred_element_type=F32)
    act_ref[...] = (gate * uv).astype(act_ref.dtype)


def _ffn_up(x, gain, wg, wv, cw, cb, carry, *, mode, seq_len, name):
    m, k = x.shape
    nf = wg.shape[1]
    tm = _pick_tile(m, (1024,))
    tn = _pick_tile(nf, (512, 256, 128))
    xs = pl.BlockSpec((tm, k), lambda i, j: (i, 0))
    gs = pl.BlockSpec((1, k), lambda i, j: (0, 0))
    ws = pl.BlockSpec((k, tn), lambda i, j: (0, j))
    cws = pl.BlockSpec((CONV_W, tn), lambda i, j: (0, j))
    cbs = pl.BlockSpec((1, tn), lambda i, j: (0, j))
    ts = pl.BlockSpec((tm, tn), lambda i, j: (i, j))
    in_specs = [xs, gs, ws, ws, cws, cbs]
    args = [x, gain, wg, wv, cw, cb]
    scratch = [pltpu.VMEM((tm, k), BF16)]
    if mode == "prompt":
        assert seq_len % tm == 0
        scratch.append(pltpu.VMEM((nf // tn, 8, tn), F32))
        st_spec = pl.BlockSpec((None, 8, tn), lambda i, j: (i, 0, j))
        st_shape = jax.ShapeDtypeStruct((m // tm, 8, nf), F32)
    else:
        assert tm % seq_len == 0 and seq_len == 8
        bs = pl.BlockSpec((tm // seq_len, CONV_W - 1, tn), lambda i, j: (i, 0, j))
        in_specs.append(bs)
        args.append(carry)
        st_spec = bs
        st_shape = jax.ShapeDtypeStruct((m // seq_len, CONV_W - 1, nf), F32)
    return pl.pallas_call(
        functools.partial(_ffn_up_body, mode=mode, seq_len=seq_len, tiles_per_seq=max(seq_len // tm, 1),
                          row_chunk=128),
        grid=(m // tm, nf // tn),
        in_specs=in_specs,
        out_specs=[ts, st_spec],
        out_shape=[jax.ShapeDtypeStruct((m, nf), BF16), st_shape],
        scratch_shapes=scratch,
        compiler_params=_cparams(2),
        name=name,
    )(*args)


def _pad_cols(w, n):
    return jnp.pad(w, ((0, 0), (0, n - w.shape[1])))


def _layer(x, mem_k, mem_v, s_gla, s_hgrn, conv_buf, p, *, mode, layer, n_seq, seq_len, tag):
    t, d = x.shape
    nm = lambda s: f"{s}_{tag}"
    ga_off, rank = p["ga_off"], p["rank"]
    w_in_t = p["w_in_t"]
    in_proj = functools.partial(_dense, x, w_in_t, gain=p["norm_mix_g"], prologue="norm", w_transposed=True,
                                tn=2048)
    za, zga = in_proj(out_dtype=BF16, name=nm("in_proj_a"), col_off=0, n_cols=ga_off, extra_cols=(ga_off, rank))
    zb = in_proj(out_dtype=BF16, name=nm("in_proj_b"), col_off=ga_off + rank,
                 n_cols=w_in_t.shape[0] - ga_off - rank)
    og_gla, s_gla_new = _gla(za, zga, p["gla_w_a_up"], p["gla_b_a"], p["gla_norm_g"], s_gla,
                             mode=mode, layer=layer, n_seq=n_seq, seq_len=seq_len, d_model=d, name=nm("gla"))
    o_mix, s_hgrn_new = _hgrn(zb, og_gla, p["hgrn_lower_bound"], p["hgrn_norm_g"], s_hgrn,
                              mode=mode, layer=layer, n_seq=n_seq, seq_len=seq_len, d_model=d, name=nm("hgrn"))
    if mode == "prompt":
        x2 = _mid_prompt(o_mix, x, mem_k.astype(BF16), mem_v.astype(BF16), p["w_out"], p["w_mem_q"], p["w_mem_o"],
                         p["norm_xattn_g"], n_seq=n_seq, seq_len=seq_len, name=nm("mid"))
    else:
        x1 = _dense(o_mix, p["w_out"], res=x, prologue="plain", out_dtype=F32, name=nm("out_proj"))
        mq = _dense(x1, p["w_mem_q"], gain=p["norm_xattn_g"], prologue="norm", out_dtype=F32, name=nm("mem_q"))
        mo = _xattn(mq, mem_k, mem_v, mode=mode, layer=layer, n_seq=n_seq, seq_len=seq_len, name=nm("xattn"))
        x2 = _dense(mo, p["w_mem_o"], res=x1, prologue="cast", out_dtype=F32, name=nm("mem_o"))

    nf = p["w_up_gate"].shape[1]
    d_ff = p["d_ff"]
    carry = None if mode == "prompt" else jnp.pad(conv_buf[layer], ((0, 0), (0, 0), (0, nf - d_ff)))
    act, ug_rows = _ffn_up(x2, p["norm_ffn_g"], p["w_up_gate"], p["w_up_val"], p["ffn_conv_w"], p["ffn_conv_b"],
                           carry, mode=mode, seq_len=seq_len, name=nm("ffn_up"))
    final = p["norm_final_g"] if p["is_last"] else None
    y = _dense(act, p["w_down"], res=x2, prologue="plain", out_dtype=F32, name=nm("ffn_down"),
               final_gain=final, tm=512 if final is not None else None)

    if mode == "prompt":
        tiles_per_seq = ug_rows.shape[0] // n_seq
        conv_new = ug_rows[tiles_per_seq - 1::tiles_per_seq, 8 - (CONV_W - 1):, :d_ff]
    else:
        conv_new = ug_rows[:, :, :d_ff]
    return y, s_gla_new, s_hgrn_new, conv_new


def kernel(x_prompt, x_sample, mem_prompt, state_gla, state_hgrn, state_ffn_conv, cache_mem_k, cache_mem_v,
           norm_mix_g, w_in, gla_w_a_up, gla_b_a, gla_norm_g, hgrn_lower_bound, hgrn_norm_g, w_out,
           norm_xattn_g, norm_mem_g, w_mem_q, w_mem_k, w_mem_v, w_mem_o,
           norm_ffn_g, w_ffn_up, ffn_conv_w, ffn_conv_b, w_ffn_down, norm_final_g):
    depth = w_in.shape[0]
    bp, lp, d = x_prompt.shape
    bs, ls, _ = x_sample.shape
    n_mem = mem_prompt.shape[1]
    d_ff = w_ffn_down.shape[1]
    rank = gla_w_a_up.shape[1]
    qk = gla_w_a_up.shape[2]
    ga_off = 2 * qk + 2 * d
    nf = -(-d_ff // 512) * 512

    xp = x_prompt.reshape(bp * lp, d)
    xs = x_sample.reshape(bs * ls, d)
    mem = mem_prompt.reshape(bp * n_mem, d)
    row = lambda v: v.reshape(1, -1)

    outs = {k: [] for k in ("gla_p", "hgrn_p", "conv_p", "mk_p", "mv_p", "gla_s", "hgrn_s", "conv_s")}
    for l in range(depth):
        p = {
            "w_out": w_out[l].astype(BF16), "w_mem_q": w_mem_q[l].astype(BF16), "w_mem_o": w_mem_o[l].astype(BF16),
            "w_in_t": jnp.swapaxes(w_in[l], 0, 1).astype(BF16),
            "ga_off": ga_off, "rank": rank,
            "gla_w_a_up": gla_w_a_up[l].astype(BF16),
            "gla_b_a": row(gla_b_a[l]), "gla_norm_g": row(gla_norm_g[l]),
            "hgrn_lower_bound": hgrn_lower_bound, "hgrn_norm_g": row(hgrn_norm_g[l]),
            "norm_mix_g": row(norm_mix_g[l]), "norm_xattn_g": row(norm_xattn_g[l]),
            "norm_ffn_g": row(norm_ffn_g[l]), "norm_final_g": row(norm_final_g), "is_last": l == depth - 1,
            "w_up_gate": _pad_cols(w_ffn_up[l][:, :d_ff], nf).astype(BF16),
            "w_up_val": _pad_cols(w_ffn_up[l][:, d_ff:], nf).astype(BF16),
            "ffn_conv_w": _pad_cols(ffn_conv_w[l], nf), "ffn_conv_b": _pad_cols(row(ffn_conv_b[l]), nf),
            "w_down": jnp.pad(w_ffn_down[l], ((0, nf - d_ff), (0, 0))).astype(BF16),
            "d_ff": d_ff,
        }
        g_mem = row(norm_mem_g[l])
        mk = _dense(mem, w_mem_k, gain=g_mem, prologue="norm", out_dtype=F32, name=f"mem_k_{l}", layer=l)
        mv = _dense(mem, w_mem_v, gain=g_mem, prologue="norm", out_dtype=F32, name=f"mem_v_{l}", layer=l)
        mk3, mv3 = mk.reshape(bp, n_mem, d), mv.reshape(bp, n_mem, d)
        xp, sg, sh, cb = _layer(xp, mk3, mv3, None, None, None, p, mode="prompt", layer=l,
                                n_seq=bp, seq_len=lp, tag=f"p{l}")
        outs["gla_p"].append(sg); outs["hgrn_p"].append(sh); outs["conv_p"].append(cb)
        outs["mk_p"].append(mk3.reshape(bp, n_mem, MEM_HEADS, d // MEM_HEADS))
        outs["mv_p"].append(mv3.reshape(bp, n_mem, MEM_HEADS, d // MEM_HEADS))
        xs, sg2, sh2, cb2 = _layer(xs, cache_mem_k, cache_mem_v, state_gla, state_hgrn, state_ffn_conv, p,
                                   mode="sample", layer=l, n_seq=bs, seq_len=ls, tag=f"s{l}")
        outs["gla_s"].append(sg2); outs["hgrn_s"].append(sh2); outs["conv_s"].append(cb2)

    y_prompt = xp.reshape(bp, lp, d)
    y_sample = xs.reshape(bs, ls, d)
    st = lambda k: jnp.stack(outs[k])
    return (y_prompt, y_sample, st("gla_p"), st("hgrn_p"), st("conv_p"), st("mk_p"), st("mv_p"),
            st("gla_s"), st("hgrn_s"), st("conv_s"))
```

```python
import functools
import math

import jax
import jax.numpy as jnp
from jax import lax
from jax.experimental import pallas as pl
from jax.experimental.pallas import tpu as pltpu

F32 = jnp.float32
BF16 = jnp.bfloat16

LANE = 128
BF16_ROWS = 16
V7X_VMEM_LIMIT = 56 * 1024 * 1024

EPS = 1e-6
GLA_HEADS = 4
GLA_GATE_NORM = 16.0
HGRN_DK = 128
MEM_HEADS = 4
CONV_W = 3
REF_CHUNK = 16
PROMPT_BLOCK = 64
PROMPT_STEP = 256
SAMPLE_SEQS = 2


def _cparams(n_axes):
    return pltpu.CompilerParams(dimension_semantics=("arbitrary",) * n_axes,
                                vmem_limit_bytes=V7X_VMEM_LIMIT)


def _sigmoid(x):
    return 0.5 * jnp.tanh(0.5 * x) + 0.5


def _pick_tile(n, candidates):
    for c in candidates:
        if n % c == 0:
            return c
    return n


def _dense_body(*refs, prologue, has_res, has_extra, has_final, row_chunk, w_transposed):
    it = iter(refs)
    x_ref = next(it)
    g_ref = next(it) if prologue == "norm" else None
    w_ref = next(it)
    we_ref = next(it) if has_extra else None
    r_ref = next(it) if has_res else None
    fg_ref = next(it) if has_final else None
    o_ref = next(it)
    oe_ref = next(it) if has_extra else None
    h_ref = next(it) if prologue != "plain" else None
    j = pl.program_id(1)
    nt = (((1,), (1,)), ((), ()))

    if prologue != "plain":
        @pl.when(j == 0)
        def _():
            def body(c, carry):
                rows = pl.ds(pl.multiple_of(c * row_chunk, row_chunk), row_chunk)
                xf = x_ref[rows, :].astype(F32)
                if prologue == "norm":
                    ms = jnp.mean(xf * xf, axis=-1, keepdims=True)
                    xf = xf * lax.rsqrt(ms + EPS) * g_ref[...]
                h_ref[rows, :] = xf.astype(BF16)
                return carry
            lax.fori_loop(0, x_ref.shape[0] // row_chunk, body, 0)
            if has_extra:
                oe_ref[...] = lax.dot_general(h_ref[...], we_ref[...], nt, preferred_element_type=F32)
        lhs = h_ref[...]
    else:
        lhs = x_ref[...]
    w = w_ref[...].astype(BF16)
    if w_transposed:
        acc = lax.dot_general(lhs, w, nt, preferred_element_type=F32)
    else:
        acc = jnp.dot(lhs, w, preferred_element_type=F32)
    if has_res:
        acc = acc + r_ref[...]
    if not has_final:
        o_ref[...] = acc.astype(o_ref.dtype)
    else:
        tn = acc.shape[1]
        o_ref[:, pl.ds(pl.multiple_of(j * tn, tn), tn)] = acc

        @pl.when(j == pl.num_programs(1) - 1)
        def _():
            def body(c, carry):
                rows = pl.ds(pl.multiple_of(c * row_chunk, row_chunk), row_chunk)
                xf = o_ref[rows, :]
                ms = jnp.mean(xf * xf, axis=-1, keepdims=True)
                o_ref[rows, :] = xf * lax.rsqrt(ms + EPS) * fg_ref[...]
                return carry
            lax.fori_loop(0, o_ref.shape[0] // row_chunk, body, 0)


def _dense(x, w, *, gain=None, res=None, prologue, out_dtype, name, layer=0, col_off=0, n_cols=None,
           w_transposed=False, extra_cols=None, final_gain=None, tm=None, tn=None):
    m, k = x.shape
    n = n_cols if n_cols is not None else w.shape[-1]
    if tm is None:
        tm = _pick_tile(m, (1024, 512, 256, 128, 64, 32, 16))
    if tn is None:
        tn = _pick_tile(n, (1024, 512, 256, 128) if k <= 2048 else (512, 256, 128))
    assert m % tm == 0 and n % tn == 0
    row_chunk = min(tm, 128)
    in_specs = [pl.BlockSpec((tm, k), lambda i, j: (i, 0))]
    args = [x]
    if prologue == "norm":
        in_specs.append(pl.BlockSpec((1, k), lambda i, j: (0, 0)))
        args.append(gain)
    if w_transposed:
        assert col_off % BF16_ROWS == 0 and tn % BF16_ROWS == 0
        in_specs.append(pl.BlockSpec((pl.Element(tn), pl.Element(k)),
                                     lambda i, j: (pl.multiple_of(col_off + j * tn, BF16_ROWS), 0)))
    elif w.ndim == 3:
        assert col_off % tn == 0
        in_specs.append(pl.BlockSpec((None, k, tn), lambda i, j: (layer, 0, j + col_off // tn)))
    else:
        in_specs.append(pl.BlockSpec((k, tn), lambda i, j: (0, j)))
    args.append(w)
    out_specs = [pl.BlockSpec((tm, tn), lambda i, j: (i, j))]
    out_shape = [jax.ShapeDtypeStruct((m, n), out_dtype)]
    if extra_cols is not None:
        e_off, e_n = extra_cols
        assert w_transposed and prologue != "plain" and e_off % BF16_ROWS == 0 and e_n % BF16_ROWS == 0
        in_specs.append(pl.BlockSpec((pl.Element(e_n), pl.Element(k)), lambda i, j: (e_off, 0)))
        args.append(w)
        out_specs.append(pl.BlockSpec((tm, e_n), lambda i, j: (i, 0)))
        out_shape.append(jax.ShapeDtypeStruct((m, e_n), F32))
    if res is not None:
        in_specs.append(pl.BlockSpec((tm, tn), lambda i, j: (i, j)))
        args.append(res)
    if final_gain is not None:
        assert out_dtype == F32
        in_specs.append(pl.BlockSpec((1, n), lambda i, j: (0, 0)))
        args.append(final_gain)
        out_specs[0] = pl.BlockSpec((tm, n), lambda i, j: (i, 0))
    scratch = [] if prologue == "plain" else [pltpu.VMEM((tm, k), BF16)]
    outs = pl.pallas_call(
        functools.partial(_dense_body, prologue=prologue, has_res=res is not None,
                          has_extra=extra_cols is not None, has_final=final_gain is not None,
                          row_chunk=row_chunk, w_transposed=w_transposed),
        grid=(m // tm, n // tn),
        in_specs=in_specs,
        out_specs=out_specs,
        out_shape=out_shape,
        scratch_shapes=scratch,
        compiler_params=_cparams(2),
        name=name,
    )(*args)
    return outs if extra_cols is not None else outs[0]


def _cumsum_rows(x, group):
    rows = lax.broadcasted_iota(jnp.int32, x.shape, 0) % group
    shift = 1
    while shift < group:
        rolled = pltpu.roll(x, shift, axis=0)
        x = x + jnp.where(rows >= shift, rolled, 0.0)
        shift *= 2
    return x


def _lockstep(gens):
    gens = list(gens)
    results = [None] * len(gens)
    live = list(range(len(gens)))
    while live:
        still = []
        for n in live:
            try:
                next(gens[n])
                still.append(n)
            except StopIteration as stop:
                results[n] = stop.value
        live = still
    return results


def _recurrence_block(q, k, v, la, s_read, *, sub, mm_dtype):
    bt, kd = q.shape
    vd = v.shape[1]
    ns = bt // sub
    b_loc = _cumsum_rows(la, sub)
    qd = q * jnp.exp(b_loc)
    ki = k * jnp.exp(-b_loc)
    tot = [b_loc[(i + 1) * sub - 1:(i + 1) * sub, :] for i in range(ns)]
    pre = [jnp.zeros((1, kd), F32)]
    for i in range(ns):
        pre.append(pre[-1] + tot[i])
    sl = [slice(i * sub, (i + 1) * sub) for i in range(ns)]
    vb = v.astype(mm_dtype)
    qdm = qd.astype(mm_dtype)
    q_state = jnp.concatenate([qd[sl[i]] * jnp.exp(pre[i]) for i in range(ns)], axis=0) if ns > 1 else qd
    q_state = q_state.astype(mm_dtype)
    ke = [ki[sl[i]] * jnp.exp(tot[i]) for i in range(ns)]
    kmats = []
    for i in range(ns):
        parts = [ke[j] if j == i - 1 else ke[j] * jnp.exp(pre[i] - pre[j + 1]) for j in range(i)]
        parts.append(ki[sl[i]])
        parts += [jnp.zeros((sub, kd), F32)] * (ns - 1 - i)
        kmats.append((jnp.concatenate(parts, axis=0) if ns > 1 else parts[0]).astype(mm_dtype))
    k_end = jnp.concatenate([ke[i] if i == ns - 1 else ke[i] * jnp.exp(pre[ns] - pre[i + 1])
                             for i in range(ns)], axis=0) if ns > 1 else ke[0]
    k_end = k_end.astype(mm_dtype)
    decay = jnp.broadcast_to(jnp.exp(pre[ns]), (LANE, kd)).T
    yield

    s_prev = s_read()
    o_state = jnp.dot(q_state, s_prev.astype(mm_dtype), preferred_element_type=F32)
    a_rows = [lax.dot_general(qdm[sl[i]], kmats[i], (((1,), (1,)), ((), ())), preferred_element_type=F32)
              for i in range(ns)]
    ds = lax.dot_general(k_end, vb, (((0,), (0,)), ((), ())), preferred_element_type=F32)
    yield

    a = jnp.concatenate(a_rows, axis=0) if ns > 1 else a_rows[0]
    row = lax.broadcasted_iota(jnp.int32, a.shape, 0)
    col = lax.broadcasted_iota(jnp.int32, a.shape, 1)
    a = jnp.where(col <= row, a, 0.0).astype(mm_dtype)
    o = jnp.dot(a, vb, preferred_element_type=F32) + o_state
    s_new = jnp.concatenate([decay * s_prev[:, c * LANE:(c + 1) * LANE] for c in range(vd // LANE)],
                            axis=1) + ds
    yield
    return o, s_new


def _head_gated_norm(o, gnorm, gate):
    ms = jnp.mean(o * o, axis=-1, keepdims=True)
    return o * lax.rsqrt(ms + EPS) * gnorm * (gate * _sigmoid(gate))


def _log_sigmoid(x):
    return jnp.minimum(x, 0.0) - jnp.log(1.0 + jnp.exp(-jnp.abs(x)))


def _mixer_body(*refs, mode, layer, gla_heads, gla_dk, gla_dv, hgrn_heads, hgrn_dk, block, sub, mm_dtype):
    (q_ref, k_ref, v_ref, g_ref, ga_ref, wup_ref, ba_ref, gng_ref,
     hq_ref, hf_ref, hi_ref, hg_ref, ma_ref, mb_ref, lb_ref, gnh_ref) = refs[:16]
    if mode == "prompt":
        om_ref, sg_out, sh_out = refs[16:]
        sg_in = sh_in = None
    else:
        sg_in, sh_in, om_ref, sg_out, sh_out = refs[16:]

    p = lb_ref[...]
    e = jnp.exp(p - jnp.max(p, axis=0, keepdims=True))
    lb_all = jnp.sum(e[:layer + 1], axis=0, keepdims=True) / jnp.sum(e, axis=0, keepdims=True)

    def gla_head(h, load, s_read, shared):
        ks = slice(h * gla_dk, (h + 1) * gla_dk)
        vs = slice(h * gla_dv, (h + 1) * gla_dv)
        ga = load(ga_ref, slice(None)).astype(BF16)
        a_logit = jnp.dot(ga, wup_ref[:, ks], preferred_element_type=F32) + ba_ref[:, ks]
        yield
        la = _log_sigmoid(a_logit) * (1.0 / GLA_GATE_NORM)
        q = load(q_ref, ks) * (gla_dk ** -0.5)
        o, s_new = yield from _recurrence_block(q, load(k_ref, ks), load(v_ref, vs), la, s_read,
                                                sub=sub, mm_dtype=mm_dtype)
        shared[h] = _head_gated_norm(o, gng_ref[:, vs], load(g_ref, vs))
        return s_new

    def hgrn_head(h, load, s_read, shared):
        ks = slice(h * hgrn_dk, (h + 1) * hgrn_dk)
        lb = lb_all[:, ks]
        f = lb + (1.0 - lb) * _sigmoid(load(hf_ref, ks))
        hq = load(hq_ref, ks)
        q = hq * _sigmoid(hq) * (hgrn_dk ** -0.5)
        yield
        o, s_new = yield from _recurrence_block(q, 1.0 - f, load(hi_ref, ks), jnp.log(f), s_read,
                                                sub=sub, mm_dtype=mm_dtype)
        o_h = _head_gated_norm(o, gnh_ref[:, ks], load(hg_ref, ks))
        per = gla_dv // hgrn_dk
        o_gla = shared[h // per][:, (h % per) * hgrn_dk:(h % per + 1) * hgrn_dk]
        mix = _sigmoid(load(ma_ref, ks)) * o_gla + _sigmoid(load(mb_ref, ks)) * o_h
        return mix, s_new

    def run(load, sg_read, sh_read):
        shared = {}
        gens = [gla_head(h, load, functools.partial(sg_read, h), shared) for h in range(gla_heads)]
        gens += [hgrn_head(h, load, functools.partial(sh_read, h), shared) for h in range(hgrn_heads)]
        res = _lockstep(gens)
        return res[:gla_heads], res[gla_heads:]

    if mode == "prompt":
        @pl.when(pl.program_id(1) == 0)
        def _():
            sg_out[...] = jnp.zeros(sg_out.shape, F32)
            sh_out[...] = jnp.zeros(sh_out.shape, F32)

        def body(c, carry):
            rows = pl.ds(pl.multiple_of(c * block, block), block)
            load = lambda ref, cols: ref[rows, cols].astype(F32)
            g_res, h_res = run(load, lambda h: sg_out[0, h], lambda h: sh_out[0, h])
            for h, s_new in enumerate(g_res):
                sg_out[0, h] = s_new
            for h, (mix, s_new) in enumerate(h_res):
                sh_out[0, h] = s_new
                om_ref[rows, h * hgrn_dk:(h + 1) * hgrn_dk] = mix.astype(om_ref.dtype)
            return carry
        lax.fori_loop(0, q_ref.shape[0] // block, body, 0)
    else:
        n_seq = q_ref.shape[0] // block
        mixes = []
        for s in range(n_seq):
            load = lambda ref, cols, s=s: ref[:, cols].astype(F32)[s * block:(s + 1) * block]
            g_res, h_res = run(load, lambda h, s=s: sg_in[s, h], lambda h, s=s: sh_in[s, h])
            for h, s_new in enumerate(g_res):
                sg_out[s, h] = s_new
            for h, (mix, s_new) in enumerate(h_res):
                sh_out[s, h] = s_new
            mixes.append([mix for mix, _ in h_res])
        for h in range(hgrn_heads):
            om_ref[:, h * hgrn_dk:(h + 1) * hgrn_dk] = jnp.concatenate(
                [mixes[s][h] for s in range(n_seq)], axis=0).astype(om_ref.dtype)


def _mixer(za, zb, zga, p, s_gla, s_hgrn, *, mode, layer, n_seq, seq_len, d_model, name):
    gh, hk = GLA_HEADS, HGRN_DK
    gk, gv = d_model // 2 // gh, d_model // gh
    hh = d_model // hk
    t = za.shape[0]
    if mode == "prompt":
        step = PROMPT_STEP
        n_t = seq_len // step
        grid = (n_seq, n_t)
        block, sub, mm_dtype = PROMPT_BLOCK, REF_CHUNK, BF16
        rowmap = lambda col: (lambda b, i: (b * n_t + i, col))
        const = lambda b, i: (0, 0)
        st = lambda *dims: pl.BlockSpec((1,) + dims, lambda b, i: (b, 0, 0, 0))
    else:
        step = SAMPLE_SEQS * seq_len
        grid = (n_seq // SAMPLE_SEQS,)
        block = seq_len
        sub, mm_dtype = math.gcd(REF_CHUNK, seq_len), F32
        rowmap = lambda col: (lambda i: (i, col))
        const = lambda i: (0, 0)
        st = lambda *dims: pl.BlockSpec((SAMPLE_SEQS,) + dims, lambda i: (i, 0, 0, 0))
    zs = lambda width, col: pl.BlockSpec((step, width), rowmap(col))
    whole = lambda a: pl.BlockSpec(a.shape, const)
    kw = gh * gk
    in_specs = [zs(kw, 0), zs(kw, 1), zs(d_model, 1), zs(d_model, 2), zs(zga.shape[1], 0),
                whole(p["gla_w_a_up"]), whole(p["gla_b_a"]), whole(p["gla_norm_g"])]
    in_specs += [zs(d_model, c) for c in range(6)] + [whole(p["hgrn_lower_bound"]), whole(p["hgrn_norm_g"])]
    args = [za, za, za, za, zga, p["gla_w_a_up"], p["gla_b_a"], p["gla_norm_g"],
            zb, zb, zb, zb, zb, zb, p["hgrn_lower_bound"], p["hgrn_norm_g"]]
    if mode != "prompt":
        in_specs += [pl.BlockSpec((None, SAMPLE_SEQS, gh, gk, gv), lambda i: (layer, i, 0, 0, 0)),
                     pl.BlockSpec((None, SAMPLE_SEQS, hh, hk, hk), lambda i: (layer, i, 0, 0, 0))]
        args += [s_gla, s_hgrn]
    return pl.pallas_call(
        functools.partial(_mixer_body, mode=mode, layer=layer, gla_heads=gh, gla_dk=gk, gla_dv=gv,
                          hgrn_heads=hh, hgrn_dk=hk, block=block, sub=sub, mm_dtype=mm_dtype),
        grid=grid,
        in_specs=in_specs,
        out_specs=[zs(d_model, 0), st(gh, gk, gv), st(hh, hk, hk)],
        out_shape=[jax.ShapeDtypeStruct((t, d_model), BF16),
                   jax.ShapeDtypeStruct((n_seq, gh, gk, gv), F32),
                   jax.ShapeDtypeStruct((n_seq, hh, hk, hk), F32)],
        compiler_params=_cparams(len(grid)),
        name=name,
    )(*args)


def _xattn_sample_body(q_ref, k_ref, v_ref, o_ref, *, heads, dh, seq_len):
    scale = dh ** -0.5
    n_mem = k_ref.shape[1]
    for s in range(k_ref.shape[0]):
        rows = slice(s * seq_len, (s + 1) * seq_len)
        q = q_ref[rows, :]
        q2 = jnp.concatenate([q[:, h * dh:(h + 1) * dh] for h in range(heads)], axis=0)
        k2 = k_ref[s].reshape(n_mem * heads, dh)
        v2 = v_ref[s].reshape(n_mem * heads, dh)
        sc = lax.dot_general(q2, k2, (((1,), (1,)), ((), ())), preferred_element_type=F32) * scale
        q_head = lax.broadcasted_iota(jnp.int32, sc.shape, 0) // seq_len
        k_head = lax.broadcasted_iota(jnp.int32, sc.shape, 1) % heads
        sc = jnp.where(q_head == k_head, sc, -jnp.inf)
        p = jnp.exp(sc - jnp.max(sc, axis=-1, keepdims=True))
        o2 = jnp.dot(p, v2, preferred_element_type=F32) / jnp.sum(p, axis=-1, keepdims=True)
        for h in range(heads):
            o_ref[rows, h * dh:(h + 1) * dh] = o2[h * seq_len:(h + 1) * seq_len].astype(o_ref.dtype)


def _xattn_sample(mq, mem_k, mem_v, *, layer, n_seq, seq_len, name):
    t, d = mq.shape
    dh = d // MEM_HEADS
    n_mem = mem_k.shape[2]
    q_spec = pl.BlockSpec((SAMPLE_SEQS * seq_len, d), lambda i: (i, 0))
    kv_spec = pl.BlockSpec((None, SAMPLE_SEQS, n_mem, MEM_HEADS, dh), lambda i: (layer, i, 0, 0, 0))
    return pl.pallas_call(
        functools.partial(_xattn_sample_body, heads=MEM_HEADS, dh=dh, seq_len=seq_len),
        grid=(n_seq // SAMPLE_SEQS,),
        in_specs=[q_spec, kv_spec, kv_spec],
        out_specs=q_spec,
        out_shape=jax.ShapeDtypeStruct((t, d), F32),
        compiler_params=_cparams(1),
        name=name,
    )(mq, mem_k, mem_v)


def _mid_body(om_ref, x_ref, k_ref, v_ref, wo_ref, wq_ref, wm_ref, g_ref, o_ref, *, heads, dh):
    scale = dh ** -0.5
    x1 = x_ref[...] + jnp.dot(om_ref[...], wo_ref[...], preferred_element_type=F32)
    ms = jnp.mean(x1 * x1, axis=-1, keepdims=True)
    hx = (x1 * lax.rsqrt(ms + EPS) * g_ref[...]).astype(BF16)
    mq = jnp.dot(hx, wq_ref[...], preferred_element_type=F32).astype(BF16)
    heads_out = []
    for h in range(heads):
        hs = slice(h * dh, (h + 1) * dh)
        sc = lax.dot_general(mq[:, hs], k_ref[0, :, hs], (((1,), (1,)), ((), ())),
                             preferred_element_type=F32) * scale
        p = jnp.exp(sc - jnp.max(sc, axis=-1, keepdims=True))
        o = jnp.dot(p.astype(BF16), v_ref[0, :, hs], preferred_element_type=F32) / jnp.sum(p, axis=-1, keepdims=True)
        heads_out.append(o.astype(BF16))
    mo = jnp.concatenate(heads_out, axis=1)
    o_ref[...] = x1 + jnp.dot(mo, wm_ref[...], preferred_element_type=F32)


def _mid_prompt(o_mix, x, mem_k, mem_v, w_out, w_q, w_o, gain, *, n_seq, seq_len, name):
    t, d = x.shape
    n_mem = mem_k.shape[1]
    tm = 256
    n_t = seq_len // tm
    rows = lambda dtype_rows: pl.BlockSpec((tm, d), lambda b, i: (b * n_t + i, 0))
    kv = pl.BlockSpec((1, n_mem, d), lambda b, i: (b, 0, 0))
    wspec = pl.BlockSpec((d, d), lambda b, i: (0, 0), pipeline_mode=pl.Buffered(1))
    return pl.pallas_call(
        functools.partial(_mid_body, heads=MEM_HEADS, dh=d // MEM_HEADS),
        grid=(n_seq, n_t),
        in_specs=[rows(BF16), rows(F32), kv, kv, wspec, wspec, wspec, pl.BlockSpec((1, d), lambda b, i: (0, 0))],
        out_specs=rows(F32),
        out_shape=jax.ShapeDtypeStruct((t, d), F32),
        compiler_params=_cparams(2),
        name=name,
    )(o_mix, x, mem_k, mem_v, w_out, w_q, w_o, gain)


def _ffn_up_body(*refs, mode, seq_len, tiles_per_seq, row_chunk):
    if mode == "prompt":
        x_ref, g_ref, wg_ref, wv_ref, cw_ref, cb_ref, act_ref, st_ref, h_ref, tail_ref = refs
        p1_ref = p2_ref = None
    else:
        x_ref, g_ref, wg_ref, wv_ref, cw_ref, cb_ref, buf_ref, act_ref, st_ref, h_ref = refs
        tail_ref = None
    i = pl.program_id(0)
    j = pl.program_id(1)

    @pl.when(j == 0)
    def _():
        def body(c, carry):
            rows = pl.ds(pl.multiple_of(c * row_chunk, row_chunk), row_chunk)
            xf = x_ref[rows, :]
            ms = jnp.mean(xf * xf, axis=-1, keepdims=True)
            h_ref[rows, :] = (xf * lax.rsqrt(ms + EPS) * g_ref[...]).astype(BF16)
            return carry
        lax.fori_loop(0, x_ref.shape[0] // row_chunk, body, 0)

    if mode == "prompt":
        @pl.when(i % tiles_per_seq == 0)
        def _():
            tail_ref[j] = jnp.zeros(tail_ref.shape[1:], F32)

    h = h_ref[...]
    ug = jnp.dot(h, wg_ref[...], preferred_element_type=F32)
    tm = ug.shape[0]
    row = lax.broadcasted_iota(jnp.int32, ug.shape, 0)
    roll1 = pltpu.roll(ug, 1, axis=0)
    roll2 = pltpu.roll(ug, 2, axis=0)
    if mode == "prompt":
        tail = tail_ref[j]
        prev1 = tail[7:8, :]
        prev2 = tail[6:7, :]
        sh1 = jnp.where(row >= 1, roll1, prev1)
        sh2 = jnp.where(row >= 2, roll2, jnp.where(row == 0, prev2, prev1))
        tail_ref[j] = ug[tm - 8:, :]
        st_ref[...] = ug[tm - 8:, :]
    else:
        buf = buf_ref[...]
        n_seq = buf.shape[0]
        spread = lambda r: jnp.broadcast_to(buf[:, r:r + 1, :], (n_seq, seq_len, buf.shape[2])).reshape(ug.shape)
        prev2, prev1 = spread(0), spread(1)
        pos = row % seq_len
        sh1 = jnp.where(pos >= 1, roll1, prev1)
        sh2 = jnp.where(pos >= 2, roll2, jnp.where(pos == 0, prev2, prev1))
        st_ref[...] = ug.reshape(n_seq, seq_len, ug.shape[1])[:, seq_len - (CONV_W - 1):, :]
    conv = cw_ref[0:1, :] * sh2 + cw_ref[1:2, :] * sh1 + cw_ref[2:3, :] * ug + cb_ref[...]
    gate = conv * _sigmoid(conv)
    uv = jnp.dot(h, wv_ref[...], preferred_element_type=F32)
    act_ref[...] = (gate * uv).astype(act_ref.dtype)


def _ffn_up(x, gain, w_up, cw, cb, carry, *, mode, seq_len, name):
    m, k = x.shape
    nf = w_up.shape[1] // 2
    tm = _pick_tile(m, (1024,))
    tn = _pick_tile(nf, (512, 256, 128))
    xs = pl.BlockSpec((tm, k), lambda i, j: (i, 0))
    gs = pl.BlockSpec((1, k), lambda i, j: (0, 0))
    wgs = pl.BlockSpec((k, tn), lambda i, j: (0, j))
    wvs = pl.BlockSpec((k, tn), lambda i, j: (0, j + nf // tn))
    cws = pl.BlockSpec((CONV_W, tn), lambda i, j: (0, j))
    cbs = pl.BlockSpec((1, tn), lambda i, j: (0, j))
    ts = pl.BlockSpec((tm, tn), lambda i, j: (i, j))
    in_specs = [xs, gs, wgs, wvs, cws, cbs]
    args = [x, gain, w_up, w_up, cw, cb]
    scratch = [pltpu.VMEM((tm, k), BF16)]
    if mode == "prompt":
        assert seq_len % tm == 0
        scratch.append(pltpu.VMEM((nf // tn, 8, tn), F32))
        st_spec = pl.BlockSpec((None, 8, tn), lambda i, j: (i, 0, j))
        st_shape = jax.ShapeDtypeStruct((m // tm, 8, nf), F32)
    else:
        assert tm % seq_len == 0 and seq_len == 8
        bs = pl.BlockSpec((tm // seq_len, CONV_W - 1, tn), lambda i, j: (i, 0, j))
        in_specs.append(bs)
        args.append(carry)
        st_spec = bs
        st_shape = jax.ShapeDtypeStruct((m // seq_len, CONV_W - 1, nf), F32)
    return pl.pallas_call(
        functools.partial(_ffn_up_body, mode=mode, seq_len=seq_len, tiles_per_seq=max(seq_len // tm, 1),
                          row_chunk=128),
        grid=(m // tm, nf // tn),
        in_specs=in_specs,
        out_specs=[ts, st_spec],
        out_shape=[jax.ShapeDtypeStruct((m, nf), BF16), st_shape],
        scratch_shapes=scratch,
        compiler_params=_cparams(2),
        name=name,
    )(*args)


def _pad_cols(w, n):
    return jnp.pad(w, ((0, 0), (0, n - w.shape[1])))


def _layer(x, mem_k, mem_v, s_gla, s_hgrn, conv_buf, p, *, mode, layer, n_seq, seq_len, tag):
    t, d = x.shape
    nm = lambda s: f"{s}_{tag}"
    ga_off, rank = p["ga_off"], p["rank"]
    w_in_t = p["w_in_t"]
    in_proj = functools.partial(_dense, x, w_in_t, gain=p["norm_mix_g"], prologue="norm", w_transposed=True,
                                tn=2048)
    za, zga = in_proj(out_dtype=BF16, name=nm("in_proj_a"), col_off=0, n_cols=ga_off, extra_cols=(ga_off, rank))
    zb = in_proj(out_dtype=BF16, name=nm("in_proj_b"), col_off=ga_off + rank,
                 n_cols=w_in_t.shape[0] - ga_off - rank)
    o_mix, s_gla_new, s_hgrn_new = _mixer(za, zb, zga, p, s_gla, s_hgrn, mode=mode, layer=layer, n_seq=n_seq,
                                          seq_len=seq_len, d_model=d, name=nm("mixer"))
    if mode == "prompt":
        x2 = _mid_prompt(o_mix, x, mem_k.astype(BF16), mem_v.astype(BF16), p["w_out"], p["w_mem_q"], p["w_mem_o"],
                         p["norm_xattn_g"], n_seq=n_seq, seq_len=seq_len, name=nm("mid"))
    else:
        x1 = _dense(o_mix, p["w_out"], res=x, prologue="plain", out_dtype=F32, name=nm("out_proj"))
        mq = _dense(x1, p["w_mem_q"], gain=p["norm_xattn_g"], prologue="norm", out_dtype=F32, name=nm("mem_q"))
        mo = _xattn_sample(mq, mem_k, mem_v, layer=layer, n_seq=n_seq, seq_len=seq_len, name=nm("xattn"))
        x2 = _dense(mo, p["w_mem_o"], res=x1, prologue="cast", out_dtype=F32, name=nm("mem_o"))

    nf, d_ff = p["nf"], p["d_ff"]
    carry = None if mode == "prompt" else jnp.pad(conv_buf[layer], ((0, 0), (0, 0), (0, nf - d_ff)))
    act, ug_rows = _ffn_up(x2, p["norm_ffn_g"], p["w_up"], p["ffn_conv_w"], p["ffn_conv_b"],
                           carry, mode=mode, seq_len=seq_len, name=nm("ffn_up"))
    final = p["norm_final_g"] if p["is_last"] else None
    y = _dense(act, p["w_down"], res=x2, prologue="plain", out_dtype=F32, name=nm("ffn_down"),
               final_gain=final, tm=512 if final is not None else None, tn=1024 if final is not None else None)

    if mode == "prompt":
        tiles_per_seq = ug_rows.shape[0] // n_seq
        conv_new = ug_rows[tiles_per_seq - 1::tiles_per_seq, 8 - (CONV_W - 1):, :d_ff]
    else:
        conv_new = ug_rows[:, :, :d_ff]
    return y, s_gla_new, s_hgrn_new, conv_new


def kernel(x_prompt, x_sample, mem_prompt, state_gla, state_hgrn, state_ffn_conv, cache_mem_k, cache_mem_v,
           norm_mix_g, w_in, gla_w_a_up, gla_b_a, gla_norm_g, hgrn_lower_bound, hgrn_norm_g, w_out,
           norm_xattn_g, norm_mem_g, w_mem_q, w_mem_k, w_mem_v, w_mem_o,
           norm_ffn_g, w_ffn_up, ffn_conv_w, ffn_conv_b, w_ffn_down, norm_final_g):
    depth = w_in.shape[0]
    bp, lp, d = x_prompt.shape
    bs, ls, _ = x_sample.shape
    n_mem = mem_prompt.shape[1]
    d_ff = w_ffn_down.shape[1]
    rank = gla_w_a_up.shape[1]
    qk = gla_w_a_up.shape[2]
    ga_off = 2 * qk + 2 * d
    nf = -(-d_ff // 512) * 512

    xp = x_prompt.reshape(bp * lp, d)
    xs = x_sample.reshape(bs * ls, d)
    mem = mem_prompt.reshape(bp * n_mem, d)
    row = lambda v: v.reshape(1, -1)

    outs = {k: [] for k in ("gla_p", "hgrn_p", "conv_p", "mk_p", "mv_p", "gla_s", "hgrn_s", "conv_s")}
    for l in range(depth):
        p = {
            "w_out": w_out[l].astype(BF16), "w_mem_q": w_mem_q[l].astype(BF16), "w_mem_o": w_mem_o[l].astype(BF16),
            "w_in_t": jnp.swapaxes(w_in[l], 0, 1).astype(BF16),
            "ga_off": ga_off, "rank": rank,
            "gla_w_a_up": gla_w_a_up[l].astype(BF16),
            "gla_b_a": row(gla_b_a[l]), "gla_norm_g": row(gla_norm_g[l]),
            "hgrn_lower_bound": hgrn_lower_bound, "hgrn_norm_g": row(hgrn_norm_g[l]),
            "norm_mix_g": row(norm_mix_g[l]), "norm_xattn_g": row(norm_xattn_g[l]),
            "norm_ffn_g": row(norm_ffn_g[l]), "norm_final_g": row(norm_final_g), "is_last": l == depth - 1,
            "w_up": jnp.concatenate([w_ffn_up[l][:, :d_ff].astype(BF16), jnp.zeros((d, nf - d_ff), BF16),
                                     w_ffn_up[l][:, d_ff:].astype(BF16), jnp.zeros((d, nf - d_ff), BF16)], axis=1),
            "ffn_conv_w": _pad_cols(ffn_conv_w[l], nf), "ffn_conv_b": _pad_cols(row(ffn_conv_b[l]), nf),
            "w_down": jnp.concatenate([w_ffn_down[l].astype(BF16), jnp.zeros((nf - d_ff, d), BF16)], axis=0),
            "d_ff": d_ff, "nf": nf,
        }
        g_mem = row(norm_mem_g[l])
        mk = _dense(mem, w_mem_k, gain=g_mem, prologue="norm", out_dtype=F32, name=f"mem_k_{l}", layer=l)
        mv = _dense(mem, w_mem_v, gain=g_mem, prologue="norm", out_dtype=F32, name=f"mem_v_{l}", layer=l)
        mk3, mv3 = mk.reshape(bp, n_mem, d), mv.reshape(bp, n_mem, d)
        xp, sg, sh, cb = _layer(xp, mk3, mv3, None, None, None, p, mode="prompt", layer=l,
                                n_seq=bp, seq_len=lp, tag=f"p{l}")
        outs["gla_p"].append(sg); outs["hgrn_p"].append(sh); outs["conv_p"].append(cb)
        outs["mk_p"].append(mk3.reshape(bp, n_mem, MEM_HEADS, d // MEM_HEADS))
        outs["mv_p"].append(mv3.reshape(bp, n_mem, MEM_HEADS, d // MEM_HEADS))
        xs, sg2, sh2, cb2 = _layer(xs, cache_mem_k, cache_mem_v, state_gla, state_hgrn, state_ffn_conv, p,
                                   mode="sample", layer=l, n_seq=bs, seq_len=ls, tag=f"s{l}")
        outs["gla_s"].append(sg2); outs["hgrn_s"].append(sh2); outs["conv_s"].append(cb2)

    y_prompt = xp.reshape(bp, lp, d)
    y_sample = xs.reshape(bs, ls, d)
    st = lambda k: jnp.stack(outs[k])
    return (y_prompt, y_sample, st("gla_p"), st("hgrn_p"), st("conv_p"), st("mk_p"), st("mv_p"),
            st("gla_s"), st("hgrn_s"), st("conv_s"))
```

```python
import functools
import math

import jax
import jax.numpy as jnp
from jax import lax
from jax.experimental import pallas as pl
from jax.experimental.pallas import tpu as pltpu

F32 = jnp.float32
BF16 = jnp.bfloat16

LANE = 128
BF16_ROWS = 16
V7X_VMEM_LIMIT = 56 * 1024 * 1024

EPS = 1e-6
GLA_HEADS = 4
GLA_GATE_NORM = 16.0
HGRN_DK = 128
MEM_HEADS = 4
CONV_W = 3
REF_CHUNK = 16
PROMPT_BLOCK = 64
PROMPT_STEP = 256
SAMPLE_SEQS = 2


def _cparams(n_axes):
    return pltpu.CompilerParams(dimension_semantics=("arbitrary",) * n_axes,
                                vmem_limit_bytes=V7X_VMEM_LIMIT)


def _sigmoid(x):
    return 0.5 * jnp.tanh(0.5 * x) + 0.5


def _pick_tile(n, candidates):
    for c in candidates:
        if n % c == 0:
            return c
    return n


def _dense_body(*refs, prologue, has_res, has_extra, has_final, row_chunk, w_transposed):
    it = iter(refs)
    x_ref = next(it)
    g_ref = next(it) if prologue == "norm" else None
    w_ref = next(it)
    we_ref = next(it) if has_extra else None
    r_ref = next(it) if has_res else None
    fg_ref = next(it) if has_final else None
    o_ref = next(it)
    oe_ref = next(it) if has_extra else None
    h_ref = next(it) if prologue != "plain" else None
    j = pl.program_id(1)
    nt = (((1,), (1,)), ((), ()))

    if prologue != "plain":
        @pl.when(j == 0)
        def _():
            def body(c, carry):
                rows = pl.ds(pl.multiple_of(c * row_chunk, row_chunk), row_chunk)
                xf = x_ref[rows, :].astype(F32)
                if prologue == "norm":
                    ms = jnp.mean(xf * xf, axis=-1, keepdims=True)
                    xf = xf * lax.rsqrt(ms + EPS) * g_ref[...]
                h_ref[rows, :] = xf.astype(BF16)
                return carry
            lax.fori_loop(0, x_ref.shape[0] // row_chunk, body, 0)
            if has_extra:
                oe_ref[...] = lax.dot_general(h_ref[...], we_ref[...], nt, preferred_element_type=F32)
        lhs = h_ref[...]
    else:
        lhs = x_ref[...]
    w = w_ref[...].astype(BF16)
    if w_transposed:
        acc = lax.dot_general(lhs, w, nt, preferred_element_type=F32)
    else:
        acc = jnp.dot(lhs, w, preferred_element_type=F32)
    if has_res:
        acc = acc + r_ref[...]
    if not has_final:
        o_ref[...] = acc.astype(o_ref.dtype)
    else:
        tn = acc.shape[1]
        o_ref[:, pl.ds(pl.multiple_of(j * tn, tn), tn)] = acc

        @pl.when(j == pl.num_programs(1) - 1)
        def _():
            def body(c, carry):
                rows = pl.ds(pl.multiple_of(c * row_chunk, row_chunk), row_chunk)
                xf = o_ref[rows, :]
                ms = jnp.mean(xf * xf, axis=-1, keepdims=True)
                o_ref[rows, :] = xf * lax.rsqrt(ms + EPS) * fg_ref[...]
                return carry
            lax.fori_loop(0, o_ref.shape[0] // row_chunk, body, 0)


def _dense(x, w, *, gain=None, res=None, prologue, out_dtype, name, layer=0, col_off=0, n_cols=None,
           w_transposed=False, extra_cols=None, final_gain=None, tm=None, tn=None):
    m = x.shape[0]
    k = x.shape[1] if w_transposed else w.shape[-2]
    assert k == x.shape[1] or (k % LANE == 0 and k < x.shape[1] and prologue == "plain")
    n = n_cols if n_cols is not None else w.shape[-1]
    if tm is None:
        tm = _pick_tile(m, (1024, 512, 256, 128, 64, 32, 16))
    if tn is None:
        tn = _pick_tile(n, (1024, 512, 256, 128) if k <= 2048 else (512, 256, 128))
    assert m % tm == 0 and n % tn == 0
    row_chunk = min(tm, 128)
    in_specs = [pl.BlockSpec((tm, k), lambda i, j: (i, 0))]
    args = [x]
    if prologue == "norm":
        in_specs.append(pl.BlockSpec((1, k), lambda i, j: (0, 0)))
        args.append(gain)
    if w_transposed:
        assert col_off % BF16_ROWS == 0 and tn % BF16_ROWS == 0
        in_specs.append(pl.BlockSpec((pl.Element(tn), pl.Element(k)),
                                     lambda i, j: (pl.multiple_of(col_off + j * tn, BF16_ROWS), 0)))
    elif w.ndim == 3:
        assert col_off % tn == 0
        in_specs.append(pl.BlockSpec((None, k, tn), lambda i, j: (layer, 0, j + col_off // tn)))
    else:
        in_specs.append(pl.BlockSpec((k, tn), lambda i, j: (0, j)))
    args.append(w)
    out_specs = [pl.BlockSpec((tm, tn), lambda i, j: (i, j))]
    out_shape = [jax.ShapeDtypeStruct((m, n), out_dtype)]
    if extra_cols is not None:
        e_off, e_n = extra_cols
        assert w_transposed and prologue != "plain" and e_off % BF16_ROWS == 0 and e_n % BF16_ROWS == 0
        in_specs.append(pl.BlockSpec((pl.Element(e_n), pl.Element(k)), lambda i, j: (e_off, 0)))
        args.append(w)
        out_specs.append(pl.BlockSpec((tm, e_n), lambda i, j: (i, 0)))
        out_shape.append(jax.ShapeDtypeStruct((m, e_n), F32))
    if res is not None:
        in_specs.append(pl.BlockSpec((tm, tn), lambda i, j: (i, j)))
        args.append(res)
    if final_gain is not None:
        assert out_dtype == F32
        in_specs.append(pl.BlockSpec((1, n), lambda i, j: (0, 0)))
        args.append(final_gain)
        out_specs[0] = pl.BlockSpec((tm, n), lambda i, j: (i, 0))
    scratch = [] if prologue == "plain" else [pltpu.VMEM((tm, k), BF16)]
    outs = pl.pallas_call(
        functools.partial(_dense_body, prologue=prologue, has_res=res is not None,
                          has_extra=extra_cols is not None, has_final=final_gain is not None,
                          row_chunk=row_chunk, w_transposed=w_transposed),
        grid=(m // tm, n // tn),
        in_specs=in_specs,
        out_specs=out_specs,
        out_shape=out_shape,
        scratch_shapes=scratch,
        compiler_params=_cparams(2),
        name=name,
    )(*args)
    return outs if extra_cols is not None else outs[0]


def _cumsum_rows(x, group):
    rows = lax.broadcasted_iota(jnp.int32, x.shape, 0) % group
    shift = 1
    while shift < group:
        rolled = pltpu.roll(x, shift, axis=0)
        x = x + jnp.where(rows >= shift, rolled, 0.0)
        shift *= 2
    return x


def _cumsum_rows_mxu(x, group):
    n = x.shape[0]
    row = lax.broadcasted_iota(jnp.int32, (n, n), 0)
    col = lax.broadcasted_iota(jnp.int32, (n, n), 1)
    tri = jnp.where(jnp.logical_and(col <= row, col // group == row // group), 1.0, 0.0).astype(BF16)
    hi = x.astype(BF16)
    lo = (x - hi.astype(F32)).astype(BF16)
    return (jnp.dot(tri, hi, preferred_element_type=F32) + jnp.dot(tri, lo, preferred_element_type=F32))


def _lockstep(gens):
    gens = list(gens)
    results = [None] * len(gens)
    live = list(range(len(gens)))
    while live:
        still = []
        for n in live:
            try:
                next(gens[n])
                still.append(n)
            except StopIteration as stop:
                results[n] = stop.value
        live = still
    return results


def _recurrence_block(q, k, v, la, s_read, *, sub, mm_dtype):
    bt, kd = q.shape
    vd = v.shape[1]
    ns = bt // sub
    b_loc = _cumsum_rows_mxu(la, sub) if mm_dtype == BF16 else _cumsum_rows(la, sub)
    qd = q * jnp.exp(b_loc)
    ki = k * jnp.exp(-b_loc)
    tot = [b_loc[(i + 1) * sub - 1:(i + 1) * sub, :] for i in range(ns)]
    pre = [jnp.zeros((1, kd), F32)]
    for i in range(ns):
        pre.append(pre[-1] + tot[i])
    sl = [slice(i * sub, (i + 1) * sub) for i in range(ns)]
    vb = v.astype(mm_dtype)
    qdm = qd.astype(mm_dtype)
    q_state = jnp.concatenate([qd[sl[i]] * jnp.exp(pre[i]) for i in range(ns)], axis=0) if ns > 1 else qd
    q_state = q_state.astype(mm_dtype)
    ke = [ki[sl[i]] * jnp.exp(tot[i]) for i in range(ns)]
    kmats = []
    for i in range(ns):
        parts = [ke[j] if j == i - 1 else ke[j] * jnp.exp(pre[i] - pre[j + 1]) for j in range(i)]
        parts.append(ki[sl[i]])
        parts += [jnp.zeros((sub, kd), F32)] * (ns - 1 - i)
        kmats.append((jnp.concatenate(parts, axis=0) if ns > 1 else parts[0]).astype(mm_dtype))
    k_end = jnp.concatenate([ke[i] if i == ns - 1 else ke[i] * jnp.exp(pre[ns] - pre[i + 1])
                             for i in range(ns)], axis=0) if ns > 1 else ke[0]
    k_end = k_end.astype(mm_dtype)
    decay = jnp.broadcast_to(jnp.exp(pre[ns]), (LANE, kd)).T
    yield

    s_prev = s_read()
    o_state = jnp.dot(q_state, s_prev.astype(mm_dtype), preferred_element_type=F32)
    a_rows = [lax.dot_general(qdm[sl[i]], kmats[i], (((1,), (1,)), ((), ())), preferred_element_type=F32)
              for i in range(ns)]
    ds = lax.dot_general(k_end, vb, (((0,), (0,)), ((), ())), preferred_element_type=F32)
    yield

    a = jnp.concatenate(a_rows, axis=0) if ns > 1 else a_rows[0]
    row = lax.broadcasted_iota(jnp.int32, a.shape, 0)
    col = lax.broadcasted_iota(jnp.int32, a.shape, 1)
    a = jnp.where(col <= row, a, 0.0).astype(mm_dtype)
    o = jnp.dot(a, vb, preferred_element_type=F32) + o_state
    s_new = jnp.concatenate([decay * s_prev[:, c * LANE:(c + 1) * LANE] for c in range(vd // LANE)],
                            axis=1) + ds
    yield
    return o, s_new


def _head_gated_norm(o, gnorm, gate):
    ms = jnp.mean(o * o, axis=-1, keepdims=True)
    return o * lax.rsqrt(ms + EPS) * gnorm * (gate * _sigmoid(gate))


def _log_sigmoid(x):
    return jnp.minimum(x, 0.0) - jnp.log(1.0 + jnp.exp(-jnp.abs(x)))


def _mixer_body(*refs, mode, layer, gla_heads, gla_dk, gla_dv, hgrn_heads, hgrn_dk, block, sub, mm_dtype):
    (q_ref, k_ref, v_ref, g_ref, ga_ref, wup_ref, ba_ref, gng_ref,
     hq_ref, hf_ref, hi_ref, hg_ref, ma_ref, mb_ref, lb_ref, gnh_ref) = refs[:16]
    if mode == "prompt":
        om_ref, sg_out, sh_out = refs[16:]
        sg_in = sh_in = None
    else:
        sg_in, sh_in, om_ref, sg_out, sh_out = refs[16:]

    p = lb_ref[...]
    e = jnp.exp(p - jnp.max(p, axis=0, keepdims=True))
    lb_all = jnp.sum(e[:layer + 1], axis=0, keepdims=True) / jnp.sum(e, axis=0, keepdims=True)

    def gla_head(h, load, s_read, shared):
        ks = slice(h * gla_dk, (h + 1) * gla_dk)
        vs = slice(h * gla_dv, (h + 1) * gla_dv)
        ga = load(ga_ref, slice(None)).astype(BF16)
        a_logit = jnp.dot(ga, wup_ref[:, ks], preferred_element_type=F32) + ba_ref[:, ks]
        yield
        la = _log_sigmoid(a_logit) * (1.0 / GLA_GATE_NORM)
        q = load(q_ref, ks) * (gla_dk ** -0.5)
        o, s_new = yield from _recurrence_block(q, load(k_ref, ks), load(v_ref, vs), la, s_read,
                                                sub=sub, mm_dtype=mm_dtype)
        shared[h] = _head_gated_norm(o, gng_ref[:, vs], load(g_ref, vs))
        return s_new

    def hgrn_head(h, load, s_read, shared):
        ks = slice(h * hgrn_dk, (h + 1) * hgrn_dk)
        lb = lb_all[:, ks]
        f = lb + (1.0 - lb) * _sigmoid(load(hf_ref, ks))
        hq = load(hq_ref, ks)
        q = hq * _sigmoid(hq) * (hgrn_dk ** -0.5)
        yield
        o, s_new = yield from _recurrence_block(q, 1.0 - f, load(hi_ref, ks), jnp.log(f), s_read,
                                                sub=sub, mm_dtype=mm_dtype)
        o_h = _head_gated_norm(o, gnh_ref[:, ks], load(hg_ref, ks))
        per = gla_dv // hgrn_dk
        o_gla = shared[h // per][:, (h % per) * hgrn_dk:(h % per + 1) * hgrn_dk]
        mix = _sigmoid(load(ma_ref, ks)) * o_gla + _sigmoid(load(mb_ref, ks)) * o_h
        return mix, s_new

    def run(load, sg_read, sh_read):
        shared = {}
        gens = [gla_head(h, load, functools.partial(sg_read, h), shared) for h in range(gla_heads)]
        gens += [hgrn_head(h, load, functools.partial(sh_read, h), shared) for h in range(hgrn_heads)]
        res = _lockstep(gens)
        return res[:gla_heads], res[gla_heads:]

    if mode == "prompt":
        @pl.when(pl.program_id(1) == 0)
        def _():
            sg_out[...] = jnp.zeros(sg_out.shape, F32)
            sh_out[...] = jnp.zeros(sh_out.shape, F32)

        def body(c, carry):
            rows = pl.ds(pl.multiple_of(c * block, block), block)
            load = lambda ref, cols: ref[rows, cols].astype(F32)
            g_res, h_res = run(load, lambda h: sg_out[0, h], lambda h: sh_out[0, h])
            for h, s_new in enumerate(g_res):
                sg_out[0, h] = s_new
            for h, (mix, s_new) in enumerate(h_res):
                sh_out[0, h] = s_new
                om_ref[rows, h * hgrn_dk:(h + 1) * hgrn_dk] = mix.astype(om_ref.dtype)
            return carry
        lax.fori_loop(0, q_ref.shape[0] // block, body, 0)
    else:
        n_seq = q_ref.shape[0] // block
        mixes = []
        for s in range(n_seq):
            load = lambda ref, cols, s=s: ref[:, cols].astype(F32)[s * block:(s + 1) * block]
            g_res, h_res = run(load, lambda h, s=s: sg_in[s, h], lambda h, s=s: sh_in[s, h])
            for h, s_new in enumerate(g_res):
                sg_out[s, h] = s_new
            for h, (mix, s_new) in enumerate(h_res):
                sh_out[s, h] = s_new
            mixes.append([mix for mix, _ in h_res])
        for h in range(hgrn_heads):
            om_ref[:, h * hgrn_dk:(h + 1) * hgrn_dk] = jnp.concatenate(
                [mixes[s][h] for s in range(n_seq)], axis=0).astype(om_ref.dtype)


def _mixer(za, zb, zga, p, s_gla, s_hgrn, *, mode, layer, n_seq, seq_len, d_model, name):
    gh, hk = GLA_HEADS, HGRN_DK
    gk, gv = d_model // 2 // gh, d_model // gh
    hh = d_model // hk
    t = za.shape[0]
    if mode == "prompt":
        step = PROMPT_STEP
        n_t = seq_len // step
        grid = (n_seq, n_t)
        block, sub, mm_dtype = PROMPT_BLOCK, REF_CHUNK, BF16
        rowmap = lambda col: (lambda b, i: (b * n_t + i, col))
        const = lambda b, i: (0, 0)
        st = lambda *dims: pl.BlockSpec((1,) + dims, lambda b, i: (b, 0, 0, 0))
    else:
        step = SAMPLE_SEQS * seq_len
        grid = (n_seq // SAMPLE_SEQS,)
        block = seq_len
        sub, mm_dtype = math.gcd(REF_CHUNK, seq_len), F32
        rowmap = lambda col: (lambda i: (i, col))
        const = lambda i: (0, 0)
        st = lambda *dims: pl.BlockSpec((SAMPLE_SEQS,) + dims, lambda i: (i, 0, 0, 0))
    zs = lambda width, col: pl.BlockSpec((step, width), rowmap(col))
    whole = lambda a: pl.BlockSpec(a.shape, const)
    kw = gh * gk
    in_specs = [zs(kw, 0), zs(kw, 1), zs(d_model, 1), zs(d_model, 2), zs(zga.shape[1], 0),
                whole(p["gla_w_a_up"]), whole(p["gla_b_a"]), whole(p["gla_norm_g"])]
    in_specs += [zs(d_model, c) for c in range(6)] + [whole(p["hgrn_lower_bound"]), whole(p["hgrn_norm_g"])]
    args = [za, za, za, za, zga, p["gla_w_a_up"], p["gla_b_a"], p["gla_norm_g"],
            zb, zb, zb, zb, zb, zb, p["hgrn_lower_bound"], p["hgrn_norm_g"]]
    if mode != "prompt":
        in_specs += [pl.BlockSpec((None, SAMPLE_SEQS, gh, gk, gv), lambda i: (layer, i, 0, 0, 0)),
                     pl.BlockSpec((None, SAMPLE_SEQS, hh, hk, hk), lambda i: (layer, i, 0, 0, 0))]
        args += [s_gla, s_hgrn]
    return pl.pallas_call(
        functools.partial(_mixer_body, mode=mode, layer=layer, gla_heads=gh, gla_dk=gk, gla_dv=gv,
                          hgrn_heads=hh, hgrn_dk=hk, block=block, sub=sub, mm_dtype=mm_dtype),
        grid=grid,
        in_specs=in_specs,
        out_specs=[zs(d_model, 0), st(gh, gk, gv), st(hh, hk, hk)],
        out_shape=[jax.ShapeDtypeStruct((t, d_model), BF16),
                   jax.ShapeDtypeStruct((n_seq, gh, gk, gv), F32),
                   jax.ShapeDtypeStruct((n_seq, hh, hk, hk), F32)],
        compiler_params=_cparams(len(grid)),
        name=name,
    )(*args)


def _xattn_sample_body(q_ref, k_ref, v_ref, o_ref, *, heads, dh, seq_len):
    scale = dh ** -0.5
    n_mem = k_ref.shape[1]
    for s in range(k_ref.shape[0]):
        rows = slice(s * seq_len, (s + 1) * seq_len)
        q = q_ref[rows, :]
        q2 = jnp.concatenate([q[:, h * dh:(h + 1) * dh] for h in range(heads)], axis=0)
        k2 = k_ref[s].reshape(n_mem * heads, dh)
        v2 = v_ref[s].reshape(n_mem * heads, dh)
        sc = lax.dot_general(q2, k2, (((1,), (1,)), ((), ())), preferred_element_type=F32) * scale
        q_head = lax.broadcasted_iota(jnp.int32, sc.shape, 0) // seq_len
        k_head = lax.broadcasted_iota(jnp.int32, sc.shape, 1) % heads
        sc = jnp.where(q_head == k_head, sc, -jnp.inf)
        p = jnp.exp(sc - jnp.max(sc, axis=-1, keepdims=True))
        o2 = jnp.dot(p, v2, preferred_element_type=F32) / jnp.sum(p, axis=-1, keepdims=True)
        for h in range(heads):
            o_ref[rows, h * dh:(h + 1) * dh] = o2[h * seq_len:(h + 1) * seq_len].astype(o_ref.dtype)


def _xattn_sample(mq, mem_k, mem_v, *, layer, n_seq, seq_len, name):
    t, d = mq.shape
    dh = d // MEM_HEADS
    n_mem = mem_k.shape[2]
    q_spec = pl.BlockSpec((SAMPLE_SEQS * seq_len, d), lambda i: (i, 0))
    kv_spec = pl.BlockSpec((None, SAMPLE_SEQS, n_mem, MEM_HEADS, dh), lambda i: (layer, i, 0, 0, 0))
    return pl.pallas_call(
        functools.partial(_xattn_sample_body, heads=MEM_HEADS, dh=dh, seq_len=seq_len),
        grid=(n_seq // SAMPLE_SEQS,),
        in_specs=[q_spec, kv_spec, kv_spec],
        out_specs=q_spec,
        out_shape=jax.ShapeDtypeStruct((t, d), F32),
        compiler_params=_cparams(1),
        name=name,
    )(mq, mem_k, mem_v)


def _mid_body(om_ref, x_ref, k_ref, v_ref, wo_ref, wq_ref, wm_ref, g_ref, o_ref, *, heads, dh):
    scale = dh ** -0.5
    x1 = x_ref[...] + jnp.dot(om_ref[...], wo_ref[...], preferred_element_type=F32)
    ms = jnp.mean(x1 * x1, axis=-1, keepdims=True)
    hx = (x1 * lax.rsqrt(ms + EPS) * g_ref[...]).astype(BF16)
    mq = jnp.dot(hx, wq_ref[...], preferred_element_type=F32).astype(BF16)
    heads_out = []
    for h in range(heads):
        hs = slice(h * dh, (h + 1) * dh)
        sc = lax.dot_general(mq[:, hs], k_ref[0, :, hs], (((1,), (1,)), ((), ())),
                             preferred_element_type=F32) * scale
        p = jnp.exp(sc - jnp.max(sc, axis=-1, keepdims=True))
        o = jnp.dot(p.astype(BF16), v_ref[0, :, hs], preferred_element_type=F32) / jnp.sum(p, axis=-1, keepdims=True)
        heads_out.append(o.astype(BF16))
    mo = jnp.concatenate(heads_out, axis=1)
    o_ref[...] = x1 + jnp.dot(mo, wm_ref[...], preferred_element_type=F32)


def _mid_prompt(o_mix, x, mem_k, mem_v, w_out, w_q, w_o, gain, *, n_seq, seq_len, name):
    t, d = x.shape
    n_mem = mem_k.shape[1]
    tm = 256
    n_t = seq_len // tm
    rows = lambda dtype_rows: pl.BlockSpec((tm, d), lambda b, i: (b * n_t + i, 0))
    kv = pl.BlockSpec((1, n_mem, d), lambda b, i: (b, 0, 0))
    wspec = pl.BlockSpec((d, d), lambda b, i: (0, 0), pipeline_mode=pl.Buffered(1))
    return pl.pallas_call(
        functools.partial(_mid_body, heads=MEM_HEADS, dh=d // MEM_HEADS),
        grid=(n_seq, n_t),
        in_specs=[rows(BF16), rows(F32), kv, kv, wspec, wspec, wspec, pl.BlockSpec((1, d), lambda b, i: (0, 0))],
        out_specs=rows(F32),
        out_shape=jax.ShapeDtypeStruct((t, d), F32),
        compiler_params=_cparams(2),
        name=name,
    )(o_mix, x, mem_k, mem_v, w_out, w_q, w_o, gain)


def _ffn_up_body(*refs, mode, seq_len, tiles_per_seq, row_chunk, last_shift):
    if mode == "prompt":
        x_ref, g_ref, wg_ref, wv_ref, cw_ref, cb_ref, act_ref, st_ref, h_ref, tail_ref = refs
        p1_ref = p2_ref = None
    else:
        x_ref, g_ref, wg_ref, wv_ref, cw_ref, cb_ref, buf_ref, act_ref, st_ref, h_ref = refs
        tail_ref = None
    i = pl.program_id(0)
    j = pl.program_id(1)

    @pl.when(j == 0)
    def _():
        def body(c, carry):
            rows = pl.ds(pl.multiple_of(c * row_chunk, row_chunk), row_chunk)
            xf = x_ref[rows, :]
            ms = jnp.mean(xf * xf, axis=-1, keepdims=True)
            h_ref[rows, :] = (xf * lax.rsqrt(ms + EPS) * g_ref[...]).astype(BF16)
            return carry
        lax.fori_loop(0, x_ref.shape[0] // row_chunk, body, 0)

    if mode == "prompt":
        @pl.when(i % tiles_per_seq == 0)
        def _():
            tail_ref[j] = jnp.zeros(tail_ref.shape[1:], F32)

    def step(shift):
        def place(a):
            if shift == 0:
                return a
            return jnp.concatenate([a[..., shift:], jnp.zeros(a.shape[:-1] + (shift,), a.dtype)], axis=-1)

        h = h_ref[...]
        ug = jnp.dot(h, wg_ref[...], preferred_element_type=F32)
        tm = ug.shape[0]
        row = lax.broadcasted_iota(jnp.int32, ug.shape, 0)
        roll1 = pltpu.roll(ug, 1, axis=0)
        roll2 = pltpu.roll(ug, 2, axis=0)
        if mode == "prompt":
            tail = tail_ref[j]
            prev1 = tail[7:8, :]
            prev2 = tail[6:7, :]
            sh1 = jnp.where(row >= 1, roll1, prev1)
            sh2 = jnp.where(row >= 2, roll2, jnp.where(row == 0, prev2, prev1))
            tail_ref[j] = ug[tm - 8:, :]
            st_ref[...] = place(ug[tm - 8:, :])
        else:
            buf = buf_ref[...]
            n_seq = buf.shape[0]
            spread = lambda r: jnp.broadcast_to(buf[:, r:r + 1, :],
                                                (n_seq, seq_len, buf.shape[2])).reshape(ug.shape)
            prev2, prev1 = spread(0), spread(1)
            pos = row % seq_len
            sh1 = jnp.where(pos >= 1, roll1, prev1)
            sh2 = jnp.where(pos >= 2, roll2, jnp.where(pos == 0, prev2, prev1))
            st_ref[...] = place(ug.reshape(n_seq, seq_len, ug.shape[1])[:, seq_len - (CONV_W - 1):, :])
        conv = cw_ref[0:1, :] * sh2 + cw_ref[1:2, :] * sh1 + cw_ref[2:3, :] * ug + cb_ref[...]
        gate = conv * _sigmoid(conv)
        uv = jnp.dot(h, wv_ref[...], preferred_element_type=F32)
        act_ref[...] = place((gate * uv).astype(act_ref.dtype))

    if last_shift == 0:
        step(0)
    else:
        last = pl.num_programs(1) - 1
        pl.when(j < last)(functools.partial(step, 0))
        pl.when(j == last)(functools.partial(step, last_shift))


def _ffn_up(x, gain, w_up, cw, cb, carry, *, mode, seq_len, name):
    m, k = x.shape
    d_ff = w_up.shape[1] // 2
    tm = _pick_tile(m, (1024,))
    tn = 4 * LANE
    nf = -(-d_ff // tn) * tn
    last_shift = nf - d_ff
    assert d_ff % LANE == 0 and d_ff >= tn
    col = lambda j, base=0: pl.multiple_of(base + jnp.minimum(j * tn, d_ff - tn), LANE)
    elem = lambda *dims: tuple(pl.Element(n) for n in dims)
    xs = pl.BlockSpec((tm, k), lambda i, j: (i, 0))
    gs = pl.BlockSpec((1, k), lambda i, j: (0, 0))
    wgs = pl.BlockSpec(elem(k, tn), lambda i, j: (0, col(j)))
    wvs = pl.BlockSpec(elem(k, tn), lambda i, j: (0, col(j, d_ff)))
    cws = pl.BlockSpec(elem(CONV_W, tn), lambda i, j: (0, col(j)))
    cbs = pl.BlockSpec(elem(1, tn), lambda i, j: (0, col(j)))
    ts = pl.BlockSpec((tm, tn), lambda i, j: (i, j))
    in_specs = [xs, gs, wgs, wvs, cws, cbs]
    args = [x, gain, w_up, w_up, cw, cb]
    scratch = [pltpu.VMEM((tm, k), BF16)]
    if mode == "prompt":
        assert seq_len % tm == 0
        scratch.append(pltpu.VMEM((nf // tn, 8, tn), F32))
        st_spec = pl.BlockSpec((None, 8, tn), lambda i, j: (i, 0, j))
        st_shape = jax.ShapeDtypeStruct((m // tm, 8, nf), F32)
    else:
        assert tm % seq_len == 0 and seq_len == 8
        n_blk = tm // seq_len
        in_specs.append(pl.BlockSpec(elem(n_blk, CONV_W - 1, tn), lambda i, j: (i * n_blk, 0, col(j))))
        args.append(carry)
        st_spec = pl.BlockSpec((n_blk, CONV_W - 1, tn), lambda i, j: (i, 0, j))
        st_shape = jax.ShapeDtypeStruct((m // seq_len, CONV_W - 1, nf), F32)
    return pl.pallas_call(
        functools.partial(_ffn_up_body, mode=mode, seq_len=seq_len, tiles_per_seq=max(seq_len // tm, 1),
                          row_chunk=128, last_shift=last_shift),
        grid=(m // tm, nf // tn),
        in_specs=in_specs,
        out_specs=[ts, st_spec],
        out_shape=[jax.ShapeDtypeStruct((m, nf), BF16), st_shape],
        scratch_shapes=scratch,
        compiler_params=_cparams(2),
        name=name,
    )(*args)


def _layer(x, mem_k, mem_v, s_gla, s_hgrn, conv_buf, p, *, mode, layer, n_seq, seq_len, tag):
    t, d = x.shape
    nm = lambda s: f"{s}_{tag}"
    ga_off, rank = p["ga_off"], p["rank"]
    w_in_t = p["w_in_t"]
    in_proj = functools.partial(_dense, x, w_in_t, gain=p["norm_mix_g"], prologue="norm", w_transposed=True,
                                tn=d)
    za, zga = in_proj(out_dtype=BF16, name=nm("in_proj_a"), col_off=0, n_cols=ga_off, extra_cols=(ga_off, rank))
    zb = in_proj(out_dtype=BF16, name=nm("in_proj_b"), col_off=ga_off + rank,
                 n_cols=w_in_t.shape[0] - ga_off - rank)
    o_mix, s_gla_new, s_hgrn_new = _mixer(za, zb, zga, p, s_gla, s_hgrn, mode=mode, layer=layer, n_seq=n_seq,
                                          seq_len=seq_len, d_model=d, name=nm("mixer"))
    if mode == "prompt":
        x2 = _mid_prompt(o_mix, x, mem_k.astype(BF16), mem_v.astype(BF16), p["w_out"], p["w_mem_q"], p["w_mem_o"],
                         p["norm_xattn_g"], n_seq=n_seq, seq_len=seq_len, name=nm("mid"))
    else:
        x1 = _dense(o_mix, p["w_out"], res=x, prologue="plain", out_dtype=F32, name=nm("out_proj"))
        mq = _dense(x1, p["w_mem_q"], gain=p["norm_xattn_g"], prologue="norm", out_dtype=F32, name=nm("mem_q"))
        mo = _xattn_sample(mq, mem_k, mem_v, layer=layer, n_seq=n_seq, seq_len=seq_len, name=nm("xattn"))
        x2 = _dense(mo, p["w_mem_o"], res=x1, prologue="cast", out_dtype=F32, name=nm("mem_o"))

    d_ff = p["d_ff"]
    carry = None if mode == "prompt" else conv_buf[layer]
    act, ug_rows = _ffn_up(x2, p["norm_ffn_g"], p["w_up"], p["ffn_conv_w"], p["ffn_conv_b"],
                           carry, mode=mode, seq_len=seq_len, name=nm("ffn_up"))
    final = p["norm_final_g"] if p["is_last"] else None
    y = _dense(act, p["w_down"], res=x2, prologue="plain", out_dtype=F32, name=nm("ffn_down"),
               final_gain=final, tm=512 if final is not None else None, tn=1024 if final is not None else None)

    if mode == "prompt":
        tiles_per_seq = ug_rows.shape[0] // n_seq
        conv_new = ug_rows[tiles_per_seq - 1::tiles_per_seq, 8 - (CONV_W - 1):, :d_ff]
    else:
        conv_new = ug_rows[:, :, :d_ff]
    return y, s_gla_new, s_hgrn_new, conv_new


def kernel(x_prompt, x_sample, mem_prompt, state_gla, state_hgrn, state_ffn_conv, cache_mem_k, cache_mem_v,
           norm_mix_g, w_in, gla_w_a_up, gla_b_a, gla_norm_g, hgrn_lower_bound, hgrn_norm_g, w_out,
           norm_xattn_g, norm_mem_g, w_mem_q, w_mem_k, w_mem_v, w_mem_o,
           norm_ffn_g, w_ffn_up, ffn_conv_w, ffn_conv_b, w_ffn_down, norm_final_g):
    depth = w_in.shape[0]
    bp, lp, d = x_prompt.shape
    bs, ls, _ = x_sample.shape
    n_mem = mem_prompt.shape[1]
    d_ff = w_ffn_down.shape[1]
    rank = gla_w_a_up.shape[1]
    qk = gla_w_a_up.shape[2]
    ga_off = 2 * qk + 2 * d

    xp = x_prompt.reshape(bp * lp, d)
    xs = x_sample.reshape(bs * ls, d)
    mem = mem_prompt.reshape(bp * n_mem, d)
    row = lambda v: v.reshape(1, -1)

    outs = {k: [] for k in ("gla_p", "hgrn_p", "conv_p", "mk_p", "mv_p", "gla_s", "hgrn_s", "conv_s")}
    for l in range(depth):
        p = {
            "w_out": w_out[l].astype(BF16), "w_mem_q": w_mem_q[l].astype(BF16), "w_mem_o": w_mem_o[l].astype(BF16),
            "w_in_t": jnp.swapaxes(w_in[l], 0, 1).astype(BF16),
            "ga_off": ga_off, "rank": rank,
            "gla_w_a_up": gla_w_a_up[l].astype(BF16),
            "gla_b_a": row(gla_b_a[l]), "gla_norm_g": row(gla_norm_g[l]),
            "hgrn_lower_bound": hgrn_lower_bound, "hgrn_norm_g": row(hgrn_norm_g[l]),
            "norm_mix_g": row(norm_mix_g[l]), "norm_xattn_g": row(norm_xattn_g[l]),
            "norm_ffn_g": row(norm_ffn_g[l]), "norm_final_g": row(norm_final_g), "is_last": l == depth - 1,
            "w_up": w_ffn_up[l].astype(BF16), "w_down": w_ffn_down[l].astype(BF16),
            "ffn_conv_w": ffn_conv_w[l], "ffn_conv_b": row(ffn_conv_b[l]),
            "d_ff": d_ff,
        }
        g_mem = row(norm_mem_g[l])
        mk = _dense(mem, w_mem_k, gain=g_mem, prologue="norm", out_dtype=F32, name=f"mem_k_{l}", layer=l)
        mv = _dense(mem, w_mem_v, gain=g_mem, prologue="norm", out_dtype=F32, name=f"mem_v_{l}", layer=l)
        mk3, mv3 = mk.reshape(bp, n_mem, d), mv.reshape(bp, n_mem, d)
        xp, sg, sh, cb = _layer(xp, mk3, mv3, None, None, None, p, mode="prompt", layer=l,
                                n_seq=bp, seq_len=lp, tag=f"p{l}")
        outs["gla_p"].append(sg); outs["hgrn_p"].append(sh); outs["conv_p"].append(cb)
        outs["mk_p"].append(mk3.reshape(bp, n_mem, MEM_HEADS, d // MEM_HEADS))
        outs["mv_p"].append(mv3.reshape(bp, n_mem, MEM_HEADS, d // MEM_HEADS))
        xs, sg2, sh2, cb2 = _layer(xs, cache_mem_k, cache_mem_v, state_gla, state_hgrn, state_ffn_conv, p,
                                   mode="sample", layer=l, n_seq=bs, seq_len=ls, tag=f"s{l}")
        outs["gla_s"].append(sg2); outs["hgrn_s"].append(sh2); outs["conv_s"].append(cb2)

    y_prompt = xp.reshape(bp, lp, d)
    y_sample = xs.reshape(bs, ls, d)
    st = lambda k: jnp.stack(outs[k])
    return (y_prompt, y_sample, st("gla_p"), st("hgrn_p"), st("conv_p"), st("mk_p"), st("mv_p"),
            st("gla_s"), st("hgrn_s"), st("conv_s"))
```

```python
import functools
import math

import jax
import jax.numpy as jnp
from jax import lax
from jax.experimental import pallas as pl
from jax.experimental.pallas import tpu as pltpu

F32 = jnp.float32
BF16 = jnp.bfloat16

LANE = 128
BF16_ROWS = 16
V7X_VMEM_LIMIT = 56 * 1024 * 1024

EPS = 1e-6
GLA_HEADS = 4
GLA_GATE_NORM = 16.0
HGRN_DK = 128
MEM_HEADS = 4
CONV_W = 3
REF_CHUNK = 16
PROMPT_BLOCK = 64
PROMPT_STEP = 256
SAMPLE_SEQS = 2


def _cparams(n_axes):
    return pltpu.CompilerParams(dimension_semantics=("arbitrary",) * n_axes,
                                vmem_limit_bytes=V7X_VMEM_LIMIT)


def _sigmoid(x):
    return 0.5 * jnp.tanh(0.5 * x) + 0.5


def _pick_tile(n, candidates):
    for c in candidates:
        if n % c == 0:
            return c
    return n


def _dense_body(*refs, prologue, has_res, has_extra, has_final, row_chunk, w_transposed):
    it = iter(refs)
    x_ref = next(it)
    g_ref = next(it) if prologue == "norm" else None
    w_ref = next(it)
    we_ref = next(it) if has_extra else None
    r_ref = next(it) if has_res else None
    fg_ref = next(it) if has_final else None
    o_ref = next(it)
    oe_ref = next(it) if has_extra else None
    h_ref = next(it) if prologue != "plain" else None
    j = pl.program_id(1)
    nt = (((1,), (1,)), ((), ()))

    if prologue != "plain":
        @pl.when(j == 0)
        def _():
            def body(c, carry):
                rows = pl.ds(pl.multiple_of(c * row_chunk, row_chunk), row_chunk)
                xf = x_ref[rows, :].astype(F32)
                if prologue == "norm":
                    ms = jnp.mean(xf * xf, axis=-1, keepdims=True)
                    xf = xf * lax.rsqrt(ms + EPS) * g_ref[...]
                h_ref[rows, :] = xf.astype(BF16)
                return carry
            lax.fori_loop(0, x_ref.shape[0] // row_chunk, body, 0)
            if has_extra:
                oe_ref[...] = lax.dot_general(h_ref[...], we_ref[...], nt, preferred_element_type=F32)
        lhs = h_ref[...]
    else:
        lhs = x_ref[...]
    w = w_ref[...].astype(BF16)
    if w_transposed:
        acc = lax.dot_general(lhs, w, nt, preferred_element_type=F32)
    else:
        acc = jnp.dot(lhs, w, preferred_element_type=F32)
    if has_res:
        acc = acc + r_ref[...]
    if not has_final:
        o_ref[...] = acc.astype(o_ref.dtype)
    else:
        tn = acc.shape[1]
        o_ref[:, pl.ds(pl.multiple_of(j * tn, tn), tn)] = acc

        @pl.when(j == pl.num_programs(1) - 1)
        def _():
            def body(c, carry):
                rows = pl.ds(pl.multiple_of(c * row_chunk, row_chunk), row_chunk)
                xf = o_ref[rows, :]
                ms = jnp.mean(xf * xf, axis=-1, keepdims=True)
                o_ref[rows, :] = xf * lax.rsqrt(ms + EPS) * fg_ref[...]
                return carry
            lax.fori_loop(0, o_ref.shape[0] // row_chunk, body, 0)


def _dense(x, w, *, gain=None, res=None, prologue, out_dtype, name, layer=0, col_off=0, n_cols=None,
           w_transposed=False, extra_cols=None, final_gain=None, tm=None, tn=None, skip_cols=None):
    m = x.shape[0]
    k = x.shape[1] if w_transposed else w.shape[-2]
    assert k == x.shape[1] or (k % LANE == 0 and k < x.shape[1] and prologue == "plain")
    n = n_cols if n_cols is not None else w.shape[-1]
    if tm is None:
        tm = _pick_tile(m, (1024, 512, 256, 128, 64, 32, 16))
    if tn is None:
        tn = _pick_tile(n, (1024, 512, 256, 128) if k <= 2048 else (512, 256, 128))
    assert m % tm == 0 and n % tn == 0
    row_chunk = min(tm, 128)
    in_specs = [pl.BlockSpec((tm, k), lambda i, j: (i, 0))]
    args = [x]
    if prologue == "norm":
        in_specs.append(pl.BlockSpec((1, k), lambda i, j: (0, 0)))
        args.append(gain)
    if w_transposed:
        assert col_off % BF16_ROWS == 0 and tn % BF16_ROWS == 0
        s_at, s_w = skip_cols if skip_cols is not None else (n, 0)
        assert s_at % tn == 0 and s_w % BF16_ROWS == 0
        in_specs.append(pl.BlockSpec(
            (pl.Element(tn), pl.Element(k)),
            lambda i, j: (pl.multiple_of(col_off + j * tn + jnp.where(j * tn >= s_at, s_w, 0), BF16_ROWS), 0)))
    elif w.ndim == 3:
        assert col_off % tn == 0
        in_specs.append(pl.BlockSpec((None, k, tn), lambda i, j: (layer, 0, j + col_off // tn)))
    else:
        in_specs.append(pl.BlockSpec((k, tn), lambda i, j: (0, j)))
    args.append(w)
    out_specs = [pl.BlockSpec((tm, tn), lambda i, j: (i, j))]
    out_shape = [jax.ShapeDtypeStruct((m, n), out_dtype)]
    if extra_cols is not None:
        e_off, e_n = extra_cols
        assert w_transposed and prologue != "plain" and e_off % BF16_ROWS == 0 and e_n % BF16_ROWS == 0
        in_specs.append(pl.BlockSpec((pl.Element(e_n), pl.Element(k)), lambda i, j: (e_off, 0)))
        args.append(w)
        out_specs.append(pl.BlockSpec((tm, e_n), lambda i, j: (i, 0)))
        out_shape.append(jax.ShapeDtypeStruct((m, e_n), F32))
    if res is not None:
        in_specs.append(pl.BlockSpec((tm, tn), lambda i, j: (i, j)))
        args.append(res)
    if final_gain is not None:
        assert out_dtype == F32
        in_specs.append(pl.BlockSpec((1, n), lambda i, j: (0, 0)))
        args.append(final_gain)
        out_specs[0] = pl.BlockSpec((tm, n), lambda i, j: (i, 0))
    scratch = [] if prologue == "plain" else [pltpu.VMEM((tm, k), BF16)]
    outs = pl.pallas_call(
        functools.partial(_dense_body, prologue=prologue, has_res=res is not None,
                          has_extra=extra_cols is not None, has_final=final_gain is not None,
                          row_chunk=row_chunk, w_transposed=w_transposed),
        grid=(m // tm, n // tn),
        in_specs=in_specs,
        out_specs=out_specs,
        out_shape=out_shape,
        scratch_shapes=scratch,
        compiler_params=_cparams(2),
        name=name,
    )(*args)
    return outs if extra_cols is not None else outs[0]


def _cumsum_rows(x, group):
    rows = lax.broadcasted_iota(jnp.int32, x.shape, 0) % group
    shift = 1
    while shift < group:
        rolled = pltpu.roll(x, shift, axis=0)
        x = x + jnp.where(rows >= shift, rolled, 0.0)
        shift *= 2
    return x


def _cumsum_rows_mxu(x, group):
    n = x.shape[0]
    row = lax.broadcasted_iota(jnp.int32, (n, n), 0)
    col = lax.broadcasted_iota(jnp.int32, (n, n), 1)
    tri = jnp.where(jnp.logical_and(col <= row, col // group == row // group), 1.0, 0.0).astype(BF16)
    hi = x.astype(BF16)
    lo = (x - hi.astype(F32)).astype(BF16)
    return (jnp.dot(tri, hi, preferred_element_type=F32) + jnp.dot(tri, lo, preferred_element_type=F32))


def _lockstep(gens):
    gens = list(gens)
    results = [None] * len(gens)
    live = list(range(len(gens)))
    while live:
        still = []
        for n in live:
            try:
                next(gens[n])
                still.append(n)
            except StopIteration as stop:
                results[n] = stop.value
        live = still
    return results


def _recurrence_block(q, k, v, la, s_read, *, sub, mm_dtype):
    bt, kd = q.shape
    vd = v.shape[1]
    ns = bt // sub
    b_loc = _cumsum_rows_mxu(la, sub) if mm_dtype == BF16 else _cumsum_rows(la, sub)
    qd = q * jnp.exp(b_loc)
    ki = k * jnp.exp(-b_loc)
    tot = [b_loc[(i + 1) * sub - 1:(i + 1) * sub, :] for i in range(ns)]
    pre = [jnp.zeros((1, kd), F32)]
    for i in range(ns):
        pre.append(pre[-1] + tot[i])
    sl = [slice(i * sub, (i + 1) * sub) for i in range(ns)]
    vb = v.astype(mm_dtype)
    qdm = qd.astype(mm_dtype)
    q_state = jnp.concatenate([qd[sl[i]] * jnp.exp(pre[i]) for i in range(ns)], axis=0) if ns > 1 else qd
    q_state = q_state.astype(mm_dtype)
    ke = [ki[sl[i]] * jnp.exp(tot[i]) for i in range(ns)]
    kmats = []
    for i in range(ns):
        parts = [ke[j] if j == i - 1 else ke[j] * jnp.exp(pre[i] - pre[j + 1]) for j in range(i)]
        parts.append(ki[sl[i]])
        parts += [jnp.zeros((sub, kd), F32)] * (ns - 1 - i)
        kmats.append((jnp.concatenate(parts, axis=0) if ns > 1 else parts[0]).astype(mm_dtype))
    k_end = jnp.concatenate([ke[i] if i == ns - 1 else ke[i] * jnp.exp(pre[ns] - pre[i + 1])
                             for i in range(ns)], axis=0) if ns > 1 else ke[0]
    k_end = k_end.astype(mm_dtype)
    decay = jnp.broadcast_to(jnp.exp(pre[ns]), (LANE, kd)).T
    yield

    s_prev = s_read()
    o_state = jnp.dot(q_state, s_prev.astype(mm_dtype), preferred_element_type=F32)
    a_rows = [lax.dot_general(qdm[sl[i]], kmats[i], (((1,), (1,)), ((), ())), preferred_element_type=F32)
              for i in range(ns)]
    ds = lax.dot_general(k_end, vb, (((0,), (0,)), ((), ())), preferred_element_type=F32)
    yield

    a = jnp.concatenate(a_rows, axis=0) if ns > 1 else a_rows[0]
    row = lax.broadcasted_iota(jnp.int32, a.shape, 0)
    col = lax.broadcasted_iota(jnp.int32, a.shape, 1)
    a = jnp.where(col <= row, a, 0.0).astype(mm_dtype)
    o = jnp.dot(a, vb, preferred_element_type=F32) + o_state
    s_new = jnp.concatenate([decay * s_prev[:, c * LANE:(c + 1) * LANE] for c in range(vd // LANE)],
                            axis=1) + ds
    yield
    return o, s_new


def _head_gated_norm(o, gnorm, gate):
    ms = jnp.mean(o * o, axis=-1, keepdims=True)
    return o * lax.rsqrt(ms + EPS) * gnorm * (gate * _sigmoid(gate))


def _log_sigmoid(x):
    return jnp.minimum(x, 0.0) - jnp.log(1.0 + jnp.exp(-jnp.abs(x)))


def _mixer_body(*refs, mode, layer, gla_heads, gla_dk, gla_dv, hgrn_heads, hgrn_dk, block, sub, mm_dtype):
    (q_ref, k_ref, v_ref, g_ref, ga_ref, wup_ref, ba_ref, gng_ref,
     hq_ref, hf_ref, hi_ref, hg_ref, ma_ref, mb_ref, lb_ref, gnh_ref) = refs[:16]
    if mode == "prompt":
        om_ref, sg_out, sh_out = refs[16:]
        sg_in = sh_in = None
    else:
        sg_in, sh_in, om_ref, sg_out, sh_out = refs[16:]

    p = lb_ref[...]
    e = jnp.exp(p - jnp.max(p, axis=0, keepdims=True))
    lb_all = jnp.sum(e[:layer + 1], axis=0, keepdims=True) / jnp.sum(e, axis=0, keepdims=True)

    def gla_head(h, load, s_read, shared):
        ks = slice(h * gla_dk, (h + 1) * gla_dk)
        vs = slice(h * gla_dv, (h + 1) * gla_dv)
        ga = load(ga_ref, slice(None)).astype(BF16)
        a_logit = jnp.dot(ga, wup_ref[:, ks], preferred_element_type=F32) + ba_ref[:, ks]
        yield
        la = _log_sigmoid(a_logit) * (1.0 / GLA_GATE_NORM)
        q = load(q_ref, ks) * (gla_dk ** -0.5)
        o, s_new = yield from _recurrence_block(q, load(k_ref, ks), load(v_ref, vs), la, s_read,
                                                sub=sub, mm_dtype=mm_dtype)
        shared[h] = _head_gated_norm(o, gng_ref[:, vs], load(g_ref, vs))
        return s_new

    def hgrn_head(h, load, s_read, shared):
        ks = slice(h * hgrn_dk, (h + 1) * hgrn_dk)
        lb = lb_all[:, ks]
        f = lb + (1.0 - lb) * _sigmoid(load(hf_ref, ks))
        hq = load(hq_ref, ks)
        q = hq * _sigmoid(hq) * (hgrn_dk ** -0.5)
        yield
        o, s_new = yield from _recurrence_block(q, 1.0 - f, load(hi_ref, ks), jnp.log(f), s_read,
                                                sub=sub, mm_dtype=mm_dtype)
        o_h = _head_gated_norm(o, gnh_ref[:, ks], load(hg_ref, ks))
        per = gla_dv // hgrn_dk
        o_gla = shared[h // per][:, (h % per) * hgrn_dk:(h % per + 1) * hgrn_dk]
        mix = _sigmoid(load(ma_ref, ks)) * o_gla + _sigmoid(load(mb_ref, ks)) * o_h
        return mix, s_new

    def run(load, sg_read, sh_read):
        shared = {}
        gens = [gla_head(h, load, functools.partial(sg_read, h), shared) for h in range(gla_heads)]
        gens += [hgrn_head(h, load, functools.partial(sh_read, h), shared) for h in range(hgrn_heads)]
        res = _lockstep(gens)
        return res[:gla_heads], res[gla_heads:]

    if mode == "prompt":
        @pl.when(pl.program_id(1) == 0)
        def _():
            sg_out[...] = jnp.zeros(sg_out.shape, F32)
            sh_out[...] = jnp.zeros(sh_out.shape, F32)

        def body(c, carry):
            rows = pl.ds(pl.multiple_of(c * block, block), block)
            load = lambda ref, cols: ref[rows, cols].astype(F32)
            g_res, h_res = run(load, lambda h: sg_out[0, h], lambda h: sh_out[0, h])
            for h, s_new in enumerate(g_res):
                sg_out[0, h] = s_new
            for h, (mix, s_new) in enumerate(h_res):
                sh_out[0, h] = s_new
                om_ref[rows, h * hgrn_dk:(h + 1) * hgrn_dk] = mix.astype(om_ref.dtype)
            return carry
        lax.fori_loop(0, q_ref.shape[0] // block, body, 0)
    else:
        n_seq = q_ref.shape[0] // block
        mixes = []
        for s in range(n_seq):
            load = lambda ref, cols, s=s: ref[:, cols].astype(F32)[s * block:(s + 1) * block]
            g_res, h_res = run(load, lambda h, s=s: sg_in[s, h], lambda h, s=s: sh_in[s, h])
            for h, s_new in enumerate(g_res):
                sg_out[s, h] = s_new
            for h, (mix, s_new) in enumerate(h_res):
                sh_out[s, h] = s_new
            mixes.append([mix for mix, _ in h_res])
        for h in range(hgrn_heads):
            om_ref[:, h * hgrn_dk:(h + 1) * hgrn_dk] = jnp.concatenate(
                [mixes[s][h] for s in range(n_seq)], axis=0).astype(om_ref.dtype)


def _mixer(z, zga, p, s_gla, s_hgrn, *, mode, layer, n_seq, seq_len, d_model, name):
    gh, hk = GLA_HEADS, HGRN_DK
    gk, gv = d_model // 2 // gh, d_model // gh
    hh = d_model // hk
    t = z.shape[0]
    if mode == "prompt":
        step = PROMPT_STEP
        n_t = seq_len // step
        grid = (n_seq, n_t)
        block, sub, mm_dtype = PROMPT_BLOCK, REF_CHUNK, BF16
        rowmap = lambda col: (lambda b, i: (b * n_t + i, col))
        const = lambda b, i: (0, 0)
        st = lambda *dims: pl.BlockSpec((1,) + dims, lambda b, i: (b, 0, 0, 0))
    else:
        step = SAMPLE_SEQS * seq_len
        grid = (n_seq // SAMPLE_SEQS,)
        block = seq_len
        sub, mm_dtype = math.gcd(REF_CHUNK, seq_len), F32
        rowmap = lambda col: (lambda i: (i, col))
        const = lambda i: (0, 0)
        st = lambda *dims: pl.BlockSpec((SAMPLE_SEQS,) + dims, lambda i: (i, 0, 0, 0))
    zs = lambda width, col: pl.BlockSpec((step, width), rowmap(col))
    whole = lambda a: pl.BlockSpec(a.shape, const)
    kw = gh * gk
    in_specs = [zs(kw, 0), zs(kw, 1), zs(d_model, 1), zs(d_model, 2), zs(zga.shape[1], 0),
                whole(p["gla_w_a_up"]), whole(p["gla_b_a"]), whole(p["gla_norm_g"])]
    in_specs += [zs(d_model, 3 + c) for c in range(6)] + [whole(p["hgrn_lower_bound"]), whole(p["hgrn_norm_g"])]
    args = [z, z, z, z, zga, p["gla_w_a_up"], p["gla_b_a"], p["gla_norm_g"],
            z, z, z, z, z, z, p["hgrn_lower_bound"], p["hgrn_norm_g"]]
    if mode != "prompt":
        in_specs += [pl.BlockSpec((None, SAMPLE_SEQS, gh, gk, gv), lambda i: (layer, i, 0, 0, 0)),
                     pl.BlockSpec((None, SAMPLE_SEQS, hh, hk, hk), lambda i: (layer, i, 0, 0, 0))]
        args += [s_gla, s_hgrn]
    return pl.pallas_call(
        functools.partial(_mixer_body, mode=mode, layer=layer, gla_heads=gh, gla_dk=gk, gla_dv=gv,
                          hgrn_heads=hh, hgrn_dk=hk, block=block, sub=sub, mm_dtype=mm_dtype),
        grid=grid,
        in_specs=in_specs,
        out_specs=[zs(d_model, 0), st(gh, gk, gv), st(hh, hk, hk)],
        out_shape=[jax.ShapeDtypeStruct((t, d_model), BF16),
                   jax.ShapeDtypeStruct((n_seq, gh, gk, gv), F32),
                   jax.ShapeDtypeStruct((n_seq, hh, hk, hk), F32)],
        compiler_params=_cparams(len(grid)),
        name=name,
    )(*args)


def _xattn_sample_body(q_ref, k_ref, v_ref, o_ref, *, heads, dh, seq_len):
    scale = dh ** -0.5
    n_mem = k_ref.shape[1]
    for s in range(k_ref.shape[0]):
        rows = slice(s * seq_len, (s + 1) * seq_len)
        q = q_ref[rows, :]
        q2 = jnp.concatenate([q[:, h * dh:(h + 1) * dh] for h in range(heads)], axis=0)
        k2 = k_ref[s].reshape(n_mem * heads, dh)
        v2 = v_ref[s].reshape(n_mem * heads, dh)
        sc = lax.dot_general(q2, k2, (((1,), (1,)), ((), ())), preferred_element_type=F32) * scale
        q_head = lax.broadcasted_iota(jnp.int32, sc.shape, 0) // seq_len
        k_head = lax.broadcasted_iota(jnp.int32, sc.shape, 1) % heads
        sc = jnp.where(q_head == k_head, sc, -jnp.inf)
        p = jnp.exp(sc - jnp.max(sc, axis=-1, keepdims=True))
        o2 = jnp.dot(p, v2, preferred_element_type=F32) / jnp.sum(p, axis=-1, keepdims=True)
        for h in range(heads):
            o_ref[rows, h * dh:(h + 1) * dh] = o2[h * seq_len:(h + 1) * seq_len].astype(o_ref.dtype)


def _xattn_sample(mq, mem_k, mem_v, *, layer, n_seq, seq_len, name):
    t, d = mq.shape
    dh = d // MEM_HEADS
    n_mem = mem_k.shape[2]
    q_spec = pl.BlockSpec((SAMPLE_SEQS * seq_len, d), lambda i: (i, 0))
    kv_spec = pl.BlockSpec((None, SAMPLE_SEQS, n_mem, MEM_HEADS, dh), lambda i: (layer, i, 0, 0, 0))
    return pl.pallas_call(
        functools.partial(_xattn_sample_body, heads=MEM_HEADS, dh=dh, seq_len=seq_len),
        grid=(n_seq // SAMPLE_SEQS,),
        in_specs=[q_spec, kv_spec, kv_spec],
        out_specs=q_spec,
        out_shape=jax.ShapeDtypeStruct((t, d), F32),
        compiler_params=_cparams(1),
        name=name,
    )(mq, mem_k, mem_v)


def _mid_body(om_ref, x_ref, k_ref, v_ref, wo_ref, wq_ref, wm_ref, g_ref, o_ref, *, heads, dh):
    scale = dh ** -0.5
    x1 = x_ref[...] + jnp.dot(om_ref[...], wo_ref[...], preferred_element_type=F32)
    ms = jnp.mean(x1 * x1, axis=-1, keepdims=True)
    hx = (x1 * lax.rsqrt(ms + EPS) * g_ref[...]).astype(BF16)
    mq = jnp.dot(hx, wq_ref[...], preferred_element_type=F32).astype(BF16)
    heads_out = []
    for h in range(heads):
        hs = slice(h * dh, (h + 1) * dh)
        sc = lax.dot_general(mq[:, hs], k_ref[0, :, hs], (((1,), (1,)), ((), ())),
                             preferred_element_type=F32) * scale
        p = jnp.exp(sc - jnp.max(sc, axis=-1, keepdims=True))
        o = jnp.dot(p.astype(BF16), v_ref[0, :, hs], preferred_element_type=F32) / jnp.sum(p, axis=-1, keepdims=True)
        heads_out.append(o.astype(BF16))
    mo = jnp.concatenate(heads_out, axis=1)
    o_ref[...] = x1 + jnp.dot(mo, wm_ref[...], preferred_element_type=F32)


def _mid_prompt(o_mix, x, mem_k, mem_v, w_out, w_q, w_o, gain, *, n_seq, seq_len, name):
    t, d = x.shape
    n_mem = mem_k.shape[1]
    tm = 256
    n_t = seq_len // tm
    rows = lambda dtype_rows: pl.BlockSpec((tm, d), lambda b, i: (b * n_t + i, 0))
    kv = pl.BlockSpec((1, n_mem, d), lambda b, i: (b, 0, 0))
    wspec = pl.BlockSpec((d, d), lambda b, i: (0, 0), pipeline_mode=pl.Buffered(1))
    return pl.pallas_call(
        functools.partial(_mid_body, heads=MEM_HEADS, dh=d // MEM_HEADS),
        grid=(n_seq, n_t),
        in_specs=[rows(BF16), rows(F32), kv, kv, wspec, wspec, wspec, pl.BlockSpec((1, d), lambda b, i: (0, 0))],
        out_specs=rows(F32),
        out_shape=jax.ShapeDtypeStruct((t, d), F32),
        compiler_params=_cparams(2),
        name=name,
    )(o_mix, x, mem_k, mem_v, w_out, w_q, w_o, gain)


def _ffn_up_body(*refs, mode, seq_len, tiles_per_seq, row_chunk, last_shift):
    if mode == "prompt":
        x_ref, g_ref, wg_ref, wv_ref, cw_ref, cb_ref, act_ref, st_ref, h_ref, tail_ref = refs
        p1_ref = p2_ref = None
    else:
        x_ref, g_ref, wg_ref, wv_ref, cw_ref, cb_ref, buf_ref, act_ref, st_ref, h_ref = refs
        tail_ref = None
    i = pl.program_id(0)
    j = pl.program_id(1)

    @pl.when(j == 0)
    def _():
        def body(c, carry):
            rows = pl.ds(pl.multiple_of(c * row_chunk, row_chunk), row_chunk)
            xf = x_ref[rows, :]
            ms = jnp.mean(xf * xf, axis=-1, keepdims=True)
            h_ref[rows, :] = (xf * lax.rsqrt(ms + EPS) * g_ref[...]).astype(BF16)
            return carry
        lax.fori_loop(0, x_ref.shape[0] // row_chunk, body, 0)

    if mode == "prompt":
        @pl.when(i % tiles_per_seq == 0)
        def _():
            tail_ref[j] = jnp.zeros(tail_ref.shape[1:], F32)

    def step(shift):
        def place(a):
            if shift == 0:
                return a
            return jnp.concatenate([a[..., shift:], jnp.zeros(a.shape[:-1] + (shift,), a.dtype)], axis=-1)

        h = h_ref[...]
        ug = jnp.dot(h, wg_ref[...], preferred_element_type=F32)
        tm = ug.shape[0]
        row = lax.broadcasted_iota(jnp.int32, ug.shape, 0)
        roll1 = pltpu.roll(ug, 1, axis=0)
        roll2 = pltpu.roll(ug, 2, axis=0)
        if mode == "prompt":
            tail = tail_ref[j]
            prev1 = tail[7:8, :]
            prev2 = tail[6:7, :]
            sh1 = jnp.where(row >= 1, roll1, prev1)
            sh2 = jnp.where(row >= 2, roll2, jnp.where(row == 0, prev2, prev1))
            tail_ref[j] = ug[tm - 8:, :]
            st_ref[...] = place(ug[tm - 8:, :])
        else:
            buf = buf_ref[...]
            n_seq = buf.shape[0]
            spread = lambda r: jnp.broadcast_to(buf[:, r:r + 1, :],
                                                (n_seq, seq_len, buf.shape[2])).reshape(ug.shape)
            prev2, prev1 = spread(0), spread(1)
            pos = row % seq_len
            sh1 = jnp.where(pos >= 1, roll1, prev1)
            sh2 = jnp.where(pos >= 2, roll2, jnp.where(pos == 0, prev2, prev1))
            st_ref[...] = place(ug.reshape(n_seq, seq_len, ug.shape[1])[:, seq_len - (CONV_W - 1):, :])
        conv = cw_ref[0:1, :] * sh2 + cw_ref[1:2, :] * sh1 + cw_ref[2:3, :] * ug + cb_ref[...]
        gate = conv * _sigmoid(conv)
        uv = jnp.dot(h, wv_ref[...], preferred_element_type=F32)
        act_ref[...] = place((gate * uv).astype(act_ref.dtype))

    if last_shift == 0:
        step(0)
    else:
        last = pl.num_programs(1) - 1
        pl.when(j < last)(functools.partial(step, 0))
        pl.when(j == last)(functools.partial(step, last_shift))


def _ffn_up(x, gain, w_up, cw, cb, carry, *, mode, seq_len, name):
    m, k = x.shape
    d_ff = w_up.shape[1] // 2
    tm = _pick_tile(m, (1024,))
    tn = 4 * LANE
    nf = -(-d_ff // tn) * tn
    last_shift = nf - d_ff
    assert d_ff % LANE == 0 and d_ff >= tn
    col = lambda j, base=0: pl.multiple_of(base + jnp.minimum(j * tn, d_ff - tn), LANE)
    elem = lambda *dims: tuple(pl.Element(n) for n in dims)
    xs = pl.BlockSpec((tm, k), lambda i, j: (i, 0))
    gs = pl.BlockSpec((1, k), lambda i, j: (0, 0))
    wgs = pl.BlockSpec(elem(k, tn), lambda i, j: (0, col(j)))
    wvs = pl.BlockSpec(elem(k, tn), lambda i, j: (0, col(j, d_ff)))
    cws = pl.BlockSpec(elem(CONV_W, tn), lambda i, j: (0, col(j)))
    cbs = pl.BlockSpec(elem(1, tn), lambda i, j: (0, col(j)))
    ts = pl.BlockSpec((tm, tn), lambda i, j: (i, j))
    in_specs = [xs, gs, wgs, wvs, cws, cbs]
    args = [x, gain, w_up, w_up, cw, cb]
    scratch = [pltpu.VMEM((tm, k), BF16)]
    if mode == "prompt":
        assert seq_len % tm == 0
        scratch.append(pltpu.VMEM((nf // tn, 8, tn), F32))
        st_spec = pl.BlockSpec((None, 8, tn), lambda i, j: (i, 0, j))
        st_shape = jax.ShapeDtypeStruct((m // tm, 8, nf), F32)
    else:
        assert tm % seq_len == 0 and seq_len == 8
        n_blk = tm // seq_len
        in_specs.append(pl.BlockSpec(elem(n_blk, CONV_W - 1, tn), lambda i, j: (i * n_blk, 0, col(j))))
        args.append(carry)
        st_spec = pl.BlockSpec((n_blk, CONV_W - 1, tn), lambda i, j: (i, 0, j))
        st_shape = jax.ShapeDtypeStruct((m // seq_len, CONV_W - 1, nf), F32)
    return pl.pallas_call(
        functools.partial(_ffn_up_body, mode=mode, seq_len=seq_len, tiles_per_seq=max(seq_len // tm, 1),
                          row_chunk=128, last_shift=last_shift),
        grid=(m // tm, nf // tn),
        in_specs=in_specs,
        out_specs=[ts, st_spec],
        out_shape=[jax.ShapeDtypeStruct((m, nf), BF16), st_shape],
        scratch_shapes=scratch,
        compiler_params=_cparams(2),
        name=name,
    )(*args)


def _layer(x, mem_k, mem_v, s_gla, s_hgrn, conv_buf, p, *, mode, layer, n_seq, seq_len, tag):
    t, d = x.shape
    nm = lambda s: f"{s}_{tag}"
    ga_off, rank = p["ga_off"], p["rank"]
    w_in_t = p["w_in_t"]
    in_proj = functools.partial(_dense, x, w_in_t, gain=p["norm_mix_g"], prologue="norm", w_transposed=True,
                                tn=d)
    z, zga = in_proj(out_dtype=BF16, name=nm("in_proj"), n_cols=w_in_t.shape[0] - rank,
                     skip_cols=(ga_off, rank), extra_cols=(ga_off, rank))
    o_mix, s_gla_new, s_hgrn_new = _mixer(z, zga, p, s_gla, s_hgrn, mode=mode, layer=layer, n_seq=n_seq,
                                          seq_len=seq_len, d_model=d, name=nm("mixer"))
    if mode == "prompt":
        x2 = _mid_prompt(o_mix, x, mem_k.astype(BF16), mem_v.astype(BF16), p["w_out"], p["w_mem_q"], p["w_mem_o"],
                         p["norm_xattn_g"], n_seq=n_seq, seq_len=seq_len, name=nm("mid"))
    else:
        x1 = _dense(o_mix, p["w_out"], res=x, prologue="plain", out_dtype=F32, name=nm("out_proj"))
        mq = _dense(x1, p["w_mem_q"], gain=p["norm_xattn_g"], prologue="norm", out_dtype=F32, name=nm("mem_q"))
        mo = _xattn_sample(mq, mem_k, mem_v, layer=layer, n_seq=n_seq, seq_len=seq_len, name=nm("xattn"))
        x2 = _dense(mo, p["w_mem_o"], res=x1, prologue="cast", out_dtype=F32, name=nm("mem_o"))

    d_ff = p["d_ff"]
    carry = None if mode == "prompt" else conv_buf[layer]
    act, ug_rows = _ffn_up(x2, p["norm_ffn_g"], p["w_up"], p["ffn_conv_w"], p["ffn_conv_b"],
                           carry, mode=mode, seq_len=seq_len, name=nm("ffn_up"))
    final = p["norm_final_g"] if p["is_last"] else None
    y = _dense(act, p["w_down"], res=x2, prologue="plain", out_dtype=F32, name=nm("ffn_down"),
               final_gain=final, tm=512 if final is not None else None, tn=1024 if final is not None else None)

    if mode == "prompt":
        tiles_per_seq = ug_rows.shape[0] // n_seq
        conv_new = ug_rows[tiles_per_seq - 1::tiles_per_seq, 8 - (CONV_W - 1):, :d_ff]
    else:
        conv_new = ug_rows[:, :, :d_ff]
    return y, s_gla_new, s_hgrn_new, conv_new


def kernel(x_prompt, x_sample, mem_prompt, state_gla, state_hgrn, state_ffn_conv, cache_mem_k, cache_mem_v,
           norm_mix_g, w_in, gla_w_a_up, gla_b_a, gla_norm_g, hgrn_lower_bound, hgrn_norm_g, w_out,
           norm_xattn_g, norm_mem_g, w_mem_q, w_mem_k, w_mem_v, w_mem_o,
           norm_ffn_g, w_ffn_up, ffn_conv_w, ffn_conv_b, w_ffn_down, norm_final_g):
    depth = w_in.shape[0]
    bp, lp, d = x_prompt.shape
    bs, ls, _ = x_sample.shape
    n_mem = mem_prompt.shape[1]
    d_ff = w_ffn_down.shape[1]
    rank = gla_w_a_up.shape[1]
    qk = gla_w_a_up.shape[2]
    ga_off = 2 * qk + 2 * d

    xp = x_prompt.reshape(bp * lp, d)
    xs = x_sample.reshape(bs * ls, d)
    mem = mem_prompt.reshape(bp * n_mem, d)
    row = lambda v: v.reshape(1, -1)

    outs = {k: [] for k in ("gla_p", "hgrn_p", "conv_p", "mk_p", "mv_p", "gla_s", "hgrn_s", "conv_s")}
    for l in range(depth):
        p = {
            "w_out": w_out[l].astype(BF16), "w_mem_q": w_mem_q[l].astype(BF16), "w_mem_o": w_mem_o[l].astype(BF16),
            "w_in_t": jnp.swapaxes(w_in[l], 0, 1).astype(BF16),
            "ga_off": ga_off, "rank": rank,
            "gla_w_a_up": gla_w_a_up[l].astype(BF16),
            "gla_b_a": row(gla_b_a[l]), "gla_norm_g": row(gla_norm_g[l]),
            "hgrn_lower_bound": hgrn_lower_bound, "hgrn_norm_g": row(hgrn_norm_g[l]),
            "norm_mix_g": row(norm_mix_g[l]), "norm_xattn_g": row(norm_xattn_g[l]),
            "norm_ffn_g": row(norm_ffn_g[l]), "norm_final_g": row(norm_final_g), "is_last": l == depth - 1,
            "w_up": w_ffn_up[l].astype(BF16), "w_down": w_ffn_down[l].astype(BF16),
            "ffn_conv_w": ffn_conv_w[l], "ffn_conv_b": row(ffn_conv_b[l]),
            "d_ff": d_ff,
        }
        g_mem = row(norm_mem_g[l])
        mk = _dense(mem, w_mem_k, gain=g_mem, prologue="norm", out_dtype=F32, name=f"mem_k_{l}", layer=l)
        mv = _dense(mem, w_mem_v, gain=g_mem, prologue="norm", out_dtype=F32, name=f"mem_v_{l}", layer=l)
        mk3, mv3 = mk.reshape(bp, n_mem, d), mv.reshape(bp, n_mem, d)
        xp, sg, sh, cb = _layer(xp, mk3, mv3, None, None, None, p, mode="prompt", layer=l,
                                n_seq=bp, seq_len=lp, tag=f"p{l}")
        outs["gla_p"].append(sg); outs["hgrn_p"].append(sh); outs["conv_p"].append(cb)
        outs["mk_p"].append(mk3.reshape(bp, n_mem, MEM_HEADS, d // MEM_HEADS))
        outs["mv_p"].append(mv3.reshape(bp, n_mem, MEM_HEADS, d // MEM_HEADS))
        xs, sg2, sh2, cb2 = _layer(xs, cache_mem_k, cache_mem_v, state_gla, state_hgrn, state_ffn_conv, p,
                                   mode="sample", layer=l, n_seq=bs, seq_len=ls, tag=f"s{l}")
        outs["gla_s"].append(sg2); outs["hgrn_s"].append(sh2); outs["conv_s"].append(cb2)

    y_prompt = xp.reshape(bp, lp, d)
    y_sample = xs.reshape(bs, ls, d)
    st = lambda k: jnp.stack(outs[k])
    return (y_prompt, y_sample, st("gla_p"), st("hgrn_p"), st("conv_p"), st("mk_p"), st("mv_p"),
            st("gla_s"), st("hgrn_s"), st("conv_s"))
```

```python
import functools
import math

import jax
import jax.numpy as jnp
from jax import lax
from jax.experimental import pallas as pl
from jax.experimental.pallas import tpu as pltpu

F32 = jnp.float32
BF16 = jnp.bfloat16

LANE = 128
BF16_ROWS = 16
V7X_VMEM_LIMIT = 56 * 1024 * 1024

EPS = 1e-6
GLA_HEADS = 4
GLA_GATE_NORM = 16.0
HGRN_DK = 128
MEM_HEADS = 4
CONV_W = 3
REF_CHUNK = 16
PROMPT_BLOCK = 64
SAMPLE_SEQS = 2


def _cparams(n_axes):
    return pltpu.CompilerParams(dimension_semantics=("arbitrary",) * n_axes,
                                vmem_limit_bytes=V7X_VMEM_LIMIT)


def _sigmoid(x):
    return 0.5 * jnp.tanh(0.5 * x) + 0.5


def _pick_tile(n, candidates):
    for c in candidates:
        if n % c == 0:
            return c
    return n


def _dense_body(*refs, prologue, has_res, has_extra, has_final, row_chunk, w_transposed):
    it = iter(refs)
    x_ref = next(it)
    g_ref = next(it) if prologue == "norm" else None
    w_ref = next(it)
    we_ref = next(it) if has_extra else None
    r_ref = next(it) if has_res else None
    fg_ref = next(it) if has_final else None
    o_ref = next(it)
    oe_ref = next(it) if has_extra else None
    h_ref = next(it) if prologue != "plain" else None
    j = pl.program_id(1)
    nt = (((1,), (1,)), ((), ()))

    if prologue != "plain":
        @pl.when(j == 0)
        def _():
            def body(c, carry):
                rows = pl.ds(pl.multiple_of(c * row_chunk, row_chunk), row_chunk)
                xf = x_ref[rows, :].astype(F32)
                if prologue == "norm":
                    ms = jnp.mean(xf * xf, axis=-1, keepdims=True)
                    xf = xf * lax.rsqrt(ms + EPS) * g_ref[...]
                h_ref[rows, :] = xf.astype(BF16)
                return carry
            lax.fori_loop(0, x_ref.shape[0] // row_chunk, body, 0)
            if has_extra:
                oe_ref[...] = lax.dot_general(h_ref[...], we_ref[...], nt, preferred_element_type=F32)
        lhs = h_ref[...]
    else:
        lhs = x_ref[...]
    w = w_ref[...].astype(BF16)
    if w_transposed:
        acc = lax.dot_general(lhs, w, nt, preferred_element_type=F32)
    else:
        acc = jnp.dot(lhs, w, preferred_element_type=F32)
    if has_res:
        acc = acc + r_ref[...]
    if not has_final:
        o_ref[...] = acc.astype(o_ref.dtype)
    else:
        tn = acc.shape[1]
        o_ref[:, pl.ds(pl.multiple_of(j * tn, tn), tn)] = acc

        @pl.when(j == pl.num_programs(1) - 1)
        def _():
            def body(c, carry):
                rows = pl.ds(pl.multiple_of(c * row_chunk, row_chunk), row_chunk)
                xf = o_ref[rows, :]
                ms = jnp.mean(xf * xf, axis=-1, keepdims=True)
                o_ref[rows, :] = xf * lax.rsqrt(ms + EPS) * fg_ref[...]
                return carry
            lax.fori_loop(0, o_ref.shape[0] // row_chunk, body, 0)


def _dense(x, w, *, gain=None, res=None, prologue, out_dtype, name, layer=0, col_off=0, n_cols=None,
           w_transposed=False, extra_cols=None, final_gain=None, tm=None, tn=None, skip_cols=None):
    m = x.shape[0]
    k = x.shape[1] if w_transposed else w.shape[-2]
    assert k == x.shape[1] or (k % LANE == 0 and k < x.shape[1] and prologue == "plain")
    n = n_cols if n_cols is not None else w.shape[-1]
    if tm is None:
        tm = _pick_tile(m, (1024, 512, 256, 128, 64, 32, 16))
    if tn is None:
        tn = _pick_tile(n, (1024, 512, 256, 128) if k <= 2048 else (512, 256, 128))
    assert m % tm == 0 and n % tn == 0
    row_chunk = min(tm, 128)
    in_specs = [pl.BlockSpec((tm, k), lambda i, j: (i, 0))]
    args = [x]
    if prologue == "norm":
        in_specs.append(pl.BlockSpec((1, k), lambda i, j: (0, 0)))
        args.append(gain)
    if w_transposed:
        assert col_off % BF16_ROWS == 0 and tn % BF16_ROWS == 0
        s_at, s_w = skip_cols if skip_cols is not None else (n, 0)
        assert s_at % tn == 0 and s_w % BF16_ROWS == 0
        in_specs.append(pl.BlockSpec(
            (pl.Element(tn), pl.Element(k)),
            lambda i, j: (pl.multiple_of(col_off + j * tn + jnp.where(j * tn >= s_at, s_w, 0), BF16_ROWS), 0)))
    elif w.ndim == 3:
        assert col_off % tn == 0
        in_specs.append(pl.BlockSpec((None, k, tn), lambda i, j: (layer, 0, j + col_off // tn)))
    else:
        in_specs.append(pl.BlockSpec((k, tn), lambda i, j: (0, j)))
    args.append(w)
    out_specs = [pl.BlockSpec((tm, tn), lambda i, j: (i, j))]
    out_shape = [jax.ShapeDtypeStruct((m, n), out_dtype)]
    if extra_cols is not None:
        e_off, e_n = extra_cols
        assert w_transposed and prologue != "plain" and e_off % BF16_ROWS == 0 and e_n % BF16_ROWS == 0
        in_specs.append(pl.BlockSpec((pl.Element(e_n), pl.Element(k)), lambda i, j: (e_off, 0)))
        args.append(w)
        out_specs.append(pl.BlockSpec((tm, e_n), lambda i, j: (i, 0)))
        out_shape.append(jax.ShapeDtypeStruct((m, e_n), F32))
    if res is not None:
        in_specs.append(pl.BlockSpec((tm, tn), lambda i, j: (i, j)))
        args.append(res)
    if final_gain is not None:
        assert out_dtype == F32
        in_specs.append(pl.BlockSpec((1, n), lambda i, j: (0, 0)))
        args.append(final_gain)
        out_specs[0] = pl.BlockSpec((tm, n), lambda i, j: (i, 0))
    scratch = [] if prologue == "plain" else [pltpu.VMEM((tm, k), BF16)]
    outs = pl.pallas_call(
        functools.partial(_dense_body, prologue=prologue, has_res=res is not None,
                          has_extra=extra_cols is not None, has_final=final_gain is not None,
                          row_chunk=row_chunk, w_transposed=w_transposed),
        grid=(m // tm, n // tn),
        in_specs=in_specs,
        out_specs=out_specs,
        out_shape=out_shape,
        scratch_shapes=scratch,
        compiler_params=_cparams(2),
        name=name,
    )(*args)
    return outs if extra_cols is not None else outs[0]


def _cumsum_rows(x, group):
    rows = lax.broadcasted_iota(jnp.int32, x.shape, 0) % group
    shift = 1
    while shift < group:
        rolled = pltpu.roll(x, shift, axis=0)
        x = x + jnp.where(rows >= shift, rolled, 0.0)
        shift *= 2
    return x


def _cumsum_rows_mxu(x, group):
    n = x.shape[0]
    row = lax.broadcasted_iota(jnp.int32, (n, n), 0)
    col = lax.broadcasted_iota(jnp.int32, (n, n), 1)
    tri = jnp.where(jnp.logical_and(col <= row, col // group == row // group), 1.0, 0.0).astype(BF16)
    hi = x.astype(BF16)
    lo = (x - hi.astype(F32)).astype(BF16)
    return (jnp.dot(tri, hi, preferred_element_type=F32) + jnp.dot(tri, lo, preferred_element_type=F32))


def _lockstep(gens):
    gens = list(gens)
    results = [None] * len(gens)
    live = list(range(len(gens)))
    while live:
        still = []
        for n in live:
            try:
                next(gens[n])
                still.append(n)
            except StopIteration as stop:
                results[n] = stop.value
        live = still
    return results


def _recurrence_block(q, k, v, la, s_read, *, sub, mm_dtype):
    bt, kd = q.shape
    vd = v.shape[1]
    ns = bt // sub
    b_loc = _cumsum_rows_mxu(la, sub) if mm_dtype == BF16 else _cumsum_rows(la, sub)
    qd = q * jnp.exp(b_loc)
    ki = k * jnp.exp(-b_loc)
    tot = [b_loc[(i + 1) * sub - 1:(i + 1) * sub, :] for i in range(ns)]
    pre = [jnp.zeros((1, kd), F32)]
    for i in range(ns):
        pre.append(pre[-1] + tot[i])
    sl = [slice(i * sub, (i + 1) * sub) for i in range(ns)]
    vb = v.astype(mm_dtype)
    qdm = qd.astype(mm_dtype)
    q_state = jnp.concatenate([qd[sl[i]] * jnp.exp(pre[i]) for i in range(ns)], axis=0) if ns > 1 else qd
    q_state = q_state.astype(mm_dtype)
    ke = [ki[sl[i]] * jnp.exp(tot[i]) for i in range(ns)]
    kmats = []
    for i in range(ns):
        parts = [ke[j] if j == i - 1 else ke[j] * jnp.exp(pre[i] - pre[j + 1]) for j in range(i)]
        parts.append(ki[sl[i]])
        parts += [jnp.zeros((sub, kd), F32)] * (ns - 1 - i)
        kmats.append((jnp.concatenate(parts, axis=0) if ns > 1 else parts[0]).astype(mm_dtype))
    k_end = jnp.concatenate([ke[i] if i == ns - 1 else ke[i] * jnp.exp(pre[ns] - pre[i + 1])
                             for i in range(ns)], axis=0) if ns > 1 else ke[0]
    k_end = k_end.astype(mm_dtype)
    decay = jnp.broadcast_to(jnp.exp(pre[ns]), (LANE, kd)).T
    yield

    s_prev = s_read()
    o_state = jnp.dot(q_state, s_prev.astype(mm_dtype), preferred_element_type=F32)
    a_rows = [lax.dot_general(qdm[sl[i]], kmats[i], (((1,), (1,)), ((), ())), preferred_element_type=F32)
              for i in range(ns)]
    ds = lax.dot_general(k_end, vb, (((0,), (0,)), ((), ())), preferred_element_type=F32)
    yield

    a = jnp.concatenate(a_rows, axis=0) if ns > 1 else a_rows[0]
    row = lax.broadcasted_iota(jnp.int32, a.shape, 0)
    col = lax.broadcasted_iota(jnp.int32, a.shape, 1)
    a = jnp.where(col <= row, a, 0.0).astype(mm_dtype)
    o = jnp.dot(a, vb, preferred_element_type=F32) + o_state
    s_new = jnp.concatenate([decay * s_prev[:, c * LANE:(c + 1) * LANE] for c in range(vd // LANE)],
                            axis=1) + ds
    yield
    return o, s_new


def _head_gated_norm(o, gnorm, gate):
    ms = jnp.mean(o * o, axis=-1, keepdims=True)
    return o * lax.rsqrt(ms + EPS) * gnorm * (gate * _sigmoid(gate))


def _log_sigmoid(x):
    return jnp.minimum(x, 0.0) - jnp.log(1.0 + jnp.exp(-jnp.abs(x)))


N_MIXER_IN = 16


def _mixer_body(*refs, steps_per_seq, prompt_kw, sample_kw):
    n_p, n_s = N_MIXER_IN, N_MIXER_IN + 2
    p_in, s_in = refs[:n_p], refs[n_p:n_p + n_s]
    p_out, s_out = refs[n_p + n_s:n_p + n_s + 3], refs[n_p + n_s + 3:]
    step = pl.program_id(0)
    _mixer_part(p_in, p_out, mode="prompt", first_step=step % steps_per_seq == 0, **prompt_kw)
    _mixer_part(s_in, s_out, mode="sample", first_step=None, **sample_kw)


def _mixer_part(ins, outs, *, mode, first_step, layer, gla_heads, gla_dk, gla_dv, hgrn_heads, hgrn_dk,
                block, sub, mm_dtype):
    (q_ref, k_ref, v_ref, g_ref, ga_ref, wup_ref, ba_ref, gng_ref,
     hq_ref, hf_ref, hi_ref, hg_ref, ma_ref, mb_ref, lb_ref, gnh_ref) = ins[:N_MIXER_IN]
    om_ref, sg_out, sh_out = outs
    sg_in, sh_in = (None, None) if mode == "prompt" else ins[N_MIXER_IN:]

    p = lb_ref[...]
    e = jnp.exp(p - jnp.max(p, axis=0, keepdims=True))
    lb_all = jnp.sum(e[:layer + 1], axis=0, keepdims=True) / jnp.sum(e, axis=0, keepdims=True)

    def gla_head(h, load, s_read, shared):
        ks = slice(h * gla_dk, (h + 1) * gla_dk)
        vs = slice(h * gla_dv, (h + 1) * gla_dv)
        ga = load(ga_ref, slice(None)).astype(BF16)
        a_logit = jnp.dot(ga, wup_ref[:, ks], preferred_element_type=F32) + ba_ref[:, ks]
        yield
        la = _log_sigmoid(a_logit) * (1.0 / GLA_GATE_NORM)
        q = load(q_ref, ks) * (gla_dk ** -0.5)
        o, s_new = yield from _recurrence_block(q, load(k_ref, ks), load(v_ref, vs), la, s_read,
                                                sub=sub, mm_dtype=mm_dtype)
        shared[h] = _head_gated_norm(o, gng_ref[:, vs], load(g_ref, vs))
        return s_new

    def hgrn_head(h, load, s_read, shared):
        ks = slice(h * hgrn_dk, (h + 1) * hgrn_dk)
        lb = lb_all[:, ks]
        f = lb + (1.0 - lb) * _sigmoid(load(hf_ref, ks))
        hq = load(hq_ref, ks)
        q = hq * _sigmoid(hq) * (hgrn_dk ** -0.5)
        yield
        o, s_new = yield from _recurrence_block(q, 1.0 - f, load(hi_ref, ks), jnp.log(f), s_read,
                                                sub=sub, mm_dtype=mm_dtype)
        o_h = _head_gated_norm(o, gnh_ref[:, ks], load(hg_ref, ks))
        per = gla_dv // hgrn_dk
        o_gla = shared[h // per][:, (h % per) * hgrn_dk:(h % per + 1) * hgrn_dk]
        mix = _sigmoid(load(ma_ref, ks)) * o_gla + _sigmoid(load(mb_ref, ks)) * o_h
        return mix, s_new

    def run(load, sg_read, sh_read):
        shared = {}
        gens = [gla_head(h, load, functools.partial(sg_read, h), shared) for h in range(gla_heads)]
        gens += [hgrn_head(h, load, functools.partial(sh_read, h), shared) for h in range(hgrn_heads)]
        res = _lockstep(gens)
        return res[:gla_heads], res[gla_heads:]

    if mode == "prompt":
        @pl.when(first_step)
        def _():
            sg_out[...] = jnp.zeros(sg_out.shape, F32)
            sh_out[...] = jnp.zeros(sh_out.shape, F32)

        def body(c, carry):
            rows = pl.ds(pl.multiple_of(c * block, block), block)
            load = lambda ref, cols: ref[rows, cols].astype(F32)
            g_res, h_res = run(load, lambda h: sg_out[0, h], lambda h: sh_out[0, h])
            for h, s_new in enumerate(g_res):
                sg_out[0, h] = s_new
            for h, (mix, s_new) in enumerate(h_res):
                sh_out[0, h] = s_new
                om_ref[rows, h * hgrn_dk:(h + 1) * hgrn_dk] = mix.astype(om_ref.dtype)
            return carry
        lax.fori_loop(0, q_ref.shape[0] // block, body, 0)
    else:
        n_seq = q_ref.shape[0] // block
        mixes = []
        for s in range(n_seq):
            load = lambda ref, cols, s=s: ref[:, cols].astype(F32)[s * block:(s + 1) * block]
            g_res, h_res = run(load, lambda h, s=s: sg_in[s, h], lambda h, s=s: sh_in[s, h])
            for h, s_new in enumerate(g_res):
                sg_out[s, h] = s_new
            for h, (mix, s_new) in enumerate(h_res):
                sh_out[s, h] = s_new
            mixes.append([mix for mix, _ in h_res])
        for h in range(hgrn_heads):
            om_ref[:, h * hgrn_dk:(h + 1) * hgrn_dk] = jnp.concatenate(
                [mixes[s][h] for s in range(n_seq)], axis=0).astype(om_ref.dtype)


def _mixer(z_p, zga_p, z_s, zga_s, p, s_gla, s_hgrn, *, layer, n_prompt, prompt_len, n_sample, sample_len,
           d_model, name):
    gh, hk = GLA_HEADS, HGRN_DK
    gk, gv = d_model // 2 // gh, d_model // gh
    hh = d_model // hk
    kw = gh * gk
    n_steps = n_sample // SAMPLE_SEQS
    p_step = n_prompt * prompt_len // n_steps
    steps_per_seq = prompt_len // p_step
    assert prompt_len % p_step == 0 and p_step % PROMPT_BLOCK == 0
    s_step = SAMPLE_SEQS * sample_len
    whole = lambda a: pl.BlockSpec(a.shape, lambda i: (0,) * a.ndim)
    params = [p["gla_w_a_up"], p["gla_b_a"], p["gla_norm_g"], p["hgrn_lower_bound"], p["hgrn_norm_g"]]

    def group(z, zga, rows):
        zs = lambda width, col: pl.BlockSpec((rows, width), lambda i: (i, col))
        specs = [zs(kw, 0), zs(kw, 1), zs(d_model, 1), zs(d_model, 2), zs(zga.shape[1], 0)]
        specs += [whole(a) for a in params[:3]] + [zs(d_model, 3 + c) for c in range(6)]
        specs += [whole(a) for a in params[3:]]
        return specs, [z, z, z, z, zga] + params[:3] + [z] * 6 + params[3:], zs(d_model, 0)

    p_specs, p_args, p_om = group(z_p, zga_p, p_step)
    s_specs, s_args, s_om = group(z_s, zga_s, s_step)
    s_specs += [pl.BlockSpec((None, SAMPLE_SEQS, gh, gk, gv), lambda i: (layer, i, 0, 0, 0)),
                pl.BlockSpec((None, SAMPLE_SEQS, hh, hk, hk), lambda i: (layer, i, 0, 0, 0))]
    s_args += [s_gla, s_hgrn]
    p_st = lambda *dims: pl.BlockSpec((1,) + dims, lambda i: (i // steps_per_seq, 0, 0, 0))
    s_st = lambda *dims: pl.BlockSpec((SAMPLE_SEQS,) + dims, lambda i: (i, 0, 0, 0))
    common = dict(layer=layer, gla_heads=gh, gla_dk=gk, gla_dv=gv, hgrn_heads=hh, hgrn_dk=hk)
    prompt_kw = dict(common, block=PROMPT_BLOCK, sub=REF_CHUNK, mm_dtype=BF16)
    sample_kw = dict(common, block=sample_len, sub=math.gcd(REF_CHUNK, sample_len), mm_dtype=F32)
    shapes = lambda t, n: [jax.ShapeDtypeStruct((t, d_model), BF16), jax.ShapeDtypeStruct((n, gh, gk, gv), F32),
                           jax.ShapeDtypeStruct((n, hh, hk, hk), F32)]
    outs = pl.pallas_call(
        functools.partial(_mixer_body, steps_per_seq=steps_per_seq, prompt_kw=prompt_kw, sample_kw=sample_kw),
        grid=(n_steps,),
        in_specs=p_specs + s_specs,
        out_specs=[p_om, p_st(gh, gk, gv), p_st(hh, hk, hk), s_om, s_st(gh, gk, gv), s_st(hh, hk, hk)],
        out_shape=shapes(z_p.shape[0], n_prompt) + shapes(z_s.shape[0], n_sample),
        compiler_params=_cparams(1),
        name=name,
    )(*p_args, *s_args)
    return outs[:3], outs[3:]


def _xattn_sample_body(q_ref, k_ref, v_ref, o_ref, *, heads, dh, seq_len):
    scale = dh ** -0.5
    n_mem = k_ref.shape[1]
    for s in range(k_ref.shape[0]):
        rows = slice(s * seq_len, (s + 1) * seq_len)
        q = q_ref[rows, :]
        q2 = jnp.concatenate([q[:, h * dh:(h + 1) * dh] for h in range(heads)], axis=0)
        k2 = k_ref[s].reshape(n_mem * heads, dh)
        v2 = v_ref[s].reshape(n_mem * heads, dh)
        sc = lax.dot_general(q2, k2, (((1,), (1,)), ((), ())), preferred_element_type=F32) * scale
        q_head = lax.broadcasted_iota(jnp.int32, sc.shape, 0) // seq_len
        k_head = lax.broadcasted_iota(jnp.int32, sc.shape, 1) % heads
        sc = jnp.where(q_head == k_head, sc, -jnp.inf)
        p = jnp.exp(sc - jnp.max(sc, axis=-1, keepdims=True))
        o2 = jnp.dot(p, v2, preferred_element_type=F32) / jnp.sum(p, axis=-1, keepdims=True)
        for h in range(heads):
            o_ref[rows, h * dh:(h + 1) * dh] = o2[h * seq_len:(h + 1) * seq_len].astype(o_ref.dtype)


def _xattn_sample(mq, mem_k, mem_v, *, layer, n_seq, seq_len, name):
    t, d = mq.shape
    dh = d // MEM_HEADS
    n_mem = mem_k.shape[2]
    q_spec = pl.BlockSpec((SAMPLE_SEQS * seq_len, d), lambda i: (i, 0))
    kv_spec = pl.BlockSpec((None, SAMPLE_SEQS, n_mem, MEM_HEADS, dh), lambda i: (layer, i, 0, 0, 0))
    return pl.pallas_call(
        functools.partial(_xattn_sample_body, heads=MEM_HEADS, dh=dh, seq_len=seq_len),
        grid=(n_seq // SAMPLE_SEQS,),
        in_specs=[q_spec, kv_spec, kv_spec],
        out_specs=q_spec,
        out_shape=jax.ShapeDtypeStruct((t, d), F32),
        compiler_params=_cparams(1),
        name=name,
    )(mq, mem_k, mem_v)


def _mid_body(om_ref, x_ref, k_ref, v_ref, wo_ref, wq_ref, wm_ref, g_ref, o_ref, *, heads, dh):
    scale = dh ** -0.5
    x1 = x_ref[...] + jnp.dot(om_ref[...], wo_ref[...], preferred_element_type=F32)
    ms = jnp.mean(x1 * x1, axis=-1, keepdims=True)
    hx = (x1 * lax.rsqrt(ms + EPS) * g_ref[...]).astype(BF16)
    mq = jnp.dot(hx, wq_ref[...], preferred_element_type=F32).astype(BF16)
    heads_out = []
    for h in range(heads):
        hs = slice(h * dh, (h + 1) * dh)
        sc = lax.dot_general(mq[:, hs], k_ref[0, :, hs], (((1,), (1,)), ((), ())),
                             preferred_element_type=F32) * scale
        p = jnp.exp(sc - jnp.max(sc, axis=-1, keepdims=True))
        o = jnp.dot(p.astype(BF16), v_ref[0, :, hs], preferred_element_type=F32) / jnp.sum(p, axis=-1, keepdims=True)
        heads_out.append(o.astype(BF16))
    mo = jnp.concatenate(heads_out, axis=1)
    o_ref[...] = x1 + jnp.dot(mo, wm_ref[...], preferred_element_type=F32)


def _mid_prompt(o_mix, x, mem_k, mem_v, w_out, w_q, w_o, gain, *, n_seq, seq_len, name):
    t, d = x.shape
    n_mem = mem_k.shape[1]
    tm = 256
    n_t = seq_len // tm
    rows = lambda dtype_rows: pl.BlockSpec((tm, d), lambda b, i: (b * n_t + i, 0))
    kv = pl.BlockSpec((1, n_mem, d), lambda b, i: (b, 0, 0))
    wspec = pl.BlockSpec((d, d), lambda b, i: (0, 0), pipeline_mode=pl.Buffered(1))
    return pl.pallas_call(
        functools.partial(_mid_body, heads=MEM_HEADS, dh=d // MEM_HEADS),
        grid=(n_seq, n_t),
        in_specs=[rows(BF16), rows(F32), kv, kv, wspec, wspec, wspec, pl.BlockSpec((1, d), lambda b, i: (0, 0))],
        out_specs=rows(F32),
        out_shape=jax.ShapeDtypeStruct((t, d), F32),
        compiler_params=_cparams(2),
        name=name,
    )(o_mix, x, mem_k, mem_v, w_out, w_q, w_o, gain)


def _ffn_up_body(*refs, mode, seq_len, tiles_per_seq, row_chunk, last_shift):
    if mode == "prompt":
        x_ref, g_ref, wg_ref, wv_ref, cw_ref, cb_ref, act_ref, st_ref, h_ref, tail_ref = refs
        p1_ref = p2_ref = None
    else:
        x_ref, g_ref, wg_ref, wv_ref, cw_ref, cb_ref, buf_ref, act_ref, st_ref, h_ref = refs
        tail_ref = None
    i = pl.program_id(0)
    j = pl.program_id(1)

    @pl.when(j == 0)
    def _():
        def body(c, carry):
            rows = pl.ds(pl.multiple_of(c * row_chunk, row_chunk), row_chunk)
            xf = x_ref[rows, :]
            ms = jnp.mean(xf * xf, axis=-1, keepdims=True)
            h_ref[rows, :] = (xf * lax.rsqrt(ms + EPS) * g_ref[...]).astype(BF16)
            return carry
        lax.fori_loop(0, x_ref.shape[0] // row_chunk, body, 0)

    if mode == "prompt":
        @pl.when(i % tiles_per_seq == 0)
        def _():
            tail_ref[j] = jnp.zeros(tail_ref.shape[1:], F32)

    def step(shift):
        def place(a):
            if shift == 0:
                return a
            return jnp.concatenate([a[..., shift:], jnp.zeros(a.shape[:-1] + (shift,), a.dtype)], axis=-1)

        h = h_ref[...]
        ug = jnp.dot(h, wg_ref[...], preferred_element_type=F32)
        tm = ug.shape[0]
        row = lax.broadcasted_iota(jnp.int32, ug.shape, 0)
        roll1 = pltpu.roll(ug, 1, axis=0)
        roll2 = pltpu.roll(ug, 2, axis=0)
        if mode == "prompt":
            tail = tail_ref[j]
            prev1 = tail[7:8, :]
            prev2 = tail[6:7, :]
            sh1 = jnp.where(row >= 1, roll1, prev1)
            sh2 = jnp.where(row >= 2, roll2, jnp.where(row == 0, prev2, prev1))
            tail_ref[j] = ug[tm - 8:, :]
            st_ref[...] = place(ug[tm - 8:, :])
        else:
            buf = buf_ref[...]
            n_seq = buf.shape[0]
            spread = lambda r: jnp.broadcast_to(buf[:, r:r + 1, :],
                                                (n_seq, seq_len, buf.shape[2])).reshape(ug.shape)
            prev2, prev1 = spread(0), spread(1)
            pos = row % seq_len
            sh1 = jnp.where(pos >= 1, roll1, prev1)
            sh2 = jnp.where(pos >= 2, roll2, jnp.where(pos == 0, prev2, prev1))
            st_ref[...] = place(ug.reshape(n_seq, seq_len, ug.shape[1])[:, seq_len - (CONV_W - 1):, :])
        conv = cw_ref[0:1, :] * sh2 + cw_ref[1:2, :] * sh1 + cw_ref[2:3, :] * ug + cb_ref[...]
        gate = conv * _sigmoid(conv)
        uv = jnp.dot(h, wv_ref[...], preferred_element_type=F32)
        act_ref[...] = place((gate * uv).astype(act_ref.dtype))

    if last_shift == 0:
        step(0)
    else:
        last = pl.num_programs(1) - 1
        pl.when(j < last)(functools.partial(step, 0))
        pl.when(j == last)(functools.partial(step, last_shift))


def _ffn_up(x, gain, w_up, cw, cb, carry, *, mode, seq_len, name):
    m, k = x.shape
    d_ff = w_up.shape[1] // 2
    tm = _pick_tile(m, (1024,))
    tn = 4 * LANE
    nf = -(-d_ff // tn) * tn
    last_shift = nf - d_ff
    assert d_ff % LANE == 0 and d_ff >= tn
    col = lambda j, base=0: pl.multiple_of(base + jnp.minimum(j * tn, d_ff - tn), LANE)
    elem = lambda *dims: tuple(pl.Element(n) for n in dims)
    xs = pl.BlockSpec((tm, k), lambda i, j: (i, 0))
    gs = pl.BlockSpec((1, k), lambda i, j: (0, 0))
    wgs = pl.BlockSpec(elem(k, tn), lambda i, j: (0, col(j)))
    wvs = pl.BlockSpec(elem(k, tn), lambda i, j: (0, col(j, d_ff)))
    cws = pl.BlockSpec(elem(CONV_W, tn), lambda i, j: (0, col(j)))
    cbs = pl.BlockSpec(elem(1, tn), lambda i, j: (0, col(j)))
    ts = pl.BlockSpec((tm, tn), lambda i, j: (i, j))
    in_specs = [xs, gs, wgs, wvs, cws, cbs]
    args = [x, gain, w_up, w_up, cw, cb]
    scratch = [pltpu.VMEM((tm, k), BF16)]
    if mode == "prompt":
        assert seq_len % tm == 0
        scratch.append(pltpu.VMEM((nf // tn, 8, tn), F32))
        st_spec = pl.BlockSpec((None, 8, tn), lambda i, j: (i, 0, j))
        st_shape = jax.ShapeDtypeStruct((m // tm, 8, nf), F32)
    else:
        assert tm % seq_len == 0 and seq_len == 8
        n_blk = tm // seq_len
        in_specs.append(pl.BlockSpec(elem(n_blk, CONV_W - 1, tn), lambda i, j: (i * n_blk, 0, col(j))))
        args.append(carry)
        st_spec = pl.BlockSpec((n_blk, CONV_W - 1, tn), lambda i, j: (i, 0, j))
        st_shape = jax.ShapeDtypeStruct((m // seq_len, CONV_W - 1, nf), F32)
    return pl.pallas_call(
        functools.partial(_ffn_up_body, mode=mode, seq_len=seq_len, tiles_per_seq=max(seq_len // tm, 1),
                          row_chunk=128, last_shift=last_shift),
        grid=(m // tm, nf // tn),
        in_specs=in_specs,
        out_specs=[ts, st_spec],
        out_shape=[jax.ShapeDtypeStruct((m, nf), BF16), st_shape],
        scratch_shapes=scratch,
        compiler_params=_cparams(2),
        name=name,
    )(*args)


def _in_proj(x, p, name):
    ga_off, rank = p["ga_off"], p["rank"]
    w_in_t = p["w_in_t"]
    return _dense(x, w_in_t, gain=p["norm_mix_g"], prologue="norm", w_transposed=True, tn=x.shape[1],
                  out_dtype=BF16, name=name, n_cols=w_in_t.shape[0] - rank,
                  skip_cols=(ga_off, rank), extra_cols=(ga_off, rank))


def _after_mixer(o_mix, x, mem_k, mem_v, conv_buf, p, *, mode, layer, n_seq, seq_len, tag):
    t, d = x.shape
    nm = lambda s: f"{s}_{tag}"
    if mode == "prompt":
        x2 = _mid_prompt(o_mix, x, mem_k.astype(BF16), mem_v.astype(BF16), p["w_out"], p["w_mem_q"], p["w_mem_o"],
                         p["norm_xattn_g"], n_seq=n_seq, seq_len=seq_len, name=nm("mid"))
    else:
        x1 = _dense(o_mix, p["w_out"], res=x, prologue="plain", out_dtype=F32, name=nm("out_proj"))
        mq = _dense(x1, p["w_mem_q"], gain=p["norm_xattn_g"], prologue="norm", out_dtype=F32, name=nm("mem_q"))
        mo = _xattn_sample(mq, mem_k, mem_v, layer=layer, n_seq=n_seq, seq_len=seq_len, name=nm("xattn"))
        x2 = _dense(mo, p["w_mem_o"], res=x1, prologue="cast", out_dtype=F32, name=nm("mem_o"))

    d_ff = p["d_ff"]
    carry = None if mode == "prompt" else conv_buf[layer]
    act, ug_rows = _ffn_up(x2, p["norm_ffn_g"], p["w_up"], p["ffn_conv_w"], p["ffn_conv_b"],
                           carry, mode=mode, seq_len=seq_len, name=nm("ffn_up"))
    final = p["norm_final_g"] if p["is_last"] else None
    y = _dense(act, p["w_down"], res=x2, prologue="plain", out_dtype=F32, name=nm("ffn_down"),
               final_gain=final, tm=512 if final is not None else None, tn=1024 if final is not None else None)

    if mode == "prompt":
        tiles_per_seq = ug_rows.shape[0] // n_seq
        conv_new = ug_rows[tiles_per_seq - 1::tiles_per_seq, 8 - (CONV_W - 1):, :d_ff]
    else:
        conv_new = ug_rows[:, :, :d_ff]
    return y, conv_new


def kernel(x_prompt, x_sample, mem_prompt, state_gla, state_hgrn, state_ffn_conv, cache_mem_k, cache_mem_v,
           norm_mix_g, w_in, gla_w_a_up, gla_b_a, gla_norm_g, hgrn_lower_bound, hgrn_norm_g, w_out,
           norm_xattn_g, norm_mem_g, w_mem_q, w_mem_k, w_mem_v, w_mem_o,
           norm_ffn_g, w_ffn_up, ffn_conv_w, ffn_conv_b, w_ffn_down, norm_final_g):
    depth = w_in.shape[0]
    bp, lp, d = x_prompt.shape
    bs, ls, _ = x_sample.shape
    n_mem = mem_prompt.shape[1]
    d_ff = w_ffn_down.shape[1]
    rank = gla_w_a_up.shape[1]
    qk = gla_w_a_up.shape[2]
    ga_off = 2 * qk + 2 * d

    xp = x_prompt.reshape(bp * lp, d)
    xs = x_sample.reshape(bs * ls, d)
    mem = mem_prompt.reshape(bp * n_mem, d)
    row = lambda v: v.reshape(1, -1)

    outs = {k: [] for k in ("gla_p", "hgrn_p", "conv_p", "mk_p", "mv_p", "gla_s", "hgrn_s", "conv_s")}
    for l in range(depth):
        p = {
            "w_out": w_out[l].astype(BF16), "w_mem_q": w_mem_q[l].astype(BF16), "w_mem_o": w_mem_o[l].astype(BF16),
            "w_in_t": jnp.swapaxes(w_in[l], 0, 1).astype(BF16),
            "ga_off": ga_off, "rank": rank,
            "gla_w_a_up": gla_w_a_up[l].astype(BF16),
            "gla_b_a": row(gla_b_a[l]), "gla_norm_g": row(gla_norm_g[l]),
            "hgrn_lower_bound": hgrn_lower_bound, "hgrn_norm_g": row(hgrn_norm_g[l]),
            "norm_mix_g": row(norm_mix_g[l]), "norm_xattn_g": row(norm_xattn_g[l]),
            "norm_ffn_g": row(norm_ffn_g[l]), "norm_final_g": row(norm_final_g), "is_last": l == depth - 1,
            "w_up": w_ffn_up[l].astype(BF16), "w_down": w_ffn_down[l].astype(BF16),
            "ffn_conv_w": ffn_conv_w[l], "ffn_conv_b": row(ffn_conv_b[l]),
            "d_ff": d_ff,
        }
        g_mem = row(norm_mem_g[l])
        mk = _dense(mem, w_mem_k, gain=g_mem, prologue="norm", out_dtype=F32, name=f"mem_k_{l}", layer=l)
        mv = _dense(mem, w_mem_v, gain=g_mem, prologue="norm", out_dtype=F32, name=f"mem_v_{l}", layer=l)
        mk3, mv3 = mk.reshape(bp, n_mem, d), mv.reshape(bp, n_mem, d)
        outs["mk_p"].append(mk3.reshape(bp, n_mem, MEM_HEADS, d // MEM_HEADS))
        outs["mv_p"].append(mv3.reshape(bp, n_mem, MEM_HEADS, d // MEM_HEADS))
        zp, zga_p = _in_proj(xp, p, f"in_proj_p{l}")
        zs, zga_s = _in_proj(xs, p, f"in_proj_s{l}")
        (om_p, sg, sh), (om_s, sg2, sh2) = _mixer(
            zp, zga_p, zs, zga_s, p, state_gla, state_hgrn, layer=l, n_prompt=bp, prompt_len=lp,
            n_sample=bs, sample_len=ls, d_model=d, name=f"mixer_{l}")
        outs["gla_p"].append(sg); outs["hgrn_p"].append(sh)
        outs["gla_s"].append(sg2); outs["hgrn_s"].append(sh2)
        xp, cb = _after_mixer(om_p, xp, mk3, mv3, None, p, mode="prompt", layer=l, n_seq=bp, seq_len=lp,
                              tag=f"p{l}")
        xs, cb2 = _after_mixer(om_s, xs, cache_mem_k, cache_mem_v, state_ffn_conv, p, mode="sample", layer=l,
                               n_seq=bs, seq_len=ls, tag=f"s{l}")
        outs["conv_p"].append(cb); outs["conv_s"].append(cb2)

    y_prompt = xp.reshape(bp, lp, d)
    y_sample = xs.reshape(bs, ls, d)
    st = lambda k: jnp.stack(outs[k])
    return (y_prompt, y_sample, st("gla_p"), st("hgrn_p"), st("conv_p"), st("mk_p"), st("mv_p"),
            st("gla_s"), st("hgrn_s"), st("conv_s"))
```

```python
import functools
import math

import jax
import jax.numpy as jnp
from jax import lax
from jax.experimental import pallas as pl
from jax.experimental.pallas import tpu as pltpu

F32 = jnp.float32
BF16 = jnp.bfloat16

LANE = 128
BF16_ROWS = 16
V7X_VMEM_LIMIT = 56 * 1024 * 1024

EPS = 1e-6
GLA_HEADS = 4
GLA_GATE_NORM = 16.0
HGRN_DK = 128
MEM_HEADS = 4
CONV_W = 3
REF_CHUNK = 16
PROMPT_BLOCK = 64
SAMPLE_SEQS = 2


def _cparams(n_axes):
    return pltpu.CompilerParams(dimension_semantics=("arbitrary",) * n_axes,
                                vmem_limit_bytes=V7X_VMEM_LIMIT)


def _sigmoid(x):
    return 0.5 * jnp.tanh(0.5 * x) + 0.5


def _pick_tile(n, candidates):
    for c in candidates:
        if n % c == 0:
            return c
    return n


def _dense_body(*refs, prologue, has_res, has_extra, has_final, row_chunk, w_transposed):
    it = iter(refs)
    x_ref = next(it)
    g_ref = next(it) if prologue == "norm" else None
    w_ref = next(it)
    we_ref = next(it) if has_extra else None
    r_ref = next(it) if has_res else None
    fg_ref = next(it) if has_final else None
    o_ref = next(it)
    oe_ref = next(it) if has_extra else None
    h_ref = next(it) if prologue != "plain" else None
    j = pl.program_id(1)
    nt = (((1,), (1,)), ((), ()))

    if prologue != "plain":
        @pl.when(j == 0)
        def _():
            def body(c, carry):
                rows = pl.ds(pl.multiple_of(c * row_chunk, row_chunk), row_chunk)
                xf = x_ref[rows, :].astype(F32)
                if prologue == "norm":
                    ms = jnp.mean(xf * xf, axis=-1, keepdims=True)
                    xf = xf * lax.rsqrt(ms + EPS) * g_ref[...]
                h_ref[rows, :] = xf.astype(BF16)
                return carry
            lax.fori_loop(0, x_ref.shape[0] // row_chunk, body, 0)
            if has_extra:
                oe_ref[...] = lax.dot_general(h_ref[...], we_ref[...], nt, preferred_element_type=F32)
        lhs = h_ref[...]
    else:
        lhs = x_ref[...]
    w = w_ref[...].astype(BF16)
    if w_transposed:
        acc = lax.dot_general(lhs, w, nt, preferred_element_type=F32)
    else:
        acc = jnp.dot(lhs, w, preferred_element_type=F32)
    if has_res:
        acc = acc + r_ref[...]
    if not has_final:
        o_ref[...] = acc.astype(o_ref.dtype)
    else:
        tn = acc.shape[1]
        o_ref[:, pl.ds(pl.multiple_of(j * tn, tn), tn)] = acc

        @pl.when(j == pl.num_programs(1) - 1)
        def _():
            def body(c, carry):
                rows = pl.ds(pl.multiple_of(c * row_chunk, row_chunk), row_chunk)
                xf = o_ref[rows, :]
                ms = jnp.mean(xf * xf, axis=-1, keepdims=True)
                o_ref[rows, :] = xf * lax.rsqrt(ms + EPS) * fg_ref[...]
                return carry
            lax.fori_loop(0, o_ref.shape[0] // row_chunk, body, 0)


def _dense(x, w, *, gain=None, res=None, prologue, out_dtype, name, layer=0, col_off=0, n_cols=None,
           w_transposed=False, extra_cols=None, final_gain=None, tm=None, tn=None, skip_cols=None):
    m = x.shape[0]
    k = x.shape[1] if w_transposed else w.shape[-2]
    assert k == x.shape[1] or (k % LANE == 0 and k < x.shape[1] and prologue == "plain")
    n = n_cols if n_cols is not None else w.shape[-1]
    if tm is None:
        tm = _pick_tile(m, (1024, 512, 256, 128, 64, 32, 16))
    if tn is None:
        tn = _pick_tile(n, (1024, 512, 256, 128) if k <= 2048 else (512, 256, 128))
    assert m % tm == 0 and n % tn == 0
    row_chunk = min(tm, 128)
    in_specs = [pl.BlockSpec((tm, k), lambda i, j: (i, 0))]
    args = [x]
    if prologue == "norm":
        in_specs.append(pl.BlockSpec((1, k), lambda i, j: (0, 0)))
        args.append(gain)
    if w_transposed:
        assert col_off % BF16_ROWS == 0 and tn % BF16_ROWS == 0
        s_at, s_w = skip_cols if skip_cols is not None else (n, 0)
        assert s_at % tn == 0 and s_w % BF16_ROWS == 0
        in_specs.append(pl.BlockSpec(
            (pl.Element(tn), pl.Element(k)),
            lambda i, j: (pl.multiple_of(col_off + j * tn + jnp.where(j * tn >= s_at, s_w, 0), BF16_ROWS), 0)))
    elif w.ndim == 3:
        assert col_off % tn == 0
        in_specs.append(pl.BlockSpec((None, k, tn), lambda i, j: (layer, 0, j + col_off // tn)))
    else:
        in_specs.append(pl.BlockSpec((k, tn), lambda i, j: (0, j)))
    args.append(w)
    out_specs = [pl.BlockSpec((tm, tn), lambda i, j: (i, j))]
    out_shape = [jax.ShapeDtypeStruct((m, n), out_dtype)]
    if extra_cols is not None:
        e_off, e_n = extra_cols
        assert w_transposed and prologue != "plain" and e_off % BF16_ROWS == 0 and e_n % BF16_ROWS == 0
        in_specs.append(pl.BlockSpec((pl.Element(e_n), pl.Element(k)), lambda i, j: (e_off, 0)))
        args.append(w)
        out_specs.append(pl.BlockSpec((tm, e_n), lambda i, j: (i, 0)))
        out_shape.append(jax.ShapeDtypeStruct((m, e_n), F32))
    if res is not None:
        in_specs.append(pl.BlockSpec((tm, tn), lambda i, j: (i, j)))
        args.append(res)
    if final_gain is not None:
        assert out_dtype == F32
        in_specs.append(pl.BlockSpec((1, n), lambda i, j: (0, 0)))
        args.append(final_gain)
        out_specs[0] = pl.BlockSpec((tm, n), lambda i, j: (i, 0))
    scratch = [] if prologue == "plain" else [pltpu.VMEM((tm, k), BF16)]
    outs = pl.pallas_call(
        functools.partial(_dense_body, prologue=prologue, has_res=res is not None,
                          has_extra=extra_cols is not None, has_final=final_gain is not None,
                          row_chunk=row_chunk, w_transposed=w_transposed),
        grid=(m // tm, n // tn),
        in_specs=in_specs,
        out_specs=out_specs,
        out_shape=out_shape,
        scratch_shapes=scratch,
        compiler_params=_cparams(2),
        name=name,
    )(*args)
    return outs if extra_cols is not None else outs[0]


def _cumsum_rows(x, group):
    rows = lax.broadcasted_iota(jnp.int32, x.shape, 0) % group
    shift = 1
    while shift < group:
        rolled = pltpu.roll(x, shift, axis=0)
        x = x + jnp.where(rows >= shift, rolled, 0.0)
        shift *= 2
    return x


def _cumsum_rows_mxu(x, group):
    n = x.shape[0]
    row = lax.broadcasted_iota(jnp.int32, (n, n), 0)
    col = lax.broadcasted_iota(jnp.int32, (n, n), 1)
    tri = jnp.where(jnp.logical_and(col <= row, col // group == row // group), 1.0, 0.0).astype(BF16)
    hi = x.astype(BF16)
    lo = (x - hi.astype(F32)).astype(BF16)
    return (jnp.dot(tri, hi, preferred_element_type=F32) + jnp.dot(tri, lo, preferred_element_type=F32))


def _lockstep(gens):
    gens = list(gens)
    results = [None] * len(gens)
    live = list(range(len(gens)))
    while live:
        still = []
        for n in live:
            try:
                next(gens[n])
                still.append(n)
            except StopIteration as stop:
                results[n] = stop.value
        live = still
    return results


def _recurrence_block(q, k, v, la, s_read, *, sub, mm_dtype):
    bt, kd = q.shape
    vd = v.shape[1]
    ns = bt // sub
    b_loc = _cumsum_rows_mxu(la, sub) if mm_dtype == BF16 else _cumsum_rows(la, sub)
    qd = q * jnp.exp(b_loc)
    ki = k * jnp.exp(-b_loc)
    tot = [b_loc[(i + 1) * sub - 1:(i + 1) * sub, :] for i in range(ns)]
    pre = [jnp.zeros((1, kd), F32)]
    for i in range(ns):
        pre.append(pre[-1] + tot[i])
    sl = [slice(i * sub, (i + 1) * sub) for i in range(ns)]
    vb = v.astype(mm_dtype)
    qdm = qd.astype(mm_dtype)
    q_state = jnp.concatenate([qd[sl[i]] * jnp.exp(pre[i]) for i in range(ns)], axis=0) if ns > 1 else qd
    q_state = q_state.astype(mm_dtype)
    ke = [ki[sl[i]] * jnp.exp(tot[i]) for i in range(ns)]
    kmats = []
    for i in range(ns):
        parts = [ke[j] if j == i - 1 else ke[j] * jnp.exp(pre[i] - pre[j + 1]) for j in range(i)]
        parts.append(ki[sl[i]])
        parts += [jnp.zeros((sub, kd), F32)] * (ns - 1 - i)
        kmats.append((jnp.concatenate(parts, axis=0) if ns > 1 else parts[0]).astype(mm_dtype))
    k_end = jnp.concatenate([ke[i] if i == ns - 1 else ke[i] * jnp.exp(pre[ns] - pre[i + 1])
                             for i in range(ns)], axis=0) if ns > 1 else ke[0]
    k_end = k_end.astype(mm_dtype)
    decay = jnp.broadcast_to(jnp.exp(pre[ns]), (LANE, kd)).T
    yield

    s_prev = s_read()
    o_state = jnp.dot(q_state, s_prev.astype(mm_dtype), preferred_element_type=F32)
    a_rows = [lax.dot_general(qdm[sl[i]], kmats[i], (((1,), (1,)), ((), ())), preferred_element_type=F32)
              for i in range(ns)]
    ds = lax.dot_general(k_end, vb, (((0,), (0,)), ((), ())), preferred_element_type=F32)
    yield

    a = jnp.concatenate(a_rows, axis=0) if ns > 1 else a_rows[0]
    row = lax.broadcasted_iota(jnp.int32, a.shape, 0)
    col = lax.broadcasted_iota(jnp.int32, a.shape, 1)
    a = jnp.where(col <= row, a, 0.0).astype(mm_dtype)
    o = jnp.dot(a, vb, preferred_element_type=F32) + o_state
    s_new = jnp.concatenate([decay * s_prev[:, c * LANE:(c + 1) * LANE] for c in range(vd // LANE)],
                            axis=1) + ds
    yield
    return o, s_new


def _head_gated_norm(o, gnorm, gate):
    ms = jnp.mean(o * o, axis=-1, keepdims=True)
    return o * lax.rsqrt(ms + EPS) * gnorm * (gate * _sigmoid(gate))


def _log_sigmoid(x):
    return jnp.minimum(x, 0.0) - jnp.log(1.0 + jnp.exp(-jnp.abs(x)))


N_MIXER_IN = 16


def _mixer_body(*refs, steps_per_seq, prompt_kw, sample_kw):
    n_p, n_s = N_MIXER_IN, N_MIXER_IN + 2
    p_in, s_in = refs[:n_p], refs[n_p:n_p + n_s]
    p_out, s_out = refs[n_p + n_s:n_p + n_s + 3], refs[n_p + n_s + 3:]
    step = pl.program_id(0)
    _mixer_part(p_in, p_out, mode="prompt", first_step=step % steps_per_seq == 0, **prompt_kw)
    _mixer_part(s_in, s_out, mode="sample", first_step=None, **sample_kw)


def _mixer_part(ins, outs, *, mode, first_step, layer, gla_heads, gla_dk, gla_dv, hgrn_heads, hgrn_dk,
                block, sub, mm_dtype):
    (q_ref, k_ref, v_ref, g_ref, ga_ref, wup_ref, ba_ref, gng_ref,
     hq_ref, hf_ref, hi_ref, hg_ref, ma_ref, mb_ref, lb_ref, gnh_ref) = ins[:N_MIXER_IN]
    om_ref, sg_out, sh_out = outs
    sg_in, sh_in = (None, None) if mode == "prompt" else ins[N_MIXER_IN:]

    p = lb_ref[...]
    e = jnp.exp(p - jnp.max(p, axis=0, keepdims=True))
    lb_all = jnp.sum(e[:layer + 1], axis=0, keepdims=True) / jnp.sum(e, axis=0, keepdims=True)

    def gla_head(h, load, s_read, shared):
        ks = slice(h * gla_dk, (h + 1) * gla_dk)
        vs = slice(h * gla_dv, (h + 1) * gla_dv)
        ga = load(ga_ref, slice(None)).astype(BF16)
        a_logit = jnp.dot(ga, wup_ref[:, ks], preferred_element_type=F32) + ba_ref[:, ks]
        yield
        la = _log_sigmoid(a_logit) * (1.0 / GLA_GATE_NORM)
        q = load(q_ref, ks) * (gla_dk ** -0.5)
        o, s_new = yield from _recurrence_block(q, load(k_ref, ks), load(v_ref, vs), la, s_read,
                                                sub=sub, mm_dtype=mm_dtype)
        shared[h] = _head_gated_norm(o, gng_ref[:, vs], load(g_ref, vs))
        return s_new

    def hgrn_head(h, load, s_read, shared):
        ks = slice(h * hgrn_dk, (h + 1) * hgrn_dk)
        lb = lb_all[:, ks]
        f = lb + (1.0 - lb) * _sigmoid(load(hf_ref, ks))
        hq = load(hq_ref, ks)
        q = hq * _sigmoid(hq) * (hgrn_dk ** -0.5)
        yield
        o, s_new = yield from _recurrence_block(q, 1.0 - f, load(hi_ref, ks), jnp.log(f), s_read,
                                                sub=sub, mm_dtype=mm_dtype)
        o_h = _head_gated_norm(o, gnh_ref[:, ks], load(hg_ref, ks))
        per = gla_dv // hgrn_dk
        o_gla = shared[h // per][:, (h % per) * hgrn_dk:(h % per + 1) * hgrn_dk]
        mix = _sigmoid(load(ma_ref, ks)) * o_gla + _sigmoid(load(mb_ref, ks)) * o_h
        return mix, s_new

    def run(load, sg_read, sh_read):
        shared = {}
        gens = [gla_head(h, load, functools.partial(sg_read, h), shared) for h in range(gla_heads)]
        gens += [hgrn_head(h, load, functools.partial(sh_read, h), shared) for h in range(hgrn_heads)]
        res = _lockstep(gens)
        return res[:gla_heads], res[gla_heads:]

    if mode == "prompt":
        @pl.when(first_step)
        def _():
            sg_out[...] = jnp.zeros(sg_out.shape, F32)
            sh_out[...] = jnp.zeros(sh_out.shape, F32)

        def body(c, carry):
            rows = pl.ds(pl.multiple_of(c * block, block), block)
            load = lambda ref, cols: ref[rows, cols].astype(F32)
            g_res, h_res = run(load, lambda h: sg_out[0, h], lambda h: sh_out[0, h])
            for h, s_new in enumerate(g_res):
                sg_out[0, h] = s_new
            for h, (mix, s_new) in enumerate(h_res):
                sh_out[0, h] = s_new
                om_ref[rows, h * hgrn_dk:(h + 1) * hgrn_dk] = mix.astype(om_ref.dtype)
            return carry
        lax.fori_loop(0, q_ref.shape[0] // block, body, 0)
    else:
        n_seq = q_ref.shape[0] // block
        mixes = []
        for s in range(n_seq):
            load = lambda ref, cols, s=s: ref[:, cols].astype(F32)[s * block:(s + 1) * block]
            g_res, h_res = run(load, lambda h, s=s: sg_in[s, h], lambda h, s=s: sh_in[s, h])
            for h, s_new in enumerate(g_res):
                sg_out[s, h] = s_new
            for h, (mix, s_new) in enumerate(h_res):
                sh_out[s, h] = s_new
            mixes.append([mix for mix, _ in h_res])
        for h in range(hgrn_heads):
            om_ref[:, h * hgrn_dk:(h + 1) * hgrn_dk] = jnp.concatenate(
                [mixes[s][h] for s in range(n_seq)], axis=0).astype(om_ref.dtype)


def _mixer(z_p, zga_p, z_s, zga_s, p, s_gla, s_hgrn, *, layer, n_prompt, prompt_len, n_sample, sample_len,
           d_model, name):
    gh, hk = GLA_HEADS, HGRN_DK
    gk, gv = d_model // 2 // gh, d_model // gh
    hh = d_model // hk
    kw = gh * gk
    n_steps = n_sample // SAMPLE_SEQS
    p_step = n_prompt * prompt_len // n_steps
    steps_per_seq = prompt_len // p_step
    assert prompt_len % p_step == 0 and p_step % PROMPT_BLOCK == 0
    s_step = SAMPLE_SEQS * sample_len
    whole = lambda a: pl.BlockSpec(a.shape, lambda i: (0,) * a.ndim)
    params = [p["gla_w_a_up"], p["gla_b_a"], p["gla_norm_g"], p["hgrn_lower_bound"], p["hgrn_norm_g"]]

    def group(z, zga, rows):
        zs = lambda width, col: pl.BlockSpec((rows, width), lambda i: (i, col))
        specs = [zs(kw, 0), zs(kw, 1), zs(d_model, 1), zs(d_model, 2), zs(zga.shape[1], 0)]
        specs += [whole(a) for a in params[:3]] + [zs(d_model, 3 + c) for c in range(6)]
        specs += [whole(a) for a in params[3:]]
        return specs, [z, z, z, z, zga] + params[:3] + [z] * 6 + params[3:], zs(d_model, 0)

    p_specs, p_args, p_om = group(z_p, zga_p, p_step)
    s_specs, s_args, s_om = group(z_s, zga_s, s_step)
    s_specs += [pl.BlockSpec((None, SAMPLE_SEQS, gh, gk, gv), lambda i: (layer, i, 0, 0, 0)),
                pl.BlockSpec((None, SAMPLE_SEQS, hh, hk, hk), lambda i: (layer, i, 0, 0, 0))]
    s_args += [s_gla, s_hgrn]
    p_st = lambda *dims: pl.BlockSpec((1,) + dims, lambda i: (i // steps_per_seq, 0, 0, 0))
    s_st = lambda *dims: pl.BlockSpec((SAMPLE_SEQS,) + dims, lambda i: (i, 0, 0, 0))
    common = dict(layer=layer, gla_heads=gh, gla_dk=gk, gla_dv=gv, hgrn_heads=hh, hgrn_dk=hk)
    prompt_kw = dict(common, block=PROMPT_BLOCK, sub=REF_CHUNK, mm_dtype=BF16)
    sample_kw = dict(common, block=sample_len, sub=math.gcd(REF_CHUNK, sample_len), mm_dtype=F32)
    shapes = lambda t, n: [jax.ShapeDtypeStruct((t, d_model), BF16), jax.ShapeDtypeStruct((n, gh, gk, gv), F32),
                           jax.ShapeDtypeStruct((n, hh, hk, hk), F32)]
    outs = pl.pallas_call(
        functools.partial(_mixer_body, steps_per_seq=steps_per_seq, prompt_kw=prompt_kw, sample_kw=sample_kw),
        grid=(n_steps,),
        in_specs=p_specs + s_specs,
        out_specs=[p_om, p_st(gh, gk, gv), p_st(hh, hk, hk), s_om, s_st(gh, gk, gv), s_st(hh, hk, hk)],
        out_shape=shapes(z_p.shape[0], n_prompt) + shapes(z_s.shape[0], n_sample),
        compiler_params=_cparams(1),
        name=name,
    )(*p_args, *s_args)
    return outs[:3], outs[3:]


def _xattn_sample_body(q_ref, k_ref, v_ref, o_ref, *, heads, dh, seq_len):
    scale = dh ** -0.5
    n_mem = k_ref.shape[1]
    for s in range(k_ref.shape[0]):
        rows = slice(s * seq_len, (s + 1) * seq_len)
        q = q_ref[rows, :]
        q2 = jnp.concatenate([q[:, h * dh:(h + 1) * dh] for h in range(heads)], axis=0)
        k2 = k_ref[s].reshape(n_mem * heads, dh)
        v2 = v_ref[s].reshape(n_mem * heads, dh)
        sc = lax.dot_general(q2, k2, (((1,), (1,)), ((), ())), preferred_element_type=F32) * scale
        q_head = lax.broadcasted_iota(jnp.int32, sc.shape, 0) // seq_len
        k_head = lax.broadcasted_iota(jnp.int32, sc.shape, 1) % heads
        sc = jnp.where(q_head == k_head, sc, -jnp.inf)
        p = jnp.exp(sc - jnp.max(sc, axis=-1, keepdims=True))
        o2 = jnp.dot(p, v2, preferred_element_type=F32) / jnp.sum(p, axis=-1, keepdims=True)
        for h in range(heads):
            o_ref[rows, h * dh:(h + 1) * dh] = o2[h * seq_len:(h + 1) * seq_len].astype(o_ref.dtype)


def _mid_body(om_ref, x_ref, k_ref, v_ref, wo_ref, wq_ref, wm_ref, g_ref, o_ref, *, heads, dh):
    scale = dh ** -0.5
    x1 = x_ref[...] + jnp.dot(om_ref[...], wo_ref[...], preferred_element_type=F32)
    ms = jnp.mean(x1 * x1, axis=-1, keepdims=True)
    hx = (x1 * lax.rsqrt(ms + EPS) * g_ref[...]).astype(BF16)
    mq = jnp.dot(hx, wq_ref[...], preferred_element_type=F32).astype(BF16)
    heads_out = []
    for h in range(heads):
        hs = slice(h * dh, (h + 1) * dh)
        sc = lax.dot_general(mq[:, hs], k_ref[0, :, hs], (((1,), (1,)), ((), ())),
                             preferred_element_type=F32) * scale
        p = jnp.exp(sc - jnp.max(sc, axis=-1, keepdims=True))
        o = jnp.dot(p.astype(BF16), v_ref[0, :, hs], preferred_element_type=F32) / jnp.sum(p, axis=-1, keepdims=True)
        heads_out.append(o.astype(BF16))
    mo = jnp.concatenate(heads_out, axis=1)
    o_ref[...] = x1 + jnp.dot(mo, wm_ref[...], preferred_element_type=F32)


def _mid_prompt(o_mix, x, mem_k, mem_v, w_out, w_q, w_o, gain, *, n_seq, seq_len, name):
    t, d = x.shape
    n_mem = mem_k.shape[1]
    tm = 256
    n_t = seq_len // tm
    rows = lambda dtype_rows: pl.BlockSpec((tm, d), lambda b, i: (b * n_t + i, 0))
    kv = pl.BlockSpec((1, n_mem, d), lambda b, i: (b, 0, 0))
    wspec = pl.BlockSpec((d, d), lambda b, i: (0, 0), pipeline_mode=pl.Buffered(1))
    return pl.pallas_call(
        functools.partial(_mid_body, heads=MEM_HEADS, dh=d // MEM_HEADS),
        grid=(n_seq, n_t),
        in_specs=[rows(BF16), rows(F32), kv, kv, wspec, wspec, wspec, pl.BlockSpec((1, d), lambda b, i: (0, 0))],
        out_specs=rows(F32),
        out_shape=jax.ShapeDtypeStruct((t, d), F32),
        compiler_params=_cparams(2),
        name=name,
    )(o_mix, x, mem_k, mem_v, w_out, w_q, w_o, gain)


def _ffn_up_body(*refs, mode, seq_len, tiles_per_seq, row_chunk, last_shift, side):
    if mode == "prompt" and side is not None:
        (x_ref, g_ref, wg_ref, wv_ref, cw_ref, cb_ref, sq_ref, sk_ref, sv_ref,
         act_ref, st_ref, so_ref, h_ref, tail_ref) = refs
    elif mode == "prompt":
        x_ref, g_ref, wg_ref, wv_ref, cw_ref, cb_ref, act_ref, st_ref, h_ref, tail_ref = refs
    else:
        x_ref, g_ref, wg_ref, wv_ref, cw_ref, cb_ref, buf_ref, act_ref, st_ref, h_ref = refs
        tail_ref = None
    i = pl.program_id(0)
    j = pl.program_id(1)

    if side is not None:
        @pl.when(i * pl.num_programs(1) + j < side["n_steps"])
        def _():
            _xattn_sample_body(sq_ref, sk_ref, sv_ref, so_ref, heads=MEM_HEADS, dh=side["dh"],
                               seq_len=side["seq_len"])

    @pl.when(j == 0)
    def _():
        def body(c, carry):
            rows = pl.ds(pl.multiple_of(c * row_chunk, row_chunk), row_chunk)
            xf = x_ref[rows, :]
            ms = jnp.mean(xf * xf, axis=-1, keepdims=True)
            h_ref[rows, :] = (xf * lax.rsqrt(ms + EPS) * g_ref[...]).astype(BF16)
            return carry
        lax.fori_loop(0, x_ref.shape[0] // row_chunk, body, 0)

    if mode == "prompt":
        @pl.when(i % tiles_per_seq == 0)
        def _():
            tail_ref[j] = jnp.zeros(tail_ref.shape[1:], F32)

    def step(shift):
        def place(a):
            if shift == 0:
                return a
            return jnp.concatenate([a[..., shift:], jnp.zeros(a.shape[:-1] + (shift,), a.dtype)], axis=-1)

        h = h_ref[...]
        ug = jnp.dot(h, wg_ref[...], preferred_element_type=F32)
        tm = ug.shape[0]
        row = lax.broadcasted_iota(jnp.int32, ug.shape, 0)
        roll1 = pltpu.roll(ug, 1, axis=0)
        roll2 = pltpu.roll(ug, 2, axis=0)
        if mode == "prompt":
            tail = tail_ref[j]
            prev1 = tail[7:8, :]
            prev2 = tail[6:7, :]
            sh1 = jnp.where(row >= 1, roll1, prev1)
            sh2 = jnp.where(row >= 2, roll2, jnp.where(row == 0, prev2, prev1))
            tail_ref[j] = ug[tm - 8:, :]
            st_ref[...] = place(ug[tm - 8:, :])
        else:
            buf = buf_ref[...]
            n_seq = buf.shape[0]
            spread = lambda r: jnp.broadcast_to(buf[:, r:r + 1, :],
                                                (n_seq, seq_len, buf.shape[2])).reshape(ug.shape)
            prev2, prev1 = spread(0), spread(1)
            pos = row % seq_len
            sh1 = jnp.where(pos >= 1, roll1, prev1)
            sh2 = jnp.where(pos >= 2, roll2, jnp.where(pos == 0, prev2, prev1))
            st_ref[...] = place(ug.reshape(n_seq, seq_len, ug.shape[1])[:, seq_len - (CONV_W - 1):, :])
        conv = cw_ref[0:1, :] * sh2 + cw_ref[1:2, :] * sh1 + cw_ref[2:3, :] * ug + cb_ref[...]
        gate = conv * _sigmoid(conv)
        uv = jnp.dot(h, wv_ref[...], preferred_element_type=F32)
        act_ref[...] = place((gate * uv).astype(act_ref.dtype))

    if last_shift == 0:
        step(0)
    else:
        last = pl.num_programs(1) - 1
        pl.when(j < last)(functools.partial(step, 0))
        pl.when(j == last)(functools.partial(step, last_shift))


def _ffn_up(x, gain, w_up, cw, cb, carry, *, mode, seq_len, name, side=None):
    m, k = x.shape
    d_ff = w_up.shape[1] // 2
    tm = _pick_tile(m, (1024,))
    tn = 4 * LANE
    nf = -(-d_ff // tn) * tn
    last_shift = nf - d_ff
    assert d_ff % LANE == 0 and d_ff >= tn
    col = lambda j, base=0: pl.multiple_of(base + jnp.minimum(j * tn, d_ff - tn), LANE)
    elem = lambda *dims: tuple(pl.Element(n) for n in dims)
    xs = pl.BlockSpec((tm, k), lambda i, j: (i, 0))
    gs = pl.BlockSpec((1, k), lambda i, j: (0, 0))
    wgs = pl.BlockSpec(elem(k, tn), lambda i, j: (0, col(j)))
    wvs = pl.BlockSpec(elem(k, tn), lambda i, j: (0, col(j, d_ff)))
    cws = pl.BlockSpec(elem(CONV_W, tn), lambda i, j: (0, col(j)))
    cbs = pl.BlockSpec(elem(1, tn), lambda i, j: (0, col(j)))
    ts = pl.BlockSpec((tm, tn), lambda i, j: (i, j))
    in_specs = [xs, gs, wgs, wvs, cws, cbs]
    args = [x, gain, w_up, w_up, cw, cb]
    scratch = [pltpu.VMEM((tm, k), BF16)]
    if mode == "prompt":
        assert seq_len % tm == 0
        scratch.append(pltpu.VMEM((nf // tn, 8, tn), F32))
        st_spec = pl.BlockSpec((None, 8, tn), lambda i, j: (i, 0, j))
        st_shape = jax.ShapeDtypeStruct((m // tm, 8, nf), F32)
    else:
        assert tm % seq_len == 0 and seq_len == 8
        n_blk = tm // seq_len
        in_specs.append(pl.BlockSpec(elem(n_blk, CONV_W - 1, tn), lambda i, j: (i * n_blk, 0, col(j))))
        args.append(carry)
        st_spec = pl.BlockSpec((n_blk, CONV_W - 1, tn), lambda i, j: (i, 0, j))
        st_shape = jax.ShapeDtypeStruct((m // seq_len, CONV_W - 1, nf), F32)
    n_j = nf // tn
    out_specs = [ts, st_spec]
    out_shape = [jax.ShapeDtypeStruct((m, nf), BF16), st_shape]
    side_kw = None
    if side is not None:
        assert mode == "prompt"
        mq, cache_k, cache_v, layer, s_len = side
        n_side = mq.shape[0] // (SAMPLE_SEQS * s_len)
        assert n_side <= (m // tm) * n_j
        grp = lambda i, j: jnp.minimum(i * n_j + j, n_side - 1)
        q_spec = pl.BlockSpec((SAMPLE_SEQS * s_len, mq.shape[1]), lambda i, j: (grp(i, j), 0))
        kv_spec = pl.BlockSpec((None, SAMPLE_SEQS) + cache_k.shape[2:], lambda i, j: (layer, grp(i, j), 0, 0, 0))
        in_specs += [q_spec, kv_spec, kv_spec]
        args += [mq, cache_k, cache_v]
        out_specs.append(q_spec)
        out_shape.append(jax.ShapeDtypeStruct(mq.shape, F32))
        side_kw = dict(n_steps=n_side, dh=cache_k.shape[-1], seq_len=s_len)
    return pl.pallas_call(
        functools.partial(_ffn_up_body, mode=mode, seq_len=seq_len, tiles_per_seq=max(seq_len // tm, 1),
                          row_chunk=128, last_shift=last_shift, side=side_kw),
        grid=(m // tm, n_j),
        in_specs=in_specs,
        out_specs=out_specs,
        out_shape=out_shape,
        scratch_shapes=scratch,
        compiler_params=_cparams(2),
        name=name,
    )(*args)


def _in_proj(x, p, name):
    ga_off, rank = p["ga_off"], p["rank"]
    w_in_t = p["w_in_t"]
    return _dense(x, w_in_t, gain=p["norm_mix_g"], prologue="norm", w_transposed=True, tn=x.shape[1],
                  out_dtype=BF16, name=name, n_cols=w_in_t.shape[0] - rank,
                  skip_cols=(ga_off, rank), extra_cols=(ga_off, rank))


def _conv_ffn(x2, carry, p, *, mode, n_seq, seq_len, tag, side=None):
    d_ff = p["d_ff"]
    act, ug_rows, *mo_side = _ffn_up(x2, p["norm_ffn_g"], p["w_up"], p["ffn_conv_w"], p["ffn_conv_b"], carry,
                                     mode=mode, seq_len=seq_len, name=f"ffn_up_{tag}", side=side)
    final = p["norm_final_g"] if p["is_last"] else None
    y = _dense(act, p["w_down"], res=x2, prologue="plain", out_dtype=F32, name=f"ffn_down_{tag}",
               final_gain=final, tm=512 if final is not None else None, tn=1024 if final is not None else None)
    if mode == "prompt":
        tiles_per_seq = ug_rows.shape[0] // n_seq
        conv_new = ug_rows[tiles_per_seq - 1::tiles_per_seq, 8 - (CONV_W - 1):, :d_ff]
    else:
        conv_new = ug_rows[:, :, :d_ff]
    return y, conv_new, (mo_side[0] if mo_side else None)


def kernel(x_prompt, x_sample, mem_prompt, state_gla, state_hgrn, state_ffn_conv, cache_mem_k, cache_mem_v,
           norm_mix_g, w_in, gla_w_a_up, gla_b_a, gla_norm_g, hgrn_lower_bound, hgrn_norm_g, w_out,
           norm_xattn_g, norm_mem_g, w_mem_q, w_mem_k, w_mem_v, w_mem_o,
           norm_ffn_g, w_ffn_up, ffn_conv_w, ffn_conv_b, w_ffn_down, norm_final_g):
    depth = w_in.shape[0]
    bp, lp, d = x_prompt.shape
    bs, ls, _ = x_sample.shape
    n_mem = mem_prompt.shape[1]
    d_ff = w_ffn_down.shape[1]
    rank = gla_w_a_up.shape[1]
    qk = gla_w_a_up.shape[2]
    ga_off = 2 * qk + 2 * d

    xp = x_prompt.reshape(bp * lp, d)
    xs = x_sample.reshape(bs * ls, d)
    mem = mem_prompt.reshape(bp * n_mem, d)
    row = lambda v: v.reshape(1, -1)

    outs = {k: [] for k in ("gla_p", "hgrn_p", "conv_p", "mk_p", "mv_p", "gla_s", "hgrn_s", "conv_s")}
    for l in range(depth):
        p = {
            "w_out": w_out[l].astype(BF16), "w_mem_q": w_mem_q[l].astype(BF16), "w_mem_o": w_mem_o[l].astype(BF16),
            "w_in_t": jnp.swapaxes(w_in[l], 0, 1).astype(BF16),
            "ga_off": ga_off, "rank": rank,
            "gla_w_a_up": gla_w_a_up[l].astype(BF16),
            "gla_b_a": row(gla_b_a[l]), "gla_norm_g": row(gla_norm_g[l]),
            "hgrn_lower_bound": hgrn_lower_bound, "hgrn_norm_g": row(hgrn_norm_g[l]),
            "norm_mix_g": row(norm_mix_g[l]), "norm_xattn_g": row(norm_xattn_g[l]),
            "norm_ffn_g": row(norm_ffn_g[l]), "norm_final_g": row(norm_final_g), "is_last": l == depth - 1,
            "w_up": w_ffn_up[l].astype(BF16), "w_down": w_ffn_down[l].astype(BF16),
            "ffn_conv_w": ffn_conv_w[l], "ffn_conv_b": row(ffn_conv_b[l]),
            "d_ff": d_ff,
        }
        g_mem = row(norm_mem_g[l])
        mk = _dense(mem, w_mem_k, gain=g_mem, prologue="norm", out_dtype=F32, name=f"mem_k_{l}", layer=l)
        mv = _dense(mem, w_mem_v, gain=g_mem, prologue="norm", out_dtype=F32, name=f"mem_v_{l}", layer=l)
        mk3, mv3 = mk.reshape(bp, n_mem, d), mv.reshape(bp, n_mem, d)
        outs["mk_p"].append(mk3.reshape(bp, n_mem, MEM_HEADS, d // MEM_HEADS))
        outs["mv_p"].append(mv3.reshape(bp, n_mem, MEM_HEADS, d // MEM_HEADS))
        zp, zga_p = _in_proj(xp, p, f"in_proj_p{l}")
        zs, zga_s = _in_proj(xs, p, f"in_proj_s{l}")
        (om_p, sg, sh), (om_s, sg2, sh2) = _mixer(
            zp, zga_p, zs, zga_s, p, state_gla, state_hgrn, layer=l, n_prompt=bp, prompt_len=lp,
            n_sample=bs, sample_len=ls, d_model=d, name=f"mixer_{l}")
        outs["gla_p"].append(sg); outs["hgrn_p"].append(sh)
        outs["gla_s"].append(sg2); outs["hgrn_s"].append(sh2)
        x1_s = _dense(om_s, p["w_out"], res=xs, prologue="plain", out_dtype=F32, name=f"out_proj_s{l}")
        mq_s = _dense(x1_s, p["w_mem_q"], gain=p["norm_xattn_g"], prologue="norm", out_dtype=F32,
                      name=f"mem_q_s{l}")
        x2_p = _mid_prompt(om_p, xp, mk3.astype(BF16), mv3.astype(BF16), p["w_out"], p["w_mem_q"], p["w_mem_o"],
                           p["norm_xattn_g"], n_seq=bp, seq_len=lp, name=f"mid_p{l}")
        xp, cb, mo_s = _conv_ffn(x2_p, None, p, mode="prompt", n_seq=bp, seq_len=lp, tag=f"p{l}",
                                 side=(mq_s, cache_mem_k, cache_mem_v, l, ls))
        x2_s = _dense(mo_s, p["w_mem_o"], res=x1_s, prologue="cast", out_dtype=F32, name=f"mem_o_s{l}")
        xs, cb2, _ = _conv_ffn(x2_s, state_ffn_conv[l], p, mode="sample", n_seq=bs, seq_len=ls, tag=f"s{l}")
        outs["conv_p"].append(cb); outs["conv_s"].append(cb2)

    y_prompt = xp.reshape(bp, lp, d)
    y_sample = xs.reshape(bs, ls, d)
    st = lambda k: jnp.stack(outs[k])
    return (y_prompt, y_sample, st("gla_p"), st("hgrn_p"), st("conv_p"), st("mk_p"), st("mv_p"),
            st("gla_s"), st("hgrn_s"), st("conv_s"))
```

```python
import functools
import math

import jax
import jax.numpy as jnp
from jax import lax
from jax.experimental import pallas as pl
from jax.experimental.pallas import tpu as pltpu

F32 = jnp.float32
BF16 = jnp.bfloat16

LANE = 128
BF16_ROWS = 16
V7X_VMEM_LIMIT = 56 * 1024 * 1024

EPS = 1e-6
GLA_HEADS = 4
GLA_GATE_NORM = 16.0
HGRN_DK = 128
MEM_HEADS = 4
CONV_W = 3
REF_CHUNK = 16
PROMPT_BLOCK = 64
SAMPLE_SEQS = 2


def _cparams(n_axes):
    return pltpu.CompilerParams(dimension_semantics=("arbitrary",) * n_axes,
                                vmem_limit_bytes=V7X_VMEM_LIMIT)


def _sigmoid(x):
    return 0.5 * jnp.tanh(0.5 * x) + 0.5


def _pick_tile(n, candidates):
    for c in candidates:
        if n % c == 0:
            return c
    return n


def _side_cast_specs(arrays, n_steps, step_index):
    specs, shapes, chunks = [], [], []
    for a in arrays:
        rows = a.shape[0]
        c = max(c for c in range(1, n_steps + 1) if rows % c == 0 and (rows // c) % BF16_ROWS == 0)
        specs.append(pl.BlockSpec((rows // c, a.shape[1]),
                                  lambda *ids, c=c: (jnp.minimum(step_index(*ids), c - 1), 0)))
        shapes.append(jax.ShapeDtypeStruct(a.shape, BF16))
        chunks.append(c)
    return specs, shapes, tuple(chunks)


def _side_cast_run(step, src_refs, dst_refs, chunks):
    for src_ref, dst_ref, n_chunks in zip(src_refs, dst_refs, chunks):
        @pl.when(step < n_chunks)
        def _(src_ref=src_ref, dst_ref=dst_ref):
            dst_ref[...] = src_ref[...].astype(dst_ref.dtype)


def _dense_body(*refs, prologue, has_res, has_extra, has_final, row_chunk, w_transposed, cast_chunks):
    it = iter(refs)
    x_ref = next(it)
    g_ref = next(it) if prologue == "norm" else None
    w_ref = next(it)
    we_ref = next(it) if has_extra else None
    r_ref = next(it) if has_res else None
    fg_ref = next(it) if has_final else None
    cast_in = [next(it) for _ in cast_chunks]
    o_ref = next(it)
    oe_ref = next(it) if has_extra else None
    cast_out = [next(it) for _ in cast_chunks]
    h_ref = next(it) if prologue != "plain" else None
    j = pl.program_id(1)
    nt = (((1,), (1,)), ((), ()))

    _side_cast_run(pl.program_id(0) * pl.num_programs(1) + j, cast_in, cast_out, cast_chunks)

    if prologue != "plain":
        @pl.when(j == 0)
        def _():
            def body(c, carry):
                rows = pl.ds(pl.multiple_of(c * row_chunk, row_chunk), row_chunk)
                xf = x_ref[rows, :].astype(F32)
                if prologue == "norm":
                    ms = jnp.mean(xf * xf, axis=-1, keepdims=True)
                    xf = xf * lax.rsqrt(ms + EPS) * g_ref[...]
                h_ref[rows, :] = xf.astype(BF16)
                return carry
            lax.fori_loop(0, x_ref.shape[0] // row_chunk, body, 0)
            if has_extra:
                oe_ref[...] = lax.dot_general(h_ref[...], we_ref[...], nt, preferred_element_type=F32)
        lhs = h_ref[...]
    else:
        lhs = x_ref[...]
    w = w_ref[...].astype(BF16)
    if w_transposed:
        acc = lax.dot_general(lhs, w, nt, preferred_element_type=F32)
    else:
        acc = jnp.dot(lhs, w, preferred_element_type=F32)
    if has_res:
        acc = acc + r_ref[...]
    if not has_final:
        o_ref[...] = acc.astype(o_ref.dtype)
    else:
        tn = acc.shape[1]
        o_ref[:, pl.ds(pl.multiple_of(j * tn, tn), tn)] = acc

        @pl.when(j == pl.num_programs(1) - 1)
        def _():
            def body(c, carry):
                rows = pl.ds(pl.multiple_of(c * row_chunk, row_chunk), row_chunk)
                xf = o_ref[rows, :]
                ms = jnp.mean(xf * xf, axis=-1, keepdims=True)
                o_ref[rows, :] = xf * lax.rsqrt(ms + EPS) * fg_ref[...]
                return carry
            lax.fori_loop(0, o_ref.shape[0] // row_chunk, body, 0)


def _dense(x, w, *, gain=None, res=None, prologue, out_dtype, name, layer=0, col_off=0, n_cols=None,
           w_transposed=False, extra_cols=None, final_gain=None, tm=None, tn=None, skip_cols=None,
           side_casts=()):
    m = x.shape[0]
    k = x.shape[1] if w_transposed else w.shape[-2]
    assert k == x.shape[1] or (k % LANE == 0 and k < x.shape[1] and prologue == "plain")
    n = n_cols if n_cols is not None else w.shape[-1]
    if tm is None:
        tm = _pick_tile(m, (1024, 512, 256, 128, 64, 32, 16))
    if tn is None:
        tn = _pick_tile(n, (1024, 512, 256, 128) if k <= 2048 else (512, 256, 128))
    assert m % tm == 0 and n % tn == 0
    row_chunk = min(tm, 128)
    in_specs = [pl.BlockSpec((tm, k), lambda i, j: (i, 0))]
    args = [x]
    if prologue == "norm":
        in_specs.append(pl.BlockSpec((1, k), lambda i, j: (0, 0)))
        args.append(gain)
    if w_transposed:
        assert col_off % BF16_ROWS == 0 and tn % BF16_ROWS == 0
        s_at, s_w = skip_cols if skip_cols is not None else (n, 0)
        assert s_at % tn == 0 and s_w % BF16_ROWS == 0
        in_specs.append(pl.BlockSpec(
            (pl.Element(tn), pl.Element(k)),
            lambda i, j: (pl.multiple_of(col_off + j * tn + jnp.where(j * tn >= s_at, s_w, 0), BF16_ROWS), 0)))
    elif w.ndim == 3:
        assert col_off % tn == 0
        in_specs.append(pl.BlockSpec((None, k, tn), lambda i, j: (layer, 0, j + col_off // tn)))
    else:
        in_specs.append(pl.BlockSpec((k, tn), lambda i, j: (0, j)))
    args.append(w)
    out_specs = [pl.BlockSpec((tm, tn), lambda i, j: (i, j))]
    out_shape = [jax.ShapeDtypeStruct((m, n), out_dtype)]
    if extra_cols is not None:
        e_off, e_n = extra_cols
        assert w_transposed and prologue != "plain" and e_off % BF16_ROWS == 0 and e_n % BF16_ROWS == 0
        in_specs.append(pl.BlockSpec((pl.Element(e_n), pl.Element(k)), lambda i, j: (e_off, 0)))
        args.append(w)
        out_specs.append(pl.BlockSpec((tm, e_n), lambda i, j: (i, 0)))
        out_shape.append(jax.ShapeDtypeStruct((m, e_n), F32))
    if res is not None:
        in_specs.append(pl.BlockSpec((tm, tn), lambda i, j: (i, j)))
        args.append(res)
    if final_gain is not None:
        assert out_dtype == F32
        in_specs.append(pl.BlockSpec((1, n), lambda i, j: (0, 0)))
        args.append(final_gain)
        out_specs[0] = pl.BlockSpec((tm, n), lambda i, j: (i, 0))
    n_j = n // tn
    c_specs, c_shapes, cast_chunks = _side_cast_specs(side_casts, (m // tm) * n_j, lambda i, j: i * n_j + j)
    in_specs += c_specs
    args += list(side_casts)
    out_specs += c_specs
    out_shape += c_shapes
    scratch = [] if prologue == "plain" else [pltpu.VMEM((tm, k), BF16)]
    outs = pl.pallas_call(
        functools.partial(_dense_body, prologue=prologue, has_res=res is not None,
                          has_extra=extra_cols is not None, has_final=final_gain is not None,
                          row_chunk=row_chunk, w_transposed=w_transposed, cast_chunks=cast_chunks),
        grid=(m // tm, n_j),
        in_specs=in_specs,
        out_specs=out_specs,
        out_shape=out_shape,
        scratch_shapes=scratch,
        compiler_params=_cparams(2),
        name=name,
    )(*args)
    return outs if (extra_cols is not None or side_casts) else outs[0]


def _cumsum_rows(x, group):
    rows = lax.broadcasted_iota(jnp.int32, x.shape, 0) % group
    shift = 1
    while shift < group:
        rolled = pltpu.roll(x, shift, axis=0)
        x = x + jnp.where(rows >= shift, rolled, 0.0)
        shift *= 2
    return x


def _cumsum_rows_mxu(x, group):
    n = x.shape[0]
    row = lax.broadcasted_iota(jnp.int32, (n, n), 0)
    col = lax.broadcasted_iota(jnp.int32, (n, n), 1)
    tri = jnp.where(jnp.logical_and(col <= row, col // group == row // group), 1.0, 0.0).astype(BF16)
    hi = x.astype(BF16)
    lo = (x - hi.astype(F32)).astype(BF16)
    return (jnp.dot(tri, hi, preferred_element_type=F32) + jnp.dot(tri, lo, preferred_element_type=F32))


def _lockstep(gens):
    gens = list(gens)
    results = [None] * len(gens)
    live = list(range(len(gens)))
    while live:
        still = []
        for n in live:
            try:
                next(gens[n])
                still.append(n)
            except StopIteration as stop:
                results[n] = stop.value
        live = still
    return results


def _recurrence_block(q, k, v, la, s_read, *, sub, mm_dtype):
    bt, kd = q.shape
    vd = v.shape[1]
    ns = bt // sub
    b_loc = _cumsum_rows_mxu(la, sub) if mm_dtype == BF16 else _cumsum_rows(la, sub)
    qd = q * jnp.exp(b_loc)
    ki = k * jnp.exp(-b_loc)
    tot = [b_loc[(i + 1) * sub - 1:(i + 1) * sub, :] for i in range(ns)]
    pre = [jnp.zeros((1, kd), F32)]
    for i in range(ns):
        pre.append(pre[-1] + tot[i])
    sl = [slice(i * sub, (i + 1) * sub) for i in range(ns)]
    vb = v.astype(mm_dtype)
    qdm = qd.astype(mm_dtype)
    q_state = jnp.concatenate([qd[sl[i]] * jnp.exp(pre[i]) for i in range(ns)], axis=0) if ns > 1 else qd
    q_state = q_state.astype(mm_dtype)
    ke = [ki[sl[i]] * jnp.exp(tot[i]) for i in range(ns)]
    kmats = []
    for i in range(ns):
        parts = [ke[j] if j == i - 1 else ke[j] * jnp.exp(pre[i] - pre[j + 1]) for j in range(i)]
        parts.append(ki[sl[i]])
        parts += [jnp.zeros((sub, kd), F32)] * (ns - 1 - i)
        kmats.append((jnp.concatenate(parts, axis=0) if ns > 1 else parts[0]).astype(mm_dtype))
    k_end = jnp.concatenate([ke[i] if i == ns - 1 else ke[i] * jnp.exp(pre[ns] - pre[i + 1])
                             for i in range(ns)], axis=0) if ns > 1 else ke[0]
    k_end = k_end.astype(mm_dtype)
    decay = jnp.broadcast_to(jnp.exp(pre[ns]), (LANE, kd)).T
    yield

    s_prev = s_read()
    o_state = jnp.dot(q_state, s_prev.astype(mm_dtype), preferred_element_type=F32)
    a_rows = [lax.dot_general(qdm[sl[i]], kmats[i], (((1,), (1,)), ((), ())), preferred_element_type=F32)
              for i in range(ns)]
    ds = lax.dot_general(k_end, vb, (((0,), (0,)), ((), ())), preferred_element_type=F32)
    yield

    a = jnp.concatenate(a_rows, axis=0) if ns > 1 else a_rows[0]
    row = lax.broadcasted_iota(jnp.int32, a.shape, 0)
    col = lax.broadcasted_iota(jnp.int32, a.shape, 1)
    a = jnp.where(col <= row, a, 0.0).astype(mm_dtype)
    o = jnp.dot(a, vb, preferred_element_type=F32) + o_state
    s_new = jnp.concatenate([decay * s_prev[:, c * LANE:(c + 1) * LANE] for c in range(vd // LANE)],
                            axis=1) + ds
    yield
    return o, s_new


def _head_gated_norm(o, gnorm, gate):
    ms = jnp.mean(o * o, axis=-1, keepdims=True)
    return o * lax.rsqrt(ms + EPS) * gnorm * (gate * _sigmoid(gate))


def _log_sigmoid(x):
    return jnp.minimum(x, 0.0) - jnp.log(1.0 + jnp.exp(-jnp.abs(x)))


N_MIXER_IN = 16


def _mixer_body(*refs, steps_per_seq, prompt_kw, sample_kw, cast_chunks):
    n_p, n_s, n_c = N_MIXER_IN, N_MIXER_IN + 2, len(cast_chunks)
    p_in, s_in, c_in = refs[:n_p], refs[n_p:n_p + n_s], refs[n_p + n_s:n_p + n_s + n_c]
    outs = refs[n_p + n_s + n_c:]
    p_out, s_out, c_out = outs[:3], outs[3:6], outs[6:]
    step = pl.program_id(0)
    _side_cast_run(step, c_in, c_out, cast_chunks)
    _mixer_part(p_in, p_out, mode="prompt", first_step=step % steps_per_seq == 0, **prompt_kw)
    _mixer_part(s_in, s_out, mode="sample", first_step=None, **sample_kw)


def _mixer_part(ins, outs, *, mode, first_step, layer, gla_heads, gla_dk, gla_dv, hgrn_heads, hgrn_dk,
                block, sub, mm_dtype):
    (q_ref, k_ref, v_ref, g_ref, ga_ref, wup_ref, ba_ref, gng_ref,
     hq_ref, hf_ref, hi_ref, hg_ref, ma_ref, mb_ref, lb_ref, gnh_ref) = ins[:N_MIXER_IN]
    om_ref, sg_out, sh_out = outs
    sg_in, sh_in = (None, None) if mode == "prompt" else ins[N_MIXER_IN:]

    p = lb_ref[...]
    e = jnp.exp(p - jnp.max(p, axis=0, keepdims=True))
    lb_all = jnp.sum(e[:layer + 1], axis=0, keepdims=True) / jnp.sum(e, axis=0, keepdims=True)

    def gla_head(h, load, s_read, shared):
        ks = slice(h * gla_dk, (h + 1) * gla_dk)
        vs = slice(h * gla_dv, (h + 1) * gla_dv)
        ga = load(ga_ref, slice(None)).astype(BF16)
        a_logit = jnp.dot(ga, wup_ref[:, ks], preferred_element_type=F32) + ba_ref[:, ks]
        yield
        la = _log_sigmoid(a_logit) * (1.0 / GLA_GATE_NORM)
        q = load(q_ref, ks) * (gla_dk ** -0.5)
        o, s_new = yield from _recurrence_block(q, load(k_ref, ks), load(v_ref, vs), la, s_read,
                                                sub=sub, mm_dtype=mm_dtype)
        shared[h] = _head_gated_norm(o, gng_ref[:, vs], load(g_ref, vs))
        return s_new

    def hgrn_head(h, load, s_read, shared):
        ks = slice(h * hgrn_dk, (h + 1) * hgrn_dk)
        lb = lb_all[:, ks]
        f = lb + (1.0 - lb) * _sigmoid(load(hf_ref, ks))
        hq = load(hq_ref, ks)
        q = hq * _sigmoid(hq) * (hgrn_dk ** -0.5)
        yield
        o, s_new = yield from _recurrence_block(q, 1.0 - f, load(hi_ref, ks), jnp.log(f), s_read,
                                                sub=sub, mm_dtype=mm_dtype)
        o_h = _head_gated_norm(o, gnh_ref[:, ks], load(hg_ref, ks))
        per = gla_dv // hgrn_dk
        o_gla = shared[h // per][:, (h % per) * hgrn_dk:(h % per + 1) * hgrn_dk]
        mix = _sigmoid(load(ma_ref, ks)) * o_gla + _sigmoid(load(mb_ref, ks)) * o_h
        return mix, s_new

    def run(load, sg_read, sh_read):
        shared = {}
        gens = [gla_head(h, load, functools.partial(sg_read, h), shared) for h in range(gla_heads)]
        gens += [hgrn_head(h, load, functools.partial(sh_read, h), shared) for h in range(hgrn_heads)]
        res = _lockstep(gens)
        return res[:gla_heads], res[gla_heads:]

    if mode == "prompt":
        @pl.when(first_step)
        def _():
            sg_out[...] = jnp.zeros(sg_out.shape, F32)
            sh_out[...] = jnp.zeros(sh_out.shape, F32)

        def body(c, carry):
            rows = pl.ds(pl.multiple_of(c * block, block), block)
            load = lambda ref, cols: ref[rows, cols].astype(F32)
            g_res, h_res = run(load, lambda h: sg_out[0, h], lambda h: sh_out[0, h])
            for h, s_new in enumerate(g_res):
                sg_out[0, h] = s_new
            for h, (mix, s_new) in enumerate(h_res):
                sh_out[0, h] = s_new
                om_ref[rows, h * hgrn_dk:(h + 1) * hgrn_dk] = mix.astype(om_ref.dtype)
            return carry
        lax.fori_loop(0, q_ref.shape[0] // block, body, 0)
    else:
        n_seq = q_ref.shape[0] // block
        mixes = []
        for s in range(n_seq):
            load = lambda ref, cols, s=s: ref[:, cols].astype(F32)[s * block:(s + 1) * block]
            g_res, h_res = run(load, lambda h, s=s: sg_in[s, h], lambda h, s=s: sh_in[s, h])
            for h, s_new in enumerate(g_res):
                sg_out[s, h] = s_new
            for h, (mix, s_new) in enumerate(h_res):
                sh_out[s, h] = s_new
            mixes.append([mix for mix, _ in h_res])
        for h in range(hgrn_heads):
            om_ref[:, h * hgrn_dk:(h + 1) * hgrn_dk] = jnp.concatenate(
                [mixes[s][h] for s in range(n_seq)], axis=0).astype(om_ref.dtype)


def _mixer(z_p, zga_p, z_s, zga_s, p, s_gla, s_hgrn, *, layer, n_prompt, prompt_len, n_sample, sample_len,
           d_model, name, side_casts=()):
    gh, hk = GLA_HEADS, HGRN_DK
    gk, gv = d_model // 2 // gh, d_model // gh
    hh = d_model // hk
    kw = gh * gk
    n_steps = n_sample // SAMPLE_SEQS
    p_step = n_prompt * prompt_len // n_steps
    steps_per_seq = prompt_len // p_step
    assert prompt_len % p_step == 0 and p_step % PROMPT_BLOCK == 0
    s_step = SAMPLE_SEQS * sample_len
    whole = lambda a: pl.BlockSpec(a.shape, lambda i: (0,) * a.ndim)
    params = [p["gla_w_a_up"], p["gla_b_a"], p["gla_norm_g"], p["hgrn_lower_bound"], p["hgrn_norm_g"]]

    def group(z, zga, rows):
        zs = lambda width, col: pl.BlockSpec((rows, width), lambda i: (i, col))
        specs = [zs(kw, 0), zs(kw, 1), zs(d_model, 1), zs(d_model, 2), zs(zga.shape[1], 0)]
        specs += [whole(a) for a in params[:3]] + [zs(d_model, 3 + c) for c in range(6)]
        specs += [whole(a) for a in params[3:]]
        return specs, [z, z, z, z, zga] + params[:3] + [z] * 6 + params[3:], zs(d_model, 0)

    p_specs, p_args, p_om = group(z_p, zga_p, p_step)
    s_specs, s_args, s_om = group(z_s, zga_s, s_step)
    s_specs += [pl.BlockSpec((None, SAMPLE_SEQS, gh, gk, gv), lambda i: (layer, i, 0, 0, 0)),
                pl.BlockSpec((None, SAMPLE_SEQS, hh, hk, hk), lambda i: (layer, i, 0, 0, 0))]
    s_args += [s_gla, s_hgrn]
    p_st = lambda *dims: pl.BlockSpec((1,) + dims, lambda i: (i // steps_per_seq, 0, 0, 0))
    s_st = lambda *dims: pl.BlockSpec((SAMPLE_SEQS,) + dims, lambda i: (i, 0, 0, 0))
    common = dict(layer=layer, gla_heads=gh, gla_dk=gk, gla_dv=gv, hgrn_heads=hh, hgrn_dk=hk)
    prompt_kw = dict(common, block=PROMPT_BLOCK, sub=REF_CHUNK, mm_dtype=BF16)
    sample_kw = dict(common, block=sample_len, sub=math.gcd(REF_CHUNK, sample_len), mm_dtype=F32)
    shapes = lambda t, n: [jax.ShapeDtypeStruct((t, d_model), BF16), jax.ShapeDtypeStruct((n, gh, gk, gv), F32),
                           jax.ShapeDtypeStruct((n, hh, hk, hk), F32)]
    c_specs, c_shapes, cast_chunks = _side_cast_specs(side_casts, n_steps, lambda i: i)
    outs = pl.pallas_call(
        functools.partial(_mixer_body, steps_per_seq=steps_per_seq, prompt_kw=prompt_kw, sample_kw=sample_kw,
                          cast_chunks=cast_chunks),
        grid=(n_steps,),
        in_specs=p_specs + s_specs + c_specs,
        out_specs=[p_om, p_st(gh, gk, gv), p_st(hh, hk, hk), s_om, s_st(gh, gk, gv), s_st(hh, hk, hk)] + c_specs,
        out_shape=shapes(z_p.shape[0], n_prompt) + shapes(z_s.shape[0], n_sample) + c_shapes,
        compiler_params=_cparams(1),
        name=name,
    )(*p_args, *s_args, *side_casts)
    return outs[:3], outs[3:6], outs[6:]


def _xattn_sample_body(q_ref, k_ref, v_ref, o_ref, *, heads, dh, seq_len):
    scale = dh ** -0.5
    n_mem = k_ref.shape[1]
    for s in range(k_ref.shape[0]):
        rows = slice(s * seq_len, (s + 1) * seq_len)
        q = q_ref[rows, :]
        q2 = jnp.concatenate([q[:, h * dh:(h + 1) * dh] for h in range(heads)], axis=0)
        k2 = k_ref[s].reshape(n_mem * heads, dh)
        v2 = v_ref[s].reshape(n_mem * heads, dh)
        sc = lax.dot_general(q2, k2, (((1,), (1,)), ((), ())), preferred_element_type=F32) * scale
        q_head = lax.broadcasted_iota(jnp.int32, sc.shape, 0) // seq_len
        k_head = lax.broadcasted_iota(jnp.int32, sc.shape, 1) % heads
        sc = jnp.where(q_head == k_head, sc, -jnp.inf)
        p = jnp.exp(sc - jnp.max(sc, axis=-1, keepdims=True))
        o2 = jnp.dot(p, v2, preferred_element_type=F32) / jnp.sum(p, axis=-1, keepdims=True)
        for h in range(heads):
            o_ref[rows, h * dh:(h + 1) * dh] = o2[h * seq_len:(h + 1) * seq_len].astype(o_ref.dtype)


def _mid_body(om_ref, x_ref, k_ref, v_ref, wo_ref, wq_ref, wm_ref, g_ref, o_ref, *, heads, dh):
    scale = dh ** -0.5
    x1 = x_ref[...] + jnp.dot(om_ref[...], wo_ref[...], preferred_element_type=F32)
    ms = jnp.mean(x1 * x1, axis=-1, keepdims=True)
    hx = (x1 * lax.rsqrt(ms + EPS) * g_ref[...]).astype(BF16)
    mq = jnp.dot(hx, wq_ref[...], preferred_element_type=F32).astype(BF16)
    heads_out = []
    for h in range(heads):
        hs = slice(h * dh, (h + 1) * dh)
        sc = lax.dot_general(mq[:, hs], k_ref[0, :, hs], (((1,), (1,)), ((), ())),
                             preferred_element_type=F32) * scale
        p = jnp.exp(sc - jnp.max(sc, axis=-1, keepdims=True))
        o = jnp.dot(p.astype(BF16), v_ref[0, :, hs], preferred_element_type=F32) / jnp.sum(p, axis=-1, keepdims=True)
        heads_out.append(o.astype(BF16))
    mo = jnp.concatenate(heads_out, axis=1)
    o_ref[...] = x1 + jnp.dot(mo, wm_ref[...], preferred_element_type=F32)


def _mid_prompt(o_mix, x, mem_k, mem_v, w_out, w_q, w_o, gain, *, n_seq, seq_len, name):
    t, d = x.shape
    n_mem = mem_k.shape[1]
    tm = 256
    n_t = seq_len // tm
    rows = lambda dtype_rows: pl.BlockSpec((tm, d), lambda b, i: (b * n_t + i, 0))
    kv = pl.BlockSpec((1, n_mem, d), lambda b, i: (b, 0, 0))
    wspec = pl.BlockSpec((d, d), lambda b, i: (0, 0), pipeline_mode=pl.Buffered(1))
    return pl.pallas_call(
        functools.partial(_mid_body, heads=MEM_HEADS, dh=d // MEM_HEADS),
        grid=(n_seq, n_t),
        in_specs=[rows(BF16), rows(F32), kv, kv, wspec, wspec, wspec, pl.BlockSpec((1, d), lambda b, i: (0, 0))],
        out_specs=rows(F32),
        out_shape=jax.ShapeDtypeStruct((t, d), F32),
        compiler_params=_cparams(2),
        name=name,
    )(o_mix, x, mem_k, mem_v, w_out, w_q, w_o, gain)


def _ffn_up_body(*refs, mode, seq_len, tiles_per_seq, row_chunk, last_shift, side):
    if mode == "prompt" and side is not None:
        (x_ref, g_ref, wg_ref, wv_ref, cw_ref, cb_ref, sq_ref, sk_ref, sv_ref,
         act_ref, st_ref, so_ref, h_ref, tail_ref) = refs
    elif mode == "prompt":
        x_ref, g_ref, wg_ref, wv_ref, cw_ref, cb_ref, act_ref, st_ref, h_ref, tail_ref = refs
    else:
        x_ref, g_ref, wg_ref, wv_ref, cw_ref, cb_ref, buf_ref, act_ref, st_ref, h_ref = refs
        tail_ref = None
    i = pl.program_id(0)
    j = pl.program_id(1)

    if side is not None:
        @pl.when(i * pl.num_programs(1) + j < side["n_steps"])
        def _():
            _xattn_sample_body(sq_ref, sk_ref, sv_ref, so_ref, heads=MEM_HEADS, dh=side["dh"],
                               seq_len=side["seq_len"])

    @pl.when(j == 0)
    def _():
        def body(c, carry):
            rows = pl.ds(pl.multiple_of(c * row_chunk, row_chunk), row_chunk)
            xf = x_ref[rows, :]
            ms = jnp.mean(xf * xf, axis=-1, keepdims=True)
            h_ref[rows, :] = (xf * lax.rsqrt(ms + EPS) * g_ref[...]).astype(BF16)
            return carry
        lax.fori_loop(0, x_ref.shape[0] // row_chunk, body, 0)

    if mode == "prompt":
        @pl.when(i % tiles_per_seq == 0)
        def _():
            tail_ref[j] = jnp.zeros(tail_ref.shape[1:], F32)

    def step(shift):
        def place(a):
            if shift == 0:
                return a
            return jnp.concatenate([a[..., shift:], jnp.zeros(a.shape[:-1] + (shift,), a.dtype)], axis=-1)

        h = h_ref[...]
        ug = jnp.dot(h, wg_ref[...], preferred_element_type=F32)
        tm = ug.shape[0]
        row = lax.broadcasted_iota(jnp.int32, ug.shape, 0)
        roll1 = pltpu.roll(ug, 1, axis=0)
        roll2 = pltpu.roll(ug, 2, axis=0)
        if mode == "prompt":
            tail = tail_ref[j]
            prev1 = tail[7:8, :]
            prev2 = tail[6:7, :]
            sh1 = jnp.where(row >= 1, roll1, prev1)
            sh2 = jnp.where(row >= 2, roll2, jnp.where(row == 0, prev2, prev1))
            tail_ref[j] = ug[tm - 8:, :]
            st_ref[...] = place(ug[tm - 8:, :])
        else:
            buf = buf_ref[...]
            n_seq = buf.shape[0]
            spread = lambda r: jnp.broadcast_to(buf[:, r:r + 1, :],
                                                (n_seq, seq_len, buf.shape[2])).reshape(ug.shape)
            prev2, prev1 = spread(0), spread(1)
            pos = row % seq_len
            sh1 = jnp.where(pos >= 1, roll1, prev1)
            sh2 = jnp.where(pos >= 2, roll2, jnp.where(pos == 0, prev2, prev1))
            st_ref[...] = place(ug.reshape(n_seq, seq_len, ug.shape[1])[:, seq_len - (CONV_W - 1):, :])
        conv = cw_ref[0:1, :] * sh2 + cw_ref[1:2, :] * sh1 + cw_ref[2:3, :] * ug + cb_ref[...]
        gate = conv * _sigmoid(conv)
        uv = jnp.dot(h, wv_ref[...], preferred_element_type=F32)
        act_ref[...] = place((gate * uv).astype(act_ref.dtype))

    if last_shift == 0:
        step(0)
    else:
        last = pl.num_programs(1) - 1
        pl.when(j < last)(functools.partial(step, 0))
        pl.when(j == last)(functools.partial(step, last_shift))


def _ffn_up(x, gain, w_up, cw, cb, carry, *, mode, seq_len, name, side=None):
    m, k = x.shape
    d_ff = w_up.shape[1] // 2
    tm = _pick_tile(m, (1024,))
    tn = 4 * LANE
    nf = -(-d_ff // tn) * tn
    last_shift = nf - d_ff
    assert d_ff % LANE == 0 and d_ff >= tn
    col = lambda j, base=0: pl.multiple_of(base + jnp.minimum(j * tn, d_ff - tn), LANE)
    elem = lambda *dims: tuple(pl.Element(n) for n in dims)
    xs = pl.BlockSpec((tm, k), lambda i, j: (i, 0))
    gs = pl.BlockSpec((1, k), lambda i, j: (0, 0))
    wgs = pl.BlockSpec(elem(k, tn), lambda i, j: (0, col(j)))
    wvs = pl.BlockSpec(elem(k, tn), lambda i, j: (0, col(j, d_ff)))
    cws = pl.BlockSpec(elem(CONV_W, tn), lambda i, j: (0, col(j)))
    cbs = pl.BlockSpec(elem(1, tn), lambda i, j: (0, col(j)))
    ts = pl.BlockSpec((tm, tn), lambda i, j: (i, j))
    in_specs = [xs, gs, wgs, wvs, cws, cbs]
    args = [x, gain, w_up, w_up, cw, cb]
    scratch = [pltpu.VMEM((tm, k), BF16)]
    if mode == "prompt":
        assert seq_len % tm == 0
        scratch.append(pltpu.VMEM((nf // tn, 8, tn), F32))
        st_spec = pl.BlockSpec((None, 8, tn), lambda i, j: (i, 0, j))
        st_shape = jax.ShapeDtypeStruct((m // tm, 8, nf), F32)
    else:
        assert tm % seq_len == 0 and seq_len == 8
        n_blk = tm // seq_len
        in_specs.append(pl.BlockSpec(elem(n_blk, CONV_W - 1, tn), lambda i, j: (i * n_blk, 0, col(j))))
        args.append(carry)
        st_spec = pl.BlockSpec((n_blk, CONV_W - 1, tn), lambda i, j: (i, 0, j))
        st_shape = jax.ShapeDtypeStruct((m // seq_len, CONV_W - 1, nf), F32)
    n_j = nf // tn
    out_specs = [ts, st_spec]
    out_shape = [jax.ShapeDtypeStruct((m, nf), BF16), st_shape]
    side_kw = None
    if side is not None:
        assert mode == "prompt"
        mq, cache_k, cache_v, layer, s_len = side
        n_side = mq.shape[0] // (SAMPLE_SEQS * s_len)
        assert n_side <= (m // tm) * n_j
        grp = lambda i, j: jnp.minimum(i * n_j + j, n_side - 1)
        q_spec = pl.BlockSpec((SAMPLE_SEQS * s_len, mq.shape[1]), lambda i, j: (grp(i, j), 0))
        kv_spec = pl.BlockSpec((None, SAMPLE_SEQS) + cache_k.shape[2:], lambda i, j: (layer, grp(i, j), 0, 0, 0))
        in_specs += [q_spec, kv_spec, kv_spec]
        args += [mq, cache_k, cache_v]
        out_specs.append(q_spec)
        out_shape.append(jax.ShapeDtypeStruct(mq.shape, F32))
        side_kw = dict(n_steps=n_side, dh=cache_k.shape[-1], seq_len=s_len)
    return pl.pallas_call(
        functools.partial(_ffn_up_body, mode=mode, seq_len=seq_len, tiles_per_seq=max(seq_len // tm, 1),
                          row_chunk=128, last_shift=last_shift, side=side_kw),
        grid=(m // tm, n_j),
        in_specs=in_specs,
        out_specs=out_specs,
        out_shape=out_shape,
        scratch_shapes=scratch,
        compiler_params=_cparams(2),
        name=name,
    )(*args)


def _in_proj(x, p, name, side_casts=()):
    ga_off, rank = p["ga_off"], p["rank"]
    w_in_t = p["w_in_t"]
    return _dense(x, w_in_t, gain=p["norm_mix_g"], prologue="norm", w_transposed=True, tn=x.shape[1],
                  out_dtype=BF16, name=name, n_cols=w_in_t.shape[0] - rank,
                  skip_cols=(ga_off, rank), extra_cols=(ga_off, rank), side_casts=side_casts)


def _conv_ffn(x2, carry, p, *, mode, n_seq, seq_len, tag, side=None):
    d_ff = p["d_ff"]
    act, ug_rows, *mo_side = _ffn_up(x2, p["norm_ffn_g"], p["w_up"], p["ffn_conv_w"], p["ffn_conv_b"], carry,
                                     mode=mode, seq_len=seq_len, name=f"ffn_up_{tag}", side=side)
    final = p["norm_final_g"] if p["is_last"] else None
    y = _dense(act, p["w_down"], res=x2, prologue="plain", out_dtype=F32, name=f"ffn_down_{tag}",
               final_gain=final, tm=512 if final is not None else None, tn=1024 if final is not None else None)
    if mode == "prompt":
        tiles_per_seq = ug_rows.shape[0] // n_seq
        conv_new = ug_rows[tiles_per_seq - 1::tiles_per_seq, 8 - (CONV_W - 1):, :d_ff]
    else:
        conv_new = ug_rows[:, :, :d_ff]
    return y, conv_new, (mo_side[0] if mo_side else None)


def kernel(x_prompt, x_sample, mem_prompt, state_gla, state_hgrn, state_ffn_conv, cache_mem_k, cache_mem_v,
           norm_mix_g, w_in, gla_w_a_up, gla_b_a, gla_norm_g, hgrn_lower_bound, hgrn_norm_g, w_out,
           norm_xattn_g, norm_mem_g, w_mem_q, w_mem_k, w_mem_v, w_mem_o,
           norm_ffn_g, w_ffn_up, ffn_conv_w, ffn_conv_b, w_ffn_down, norm_final_g):
    depth = w_in.shape[0]
    bp, lp, d = x_prompt.shape
    bs, ls, _ = x_sample.shape
    n_mem = mem_prompt.shape[1]
    d_ff = w_ffn_down.shape[1]
    rank = gla_w_a_up.shape[1]
    qk = gla_w_a_up.shape[2]
    ga_off = 2 * qk + 2 * d

    xp = x_prompt.reshape(bp * lp, d)
    xs = x_sample.reshape(bs * ls, d)
    mem = mem_prompt.reshape(bp * n_mem, d)
    row = lambda v: v.reshape(1, -1)

    outs = {k: [] for k in ("gla_p", "hgrn_p", "conv_p", "mk_p", "mv_p", "gla_s", "hgrn_s", "conv_s")}
    for l in range(depth):
        p = {
            "w_in_t": jnp.swapaxes(w_in[l], 0, 1).astype(BF16),
            "ga_off": ga_off, "rank": rank,
            "gla_w_a_up": gla_w_a_up[l].astype(BF16),
            "gla_b_a": row(gla_b_a[l]), "gla_norm_g": row(gla_norm_g[l]),
            "hgrn_lower_bound": hgrn_lower_bound, "hgrn_norm_g": row(hgrn_norm_g[l]),
            "norm_mix_g": row(norm_mix_g[l]), "norm_xattn_g": row(norm_xattn_g[l]),
            "norm_ffn_g": row(norm_ffn_g[l]), "norm_final_g": row(norm_final_g), "is_last": l == depth - 1,
            "ffn_conv_w": ffn_conv_w[l], "ffn_conv_b": row(ffn_conv_b[l]),
            "d_ff": d_ff,
        }
        g_mem = row(norm_mem_g[l])
        mk = _dense(mem, w_mem_k, gain=g_mem, prologue="norm", out_dtype=F32, name=f"mem_k_{l}", layer=l)
        mv = _dense(mem, w_mem_v, gain=g_mem, prologue="norm", out_dtype=F32, name=f"mem_v_{l}", layer=l)
        mk3, mv3 = mk.reshape(bp, n_mem, d), mv.reshape(bp, n_mem, d)
        outs["mk_p"].append(mk3.reshape(bp, n_mem, MEM_HEADS, d // MEM_HEADS))
        outs["mv_p"].append(mv3.reshape(bp, n_mem, MEM_HEADS, d // MEM_HEADS))
        zp, zga_p, p["w_out"], p["w_mem_q"], p["w_mem_o"] = _in_proj(
            xp, p, f"in_proj_p{l}", side_casts=(w_out[l], w_mem_q[l], w_mem_o[l]))
        zs, zga_s = _in_proj(xs, p, f"in_proj_s{l}")
        (om_p, sg, sh), (om_s, sg2, sh2), (p["w_up"], p["w_down"]) = _mixer(
            zp, zga_p, zs, zga_s, p, state_gla, state_hgrn, layer=l, n_prompt=bp, prompt_len=lp,
            n_sample=bs, sample_len=ls, d_model=d, name=f"mixer_{l}", side_casts=(w_ffn_up[l], w_ffn_down[l]))
        outs["gla_p"].append(sg); outs["hgrn_p"].append(sh)
        outs["gla_s"].append(sg2); outs["hgrn_s"].append(sh2)
        x1_s = _dense(om_s, p["w_out"], res=xs, prologue="plain", out_dtype=F32, name=f"out_proj_s{l}")
        mq_s = _dense(x1_s, p["w_mem_q"], gain=p["norm_xattn_g"], prologue="norm", out_dtype=F32,
                      name=f"mem_q_s{l}")
        x2_p = _mid_prompt(om_p, xp, mk3.astype(BF16), mv3.astype(BF16), p["w_out"], p["w_mem_q"], p["w_mem_o"],
                           p["norm_xattn_g"], n_seq=bp, seq_len=lp, name=f"mid_p{l}")
        xp, cb, mo_s = _conv_ffn(x2_p, None, p, mode="prompt", n_seq=bp, seq_len=lp, tag=f"p{l}",
                                 side=(mq_s, cache_mem_k, cache_mem_v, l, ls))
        x2_s = _dense(mo_s, p["w_mem_o"], res=x1_s, prologue="cast", out_dtype=F32, name=f"mem_o_s{l}")
        xs, cb2, _ = _conv_ffn(x2_s, state_ffn_conv[l], p, mode="sample", n_seq=bs, seq_len=ls, tag=f"s{l}")
        outs["conv_p"].append(cb); outs["conv_s"].append(cb2)

    y_prompt = xp.reshape(bp, lp, d)
    y_sample = xs.reshape(bs, ls, d)
    st = lambda k: jnp.stack(outs[k])
    return (y_prompt, y_sample, st("gla_p"), st("hgrn_p"), st("conv_p"), st("mk_p"), st("mv_p"),
            st("gla_s"), st("hgrn_s"), st("conv_s"))
```

```python
import functools
import math

import jax
import jax.numpy as jnp
from jax import lax
from jax.experimental import pallas as pl
from jax.experimental.pallas import tpu as pltpu

F32 = jnp.float32
BF16 = jnp.bfloat16

LANE = 128
BF16_ROWS = 16
V7X_VMEM_LIMIT = 56 * 1024 * 1024

EPS = 1e-6
GLA_HEADS = 4
GLA_GATE_NORM = 16.0
HGRN_DK = 128
MEM_HEADS = 4
CONV_W = 3
REF_CHUNK = 16
PROMPT_BLOCK = 64
SAMPLE_SEQS = 2


def _cparams(n_axes):
    return pltpu.CompilerParams(dimension_semantics=("arbitrary",) * n_axes,
                                vmem_limit_bytes=V7X_VMEM_LIMIT)


def _sigmoid(x):
    return 0.5 * jnp.tanh(0.5 * x) + 0.5


def _pick_tile(n, candidates):
    for c in candidates:
        if n % c == 0:
            return c
    return n


def _side_cast_specs(arrays, n_steps, step_index):
    specs, shapes, chunks = [], [], []
    for a in arrays:
        rows = a.shape[0]
        c = max(c for c in range(1, n_steps + 1) if rows % c == 0 and (rows // c) % BF16_ROWS == 0)
        specs.append(pl.BlockSpec((rows // c, a.shape[1]),
                                  lambda *ids, c=c: (jnp.minimum(step_index(*ids), c - 1), 0)))
        shapes.append(jax.ShapeDtypeStruct(a.shape, BF16))
        chunks.append(c)
    return specs, shapes, tuple(chunks)


def _side_cast_run(step, src_refs, dst_refs, chunks):
    for src_ref, dst_ref, n_chunks in zip(src_refs, dst_refs, chunks):
        @pl.when(step < n_chunks)
        def _(src_ref=src_ref, dst_ref=dst_ref):
            dst_ref[...] = src_ref[...].astype(dst_ref.dtype)


def _dense_body(*refs, prologue, has_res, has_extra, has_final, row_chunk, w_transposed, cast_chunks):
    it = iter(refs)
    x_ref = next(it)
    g_ref = next(it) if prologue == "norm" else None
    w_ref = next(it)
    we_ref = next(it) if has_extra else None
    r_ref = next(it) if has_res else None
    fg_ref = next(it) if has_final else None
    cast_in = [next(it) for _ in cast_chunks]
    o_ref = next(it)
    oe_ref = next(it) if has_extra else None
    cast_out = [next(it) for _ in cast_chunks]
    h_ref = next(it) if prologue != "plain" else None
    j = pl.program_id(1)
    nt = (((1,), (1,)), ((), ()))

    _side_cast_run(pl.program_id(0) * pl.num_programs(1) + j, cast_in, cast_out, cast_chunks)

    if prologue != "plain":
        @pl.when(j == 0)
        def _():
            def body(c, carry):
                rows = pl.ds(pl.multiple_of(c * row_chunk, row_chunk), row_chunk)
                xf = x_ref[rows, :].astype(F32)
                if prologue == "norm":
                    ms = jnp.mean(xf * xf, axis=-1, keepdims=True)
                    xf = xf * lax.rsqrt(ms + EPS) * g_ref[...]
                h_ref[rows, :] = xf.astype(BF16)
                return carry
            lax.fori_loop(0, x_ref.shape[0] // row_chunk, body, 0)
            if has_extra:
                oe_ref[...] = lax.dot_general(h_ref[...], we_ref[...], nt, preferred_element_type=F32)
        lhs = h_ref[...]
    else:
        lhs = x_ref[...]
    w = w_ref[...].astype(BF16)
    if w_transposed:
        acc = lax.dot_general(lhs, w, nt, preferred_element_type=F32)
    else:
        acc = jnp.dot(lhs, w, preferred_element_type=F32)
    if has_res:
        acc = acc + r_ref[...]
    if not has_final:
        o_ref[...] = acc.astype(o_ref.dtype)
    else:
        tn = acc.shape[1]
        o_ref[:, pl.ds(pl.multiple_of(j * tn, tn), tn)] = acc

        @pl.when(j == pl.num_programs(1) - 1)
        def _():
            def body(c, carry):
                rows = pl.ds(pl.multiple_of(c * row_chunk, row_chunk), row_chunk)
                xf = o_ref[rows, :]
                ms = jnp.mean(xf * xf, axis=-1, keepdims=True)
                o_ref[rows, :] = xf * lax.rsqrt(ms + EPS) * fg_ref[...]
                return carry
            lax.fori_loop(0, o_ref.shape[0] // row_chunk, body, 0)


def _dense(x, w, *, gain=None, res=None, prologue, out_dtype, name, layer=0, col_off=0, n_cols=None,
           w_transposed=False, extra_cols=None, final_gain=None, tm=None, tn=None, skip_cols=None,
           side_casts=()):
    m = x.shape[0]
    k = x.shape[1] if w_transposed else w.shape[-2]
    assert k == x.shape[1] or (k % LANE == 0 and k < x.shape[1] and prologue == "plain")
    n = n_cols if n_cols is not None else w.shape[-1]
    if tm is None:
        tm = _pick_tile(m, (1024, 512, 256, 128, 64, 32, 16))
    if tn is None:
        tn = _pick_tile(n, (1024, 512, 256, 128) if k <= 2048 else (512, 256, 128))
    assert m % tm == 0 and n % tn == 0
    row_chunk = min(tm, 128)
    in_specs = [pl.BlockSpec((tm, k), lambda i, j: (i, 0))]
    args = [x]
    if prologue == "norm":
        in_specs.append(pl.BlockSpec((1, k), lambda i, j: (0, 0)))
        args.append(gain)
    if w_transposed:
        assert col_off % BF16_ROWS == 0 and tn % BF16_ROWS == 0
        s_at, s_w = skip_cols if skip_cols is not None else (n, 0)
        assert s_at % tn == 0 and s_w % BF16_ROWS == 0
        in_specs.append(pl.BlockSpec(
            (pl.Element(tn), pl.Element(k)),
            lambda i, j: (pl.multiple_of(col_off + j * tn + jnp.where(j * tn >= s_at, s_w, 0), BF16_ROWS), 0)))
    elif w.ndim == 3:
        assert col_off % tn == 0
        in_specs.append(pl.BlockSpec((None, k, tn), lambda i, j: (layer, 0, j + col_off // tn)))
    else:
        in_specs.append(pl.BlockSpec((k, tn), lambda i, j: (0, j)))
    args.append(w)
    out_specs = [pl.BlockSpec((tm, tn), lambda i, j: (i, j))]
    out_shape = [jax.ShapeDtypeStruct((m, n), out_dtype)]
    if extra_cols is not None:
        e_off, e_n = extra_cols
        assert w_transposed and prologue != "plain" and e_off % BF16_ROWS == 0 and e_n % BF16_ROWS == 0
        in_specs.append(pl.BlockSpec((pl.Element(e_n), pl.Element(k)), lambda i, j: (e_off, 0)))
        args.append(w)
        out_specs.append(pl.BlockSpec((tm, e_n), lambda i, j: (i, 0)))
        out_shape.append(jax.ShapeDtypeStruct((m, e_n), F32))
    if res is not None:
        in_specs.append(pl.BlockSpec((tm, tn), lambda i, j: (i, j)))
        args.append(res)
    if final_gain is not None:
        assert out_dtype == F32
        in_specs.append(pl.BlockSpec((1, n), lambda i, j: (0, 0)))
        args.append(final_gain)
        out_specs[0] = pl.BlockSpec((tm, n), lambda i, j: (i, 0))
    n_j = n // tn
    c_specs, c_shapes, cast_chunks = _side_cast_specs(side_casts, (m // tm) * n_j, lambda i, j: i * n_j + j)
    in_specs += c_specs
    args += list(side_casts)
    out_specs += c_specs
    out_shape += c_shapes
    scratch = [] if prologue == "plain" else [pltpu.VMEM((tm, k), BF16)]
    outs = pl.pallas_call(
        functools.partial(_dense_body, prologue=prologue, has_res=res is not None,
                          has_extra=extra_cols is not None, has_final=final_gain is not None,
                          row_chunk=row_chunk, w_transposed=w_transposed, cast_chunks=cast_chunks),
        grid=(m // tm, n_j),
        in_specs=in_specs,
        out_specs=out_specs,
        out_shape=out_shape,
        scratch_shapes=scratch,
        compiler_params=_cparams(2),
        name=name,
    )(*args)
    return outs if (extra_cols is not None or side_casts) else outs[0]


def _cumsum_rows(x, group):
    rows = lax.broadcasted_iota(jnp.int32, x.shape, 0) % group
    shift = 1
    while shift < group:
        rolled = pltpu.roll(x, shift, axis=0)
        x = x + jnp.where(rows >= shift, rolled, 0.0)
        shift *= 2
    return x


def _cumsum_rows_mxu(x, group):
    n = x.shape[0]
    row = lax.broadcasted_iota(jnp.int32, (n, n), 0)
    col = lax.broadcasted_iota(jnp.int32, (n, n), 1)
    tri = jnp.where(jnp.logical_and(col <= row, col // group == row // group), 1.0, 0.0).astype(BF16)
    hi = x.astype(BF16)
    lo = (x - hi.astype(F32)).astype(BF16)
    return (jnp.dot(tri, hi, preferred_element_type=F32) + jnp.dot(tri, lo, preferred_element_type=F32))


def _lockstep(gens):
    gens = list(gens)
    results = [None] * len(gens)
    live = list(range(len(gens)))
    while live:
        still = []
        for n in live:
            try:
                next(gens[n])
                still.append(n)
            except StopIteration as stop:
                results[n] = stop.value
        live = still
    return results


def _recurrence_block(q, k, v, la, s_read, *, sub, mm_dtype):
    bt, kd = q.shape
    vd = v.shape[1]
    ns = bt // sub
    b_loc = _cumsum_rows_mxu(la, sub) if mm_dtype == BF16 else _cumsum_rows(la, sub)
    qd = q * jnp.exp(b_loc)
    ki = k * jnp.exp(-b_loc)
    tot = [b_loc[(i + 1) * sub - 1:(i + 1) * sub, :] for i in range(ns)]
    pre = [jnp.zeros((1, kd), F32)]
    for i in range(ns):
        pre.append(pre[-1] + tot[i])
    sl = [slice(i * sub, (i + 1) * sub) for i in range(ns)]
    vb = v.astype(mm_dtype)
    qdm = qd.astype(mm_dtype)
    q_state = jnp.concatenate([qd[sl[i]] * jnp.exp(pre[i]) for i in range(ns)], axis=0) if ns > 1 else qd
    q_state = q_state.astype(mm_dtype)
    ke = [ki[sl[i]] * jnp.exp(tot[i]) for i in range(ns)]
    kmats = []
    for i in range(ns):
        parts = [ke[j] if j == i - 1 else ke[j] * jnp.exp(pre[i] - pre[j + 1]) for j in range(i)]
        parts.append(ki[sl[i]])
        parts += [jnp.zeros((sub, kd), F32)] * (ns - 1 - i)
        kmats.append((jnp.concatenate(parts, axis=0) if ns > 1 else parts[0]).astype(mm_dtype))
    k_end = jnp.concatenate([ke[i] if i == ns - 1 else ke[i] * jnp.exp(pre[ns] - pre[i + 1])
                             for i in range(ns)], axis=0) if ns > 1 else ke[0]
    k_end = k_end.astype(mm_dtype)
    decay = jnp.broadcast_to(jnp.exp(pre[ns]), (LANE, kd)).T
    yield

    s_prev = s_read()
    o_state = jnp.dot(q_state, s_prev.astype(mm_dtype), preferred_element_type=F32)
    a_rows = [lax.dot_general(qdm[sl[i]], kmats[i], (((1,), (1,)), ((), ())), preferred_element_type=F32)
              for i in range(ns)]
    ds = lax.dot_general(k_end, vb, (((0,), (0,)), ((), ())), preferred_element_type=F32)
    yield

    a = jnp.concatenate(a_rows, axis=0) if ns > 1 else a_rows[0]
    row = lax.broadcasted_iota(jnp.int32, a.shape, 0)
    col = lax.broadcasted_iota(jnp.int32, a.shape, 1)
    a = jnp.where(col <= row, a, 0.0).astype(mm_dtype)
    o = jnp.dot(a, vb, preferred_element_type=F32) + o_state
    s_new = jnp.concatenate([decay * s_prev[:, c * LANE:(c + 1) * LANE] for c in range(vd // LANE)],
                            axis=1) + ds
    yield
    return o, s_new


def _head_gated_norm(o, gnorm, gate):
    ms = jnp.mean(o * o, axis=-1, keepdims=True)
    return o * lax.rsqrt(ms + EPS) * gnorm * (gate * _sigmoid(gate))


def _log_sigmoid(x):
    return jnp.minimum(x, 0.0) - jnp.log(1.0 + jnp.exp(-jnp.abs(x)))


N_MIXER_IN = 16


def _mixer_body(*refs, steps_per_seq, prompt_kw, sample_kw, cast_chunks):
    n_p, n_s, n_c = N_MIXER_IN, N_MIXER_IN + 2, len(cast_chunks)
    p_in, s_in, c_in = refs[:n_p], refs[n_p:n_p + n_s], refs[n_p + n_s:n_p + n_s + n_c]
    outs = refs[n_p + n_s + n_c:]
    (p_om, p_sg, p_sh), (s_om, s_sg, s_sh), c_out = outs[:3], outs[3:6], outs[6:]
    s_sg_in, s_sh_in = s_in[N_MIXER_IN:]
    step = pl.program_id(0)
    _side_cast_run(step, c_in, c_out, cast_chunks)

    @pl.when(step % steps_per_seq == 0)
    def _():
        p_sg[...] = jnp.zeros(p_sg.shape, F32)
        p_sh[...] = jnp.zeros(p_sh.shape, F32)

    p_heads = _mixer_heads(p_in, **prompt_kw)
    s_heads = _mixer_heads(s_in, **sample_kw)
    p_block, s_block = prompt_kw["block"], sample_kw["block"]
    n_p_blocks = p_in[0].shape[0] // p_block
    n_s_seqs = s_in[0].shape[0] // s_block
    gh, hk = prompt_kw["gla_heads"], prompt_kw["hgrn_dk"]
    s_mixes = []
    for c in range(max(n_p_blocks, n_s_seqs)):
        gens, n_pg = [], 0
        if c < n_p_blocks:
            rows = slice(c * p_block, (c + 1) * p_block)
            load = lambda ref, cols, rows=rows: ref[rows, cols].astype(F32)
            gens += p_heads(load, lambda h: p_sg[0, h], lambda h: p_sh[0, h])
            n_pg = len(gens)
        if c < n_s_seqs:
            load = lambda ref, cols, c=c: ref[:, cols].astype(F32)[c * s_block:(c + 1) * s_block]
            gens += s_heads(load, lambda h, c=c: s_sg_in[c, h], lambda h, c=c: s_sh_in[c, h])
        res = _lockstep(gens)
        if c < n_p_blocks:
            for h, s_new in enumerate(res[:gh]):
                p_sg[0, h] = s_new
            for h, (mix, s_new) in enumerate(res[gh:n_pg]):
                p_sh[0, h] = s_new
                p_om[rows, h * hk:(h + 1) * hk] = mix.astype(p_om.dtype)
        if c < n_s_seqs:
            s_res = res[n_pg:]
            for h, s_new in enumerate(s_res[:gh]):
                s_sg[c, h] = s_new
            for h, (mix, s_new) in enumerate(s_res[gh:]):
                s_sh[c, h] = s_new
            s_mixes.append([mix for mix, _ in s_res[gh:]])
    for h in range(len(s_mixes[0])):
        s_om[:, h * hk:(h + 1) * hk] = jnp.concatenate([m[h] for m in s_mixes], axis=0).astype(s_om.dtype)


def _mixer_heads(ins, *, layer, gla_heads, gla_dk, gla_dv, hgrn_heads, hgrn_dk, block, sub, mm_dtype):
    (q_ref, k_ref, v_ref, g_ref, ga_ref, wup_ref, ba_ref, gng_ref,
     hq_ref, hf_ref, hi_ref, hg_ref, ma_ref, mb_ref, lb_ref, gnh_ref) = ins[:N_MIXER_IN]

    p = lb_ref[...]
    e = jnp.exp(p - jnp.max(p, axis=0, keepdims=True))
    lb_all = jnp.sum(e[:layer + 1], axis=0, keepdims=True) / jnp.sum(e, axis=0, keepdims=True)

    def gla_head(h, load, s_read, shared):
        ks = slice(h * gla_dk, (h + 1) * gla_dk)
        vs = slice(h * gla_dv, (h + 1) * gla_dv)
        ga = load(ga_ref, slice(None)).astype(BF16)
        a_logit = jnp.dot(ga, wup_ref[:, ks], preferred_element_type=F32) + ba_ref[:, ks]
        yield
        la = _log_sigmoid(a_logit) * (1.0 / GLA_GATE_NORM)
        q = load(q_ref, ks) * (gla_dk ** -0.5)
        o, s_new = yield from _recurrence_block(q, load(k_ref, ks), load(v_ref, vs), la, s_read,
                                                sub=sub, mm_dtype=mm_dtype)
        shared[h] = _head_gated_norm(o, gng_ref[:, vs], load(g_ref, vs))
        return s_new

    def hgrn_head(h, load, s_read, shared):
        ks = slice(h * hgrn_dk, (h + 1) * hgrn_dk)
        lb = lb_all[:, ks]
        f = lb + (1.0 - lb) * _sigmoid(load(hf_ref, ks))
        hq = load(hq_ref, ks)
        q = hq * _sigmoid(hq) * (hgrn_dk ** -0.5)
        yield
        o, s_new = yield from _recurrence_block(q, 1.0 - f, load(hi_ref, ks), jnp.log(f), s_read,
                                                sub=sub, mm_dtype=mm_dtype)
        o_h = _head_gated_norm(o, gnh_ref[:, ks], load(hg_ref, ks))
        per = gla_dv // hgrn_dk
        o_gla = shared[h // per][:, (h % per) * hgrn_dk:(h % per + 1) * hgrn_dk]
        mix = _sigmoid(load(ma_ref, ks)) * o_gla + _sigmoid(load(mb_ref, ks)) * o_h
        return mix, s_new

    def make(load, sg_read, sh_read):
        shared = {}
        gens = [gla_head(h, load, functools.partial(sg_read, h), shared) for h in range(gla_heads)]
        gens += [hgrn_head(h, load, functools.partial(sh_read, h), shared) for h in range(hgrn_heads)]
        return gens

    return make


def _mixer(z_p, zga_p, z_s, zga_s, p, s_gla, s_hgrn, *, layer, n_prompt, prompt_len, n_sample, sample_len,
           d_model, name, side_casts=()):
    gh, hk = GLA_HEADS, HGRN_DK
    gk, gv = d_model // 2 // gh, d_model // gh
    hh = d_model // hk
    kw = gh * gk
    n_steps = n_sample // SAMPLE_SEQS
    p_step = n_prompt * prompt_len // n_steps
    steps_per_seq = prompt_len // p_step
    assert prompt_len % p_step == 0 and p_step % PROMPT_BLOCK == 0
    s_step = SAMPLE_SEQS * sample_len
    whole = lambda a: pl.BlockSpec(a.shape, lambda i: (0,) * a.ndim)
    params = [p["gla_w_a_up"], p["gla_b_a"], p["gla_norm_g"], p["hgrn_lower_bound"], p["hgrn_norm_g"]]

    def group(z, zga, rows):
        zs = lambda width, col: pl.BlockSpec((rows, width), lambda i: (i, col))
        specs = [zs(kw, 0), zs(kw, 1), zs(d_model, 1), zs(d_model, 2), zs(zga.shape[1], 0)]
        specs += [whole(a) for a in params[:3]] + [zs(d_model, 3 + c) for c in range(6)]
        specs += [whole(a) for a in params[3:]]
        return specs, [z, z, z, z, zga] + params[:3] + [z] * 6 + params[3:], zs(d_model, 0)

    p_specs, p_args, p_om = group(z_p, zga_p, p_step)
    s_specs, s_args, s_om = group(z_s, zga_s, s_step)
    s_specs += [pl.BlockSpec((None, SAMPLE_SEQS, gh, gk, gv), lambda i: (layer, i, 0, 0, 0)),
                pl.BlockSpec((None, SAMPLE_SEQS, hh, hk, hk), lambda i: (layer, i, 0, 0, 0))]
    s_args += [s_gla, s_hgrn]
    p_st = lambda *dims: pl.BlockSpec((1,) + dims, lambda i: (i // steps_per_seq, 0, 0, 0))
    s_st = lambda *dims: pl.BlockSpec((SAMPLE_SEQS,) + dims, lambda i: (i, 0, 0, 0))
    common = dict(layer=layer, gla_heads=gh, gla_dk=gk, gla_dv=gv, hgrn_heads=hh, hgrn_dk=hk)
    prompt_kw = dict(common, block=PROMPT_BLOCK, sub=REF_CHUNK, mm_dtype=BF16)
    sample_kw = dict(common, block=sample_len, sub=math.gcd(REF_CHUNK, sample_len), mm_dtype=F32)
    shapes = lambda t, n: [jax.ShapeDtypeStruct((t, d_model), BF16), jax.ShapeDtypeStruct((n, gh, gk, gv), F32),
                           jax.ShapeDtypeStruct((n, hh, hk, hk), F32)]
    c_specs, c_shapes, cast_chunks = _side_cast_specs(side_casts, n_steps, lambda i: i)
    outs = pl.pallas_call(
        functools.partial(_mixer_body, steps_per_seq=steps_per_seq, prompt_kw=prompt_kw, sample_kw=sample_kw,
                          cast_chunks=cast_chunks),
        grid=(n_steps,),
        in_specs=p_specs + s_specs + c_specs,
        out_specs=[p_om, p_st(gh, gk, gv), p_st(hh, hk, hk), s_om, s_st(gh, gk, gv), s_st(hh, hk, hk)] + c_specs,
        out_shape=shapes(z_p.shape[0], n_prompt) + shapes(z_s.shape[0], n_sample) + c_shapes,
        compiler_params=_cparams(1),
        name=name,
    )(*p_args, *s_args, *side_casts)
    return outs[:3], outs[3:6], outs[6:]


def _xattn_sample_body(q_ref, k_ref, v_ref, o_ref, *, heads, dh, seq_len):
    scale = dh ** -0.5
    n_mem = k_ref.shape[1]
    for s in range(k_ref.shape[0]):
        rows = slice(s * seq_len, (s + 1) * seq_len)
        q = q_ref[rows, :]
        q2 = jnp.concatenate([q[:, h * dh:(h + 1) * dh] for h in range(heads)], axis=0)
        k2 = k_ref[s].reshape(n_mem * heads, dh)
        v2 = v_ref[s].reshape(n_mem * heads, dh)
        sc = lax.dot_general(q2, k2, (((1,), (1,)), ((), ())), preferred_element_type=F32) * scale
        q_head = lax.broadcasted_iota(jnp.int32, sc.shape, 0) // seq_len
        k_head = lax.broadcasted_iota(jnp.int32, sc.shape, 1) % heads
        sc = jnp.where(q_head == k_head, sc, -jnp.inf)
        p = jnp.exp(sc - jnp.max(sc, axis=-1, keepdims=True))
        o2 = jnp.dot(p, v2, preferred_element_type=F32) / jnp.sum(p, axis=-1, keepdims=True)
        for h in range(heads):
            o_ref[rows, h * dh:(h + 1) * dh] = o2[h * seq_len:(h + 1) * seq_len].astype(o_ref.dtype)


def _mid_body(om_ref, x_ref, k_ref, v_ref, wo_ref, wq_ref, wm_ref, g_ref, o_ref, *, heads, dh):
    scale = dh ** -0.5
    x1 = x_ref[...] + jnp.dot(om_ref[...], wo_ref[...], preferred_element_type=F32)
    ms = jnp.mean(x1 * x1, axis=-1, keepdims=True)
    hx = (x1 * lax.rsqrt(ms + EPS) * g_ref[...]).astype(BF16)
    mq = jnp.dot(hx, wq_ref[...], preferred_element_type=F32).astype(BF16)
    heads_out = []
    for h in range(heads):
        hs = slice(h * dh, (h + 1) * dh)
        sc = lax.dot_general(mq[:, hs], k_ref[0, :, hs], (((1,), (1,)), ((), ())),
                             preferred_element_type=F32) * scale
        p = jnp.exp(sc - jnp.max(sc, axis=-1, keepdims=True))
        o = jnp.dot(p.astype(BF16), v_ref[0, :, hs], preferred_element_type=F32) / jnp.sum(p, axis=-1, keepdims=True)
        heads_out.append(o.astype(BF16))
    mo = jnp.concatenate(heads_out, axis=1)
    o_ref[...] = x1 + jnp.dot(mo, wm_ref[...], preferred_element_type=F32)


def _mid_prompt(o_mix, x, mem_k, mem_v, w_out, w_q, w_o, gain, *, n_seq, seq_len, name):
    t, d = x.shape
    n_mem = mem_k.shape[1]
    tm = 256
    n_t = seq_len // tm
    rows = lambda dtype_rows: pl.BlockSpec((tm, d), lambda b, i: (b * n_t + i, 0))
    kv = pl.BlockSpec((1, n_mem, d), lambda b, i: (b, 0, 0))
    wspec = pl.BlockSpec((d, d), lambda b, i: (0, 0), pipeline_mode=pl.Buffered(1))
    return pl.pallas_call(
        functools.partial(_mid_body, heads=MEM_HEADS, dh=d // MEM_HEADS),
        grid=(n_seq, n_t),
        in_specs=[rows(BF16), rows(F32), kv, kv, wspec, wspec, wspec, pl.BlockSpec((1, d), lambda b, i: (0, 0))],
        out_specs=rows(F32),
        out_shape=jax.ShapeDtypeStruct((t, d), F32),
        compiler_params=_cparams(2),
        name=name,
    )(o_mix, x, mem_k, mem_v, w_out, w_q, w_o, gain)


def _ffn_up_body(*refs, mode, seq_len, tiles_per_seq, row_chunk, last_shift, side):
    if mode == "prompt" and side is not None:
        (x_ref, g_ref, wg_ref, wv_ref, cw_ref, cb_ref, sq_ref, sk_ref, sv_ref,
         act_ref, st_ref, so_ref, h_ref, tail_ref) = refs
    elif mode == "prompt":
        x_ref, g_ref, wg_ref, wv_ref, cw_ref, cb_ref, act_ref, st_ref, h_ref, tail_ref = refs
    else:
        x_ref, g_ref, wg_ref, wv_ref, cw_ref, cb_ref, buf_ref, act_ref, st_ref, h_ref = refs
        tail_ref = None
    i = pl.program_id(0)
    j = pl.program_id(1)

    if side is not None:
        @pl.when(i * pl.num_programs(1) + j < side["n_steps"])
        def _():
            _xattn_sample_body(sq_ref, sk_ref, sv_ref, so_ref, heads=MEM_HEADS, dh=side["dh"],
                               seq_len=side["seq_len"])

    @pl.when(j == 0)
    def _():
        def body(c, carry):
            rows = pl.ds(pl.multiple_of(c * row_chunk, row_chunk), row_chunk)
            xf = x_ref[rows, :]
            ms = jnp.mean(xf * xf, axis=-1, keepdims=True)
            h_ref[rows, :] = (xf * lax.rsqrt(ms + EPS) * g_ref[...]).astype(BF16)
            return carry
        lax.fori_loop(0, x_ref.shape[0] // row_chunk, body, 0)

    if mode == "prompt":
        @pl.when(i % tiles_per_seq == 0)
        def _():
            tail_ref[j] = jnp.zeros(tail_ref.shape[1:], F32)

    def step(shift):
        def place(a):
            if shift == 0:
                return a
            return jnp.concatenate([a[..., shift:], jnp.zeros(a.shape[:-1] + (shift,), a.dtype)], axis=-1)

        h = h_ref[...]
        ug = jnp.dot(h, wg_ref[...], preferred_element_type=F32)
        tm = ug.shape[0]
        row = lax.broadcasted_iota(jnp.int32, ug.shape, 0)
        roll1 = pltpu.roll(ug, 1, axis=0)
        roll2 = pltpu.roll(ug, 2, axis=0)
        if mode == "prompt":
            tail = tail_ref[j]
            prev1 = tail[7:8, :]
            prev2 = tail[6:7, :]
            sh1 = jnp.where(row >= 1, roll1, prev1)
            sh2 = jnp.where(row >= 2, roll2, jnp.where(row == 0, prev2, prev1))
            tail_ref[j] = ug[tm - 8:, :]
            st_ref[...] = place(ug[tm - 8:, :])
        else:
            buf = buf_ref[...]
            n_seq = buf.shape[0]
            spread = lambda r: jnp.broadcast_to(buf[:, r:r + 1, :],
                                                (n_seq, seq_len, buf.shape[2])).reshape(ug.shape)
            prev2, prev1 = spread(0), spread(1)
            pos = row % seq_len
            sh1 = jnp.where(pos >= 1, roll1, prev1)
            sh2 = jnp.where(pos >= 2, roll2, jnp.where(pos == 0, prev2, prev1))
            st_ref[...] = place(ug.reshape(n_seq, seq_len, ug.shape[1])[:, seq_len - (CONV_W - 1):, :])
        conv = cw_ref[0:1, :] * sh2 + cw_ref[1:2, :] * sh1 + cw_ref[2:3, :] * ug + cb_ref[...]
        gate = conv * _sigmoid(conv)
        uv = jnp.dot(h, wv_ref[...], preferred_element_type=F32)
        act_ref[...] = place((gate * uv).astype(act_ref.dtype))

    if last_shift == 0:
        step(0)
    else:
        last = pl.num_programs(1) - 1
        pl.when(j < last)(functools.partial(step, 0))
        pl.when(j == last)(functools.partial(step, last_shift))


def _ffn_up(x, gain, w_up, cw, cb, carry, *, mode, seq_len, name, side=None):
    m, k = x.shape
    d_ff = w_up.shape[1] // 2
    tm = _pick_tile(m, (1024,))
    tn = 4 * LANE
    nf = -(-d_ff // tn) * tn
    last_shift = nf - d_ff
    assert d_ff % LANE == 0 and d_ff >= tn
    col = lambda j, base=0: pl.multiple_of(base + jnp.minimum(j * tn, d_ff - tn), LANE)
    elem = lambda *dims: tuple(pl.Element(n) for n in dims)
    xs = pl.BlockSpec((tm, k), lambda i, j: (i, 0))
    gs = pl.BlockSpec((1, k), lambda i, j: (0, 0))
    wgs = pl.BlockSpec(elem(k, tn), lambda i, j: (0, col(j)))
    wvs = pl.BlockSpec(elem(k, tn), lambda i, j: (0, col(j, d_ff)))
    cws = pl.BlockSpec(elem(CONV_W, tn), lambda i, j: (0, col(j)))
    cbs = pl.BlockSpec(elem(1, tn), lambda i, j: (0, col(j)))
    ts = pl.BlockSpec((tm, tn), lambda i, j: (i, j))
    in_specs = [xs, gs, wgs, wvs, cws, cbs]
    args = [x, gain, w_up, w_up, cw, cb]
    scratch = [pltpu.VMEM((tm, k), BF16)]
    if mode == "prompt":
        assert seq_len % tm == 0
        scratch.append(pltpu.VMEM((nf // tn, 8, tn), F32))
        st_spec = pl.BlockSpec((None, 8, tn), lambda i, j: (i, 0, j))
        st_shape = jax.ShapeDtypeStruct((m // tm, 8, nf), F32)
    else:
        assert tm % seq_len == 0 and seq_len == 8
        n_blk = tm // seq_len
        in_specs.append(pl.BlockSpec(elem(n_blk, CONV_W - 1, tn), lambda i, j: (i * n_blk, 0, col(j))))
        args.append(carry)
        st_spec = pl.BlockSpec((n_blk, CONV_W - 1, tn), lambda i, j: (i, 0, j))
        st_shape = jax.ShapeDtypeStruct((m // seq_len, CONV_W - 1, nf), F32)
    n_j = nf // tn
    out_specs = [ts, st_spec]
    out_shape = [jax.ShapeDtypeStruct((m, nf), BF16), st_shape]
    side_kw = None
    if side is not None:
        assert mode == "prompt"
        mq, cache_k, cache_v, layer, s_len = side
        n_side = mq.shape[0] // (SAMPLE_SEQS * s_len)
        assert n_side <= (m // tm) * n_j
        grp = lambda i, j: jnp.minimum(i * n_j + j, n_side - 1)
        q_spec = pl.BlockSpec((SAMPLE_SEQS * s_len, mq.shape[1]), lambda i, j: (grp(i, j), 0))
        kv_spec = pl.BlockSpec((None, SAMPLE_SEQS) + cache_k.shape[2:], lambda i, j: (layer, grp(i, j), 0, 0, 0))
        in_specs += [q_spec, kv_spec, kv_spec]
        args += [mq, cache_k, cache_v]
        out_specs.append(q_spec)
        out_shape.append(jax.ShapeDtypeStruct(mq.shape, F32))
        side_kw = dict(n_steps=n_side, dh=cache_k.shape[-1], seq_len=s_len)
    return pl.pallas_call(
        functools.partial(_ffn_up_body, mode=mode, seq_len=seq_len, tiles_per_seq=max(seq_len // tm, 1),
                          row_chunk=128, last_shift=last_shift, side=side_kw),
        grid=(m // tm, n_j),
        in_specs=in_specs,
        out_specs=out_specs,
        out_shape=out_shape,
        scratch_shapes=scratch,
        compiler_params=_cparams(2),
        name=name,
    )(*args)


def _in_proj(x, p, name, side_casts=()):
    ga_off, rank = p["ga_off"], p["rank"]
    w_in_t = p["w_in_t"]
    return _dense(x, w_in_t, gain=p["norm_mix_g"], prologue="norm", w_transposed=True, tn=x.shape[1],
                  out_dtype=BF16, name=name, n_cols=w_in_t.shape[0] - rank,
                  skip_cols=(ga_off, rank), extra_cols=(ga_off, rank), side_casts=side_casts)


def _conv_ffn(x2, carry, p, *, mode, n_seq, seq_len, tag, side=None):
    d_ff = p["d_ff"]
    act, ug_rows, *mo_side = _ffn_up(x2, p["norm_ffn_g"], p["w_up"], p["ffn_conv_w"], p["ffn_conv_b"], carry,
                                     mode=mode, seq_len=seq_len, name=f"ffn_up_{tag}", side=side)
    final = p["norm_final_g"] if p["is_last"] else None
    y = _dense(act, p["w_down"], res=x2, prologue="plain", out_dtype=F32, name=f"ffn_down_{tag}",
               final_gain=final, tm=512 if final is not None else None, tn=1024 if final is not None else None)
    if mode == "prompt":
        tiles_per_seq = ug_rows.shape[0] // n_seq
        conv_new = ug_rows[tiles_per_seq - 1::tiles_per_seq, 8 - (CONV_W - 1):, :d_ff]
    else:
        conv_new = ug_rows[:, :, :d_ff]
    return y, conv_new, (mo_side[0] if mo_side else None)


def kernel(x_prompt, x_sample, mem_prompt, state_gla, state_hgrn, state_ffn_conv, cache_mem_k, cache_mem_v,
           norm_mix_g, w_in, gla_w_a_up, gla_b_a, gla_norm_g, hgrn_lower_bound, hgrn_norm_g, w_out,
           norm_xattn_g, norm_mem_g, w_mem_q, w_mem_k, w_mem_v, w_mem_o,
           norm_ffn_g, w_ffn_up, ffn_conv_w, ffn_conv_b, w_ffn_down, norm_final_g):
    depth = w_in.shape[0]
    bp, lp, d = x_prompt.shape
    bs, ls, _ = x_sample.shape
    n_mem = mem_prompt.shape[1]
    d_ff = w_ffn_down.shape[1]
    rank = gla_w_a_up.shape[1]
    qk = gla_w_a_up.shape[2]
    ga_off = 2 * qk + 2 * d

    xp = x_prompt.reshape(bp * lp, d)
    xs = x_sample.reshape(bs * ls, d)
    mem = mem_prompt.reshape(bp * n_mem, d)
    row = lambda v: v.reshape(1, -1)

    outs = {k: [] for k in ("gla_p", "hgrn_p", "conv_p", "mk_p", "mv_p", "gla_s", "hgrn_s", "conv_s")}
    for l in range(depth):
        p = {
            "w_in_t": jnp.swapaxes(w_in[l], 0, 1).astype(BF16),
            "ga_off": ga_off, "rank": rank,
            "gla_w_a_up": gla_w_a_up[l].astype(BF16),
            "gla_b_a": row(gla_b_a[l]), "gla_norm_g": row(gla_norm_g[l]),
            "hgrn_lower_bound": hgrn_lower_bound, "hgrn_norm_g": row(hgrn_norm_g[l]),
            "norm_mix_g": row(norm_mix_g[l]), "norm_xattn_g": row(norm_xattn_g[l]),
            "norm_ffn_g": row(norm_ffn_g[l]), "norm_final_g": row(norm_final_g), "is_last": l == depth - 1,
            "ffn_conv_w": ffn_conv_w[l], "ffn_conv_b": row(ffn_conv_b[l]),
            "d_ff": d_ff,
        }
        g_mem = row(norm_mem_g[l])
        mk = _dense(mem, w_mem_k, gain=g_mem, prologue="norm", out_dtype=F32, name=f"mem_k_{l}", layer=l)
        mv = _dense(mem, w_mem_v, gain=g_mem, prologue="norm", out_dtype=F32, name=f"mem_v_{l}", layer=l)
        mk3, mv3 = mk.reshape(bp, n_mem, d), mv.reshape(bp, n_mem, d)
        outs["mk_p"].append(mk3.reshape(bp, n_mem, MEM_HEADS, d // MEM_HEADS))
        outs["mv_p"].append(mv3.reshape(bp, n_mem, MEM_HEADS, d // MEM_HEADS))
        zp, zga_p, p["w_out"], p["w_mem_q"], p["w_mem_o"] = _in_proj(
            xp, p, f"in_proj_p{l}", side_casts=(w_out[l], w_mem_q[l], w_mem_o[l]))
        zs, zga_s = _in_proj(xs, p, f"in_proj_s{l}")
        (om_p, sg, sh), (om_s, sg2, sh2), (p["w_up"], p["w_down"]) = _mixer(
            zp, zga_p, zs, zga_s, p, state_gla, state_hgrn, layer=l, n_prompt=bp, prompt_len=lp,
            n_sample=bs, sample_len=ls, d_model=d, name=f"mixer_{l}", side_casts=(w_ffn_up[l], w_ffn_down[l]))
        outs["gla_p"].append(sg); outs["hgrn_p"].append(sh)
        outs["gla_s"].append(sg2); outs["hgrn_s"].append(sh2)
        x1_s = _dense(om_s, p["w_out"], res=xs, prologue="plain", out_dtype=F32, name=f"out_proj_s{l}")
        mq_s = _dense(x1_s, p["w_mem_q"], gain=p["norm_xattn_g"], prologue="norm", out_dtype=F32,
                      name=f"mem_q_s{l}")
        x2_p = _mid_prompt(om_p, xp, mk3.astype(BF16), mv3.astype(BF16), p["w_out"], p["w_mem_q"], p["w_mem_o"],
                           p["norm_xattn_g"], n_seq=bp, seq_len=lp, name=f"mid_p{l}")
        xp, cb, mo_s = _conv_ffn(x2_p, None, p, mode="prompt", n_seq=bp, seq_len=lp, tag=f"p{l}",
                                 side=(mq_s, cache_mem_k, cache_mem_v, l, ls))
        x2_s = _dense(mo_s, p["w_mem_o"], res=x1_s, prologue="cast", out_dtype=F32, name=f"mem_o_s{l}")
        xs, cb2, _ = _conv_ffn(x2_s, state_ffn_conv[l], p, mode="sample", n_seq=bs, seq_len=ls, tag=f"s{l}")
        outs["conv_p"].append(cb); outs["conv_s"].append(cb2)

    y_prompt = xp.reshape(bp, lp, d)
    y_sample = xs.reshape(bs, ls, d)
    st = lambda k: jnp.stack(outs[k])
    return (y_prompt, y_sample, st("gla_p"), st("hgrn_p"), st("conv_p"), st("mk_p"), st("mv_p"),
            st("gla_s"), st("hgrn_s"), st("conv_s"))
```

```python
import functools
import math

import jax
import jax.numpy as jnp
from jax import lax
from jax.experimental import pallas as pl
from jax.experimental.pallas import tpu as pltpu

F32 = jnp.float32
BF16 = jnp.bfloat16

LANE = 128
BF16_ROWS = 16
V7X_VMEM_LIMIT = 56 * 1024 * 1024

EPS = 1e-6
GLA_HEADS = 4
GLA_GATE_NORM = 16.0
HGRN_DK = 128
MEM_HEADS = 4
CONV_W = 3
REF_CHUNK = 16
PROMPT_BLOCK = 64
SAMPLE_SEQS = 2


def _cparams(n_axes):
    return pltpu.CompilerParams(dimension_semantics=("arbitrary",) * n_axes,
                                vmem_limit_bytes=V7X_VMEM_LIMIT)


def _sigmoid(x):
    return 0.5 * jnp.tanh(0.5 * x) + 0.5


def _pick_tile(n, candidates):
    for c in candidates:
        if n % c == 0:
            return c
    return n


def _side_cast_specs(arrays, n_steps, step_index):
    specs, shapes, chunks = [], [], []
    for a in arrays:
        rows = a.shape[0]
        c = max(c for c in range(1, n_steps + 1) if rows % c == 0 and (rows // c) % BF16_ROWS == 0)
        specs.append(pl.BlockSpec((rows // c, a.shape[1]),
                                  lambda *ids, c=c: (jnp.minimum(step_index(*ids), c - 1), 0)))
        shapes.append(jax.ShapeDtypeStruct(a.shape, BF16))
        chunks.append(c)
    return specs, shapes, tuple(chunks)


def _side_cast_run(step, src_refs, dst_refs, chunks):
    for src_ref, dst_ref, n_chunks in zip(src_refs, dst_refs, chunks):
        @pl.when(step < n_chunks)
        def _(src_ref=src_ref, dst_ref=dst_ref):
            dst_ref[...] = src_ref[...].astype(dst_ref.dtype)


def _dense_body(*refs, prologue, has_res, has_extra, has_final, row_chunk, w_transposed, cast_chunks):
    it = iter(refs)
    x_ref = next(it)
    g_ref = next(it) if prologue == "norm" else None
    w_ref = next(it)
    we_ref = next(it) if has_extra else None
    r_ref = next(it) if has_res else None
    fg_ref = next(it) if has_final else None
    cast_in = [next(it) for _ in cast_chunks]
    o_ref = next(it)
    oe_ref = next(it) if has_extra else None
    cast_out = [next(it) for _ in cast_chunks]
    h_ref = next(it) if prologue != "plain" else None
    j = pl.program_id(1)
    nt = (((1,), (1,)), ((), ()))

    _side_cast_run(pl.program_id(0) * pl.num_programs(1) + j, cast_in, cast_out, cast_chunks)

    if prologue != "plain":
        @pl.when(j == 0)
        def _():
            def body(c, carry):
                rows = pl.ds(pl.multiple_of(c * row_chunk, row_chunk), row_chunk)
                xf = x_ref[rows, :].astype(F32)
                if prologue == "norm":
                    ms = jnp.mean(xf * xf, axis=-1, keepdims=True)
                    xf = xf * lax.rsqrt(ms + EPS) * g_ref[...]
                h_ref[rows, :] = xf.astype(BF16)
                return carry
            lax.fori_loop(0, x_ref.shape[0] // row_chunk, body, 0)
            if has_extra:
                oe_ref[...] = lax.dot_general(h_ref[...], we_ref[...], nt, preferred_element_type=F32)
        lhs = h_ref[...]
    else:
        lhs = x_ref[...]
    w = w_ref[...].astype(BF16)
    if w_transposed:
        acc = lax.dot_general(lhs, w, nt, preferred_element_type=F32)
    else:
        acc = jnp.dot(lhs, w, preferred_element_type=F32)
    if has_res:
        acc = acc + r_ref[...]
    if not has_final:
        o_ref[...] = acc.astype(o_ref.dtype)
    else:
        tn = acc.shape[1]
        o_ref[:, pl.ds(pl.multiple_of(j * tn, tn), tn)] = acc

        @pl.when(j == pl.num_programs(1) - 1)
        def _():
            def body(c, carry):
                rows = pl.ds(pl.multiple_of(c * row_chunk, row_chunk), row_chunk)
                xf = o_ref[rows, :]
                ms = jnp.mean(xf * xf, axis=-1, keepdims=True)
                o_ref[rows, :] = xf * lax.rsqrt(ms + EPS) * fg_ref[...]
                return carry
            lax.fori_loop(0, o_ref.shape[0] // row_chunk, body, 0)


def _dense(x, w, *, gain=None, res=None, prologue, out_dtype, name, layer=0, col_off=0, n_cols=None,
           w_transposed=False, extra_cols=None, final_gain=None, tm=None, tn=None, skip_cols=None,
           side_casts=()):
    m = x.shape[0]
    k = x.shape[1] if w_transposed else w.shape[-2]
    assert k == x.shape[1] or (k % LANE == 0 and k < x.shape[1] and prologue == "plain")
    n = n_cols if n_cols is not None else w.shape[-1]
    if tm is None:
        tm = _pick_tile(m, (1024, 512, 256, 128, 64, 32, 16))
    if tn is None:
        tn = _pick_tile(n, (1024, 512, 256, 128) if k <= 2048 else (512, 256, 128))
    assert m % tm == 0 and n % tn == 0
    row_chunk = min(tm, 128)
    in_specs = [pl.BlockSpec((tm, k), lambda i, j: (i, 0))]
    args = [x]
    if prologue == "norm":
        in_specs.append(pl.BlockSpec((1, k), lambda i, j: (0, 0)))
        args.append(gain)
    if w_transposed:
        assert col_off % BF16_ROWS == 0 and tn % BF16_ROWS == 0
        s_at, s_w = skip_cols if skip_cols is not None else (n, 0)
        assert s_at % tn == 0 and s_w % BF16_ROWS == 0
        in_specs.append(pl.BlockSpec(
            (pl.Element(tn), pl.Element(k)),
            lambda i, j: (pl.multiple_of(col_off + j * tn + jnp.where(j * tn >= s_at, s_w, 0), BF16_ROWS), 0)))
    elif w.ndim == 3:
        assert col_off % tn == 0
        in_specs.append(pl.BlockSpec((None, k, tn), lambda i, j: (layer, 0, j + col_off // tn)))
    else:
        in_specs.append(pl.BlockSpec((k, tn), lambda i, j: (0, j)))
    args.append(w)
    out_specs = [pl.BlockSpec((tm, tn), lambda i, j: (i, j))]
    out_shape = [jax.ShapeDtypeStruct((m, n), out_dtype)]
    if extra_cols is not None:
        e_off, e_n = extra_cols
        assert w_transposed and prologue != "plain" and e_off % BF16_ROWS == 0 and e_n % BF16_ROWS == 0
        in_specs.append(pl.BlockSpec((pl.Element(e_n), pl.Element(k)), lambda i, j: (e_off, 0)))
        args.append(w)
        out_specs.append(pl.BlockSpec((tm, e_n), lambda i, j: (i, 0)))
        out_shape.append(jax.ShapeDtypeStruct((m, e_n), F32))
    if res is not None:
        in_specs.append(pl.BlockSpec((tm, tn), lambda i, j: (i, j)))
        args.append(res)
    if final_gain is not None:
        assert out_dtype == F32
        in_specs.append(pl.BlockSpec((1, n), lambda i, j: (0, 0)))
        args.append(final_gain)
        out_specs[0] = pl.BlockSpec((tm, n), lambda i, j: (i, 0))
    n_j = n // tn
    c_specs, c_shapes, cast_chunks = _side_cast_specs(side_casts, (m // tm) * n_j, lambda i, j: i * n_j + j)
    in_specs += c_specs
    args += list(side_casts)
    out_specs += c_specs
    out_shape += c_shapes
    scratch = [] if prologue == "plain" else [pltpu.VMEM((tm, k), BF16)]
    outs = pl.pallas_call(
        functools.partial(_dense_body, prologue=prologue, has_res=res is not None,
                          has_extra=extra_cols is not None, has_final=final_gain is not None,
                          row_chunk=row_chunk, w_transposed=w_transposed, cast_chunks=cast_chunks),
        grid=(m // tm, n_j),
        in_specs=in_specs,
        out_specs=out_specs,
        out_shape=out_shape,
        scratch_shapes=scratch,
        compiler_params=_cparams(2),
        name=name,
    )(*args)
    return outs if (extra_cols is not None or side_casts) else outs[0]


def _cumsum_rows(x, group):
    rows = lax.broadcasted_iota(jnp.int32, x.shape, 0) % group
    shift = 1
    while shift < group:
        rolled = pltpu.roll(x, shift, axis=0)
        x = x + jnp.where(rows >= shift, rolled, 0.0)
        shift *= 2
    return x


def _cumsum_rows_mxu(x, group):
    n = x.shape[0]
    row = lax.broadcasted_iota(jnp.int32, (n, n), 0)
    col = lax.broadcasted_iota(jnp.int32, (n, n), 1)
    tri = jnp.where(jnp.logical_and(col <= row, col // group == row // group), 1.0, 0.0).astype(BF16)
    hi = x.astype(BF16)
    lo = (x - hi.astype(F32)).astype(BF16)
    return (jnp.dot(tri, hi, preferred_element_type=F32) + jnp.dot(tri, lo, preferred_element_type=F32))


def _lockstep(gens):
    gens = list(gens)
    results = [None] * len(gens)
    live = list(range(len(gens)))
    while live:
        still = []
        for n in live:
            try:
                next(gens[n])
                still.append(n)
            except StopIteration as stop:
                results[n] = stop.value
        live = still
    return results


def _recurrence_block(q, k, v, la, s_read, *, sub, mm_dtype):
    bt, kd = q.shape
    vd = v.shape[1]
    ns = bt // sub
    b_loc = _cumsum_rows_mxu(la, sub) if mm_dtype == BF16 else _cumsum_rows(la, sub)
    qd = q * jnp.exp(b_loc)
    ki = k * jnp.exp(-b_loc)
    tot = [b_loc[(i + 1) * sub - 1:(i + 1) * sub, :] for i in range(ns)]
    pre = [jnp.zeros((1, kd), F32)]
    for i in range(ns):
        pre.append(pre[-1] + tot[i])
    sl = [slice(i * sub, (i + 1) * sub) for i in range(ns)]
    vb = v.astype(mm_dtype)
    qdm = qd.astype(mm_dtype)
    q_state = jnp.concatenate([qd[sl[i]] * jnp.exp(pre[i]) for i in range(ns)], axis=0) if ns > 1 else qd
    q_state = q_state.astype(mm_dtype)
    ke = [ki[sl[i]] * jnp.exp(tot[i]) for i in range(ns)]
    kmats = []
    for i in range(ns):
        parts = [ke[j] if j == i - 1 else ke[j] * jnp.exp(pre[i] - pre[j + 1]) for j in range(i)]
        parts.append(ki[sl[i]])
        parts += [jnp.zeros((sub, kd), F32)] * (ns - 1 - i)
        kmats.append((jnp.concatenate(parts, axis=0) if ns > 1 else parts[0]).astype(mm_dtype))
    k_end = jnp.concatenate([ke[i] if i == ns - 1 else ke[i] * jnp.exp(pre[ns] - pre[i + 1])
                             for i in range(ns)], axis=0) if ns > 1 else ke[0]
    k_end = k_end.astype(mm_dtype)
    decay = jnp.broadcast_to(jnp.exp(pre[ns]), (LANE, kd)).T
    yield

    s_prev = s_read()
    o_state = jnp.dot(q_state, s_prev.astype(mm_dtype), preferred_element_type=F32)
    a_rows = [lax.dot_general(qdm[sl[i]], kmats[i], (((1,), (1,)), ((), ())), preferred_element_type=F32)
              for i in range(ns)]
    ds = lax.dot_general(k_end, vb, (((0,), (0,)), ((), ())), preferred_element_type=F32)
    yield

    a = jnp.concatenate(a_rows, axis=0) if ns > 1 else a_rows[0]
    row = lax.broadcasted_iota(jnp.int32, a.shape, 0)
    col = lax.broadcasted_iota(jnp.int32, a.shape, 1)
    a = jnp.where(col <= row, a, 0.0).astype(mm_dtype)
    o = jnp.dot(a, vb, preferred_element_type=F32) + o_state
    s_new = jnp.concatenate([decay * s_prev[:, c * LANE:(c + 1) * LANE] for c in range(vd // LANE)],
                            axis=1) + ds
    yield
    return o, s_new


def _head_gated_norm(o, gnorm, gate):
    ms = jnp.mean(o * o, axis=-1, keepdims=True)
    return o * lax.rsqrt(ms + EPS) * gnorm * (gate * _sigmoid(gate))


def _log_sigmoid(x):
    return jnp.minimum(x, 0.0) - jnp.log(1.0 + jnp.exp(-jnp.abs(x)))


N_MIXER_IN = 16


def _mixer_body(*refs, steps_per_seq, prompt_kw, sample_kw, cast_chunks):
    n_p, n_s, n_c = N_MIXER_IN, N_MIXER_IN + 2, len(cast_chunks)
    p_in, s_in, c_in = refs[:n_p], refs[n_p:n_p + n_s], refs[n_p + n_s:n_p + n_s + n_c]
    outs = refs[n_p + n_s + n_c:]
    (p_om, p_sg, p_sh), (s_om, s_sg, s_sh), c_out = outs[:3], outs[3:6], outs[6:]
    s_sg_in, s_sh_in = s_in[N_MIXER_IN:]
    step = pl.program_id(0)
    _side_cast_run(step, c_in, c_out, cast_chunks)

    @pl.when(step % steps_per_seq == 0)
    def _():
        p_sg[...] = jnp.zeros(p_sg.shape, F32)
        p_sh[...] = jnp.zeros(p_sh.shape, F32)

    p_heads = _mixer_heads(p_in, **prompt_kw)
    s_heads = _mixer_heads(s_in, **sample_kw)
    p_block, s_block = prompt_kw["block"], sample_kw["block"]
    n_p_blocks = p_in[0].shape[0] // p_block
    n_s_seqs = s_in[0].shape[0] // s_block
    gh, hk = prompt_kw["gla_heads"], prompt_kw["hgrn_dk"]
    s_mixes = []
    for c in range(max(n_p_blocks, n_s_seqs)):
        gens, n_pg = [], 0
        if c < n_p_blocks:
            rows = slice(c * p_block, (c + 1) * p_block)
            load = lambda ref, cols, rows=rows: ref[rows, cols].astype(F32)
            gens += p_heads(load, lambda h: p_sg[0, h], lambda h: p_sh[0, h])
            n_pg = len(gens)
        if c < n_s_seqs:
            load = lambda ref, cols, c=c: ref[:, cols].astype(F32)[c * s_block:(c + 1) * s_block]
            gens += s_heads(load, lambda h, c=c: s_sg_in[c, h], lambda h, c=c: s_sh_in[c, h])
        res = _lockstep(gens)
        if c < n_p_blocks:
            for h, s_new in enumerate(res[:gh]):
                p_sg[0, h] = s_new
            for h, (mix, s_new) in enumerate(res[gh:n_pg]):
                p_sh[0, h] = s_new
                p_om[rows, h * hk:(h + 1) * hk] = mix.astype(p_om.dtype)
        if c < n_s_seqs:
            s_res = res[n_pg:]
            for h, s_new in enumerate(s_res[:gh]):
                s_sg[c, h] = s_new
            for h, (mix, s_new) in enumerate(s_res[gh:]):
                s_sh[c, h] = s_new
            s_mixes.append([mix for mix, _ in s_res[gh:]])
    for h in range(len(s_mixes[0])):
        s_om[:, h * hk:(h + 1) * hk] = jnp.concatenate([m[h] for m in s_mixes], axis=0).astype(s_om.dtype)


def _mixer_heads(ins, *, layer, gla_heads, gla_dk, gla_dv, hgrn_heads, hgrn_dk, block, sub, mm_dtype):
    (q_ref, k_ref, v_ref, g_ref, ga_ref, wup_ref, ba_ref, gng_ref,
     hq_ref, hf_ref, hi_ref, hg_ref, ma_ref, mb_ref, lb_ref, gnh_ref) = ins[:N_MIXER_IN]

    p = lb_ref[...]
    e = jnp.exp(p - jnp.max(p, axis=0, keepdims=True))
    lb_all = jnp.sum(e[:layer + 1], axis=0, keepdims=True) / jnp.sum(e, axis=0, keepdims=True)

    def gla_head(h, load, s_read, shared):
        ks = slice(h * gla_dk, (h + 1) * gla_dk)
        vs = slice(h * gla_dv, (h + 1) * gla_dv)
        ga = load(ga_ref, slice(None)).astype(BF16)
        a_logit = jnp.dot(ga, wup_ref[:, ks], preferred_element_type=F32) + ba_ref[:, ks]
        yield
        la = _log_sigmoid(a_logit) * (1.0 / GLA_GATE_NORM)
        q = load(q_ref, ks) * (gla_dk ** -0.5)
        o, s_new = yield from _recurrence_block(q, load(k_ref, ks), load(v_ref, vs), la, s_read,
                                                sub=sub, mm_dtype=mm_dtype)
        shared[h] = _head_gated_norm(o, gng_ref[:, vs], load(g_ref, vs))
        return s_new

    def hgrn_head(h, load, s_read, shared):
        ks = slice(h * hgrn_dk, (h + 1) * hgrn_dk)
        lb = lb_all[:, ks]
        f = lb + (1.0 - lb) * _sigmoid(load(hf_ref, ks))
        hq = load(hq_ref, ks)
        q = hq * _sigmoid(hq) * (hgrn_dk ** -0.5)
        yield
        o, s_new = yield from _recurrence_block(q, 1.0 - f, load(hi_ref, ks), jnp.log(f), s_read,
                                                sub=sub, mm_dtype=mm_dtype)
        o_h = _head_gated_norm(o, gnh_ref[:, ks], load(hg_ref, ks))
        per = gla_dv // hgrn_dk
        o_gla = shared[h // per][:, (h % per) * hgrn_dk:(h % per + 1) * hgrn_dk]
        mix = _sigmoid(load(ma_ref, ks)) * o_gla + _sigmoid(load(mb_ref, ks)) * o_h
        return mix, s_new

    def make(load, sg_read, sh_read):
        shared = {}
        gens = [gla_head(h, load, functools.partial(sg_read, h), shared) for h in range(gla_heads)]
        gens += [hgrn_head(h, load, functools.partial(sh_read, h), shared) for h in range(hgrn_heads)]
        return gens

    return make


def _mixer(z_p, zga_p, z_s, zga_s, p, s_gla, s_hgrn, *, layer, n_prompt, prompt_len, n_sample, sample_len,
           d_model, name, side_casts=()):
    gh, hk = GLA_HEADS, HGRN_DK
    gk, gv = d_model // 2 // gh, d_model // gh
    hh = d_model // hk
    kw = gh * gk
    n_steps = n_sample // SAMPLE_SEQS
    p_step = n_prompt * prompt_len // n_steps
    steps_per_seq = prompt_len // p_step
    assert prompt_len % p_step == 0 and p_step % PROMPT_BLOCK == 0
    s_step = SAMPLE_SEQS * sample_len
    whole = lambda a: pl.BlockSpec(a.shape, lambda i: (0,) * a.ndim)
    params = [p["gla_w_a_up"], p["gla_b_a"], p["gla_norm_g"], p["hgrn_lower_bound"], p["hgrn_norm_g"]]

    def group(z, zga, rows):
        zs = lambda width, col: pl.BlockSpec((rows, width), lambda i: (i, col))
        specs = [zs(kw, 0), zs(kw, 1), zs(d_model, 1), zs(d_model, 2), zs(zga.shape[1], 0)]
        specs += [whole(a) for a in params[:3]] + [zs(d_model, 3 + c) for c in range(6)]
        specs += [whole(a) for a in params[3:]]
        return specs, [z, z, z, z, zga] + params[:3] + [z] * 6 + params[3:], zs(d_model, 0)

    p_specs, p_args, p_om = group(z_p, zga_p, p_step)
    s_specs, s_args, s_om = group(z_s, zga_s, s_step)
    s_specs += [pl.BlockSpec((None, SAMPLE_SEQS, gh, gk, gv), lambda i: (layer, i, 0, 0, 0)),
                pl.BlockSpec((None, SAMPLE_SEQS, hh, hk, hk), lambda i: (layer, i, 0, 0, 0))]
    s_args += [s_gla, s_hgrn]
    p_st = lambda *dims: pl.BlockSpec((1,) + dims, lambda i: (i // steps_per_seq, 0, 0, 0))
    s_st = lambda *dims: pl.BlockSpec((SAMPLE_SEQS,) + dims, lambda i: (i, 0, 0, 0))
    common = dict(layer=layer, gla_heads=gh, gla_dk=gk, gla_dv=gv, hgrn_heads=hh, hgrn_dk=hk)
    prompt_kw = dict(common, block=PROMPT_BLOCK, sub=REF_CHUNK, mm_dtype=BF16)
    sample_kw = dict(common, block=sample_len, sub=math.gcd(REF_CHUNK, sample_len), mm_dtype=F32)
    shapes = lambda t, n: [jax.ShapeDtypeStruct((t, d_model), BF16), jax.ShapeDtypeStruct((n, gh, gk, gv), F32),
                           jax.ShapeDtypeStruct((n, hh, hk, hk), F32)]
    c_specs, c_shapes, cast_chunks = _side_cast_specs(side_casts, n_steps, lambda i: i)
    outs = pl.pallas_call(
        functools.partial(_mixer_body, steps_per_seq=steps_per_seq, prompt_kw=prompt_kw, sample_kw=sample_kw,
                          cast_chunks=cast_chunks),
        grid=(n_steps,),
        in_specs=p_specs + s_specs + c_specs,
        out_specs=[p_om, p_st(gh, gk, gv), p_st(hh, hk, hk), s_om, s_st(gh, gk, gv), s_st(hh, hk, hk)] + c_specs,
        out_shape=shapes(z_p.shape[0], n_prompt) + shapes(z_s.shape[0], n_sample) + c_shapes,
        compiler_params=_cparams(1),
        name=name,
    )(*p_args, *s_args, *side_casts)
    return outs[:3], outs[3:6], outs[6:]


def _xattn_sample_body(q_ref, k_ref, v_ref, o_ref, *, heads, dh, seq_len):
    scale = dh ** -0.5
    n_mem = k_ref.shape[1]
    for s in range(k_ref.shape[0]):
        rows = slice(s * seq_len, (s + 1) * seq_len)
        q = q_ref[rows, :]
        q2 = jnp.concatenate([q[:, h * dh:(h + 1) * dh] for h in range(heads)], axis=0)
        k2 = k_ref[s].reshape(n_mem * heads, dh)
        v2 = v_ref[s].reshape(n_mem * heads, dh)
        sc = lax.dot_general(q2, k2, (((1,), (1,)), ((), ())), preferred_element_type=F32) * scale
        q_head = lax.broadcasted_iota(jnp.int32, sc.shape, 0) // seq_len
        k_head = lax.broadcasted_iota(jnp.int32, sc.shape, 1) % heads
        sc = jnp.where(q_head == k_head, sc, -jnp.inf)
        p = jnp.exp(sc - jnp.max(sc, axis=-1, keepdims=True))
        o2 = jnp.dot(p, v2, preferred_element_type=F32) / jnp.sum(p, axis=-1, keepdims=True)
        for h in range(heads):
            o_ref[rows, h * dh:(h + 1) * dh] = o2[h * seq_len:(h + 1) * seq_len].astype(o_ref.dtype)


def _mid_body(om_ref, x_ref, k_ref, v_ref, wo_ref, wq_ref, wm_ref, g_ref, o_ref, *, heads, dh):
    scale = dh ** -0.5
    x1 = x_ref[...] + jnp.dot(om_ref[...], wo_ref[...], preferred_element_type=F32)
    ms = jnp.mean(x1 * x1, axis=-1, keepdims=True)
    hx = (x1 * lax.rsqrt(ms + EPS) * g_ref[...]).astype(BF16)
    mq = jnp.dot(hx, wq_ref[...], preferred_element_type=F32).astype(BF16)
    heads_out = []
    for h in range(heads):
        hs = slice(h * dh, (h + 1) * dh)
        sc = lax.dot_general(mq[:, hs], k_ref[0, :, hs], (((1,), (1,)), ((), ())),
                             preferred_element_type=F32) * scale
        p = jnp.exp(sc - jnp.max(sc, axis=-1, keepdims=True))
        o = jnp.dot(p.astype(BF16), v_ref[0, :, hs], preferred_element_type=F32) / jnp.sum(p, axis=-1, keepdims=True)
        heads_out.append(o.astype(BF16))
    mo = jnp.concatenate(heads_out, axis=1)
    o_ref[...] = x1 + jnp.dot(mo, wm_ref[...], preferred_element_type=F32)


def _mid_prompt(o_mix, x, mem_k, mem_v, w_out, w_q, w_o, gain, *, n_seq, seq_len, name):
    t, d = x.shape
    n_mem = mem_k.shape[1]
    tm = 256
    n_t = seq_len // tm
    rows = lambda dtype_rows: pl.BlockSpec((tm, d), lambda b, i: (b * n_t + i, 0))
    kv = pl.BlockSpec((1, n_mem, d), lambda b, i: (b, 0, 0))
    wspec = pl.BlockSpec((d, d), lambda b, i: (0, 0), pipeline_mode=pl.Buffered(1))
    return pl.pallas_call(
        functools.partial(_mid_body, heads=MEM_HEADS, dh=d // MEM_HEADS),
        grid=(n_seq, n_t),
        in_specs=[rows(BF16), rows(F32), kv, kv, wspec, wspec, wspec, pl.BlockSpec((1, d), lambda b, i: (0, 0))],
        out_specs=rows(F32),
        out_shape=jax.ShapeDtypeStruct((t, d), F32),
        compiler_params=_cparams(2),
        name=name,
    )(o_mix, x, mem_k, mem_v, w_out, w_q, w_o, gain)


def _pre_xattn_body(om_ref, x_ref, wo_ref, wq_ref, g_ref, x1_ref, mq_ref):
    x1 = x_ref[...] + jnp.dot(om_ref[...], wo_ref[...], preferred_element_type=F32)
    ms = jnp.mean(x1 * x1, axis=-1, keepdims=True)
    hx = (x1 * lax.rsqrt(ms + EPS) * g_ref[...]).astype(BF16)
    x1_ref[...] = x1
    mq_ref[...] = jnp.dot(hx, wq_ref[...], preferred_element_type=F32)


def _pre_xattn_sample(o_mix, x, w_out, w_q, gain, *, name):
    t, d = x.shape
    tm = _pick_tile(t, (256, 128, 64, 32, 16))
    rows = pl.BlockSpec((tm, d), lambda i: (i, 0))
    wspec = pl.BlockSpec((d, d), lambda i: (0, 0), pipeline_mode=pl.Buffered(1))
    return pl.pallas_call(
        _pre_xattn_body,
        grid=(t // tm,),
        in_specs=[rows, rows, wspec, wspec, pl.BlockSpec((1, d), lambda i: (0, 0))],
        out_specs=[rows, rows],
        out_shape=[jax.ShapeDtypeStruct((t, d), F32), jax.ShapeDtypeStruct((t, d), F32)],
        compiler_params=_cparams(1),
        name=name,
    )(o_mix, x, w_out, w_q, gain)


def _ffn_up_body(*refs, mode, seq_len, tiles_per_seq, row_chunk, last_shift, side):
    if mode == "prompt" and side is not None:
        (x_ref, g_ref, wg_ref, wv_ref, cw_ref, cb_ref, sq_ref, sk_ref, sv_ref,
         act_ref, st_ref, so_ref, h_ref, tail_ref) = refs
    elif mode == "prompt":
        x_ref, g_ref, wg_ref, wv_ref, cw_ref, cb_ref, act_ref, st_ref, h_ref, tail_ref = refs
    else:
        x_ref, g_ref, wg_ref, wv_ref, cw_ref, cb_ref, buf_ref, act_ref, st_ref, h_ref = refs
        tail_ref = None
    i = pl.program_id(0)
    j = pl.program_id(1)

    if side is not None:
        @pl.when(i * pl.num_programs(1) + j < side["n_steps"])
        def _():
            _xattn_sample_body(sq_ref, sk_ref, sv_ref, so_ref, heads=MEM_HEADS, dh=side["dh"],
                               seq_len=side["seq_len"])

    @pl.when(j == 0)
    def _():
        def body(c, carry):
            rows = pl.ds(pl.multiple_of(c * row_chunk, row_chunk), row_chunk)
            xf = x_ref[rows, :]
            ms = jnp.mean(xf * xf, axis=-1, keepdims=True)
            h_ref[rows, :] = (xf * lax.rsqrt(ms + EPS) * g_ref[...]).astype(BF16)
            return carry
        lax.fori_loop(0, x_ref.shape[0] // row_chunk, body, 0)

    if mode == "prompt":
        @pl.when(i % tiles_per_seq == 0)
        def _():
            tail_ref[j] = jnp.zeros(tail_ref.shape[1:], F32)

    def step(shift):
        def place(a):
            if shift == 0:
                return a
            return jnp.concatenate([a[..., shift:], jnp.zeros(a.shape[:-1] + (shift,), a.dtype)], axis=-1)

        h = h_ref[...]
        ug = jnp.dot(h, wg_ref[...], preferred_element_type=F32)
        tm = ug.shape[0]
        row = lax.broadcasted_iota(jnp.int32, ug.shape, 0)
        roll1 = pltpu.roll(ug, 1, axis=0)
        roll2 = pltpu.roll(ug, 2, axis=0)
        if mode == "prompt":
            tail = tail_ref[j]
            prev1 = tail[7:8, :]
            prev2 = tail[6:7, :]
            sh1 = jnp.where(row >= 1, roll1, prev1)
            sh2 = jnp.where(row >= 2, roll2, jnp.where(row == 0, prev2, prev1))
            tail_ref[j] = ug[tm - 8:, :]
            st_ref[...] = place(ug[tm - 8:, :])
        else:
            buf = buf_ref[...]
            n_seq = buf.shape[0]
            spread = lambda r: jnp.broadcast_to(buf[:, r:r + 1, :],
                                                (n_seq, seq_len, buf.shape[2])).reshape(ug.shape)
            prev2, prev1 = spread(0), spread(1)
            pos = row % seq_len
            sh1 = jnp.where(pos >= 1, roll1, prev1)
            sh2 = jnp.where(pos >= 2, roll2, jnp.where(pos == 0, prev2, prev1))
            st_ref[...] = place(ug.reshape(n_seq, seq_len, ug.shape[1])[:, seq_len - (CONV_W - 1):, :])
        conv = cw_ref[0:1, :] * sh2 + cw_ref[1:2, :] * sh1 + cw_ref[2:3, :] * ug + cb_ref[...]
        gate = conv * _sigmoid(conv)
        uv = jnp.dot(h, wv_ref[...], preferred_element_type=F32)
        act_ref[...] = place((gate * uv).astype(act_ref.dtype))

    if last_shift == 0:
        step(0)
    else:
        last = pl.num_programs(1) - 1
        pl.when(j < last)(functools.partial(step, 0))
        pl.when(j == last)(functools.partial(step, last_shift))


def _ffn_up(x, gain, w_up, cw, cb, carry, *, mode, seq_len, name, side=None):
    m, k = x.shape
    d_ff = w_up.shape[1] // 2
    tm = _pick_tile(m, (1024,))
    tn = 4 * LANE
    nf = -(-d_ff // tn) * tn
    last_shift = nf - d_ff
    assert d_ff % LANE == 0 and d_ff >= tn
    col = lambda j, base=0: pl.multiple_of(base + jnp.minimum(j * tn, d_ff - tn), LANE)
    elem = lambda *dims: tuple(pl.Element(n) for n in dims)
    xs = pl.BlockSpec((tm, k), lambda i, j: (i, 0))
    gs = pl.BlockSpec((1, k), lambda i, j: (0, 0))
    wgs = pl.BlockSpec(elem(k, tn), lambda i, j: (0, col(j)))
    wvs = pl.BlockSpec(elem(k, tn), lambda i, j: (0, col(j, d_ff)))
    cws = pl.BlockSpec(elem(CONV_W, tn), lambda i, j: (0, col(j)))
    cbs = pl.BlockSpec(elem(1, tn), lambda i, j: (0, col(j)))
    ts = pl.BlockSpec((tm, tn), lambda i, j: (i, j))
    in_specs = [xs, gs, wgs, wvs, cws, cbs]
    args = [x, gain, w_up, w_up, cw, cb]
    scratch = [pltpu.VMEM((tm, k), BF16)]
    if mode == "prompt":
        assert seq_len % tm == 0
        scratch.append(pltpu.VMEM((nf // tn, 8, tn), F32))
        st_spec = pl.BlockSpec((None, 8, tn), lambda i, j: (i, 0, j))
        st_shape = jax.ShapeDtypeStruct((m // tm, 8, nf), F32)
    else:
        assert tm % seq_len == 0 and seq_len == 8
        n_blk = tm // seq_len
        in_specs.append(pl.BlockSpec(elem(n_blk, CONV_W - 1, tn), lambda i, j: (i * n_blk, 0, col(j))))
        args.append(carry)
        st_spec = pl.BlockSpec((n_blk, CONV_W - 1, tn), lambda i, j: (i, 0, j))
        st_shape = jax.ShapeDtypeStruct((m // seq_len, CONV_W - 1, nf), F32)
    n_j = nf // tn
    out_specs = [ts, st_spec]
    out_shape = [jax.ShapeDtypeStruct((m, nf), BF16), st_shape]
    side_kw = None
    if side is not None:
        assert mode == "prompt"
        mq, cache_k, cache_v, layer, s_len = side
        n_side = mq.shape[0] // (SAMPLE_SEQS * s_len)
        assert n_side <= (m // tm) * n_j
        grp = lambda i, j: jnp.minimum(i * n_j + j, n_side - 1)
        q_spec = pl.BlockSpec((SAMPLE_SEQS * s_len, mq.shape[1]), lambda i, j: (grp(i, j), 0))
        kv_spec = pl.BlockSpec((None, SAMPLE_SEQS) + cache_k.shape[2:], lambda i, j: (layer, grp(i, j), 0, 0, 0))
        in_specs += [q_spec, kv_spec, kv_spec]
        args += [mq, cache_k, cache_v]
        out_specs.append(q_spec)
        out_shape.append(jax.ShapeDtypeStruct(mq.shape, F32))
        side_kw = dict(n_steps=n_side, dh=cache_k.shape[-1], seq_len=s_len)
    return pl.pallas_call(
        functools.partial(_ffn_up_body, mode=mode, seq_len=seq_len, tiles_per_seq=max(seq_len // tm, 1),
                          row_chunk=128, last_shift=last_shift, side=side_kw),
        grid=(m // tm, n_j),
        in_specs=in_specs,
        out_specs=out_specs,
        out_shape=out_shape,
        scratch_shapes=scratch,
        compiler_params=_cparams(2),
        name=name,
    )(*args)


def _in_proj(x, p, name, side_casts=()):
    ga_off, rank = p["ga_off"], p["rank"]
    w_in_t = p["w_in_t"]
    return _dense(x, w_in_t, gain=p["norm_mix_g"], prologue="norm", w_transposed=True, tn=x.shape[1],
                  out_dtype=BF16, name=name, n_cols=w_in_t.shape[0] - rank,
                  skip_cols=(ga_off, rank), extra_cols=(ga_off, rank), side_casts=side_casts)


def _conv_ffn(x2, carry, p, *, mode, n_seq, seq_len, tag, side=None):
    d_ff = p["d_ff"]
    act, ug_rows, *mo_side = _ffn_up(x2, p["norm_ffn_g"], p["w_up"], p["ffn_conv_w"], p["ffn_conv_b"], carry,
                                     mode=mode, seq_len=seq_len, name=f"ffn_up_{tag}", side=side)
    final = p["norm_final_g"] if p["is_last"] else None
    y = _dense(act, p["w_down"], res=x2, prologue="plain", out_dtype=F32, name=f"ffn_down_{tag}",
               final_gain=final, tm=512 if final is not None else None, tn=1024 if final is not None else None)
    if mode == "prompt":
        tiles_per_seq = ug_rows.shape[0] // n_seq
        conv_new = ug_rows[tiles_per_seq - 1::tiles_per_seq, 8 - (CONV_W - 1):, :d_ff]
    else:
        conv_new = ug_rows[:, :, :d_ff]
    return y, conv_new, (mo_side[0] if mo_side else None)


def kernel(x_prompt, x_sample, mem_prompt, state_gla, state_hgrn, state_ffn_conv, cache_mem_k, cache_mem_v,
           norm_mix_g, w_in, gla_w_a_up, gla_b_a, gla_norm_g, hgrn_lower_bound, hgrn_norm_g, w_out,
           norm_xattn_g, norm_mem_g, w_mem_q, w_mem_k, w_mem_v, w_mem_o,
           norm_ffn_g, w_ffn_up, ffn_conv_w, ffn_conv_b, w_ffn_down, norm_final_g):
    depth = w_in.shape[0]
    bp, lp, d = x_prompt.shape
    bs, ls, _ = x_sample.shape
    n_mem = mem_prompt.shape[1]
    d_ff = w_ffn_down.shape[1]
    rank = gla_w_a_up.shape[1]
    qk = gla_w_a_up.shape[2]
    ga_off = 2 * qk + 2 * d

    xp = x_prompt.reshape(bp * lp, d)
    xs = x_sample.reshape(bs * ls, d)
    mem = mem_prompt.reshape(bp * n_mem, d)
    row = lambda v: v.reshape(1, -1)

    outs = {k: [] for k in ("gla_p", "hgrn_p", "conv_p", "mk_p", "mv_p", "gla_s", "hgrn_s", "conv_s")}
    for l in range(depth):
        p = {
            "w_in_t": jnp.swapaxes(w_in[l], 0, 1).astype(BF16),
            "ga_off": ga_off, "rank": rank,
            "gla_w_a_up": gla_w_a_up[l].astype(BF16),
            "gla_b_a": row(gla_b_a[l]), "gla_norm_g": row(gla_norm_g[l]),
            "hgrn_lower_bound": hgrn_lower_bound, "hgrn_norm_g": row(hgrn_norm_g[l]),
            "norm_mix_g": row(norm_mix_g[l]), "norm_xattn_g": row(norm_xattn_g[l]),
            "norm_ffn_g": row(norm_ffn_g[l]), "norm_final_g": row(norm_final_g), "is_last": l == depth - 1,
            "ffn_conv_w": ffn_conv_w[l], "ffn_conv_b": row(ffn_conv_b[l]),
            "d_ff": d_ff,
        }
        g_mem = row(norm_mem_g[l])
        mk = _dense(mem, w_mem_k, gain=g_mem, prologue="norm", out_dtype=F32, name=f"mem_k_{l}", layer=l)
        mv = _dense(mem, w_mem_v, gain=g_mem, prologue="norm", out_dtype=F32, name=f"mem_v_{l}", layer=l)
        mk3, mv3 = mk.reshape(bp, n_mem, d), mv.reshape(bp, n_mem, d)
        outs["mk_p"].append(mk3.reshape(bp, n_mem, MEM_HEADS, d // MEM_HEADS))
        outs["mv_p"].append(mv3.reshape(bp, n_mem, MEM_HEADS, d // MEM_HEADS))
        zp, zga_p, p["w_out"], p["w_mem_q"], p["w_mem_o"] = _in_proj(
            xp, p, f"in_proj_p{l}", side_casts=(w_out[l], w_mem_q[l], w_mem_o[l]))
        zs, zga_s = _in_proj(xs, p, f"in_proj_s{l}")
        (om_p, sg, sh), (om_s, sg2, sh2), (p["w_up"], p["w_down"]) = _mixer(
            zp, zga_p, zs, zga_s, p, state_gla, state_hgrn, layer=l, n_prompt=bp, prompt_len=lp,
            n_sample=bs, sample_len=ls, d_model=d, name=f"mixer_{l}", side_casts=(w_ffn_up[l], w_ffn_down[l]))
        outs["gla_p"].append(sg); outs["hgrn_p"].append(sh)
        outs["gla_s"].append(sg2); outs["hgrn_s"].append(sh2)
        x1_s, mq_s = _pre_xattn_sample(om_s, xs, p["w_out"], p["w_mem_q"], p["norm_xattn_g"],
                                       name=f"pre_xattn_s{l}")
        x2_p = _mid_prompt(om_p, xp, mk3.astype(BF16), mv3.astype(BF16), p["w_out"], p["w_mem_q"], p["w_mem_o"],
                           p["norm_xattn_g"], n_seq=bp, seq_len=lp, name=f"mid_p{l}")
        xp, cb, mo_s = _conv_ffn(x2_p, None, p, mode="prompt", n_seq=bp, seq_len=lp, tag=f"p{l}",
                                 side=(mq_s, cache_mem_k, cache_mem_v, l, ls))
        x2_s = _dense(mo_s, p["w_mem_o"], res=x1_s, prologue="cast", out_dtype=F32, name=f"mem_o_s{l}")
        xs, cb2, _ = _conv_ffn(x2_s, state_ffn_conv[l], p, mode="sample", n_seq=bs, seq_len=ls, tag=f"s{l}")
        outs["conv_p"].append(cb); outs["conv_s"].append(cb2)

    y_prompt = xp.reshape(bp, lp, d)
    y_sample = xs.reshape(bs, ls, d)
    st = lambda k: jnp.stack(outs[k])
    return (y_prompt, y_sample, st("gla_p"), st("hgrn_p"), st("conv_p"), st("mk_p"), st("mv_p"),
            st("gla_s"), st("hgrn_s"), st("conv_s"))
```

```python
import functools
import math

import jax
import jax.numpy as jnp
from jax import lax
from jax.experimental import pallas as pl
from jax.experimental.pallas import tpu as pltpu

F32 = jnp.float32
BF16 = jnp.bfloat16

LANE = 128
BF16_ROWS = 16
V7X_VMEM_LIMIT = 56 * 1024 * 1024

EPS = 1e-6
GLA_HEADS = 4
GLA_GATE_NORM = 16.0
HGRN_DK = 128
MEM_HEADS = 4
CONV_W = 3
REF_CHUNK = 16
PROMPT_BLOCK = 64
SAMPLE_SEQS = 2


def _cparams(n_axes):
    return pltpu.CompilerParams(dimension_semantics=("arbitrary",) * n_axes,
                                vmem_limit_bytes=V7X_VMEM_LIMIT)


def _sigmoid(x):
    return 0.5 * jnp.tanh(0.5 * x) + 0.5


def _pick_tile(n, candidates):
    for c in candidates:
        if n % c == 0:
            return c
    return n


def _side_cast_specs(arrays, n_steps, step_index):
    specs, shapes, chunks = [], [], []
    for a in arrays:
        rows = a.shape[0]
        c = max(c for c in range(1, n_steps + 1) if rows % c == 0 and (rows // c) % BF16_ROWS == 0)
        specs.append(pl.BlockSpec((rows // c, a.shape[1]),
                                  lambda *ids, c=c: (jnp.minimum(step_index(*ids), c - 1), 0)))
        shapes.append(jax.ShapeDtypeStruct(a.shape, BF16))
        chunks.append(c)
    return specs, shapes, tuple(chunks)


def _side_cast_run(step, src_refs, dst_refs, chunks):
    for src_ref, dst_ref, n_chunks in zip(src_refs, dst_refs, chunks):
        @pl.when(step < n_chunks)
        def _(src_ref=src_ref, dst_ref=dst_ref):
            dst_ref[...] = src_ref[...].astype(dst_ref.dtype)


def _dense_body(*refs, prologue, has_res, has_extra, has_final, row_chunk, w_transposed, cast_chunks, emit_w):
    it = iter(refs)
    x_ref = next(it)
    g_ref = next(it) if prologue == "norm" else None
    w_ref = next(it)
    we_ref = next(it) if has_extra else None
    r_ref = next(it) if has_res else None
    fg_ref = next(it) if has_final else None
    cast_in = [next(it) for _ in cast_chunks]
    o_ref = next(it)
    oe_ref = next(it) if has_extra else None
    cast_out = [next(it) for _ in cast_chunks]
    wb_ref = next(it) if emit_w else None
    web_ref = next(it) if emit_w and has_extra else None
    h_ref = next(it) if prologue != "plain" else None
    j = pl.program_id(1)
    nt = (((1,), (1,)), ((), ()))

    _side_cast_run(pl.program_id(0) * pl.num_programs(1) + j, cast_in, cast_out, cast_chunks)

    if prologue != "plain":
        @pl.when(j == 0)
        def _():
            def body(c, carry):
                rows = pl.ds(pl.multiple_of(c * row_chunk, row_chunk), row_chunk)
                xf = x_ref[rows, :].astype(F32)
                if prologue == "norm":
                    ms = jnp.mean(xf * xf, axis=-1, keepdims=True)
                    xf = xf * lax.rsqrt(ms + EPS) * g_ref[...]
                h_ref[rows, :] = xf.astype(BF16)
                return carry
            lax.fori_loop(0, x_ref.shape[0] // row_chunk, body, 0)
            if has_extra:
                we = we_ref[...].astype(BF16)
                oe_ref[...] = lax.dot_general(h_ref[...], we, nt, preferred_element_type=F32)
                if emit_w:
                    web_ref[...] = we
        lhs = h_ref[...]
    else:
        lhs = x_ref[...]
    w = w_ref[...].astype(BF16)
    if emit_w:
        wb_ref[...] = w
    if w_transposed:
        acc = lax.dot_general(lhs, w, nt, preferred_element_type=F32)
    else:
        acc = jnp.dot(lhs, w, preferred_element_type=F32)
    if has_res:
        acc = acc + r_ref[...]
    if not has_final:
        o_ref[...] = acc.astype(o_ref.dtype)
    else:
        tn = acc.shape[1]
        o_ref[:, pl.ds(pl.multiple_of(j * tn, tn), tn)] = acc

        @pl.when(j == pl.num_programs(1) - 1)
        def _():
            def body(c, carry):
                rows = pl.ds(pl.multiple_of(c * row_chunk, row_chunk), row_chunk)
                xf = o_ref[rows, :]
                ms = jnp.mean(xf * xf, axis=-1, keepdims=True)
                o_ref[rows, :] = xf * lax.rsqrt(ms + EPS) * fg_ref[...]
                return carry
            lax.fori_loop(0, o_ref.shape[0] // row_chunk, body, 0)


def _dense(x, w, *, gain=None, res=None, prologue, out_dtype, name, layer=0, col_off=0, n_cols=None,
           w_transposed=False, extra_cols=None, final_gain=None, tm=None, tn=None, skip_cols=None,
           side_casts=(), emit_w_bf16=False, extra_w=None):
    m = x.shape[0]
    k = x.shape[1] if w_transposed else w.shape[-2]
    assert k == x.shape[1] or (k % LANE == 0 and k < x.shape[1] and prologue == "plain")
    n = n_cols if n_cols is not None else w.shape[-1]
    if tm is None:
        tm = _pick_tile(m, (1024, 512, 256, 128, 64, 32, 16))
    if tn is None:
        tn = _pick_tile(n, (1024, 512, 256, 128) if k <= 2048 else (512, 256, 128))
    assert m % tm == 0 and n % tn == 0
    row_chunk = min(tm, 128)
    in_specs = [pl.BlockSpec((tm, k), lambda i, j: (i, 0))]
    args = [x]
    if prologue == "norm":
        in_specs.append(pl.BlockSpec((1, k), lambda i, j: (0, 0)))
        args.append(gain)
    if w_transposed:
        assert col_off % BF16_ROWS == 0 and tn % BF16_ROWS == 0
        s_at, s_w = skip_cols if skip_cols is not None else (n, 0)
        assert s_at % tn == 0 and s_w % BF16_ROWS == 0
        w_spec = pl.BlockSpec(
            (pl.Element(tn), pl.Element(k)),
            lambda i, j: (pl.multiple_of(col_off + j * tn + jnp.where(j * tn >= s_at, s_w, 0), BF16_ROWS), 0))
        in_specs.append(w_spec)
    elif w.ndim == 3:
        assert col_off % tn == 0
        in_specs.append(pl.BlockSpec((None, k, tn), lambda i, j: (layer, 0, j + col_off // tn)))
    else:
        in_specs.append(pl.BlockSpec((k, tn), lambda i, j: (0, j)))
    args.append(w)
    out_specs = [pl.BlockSpec((tm, tn), lambda i, j: (i, j))]
    out_shape = [jax.ShapeDtypeStruct((m, n), out_dtype)]
    if extra_cols is not None:
        e_off, e_n = extra_cols
        assert w_transposed and prologue != "plain" and e_off % BF16_ROWS == 0 and e_n % BF16_ROWS == 0
        if extra_w is None:
            in_specs.append(pl.BlockSpec((pl.Element(e_n), pl.Element(k)), lambda i, j: (e_off, 0)))
            args.append(w)
        else:
            in_specs.append(pl.BlockSpec((e_n, k), lambda i, j: (0, 0)))
            args.append(extra_w)
        out_specs.append(pl.BlockSpec((tm, e_n), lambda i, j: (i, 0)))
        out_shape.append(jax.ShapeDtypeStruct((m, e_n), F32))
    if res is not None:
        in_specs.append(pl.BlockSpec((tm, tn), lambda i, j: (i, j)))
        args.append(res)
    if final_gain is not None:
        assert out_dtype == F32
        in_specs.append(pl.BlockSpec((1, n), lambda i, j: (0, 0)))
        args.append(final_gain)
        out_specs[0] = pl.BlockSpec((tm, n), lambda i, j: (i, 0))
    n_j = n // tn
    c_specs, c_shapes, cast_chunks = _side_cast_specs(side_casts, (m // tm) * n_j, lambda i, j: i * n_j + j)
    in_specs += c_specs
    args += list(side_casts)
    out_specs += c_specs
    out_shape += c_shapes
    if emit_w_bf16:
        assert w_transposed and m == tm
        out_specs.append(w_spec)
        out_shape.append(jax.ShapeDtypeStruct(w.shape, BF16))
        if extra_cols is not None:
            out_specs.append(pl.BlockSpec((extra_cols[1], k), lambda i, j: (0, 0)))
            out_shape.append(jax.ShapeDtypeStruct((extra_cols[1], k), BF16))
    scratch = [] if prologue == "plain" else [pltpu.VMEM((tm, k), BF16)]
    outs = pl.pallas_call(
        functools.partial(_dense_body, prologue=prologue, has_res=res is not None,
                          has_extra=extra_cols is not None, has_final=final_gain is not None,
                          row_chunk=row_chunk, w_transposed=w_transposed, cast_chunks=cast_chunks,
                          emit_w=emit_w_bf16),
        grid=(m // tm, n_j),
        in_specs=in_specs,
        out_specs=out_specs,
        out_shape=out_shape,
        scratch_shapes=scratch,
        compiler_params=_cparams(2),
        name=name,
    )(*args)
    return outs if (extra_cols is not None or side_casts or emit_w_bf16) else outs[0]


def _cumsum_rows(x, group):
    rows = lax.broadcasted_iota(jnp.int32, x.shape, 0) % group
    shift = 1
    while shift < group:
        rolled = pltpu.roll(x, shift, axis=0)
        x = x + jnp.where(rows >= shift, rolled, 0.0)
        shift *= 2
    return x


def _cumsum_rows_mxu(x, group):
    n = x.shape[0]
    row = lax.broadcasted_iota(jnp.int32, (n, n), 0)
    col = lax.broadcasted_iota(jnp.int32, (n, n), 1)
    tri = jnp.where(jnp.logical_and(col <= row, col // group == row // group), 1.0, 0.0).astype(BF16)
    hi = x.astype(BF16)
    lo = (x - hi.astype(F32)).astype(BF16)
    return (jnp.dot(tri, hi, preferred_element_type=F32) + jnp.dot(tri, lo, preferred_element_type=F32))


def _lockstep(gens):
    gens = list(gens)
    results = [None] * len(gens)
    live = list(range(len(gens)))
    while live:
        still = []
        for n in live:
            try:
                next(gens[n])
                still.append(n)
            except StopIteration as stop:
                results[n] = stop.value
        live = still
    return results


def _recurrence_block(q, k, v, la, s_read, *, sub, mm_dtype):
    bt, kd = q.shape
    vd = v.shape[1]
    ns = bt // sub
    b_loc = _cumsum_rows_mxu(la, sub) if mm_dtype == BF16 else _cumsum_rows(la, sub)
    qd = q * jnp.exp(b_loc)
    ki = k * jnp.exp(-b_loc)
    tot = [b_loc[(i + 1) * sub - 1:(i + 1) * sub, :] for i in range(ns)]
    pre = [jnp.zeros((1, kd), F32)]
    for i in range(ns):
        pre.append(pre[-1] + tot[i])
    sl = [slice(i * sub, (i + 1) * sub) for i in range(ns)]
    vb = v.astype(mm_dtype)
    qdm = qd.astype(mm_dtype)
    q_state = jnp.concatenate([qd[sl[i]] * jnp.exp(pre[i]) for i in range(ns)], axis=0) if ns > 1 else qd
    q_state = q_state.astype(mm_dtype)
    ke = [ki[sl[i]] * jnp.exp(tot[i]) for i in range(ns)]
    kmats = []
    for i in range(ns):
        parts = [ke[j] if j == i - 1 else ke[j] * jnp.exp(pre[i] - pre[j + 1]) for j in range(i)]
        parts.append(ki[sl[i]])
        parts += [jnp.zeros((sub, kd), F32)] * (ns - 1 - i)
        kmats.append((jnp.concatenate(parts, axis=0) if ns > 1 else parts[0]).astype(mm_dtype))
    k_end = jnp.concatenate([ke[i] if i == ns - 1 else ke[i] * jnp.exp(pre[ns] - pre[i + 1])
                             for i in range(ns)], axis=0) if ns > 1 else ke[0]
    k_end = k_end.astype(mm_dtype)
    decay = jnp.broadcast_to(jnp.exp(pre[ns]), (LANE, kd)).T
    yield

    s_prev = s_read()
    o_state = jnp.dot(q_state, s_prev.astype(mm_dtype), preferred_element_type=F32)
    a_rows = [lax.dot_general(qdm[sl[i]], kmats[i], (((1,), (1,)), ((), ())), preferred_element_type=F32)
              for i in range(ns)]
    ds = lax.dot_general(k_end, vb, (((0,), (0,)), ((), ())), preferred_element_type=F32)
    yield

    a = jnp.concatenate(a_rows, axis=0) if ns > 1 else a_rows[0]
    row = lax.broadcasted_iota(jnp.int32, a.shape, 0)
    col = lax.broadcasted_iota(jnp.int32, a.shape, 1)
    a = jnp.where(col <= row, a, 0.0).astype(mm_dtype)
    o = jnp.dot(a, vb, preferred_element_type=F32) + o_state
    s_new = jnp.concatenate([decay * s_prev[:, c * LANE:(c + 1) * LANE] for c in range(vd // LANE)],
                            axis=1) + ds
    yield
    return o, s_new


def _head_gated_norm(o, gnorm, gate):
    ms = jnp.mean(o * o, axis=-1, keepdims=True)
    return o * lax.rsqrt(ms + EPS) * gnorm * (gate * _sigmoid(gate))


def _log_sigmoid(x):
    return jnp.minimum(x, 0.0) - jnp.log(1.0 + jnp.exp(-jnp.abs(x)))


N_MIXER_IN = 16


def _mixer_body(*refs, steps_per_seq, prompt_kw, sample_kw, cast_chunks):
    n_p, n_s, n_c = N_MIXER_IN, N_MIXER_IN + 2, len(cast_chunks)
    p_in, s_in, c_in = refs[:n_p], refs[n_p:n_p + n_s], refs[n_p + n_s:n_p + n_s + n_c]
    outs = refs[n_p + n_s + n_c:]
    (p_om, p_sg, p_sh), (s_om, s_sg, s_sh), c_out = outs[:3], outs[3:6], outs[6:]
    s_sg_in, s_sh_in = s_in[N_MIXER_IN:]
    step = pl.program_id(0)
    _side_cast_run(step, c_in, c_out, cast_chunks)

    @pl.when(step % steps_per_seq == 0)
    def _():
        p_sg[...] = jnp.zeros(p_sg.shape, F32)
        p_sh[...] = jnp.zeros(p_sh.shape, F32)

    p_heads = _mixer_heads(p_in, **prompt_kw)
    s_heads = _mixer_heads(s_in, **sample_kw)
    p_block, s_block = prompt_kw["block"], sample_kw["block"]
    n_p_blocks = p_in[0].shape[0] // p_block
    n_s_seqs = s_in[0].shape[0] // s_block
    gh, hk = prompt_kw["gla_heads"], prompt_kw["hgrn_dk"]
    s_mixes = []
    for c in range(max(n_p_blocks, n_s_seqs)):
        gens, n_pg = [], 0
        if c < n_p_blocks:
            rows = slice(c * p_block, (c + 1) * p_block)
            load = lambda ref, cols, rows=rows: ref[rows, cols].astype(F32)
            gens += p_heads(load, lambda h: p_sg[0, h], lambda h: p_sh[0, h])
            n_pg = len(gens)
        if c < n_s_seqs:
            load = lambda ref, cols, c=c: ref[:, cols].astype(F32)[c * s_block:(c + 1) * s_block]
            gens += s_heads(load, lambda h, c=c: s_sg_in[c, h], lambda h, c=c: s_sh_in[c, h])
        res = _lockstep(gens)
        if c < n_p_blocks:
            for h, s_new in enumerate(res[:gh]):
                p_sg[0, h] = s_new
            for h, (mix, s_new) in enumerate(res[gh:n_pg]):
                p_sh[0, h] = s_new
                p_om[rows, h * hk:(h + 1) * hk] = mix.astype(p_om.dtype)
        if c < n_s_seqs:
            s_res = res[n_pg:]
            for h, s_new in enumerate(s_res[:gh]):
                s_sg[c, h] = s_new
            for h, (mix, s_new) in enumerate(s_res[gh:]):
                s_sh[c, h] = s_new
            s_mixes.append([mix for mix, _ in s_res[gh:]])
    for h in range(len(s_mixes[0])):
        s_om[:, h * hk:(h + 1) * hk] = jnp.concatenate([m[h] for m in s_mixes], axis=0).astype(s_om.dtype)


def _mixer_heads(ins, *, layer, gla_heads, gla_dk, gla_dv, hgrn_heads, hgrn_dk, block, sub, mm_dtype):
    (q_ref, k_ref, v_ref, g_ref, ga_ref, wup_ref, ba_ref, gng_ref,
     hq_ref, hf_ref, hi_ref, hg_ref, ma_ref, mb_ref, lb_ref, gnh_ref) = ins[:N_MIXER_IN]

    p = lb_ref[...]
    e = jnp.exp(p - jnp.max(p, axis=0, keepdims=True))
    lb_all = jnp.sum(e[:layer + 1], axis=0, keepdims=True) / jnp.sum(e, axis=0, keepdims=True)

    def gla_head(h, load, s_read, shared):
        ks = slice(h * gla_dk, (h + 1) * gla_dk)
        vs = slice(h * gla_dv, (h + 1) * gla_dv)
        ga = load(ga_ref, slice(None)).astype(BF16)
        a_logit = jnp.dot(ga, wup_ref[:, ks], preferred_element_type=F32) + ba_ref[:, ks]
        yield
        la = _log_sigmoid(a_logit) * (1.0 / GLA_GATE_NORM)
        q = load(q_ref, ks) * (gla_dk ** -0.5)
        o, s_new = yield from _recurrence_block(q, load(k_ref, ks), load(v_ref, vs), la, s_read,
                                                sub=sub, mm_dtype=mm_dtype)
        shared[h] = _head_gated_norm(o, gng_ref[:, vs], load(g_ref, vs))
        return s_new

    def hgrn_head(h, load, s_read, shared):
        ks = slice(h * hgrn_dk, (h + 1) * hgrn_dk)
        lb = lb_all[:, ks]
        f = lb + (1.0 - lb) * _sigmoid(load(hf_ref, ks))
        hq = load(hq_ref, ks)
        q = hq * _sigmoid(hq) * (hgrn_dk ** -0.5)
        yield
        o, s_new = yield from _recurrence_block(q, 1.0 - f, load(hi_ref, ks), jnp.log(f), s_read,
                                                sub=sub, mm_dtype=mm_dtype)
        o_h = _head_gated_norm(o, gnh_ref[:, ks], load(hg_ref, ks))
        per = gla_dv // hgrn_dk
        o_gla = shared[h // per][:, (h % per) * hgrn_dk:(h % per + 1) * hgrn_dk]
        mix = _sigmoid(load(ma_ref, ks)) * o_gla + _sigmoid(load(mb_ref, ks)) * o_h
        return mix, s_new

    def make(load, sg_read, sh_read):
        shared = {}
        gens = [gla_head(h, load, functools.partial(sg_read, h), shared) for h in range(gla_heads)]
        gens += [hgrn_head(h, load, functools.partial(sh_read, h), shared) for h in range(hgrn_heads)]
        return gens

    return make


def _mixer(z_p, zga_p, z_s, zga_s, p, s_gla, s_hgrn, *, layer, n_prompt, prompt_len, n_sample, sample_len,
           d_model, name, side_casts=()):
    gh, hk = GLA_HEADS, HGRN_DK
    gk, gv = d_model // 2 // gh, d_model // gh
    hh = d_model // hk
    kw = gh * gk
    n_steps = n_sample // SAMPLE_SEQS
    p_step = n_prompt * prompt_len // n_steps
    steps_per_seq = prompt_len // p_step
    assert prompt_len % p_step == 0 and p_step % PROMPT_BLOCK == 0
    s_step = SAMPLE_SEQS * sample_len
    whole = lambda a: pl.BlockSpec(a.shape, lambda i: (0,) * a.ndim)
    params = [p["gla_w_a_up"], p["gla_b_a"], p["gla_norm_g"], p["hgrn_lower_bound"], p["hgrn_norm_g"]]

    def group(z, zga, rows):
        zs = lambda width, col: pl.BlockSpec((rows, width), lambda i: (i, col))
        specs = [zs(kw, 0), zs(kw, 1), zs(d_model, 1), zs(d_model, 2), zs(zga.shape[1], 0)]
        specs += [whole(a) for a in params[:3]] + [zs(d_model, 3 + c) for c in range(6)]
        specs += [whole(a) for a in params[3:]]
        return specs, [z, z, z, z, zga] + params[:3] + [z] * 6 + params[3:], zs(d_model, 0)

    p_specs, p_args, p_om = group(z_p, zga_p, p_step)
    s_specs, s_args, s_om = group(z_s, zga_s, s_step)
    s_specs += [pl.BlockSpec((None, SAMPLE_SEQS, gh, gk, gv), lambda i: (layer, i, 0, 0, 0)),
                pl.BlockSpec((None, SAMPLE_SEQS, hh, hk, hk), lambda i: (layer, i, 0, 0, 0))]
    s_args += [s_gla, s_hgrn]
    p_st = lambda *dims: pl.BlockSpec((1,) + dims, lambda i: (i // steps_per_seq, 0, 0, 0))
    s_st = lambda *dims: pl.BlockSpec((SAMPLE_SEQS,) + dims, lambda i: (i, 0, 0, 0))
    common = dict(layer=layer, gla_heads=gh, gla_dk=gk, gla_dv=gv, hgrn_heads=hh, hgrn_dk=hk)
    prompt_kw = dict(common, block=PROMPT_BLOCK, sub=REF_CHUNK, mm_dtype=BF16)
    sample_kw = dict(common, block=sample_len, sub=math.gcd(REF_CHUNK, sample_len), mm_dtype=F32)
    shapes = lambda t, n: [jax.ShapeDtypeStruct((t, d_model), BF16), jax.ShapeDtypeStruct((n, gh, gk, gv), F32),
                           jax.ShapeDtypeStruct((n, hh, hk, hk), F32)]
    c_specs, c_shapes, cast_chunks = _side_cast_specs(side_casts, n_steps, lambda i: i)
    outs = pl.pallas_call(
        functools.partial(_mixer_body, steps_per_seq=steps_per_seq, prompt_kw=prompt_kw, sample_kw=sample_kw,
                          cast_chunks=cast_chunks),
        grid=(n_steps,),
        in_specs=p_specs + s_specs + c_specs,
        out_specs=[p_om, p_st(gh, gk, gv), p_st(hh, hk, hk), s_om, s_st(gh, gk, gv), s_st(hh, hk, hk)] + c_specs,
        out_shape=shapes(z_p.shape[0], n_prompt) + shapes(z_s.shape[0], n_sample) + c_shapes,
        compiler_params=_cparams(1),
        name=name,
    )(*p_args, *s_args, *side_casts)
    return outs[:3], outs[3:6], outs[6:]


def _xattn_sample_body(q_ref, k_ref, v_ref, o_ref, *, heads, dh, seq_len):
    scale = dh ** -0.5
    n_mem = k_ref.shape[1]
    for s in range(k_ref.shape[0]):
        rows = slice(s * seq_len, (s + 1) * seq_len)
        q = q_ref[rows, :]
        q2 = jnp.concatenate([q[:, h * dh:(h + 1) * dh] for h in range(heads)], axis=0)
        k2 = k_ref[s].reshape(n_mem * heads, dh)
        v2 = v_ref[s].reshape(n_mem * heads, dh)
        sc = lax.dot_general(q2, k2, (((1,), (1,)), ((), ())), preferred_element_type=F32) * scale
        q_head = lax.broadcasted_iota(jnp.int32, sc.shape, 0) // seq_len
        k_head = lax.broadcasted_iota(jnp.int32, sc.shape, 1) % heads
        sc = jnp.where(q_head == k_head, sc, -jnp.inf)
        p = jnp.exp(sc - jnp.max(sc, axis=-1, keepdims=True))
        o2 = jnp.dot(p, v2, preferred_element_type=F32) / jnp.sum(p, axis=-1, keepdims=True)
        for h in range(heads):
            o_ref[rows, h * dh:(h + 1) * dh] = o2[h * seq_len:(h + 1) * seq_len].astype(o_ref.dtype)


def _mid_body(om_ref, x_ref, k_ref, v_ref, wo_ref, wq_ref, wm_ref, g_ref, o_ref, *, heads, dh):
    scale = dh ** -0.5
    x1 = x_ref[...] + jnp.dot(om_ref[...], wo_ref[...], preferred_element_type=F32)
    ms = jnp.mean(x1 * x1, axis=-1, keepdims=True)
    hx = (x1 * lax.rsqrt(ms + EPS) * g_ref[...]).astype(BF16)
    mq = jnp.dot(hx, wq_ref[...], preferred_element_type=F32).astype(BF16)
    heads_out = []
    for h in range(heads):
        hs = slice(h * dh, (h + 1) * dh)
        sc = lax.dot_general(mq[:, hs], k_ref[0, :, hs], (((1,), (1,)), ((), ())),
                             preferred_element_type=F32) * scale
        p = jnp.exp(sc - jnp.max(sc, axis=-1, keepdims=True))
        o = jnp.dot(p.astype(BF16), v_ref[0, :, hs], preferred_element_type=F32) / jnp.sum(p, axis=-1, keepdims=True)
        heads_out.append(o.astype(BF16))
    mo = jnp.concatenate(heads_out, axis=1)
    o_ref[...] = x1 + jnp.dot(mo, wm_ref[...], preferred_element_type=F32)


def _mid_prompt(o_mix, x, mem_k, mem_v, w_out, w_q, w_o, gain, *, n_seq, seq_len, name):
    t, d = x.shape
    n_mem = mem_k.shape[1]
    tm = 256
    n_t = seq_len // tm
    rows = lambda dtype_rows: pl.BlockSpec((tm, d), lambda b, i: (b * n_t + i, 0))
    kv = pl.BlockSpec((1, n_mem, d), lambda b, i: (b, 0, 0))
    wspec = pl.BlockSpec((d, d), lambda b, i: (0, 0), pipeline_mode=pl.Buffered(1))
    return pl.pallas_call(
        functools.partial(_mid_body, heads=MEM_HEADS, dh=d // MEM_HEADS),
        grid=(n_seq, n_t),
        in_specs=[rows(BF16), rows(F32), kv, kv, wspec, wspec, wspec, pl.BlockSpec((1, d), lambda b, i: (0, 0))],
        out_specs=rows(F32),
        out_shape=jax.ShapeDtypeStruct((t, d), F32),
        compiler_params=_cparams(2),
        name=name,
    )(o_mix, x, mem_k, mem_v, w_out, w_q, w_o, gain)


def _pre_xattn_body(om_ref, x_ref, wo_ref, wq_ref, g_ref, x1_ref, mq_ref):
    x1 = x_ref[...] + jnp.dot(om_ref[...], wo_ref[...], preferred_element_type=F32)
    ms = jnp.mean(x1 * x1, axis=-1, keepdims=True)
    hx = (x1 * lax.rsqrt(ms + EPS) * g_ref[...]).astype(BF16)
    x1_ref[...] = x1
    mq_ref[...] = jnp.dot(hx, wq_ref[...], preferred_element_type=F32)


def _pre_xattn_sample(o_mix, x, w_out, w_q, gain, *, name):
    t, d = x.shape
    tm = _pick_tile(t, (256, 128, 64, 32, 16))
    rows = pl.BlockSpec((tm, d), lambda i: (i, 0))
    wspec = pl.BlockSpec((d, d), lambda i: (0, 0), pipeline_mode=pl.Buffered(1))
    return pl.pallas_call(
        _pre_xattn_body,
        grid=(t // tm,),
        in_specs=[rows, rows, wspec, wspec, pl.BlockSpec((1, d), lambda i: (0, 0))],
        out_specs=[rows, rows],
        out_shape=[jax.ShapeDtypeStruct((t, d), F32), jax.ShapeDtypeStruct((t, d), F32)],
        compiler_params=_cparams(1),
        name=name,
    )(o_mix, x, w_out, w_q, gain)


def _ffn_up_body(*refs, mode, seq_len, tiles_per_seq, row_chunk, last_shift, side):
    if mode == "prompt" and side is not None:
        (x_ref, g_ref, wg_ref, wv_ref, cw_ref, cb_ref, sq_ref, sk_ref, sv_ref,
         act_ref, st_ref, so_ref, h_ref, tail_ref) = refs
    elif mode == "prompt":
        x_ref, g_ref, wg_ref, wv_ref, cw_ref, cb_ref, act_ref, st_ref, h_ref, tail_ref = refs
    else:
        x_ref, g_ref, wg_ref, wv_ref, cw_ref, cb_ref, buf_ref, act_ref, st_ref, h_ref = refs
        tail_ref = None
    i = pl.program_id(0)
    j = pl.program_id(1)

    if side is not None:
        @pl.when(i * pl.num_programs(1) + j < side["n_steps"])
        def _():
            _xattn_sample_body(sq_ref, sk_ref, sv_ref, so_ref, heads=MEM_HEADS, dh=side["dh"],
                               seq_len=side["seq_len"])

    @pl.when(j == 0)
    def _():
        def body(c, carry):
            rows = pl.ds(pl.multiple_of(c * row_chunk, row_chunk), row_chunk)
            xf = x_ref[rows, :]
            ms = jnp.mean(xf * xf, axis=-1, keepdims=True)
            h_ref[rows, :] = (xf * lax.rsqrt(ms + EPS) * g_ref[...]).astype(BF16)
            return carry
        lax.fori_loop(0, x_ref.shape[0] // row_chunk, body, 0)

    if mode == "prompt":
        @pl.when(i % tiles_per_seq == 0)
        def _():
            tail_ref[j] = jnp.zeros(tail_ref.shape[1:], F32)

    def step(shift):
        def place(a):
            if shift == 0:
                return a
            return jnp.concatenate([a[..., shift:], jnp.zeros(a.shape[:-1] + (shift,), a.dtype)], axis=-1)

        h = h_ref[...]
        ug = jnp.dot(h, wg_ref[...], preferred_element_type=F32)
        tm = ug.shape[0]
        row = lax.broadcasted_iota(jnp.int32, ug.shape, 0)
        roll1 = pltpu.roll(ug, 1, axis=0)
        roll2 = pltpu.roll(ug, 2, axis=0)
        if mode == "prompt":
            tail = tail_ref[j]
            prev1 = tail[7:8, :]
            prev2 = tail[6:7, :]
            sh1 = jnp.where(row >= 1, roll1, prev1)
            sh2 = jnp.where(row >= 2, roll2, jnp.where(row == 0, prev2, prev1))
            tail_ref[j] = ug[tm - 8:, :]
            st_ref[...] = place(ug[tm - 8:, :])
        else:
            buf = buf_ref[...]
            n_seq = buf.shape[0]
            spread = lambda r: jnp.broadcast_to(buf[:, r:r + 1, :],
                                                (n_seq, seq_len, buf.shape[2])).reshape(ug.shape)
            prev2, prev1 = spread(0), spread(1)
            pos = row % seq_len
            sh1 = jnp.where(pos >= 1, roll1, prev1)
            sh2 = jnp.where(pos >= 2, roll2, jnp.where(pos == 0, prev2, prev1))
            st_ref[...] = place(ug.reshape(n_seq, seq_len, ug.shape[1])[:, seq_len - (CONV_W - 1):, :])
        conv = cw_ref[0:1, :] * sh2 + cw_ref[1:2, :] * sh1 + cw_ref[2:3, :] * ug + cb_ref[...]
        gate = conv * _sigmoid(conv)
        uv = jnp.dot(h, wv_ref[...], preferred_element_type=F32)
        act_ref[...] = place((gate * uv).astype(act_ref.dtype))

    if last_shift == 0:
        step(0)
    else:
        last = pl.num_programs(1) - 1
        pl.when(j < last)(functools.partial(step, 0))
        pl.when(j == last)(functools.partial(step, last_shift))


def _ffn_up(x, gain, w_up, cw, cb, carry, *, mode, seq_len, name, side=None):
    m, k = x.shape
    d_ff = w_up.shape[1] // 2
    tm = _pick_tile(m, (1024,))
    tn = 4 * LANE
    nf = -(-d_ff // tn) * tn
    last_shift = nf - d_ff
    assert d_ff % LANE == 0 and d_ff >= tn
    col = lambda j, base=0: pl.multiple_of(base + jnp.minimum(j * tn, d_ff - tn), LANE)
    elem = lambda *dims: tuple(pl.Element(n) for n in dims)
    xs = pl.BlockSpec((tm, k), lambda i, j: (i, 0))
    gs = pl.BlockSpec((1, k), lambda i, j: (0, 0))
    wgs = pl.BlockSpec(elem(k, tn), lambda i, j: (0, col(j)))
    wvs = pl.BlockSpec(elem(k, tn), lambda i, j: (0, col(j, d_ff)))
    cws = pl.BlockSpec(elem(CONV_W, tn), lambda i, j: (0, col(j)))
    cbs = pl.BlockSpec(elem(1, tn), lambda i, j: (0, col(j)))
    ts = pl.BlockSpec((tm, tn), lambda i, j: (i, j))
    in_specs = [xs, gs, wgs, wvs, cws, cbs]
    args = [x, gain, w_up, w_up, cw, cb]
    scratch = [pltpu.VMEM((tm, k), BF16)]
    if mode == "prompt":
        assert seq_len % tm == 0
        scratch.append(pltpu.VMEM((nf // tn, 8, tn), F32))
        st_spec = pl.BlockSpec((None, 8, tn), lambda i, j: (i, 0, j))
        st_shape = jax.ShapeDtypeStruct((m // tm, 8, nf), F32)
    else:
        assert tm % seq_len == 0 and seq_len == 8
        n_blk = tm // seq_len
        in_specs.append(pl.BlockSpec(elem(n_blk, CONV_W - 1, tn), lambda i, j: (i * n_blk, 0, col(j))))
        args.append(carry)
        st_spec = pl.BlockSpec((n_blk, CONV_W - 1, tn), lambda i, j: (i, 0, j))
        st_shape = jax.ShapeDtypeStruct((m // seq_len, CONV_W - 1, nf), F32)
    n_j = nf // tn
    out_specs = [ts, st_spec]
    out_shape = [jax.ShapeDtypeStruct((m, nf), BF16), st_shape]
    side_kw = None
    if side is not None:
        assert mode == "prompt"
        mq, cache_k, cache_v, layer, s_len = side
        n_side = mq.shape[0] // (SAMPLE_SEQS * s_len)
        assert n_side <= (m // tm) * n_j
        grp = lambda i, j: jnp.minimum(i * n_j + j, n_side - 1)
        q_spec = pl.BlockSpec((SAMPLE_SEQS * s_len, mq.shape[1]), lambda i, j: (grp(i, j), 0))
        kv_spec = pl.BlockSpec((None, SAMPLE_SEQS) + cache_k.shape[2:], lambda i, j: (layer, grp(i, j), 0, 0, 0))
        in_specs += [q_spec, kv_spec, kv_spec]
        args += [mq, cache_k, cache_v]
        out_specs.append(q_spec)
        out_shape.append(jax.ShapeDtypeStruct(mq.shape, F32))
        side_kw = dict(n_steps=n_side, dh=cache_k.shape[-1], seq_len=s_len)
    return pl.pallas_call(
        functools.partial(_ffn_up_body, mode=mode, seq_len=seq_len, tiles_per_seq=max(seq_len // tm, 1),
                          row_chunk=128, last_shift=last_shift, side=side_kw),
        grid=(m // tm, n_j),
        in_specs=in_specs,
        out_specs=out_specs,
        out_shape=out_shape,
        scratch_shapes=scratch,
        compiler_params=_cparams(2),
        name=name,
    )(*args)


def _in_proj(x, w_in_t, p, name, *, tn, side_casts=(), emit_w_bf16=False, w_ga_t=None):
    ga_off, rank = p["ga_off"], p["rank"]
    return _dense(x, w_in_t, gain=p["norm_mix_g"], prologue="norm", w_transposed=True, tn=tn,
                  out_dtype=BF16, name=name, n_cols=w_in_t.shape[0] - rank,
                  skip_cols=(ga_off, rank), extra_cols=(ga_off, rank), side_casts=side_casts,
                  emit_w_bf16=emit_w_bf16, extra_w=w_ga_t)


def _conv_ffn(x2, carry, p, *, mode, n_seq, seq_len, tag, side=None):
    d_ff = p["d_ff"]
    act, ug_rows, *mo_side = _ffn_up(x2, p["norm_ffn_g"], p["w_up"], p["ffn_conv_w"], p["ffn_conv_b"], carry,
                                     mode=mode, seq_len=seq_len, name=f"ffn_up_{tag}", side=side)
    final = p["norm_final_g"] if p["is_last"] else None
    y = _dense(act, p["w_down"], res=x2, prologue="plain", out_dtype=F32, name=f"ffn_down_{tag}",
               final_gain=final, tm=512 if final is not None else None, tn=1024 if final is not None else None)
    if mode == "prompt":
        tiles_per_seq = ug_rows.shape[0] // n_seq
        conv_new = ug_rows[tiles_per_seq - 1::tiles_per_seq, 8 - (CONV_W - 1):, :d_ff]
    else:
        conv_new = ug_rows[:, :, :d_ff]
    return y, conv_new, (mo_side[0] if mo_side else None)


def kernel(x_prompt, x_sample, mem_prompt, state_gla, state_hgrn, state_ffn_conv, cache_mem_k, cache_mem_v,
           norm_mix_g, w_in, gla_w_a_up, gla_b_a, gla_norm_g, hgrn_lower_bound, hgrn_norm_g, w_out,
           norm_xattn_g, norm_mem_g, w_mem_q, w_mem_k, w_mem_v, w_mem_o,
           norm_ffn_g, w_ffn_up, ffn_conv_w, ffn_conv_b, w_ffn_down, norm_final_g):
    depth = w_in.shape[0]
    bp, lp, d = x_prompt.shape
    bs, ls, _ = x_sample.shape
    n_mem = mem_prompt.shape[1]
    d_ff = w_ffn_down.shape[1]
    rank = gla_w_a_up.shape[1]
    qk = gla_w_a_up.shape[2]
    ga_off = 2 * qk + 2 * d

    xp = x_prompt.reshape(bp * lp, d)
    xs = x_sample.reshape(bs * ls, d)
    mem = mem_prompt.reshape(bp * n_mem, d)
    row = lambda v: v.reshape(1, -1)

    outs = {k: [] for k in ("gla_p", "hgrn_p", "conv_p", "mk_p", "mv_p", "gla_s", "hgrn_s", "conv_s")}
    for l in range(depth):
        p = {
            "ga_off": ga_off, "rank": rank,
            "gla_w_a_up": gla_w_a_up[l].astype(BF16),
            "gla_b_a": row(gla_b_a[l]), "gla_norm_g": row(gla_norm_g[l]),
            "hgrn_lower_bound": hgrn_lower_bound, "hgrn_norm_g": row(hgrn_norm_g[l]),
            "norm_mix_g": row(norm_mix_g[l]), "norm_xattn_g": row(norm_xattn_g[l]),
            "norm_ffn_g": row(norm_ffn_g[l]), "norm_final_g": row(norm_final_g), "is_last": l == depth - 1,
            "ffn_conv_w": ffn_conv_w[l], "ffn_conv_b": row(ffn_conv_b[l]),
            "d_ff": d_ff,
        }
        g_mem = row(norm_mem_g[l])
        mk = _dense(mem, w_mem_k, gain=g_mem, prologue="norm", out_dtype=F32, name=f"mem_k_{l}", layer=l)
        mv = _dense(mem, w_mem_v, gain=g_mem, prologue="norm", out_dtype=F32, name=f"mem_v_{l}", layer=l)
        mk3, mv3 = mk.reshape(bp, n_mem, d), mv.reshape(bp, n_mem, d)
        outs["mk_p"].append(mk3.reshape(bp, n_mem, MEM_HEADS, d // MEM_HEADS))
        outs["mv_p"].append(mv3.reshape(bp, n_mem, MEM_HEADS, d // MEM_HEADS))
        zs, zga_s, w_in_t, w_ga_t = _in_proj(xs, jnp.swapaxes(w_in[l], 0, 1), p, f"in_proj_s{l}", tn=d // 2,
                                             emit_w_bf16=True)
        zp, zga_p, p["w_out"], p["w_mem_q"], p["w_mem_o"] = _in_proj(
            xp, w_in_t, p, f"in_proj_p{l}", tn=d, w_ga_t=w_ga_t, side_casts=(w_out[l], w_mem_q[l], w_mem_o[l]))
        (om_p, sg, sh), (om_s, sg2, sh2), (p["w_up"], p["w_down"]) = _mixer(
            zp, zga_p, zs, zga_s, p, state_gla, state_hgrn, layer=l, n_prompt=bp, prompt_len=lp,
            n_sample=bs, sample_len=ls, d_model=d, name=f"mixer_{l}", side_casts=(w_ffn_up[l], w_ffn_down[l]))
        outs["gla_p"].append(sg); outs["hgrn_p"].append(sh)
        outs["gla_s"].append(sg2); outs["hgrn_s"].append(sh2)
        x1_s, mq_s = _pre_xattn_sample(om_s, xs, p["w_out"], p["w_mem_q"], p["norm_xattn_g"],
                                       name=f"pre_xattn_s{l}")
        x2_p = _mid_prompt(om_p, xp, mk3.astype(BF16), mv3.astype(BF16), p["w_out"], p["w_mem_q"], p["w_mem_o"],
                           p["norm_xattn_g"], n_seq=bp, seq_len=lp, name=f"mid_p{l}")
        xp, cb, mo_s = _conv_ffn(x2_p, None, p, mode="prompt", n_seq=bp, seq_len=lp, tag=f"p{l}",
                                 side=(mq_s, cache_mem_k, cache_mem_v, l, ls))
        x2_s = _dense(mo_s, p["w_mem_o"], res=x1_s, prologue="cast", out_dtype=F32, name=f"mem_o_s{l}")
        xs, cb2, _ = _conv_ffn(x2_s, state_ffn_conv[l], p, mode="sample", n_seq=bs, seq_len=ls, tag=f"s{l}")
        outs["conv_p"].append(cb); outs["conv_s"].append(cb2)

    y_prompt = xp.reshape(bp, lp, d)
    y_sample = xs.reshape(bs, ls, d)
    st = lambda k: jnp.stack(outs[k])
    return (y_prompt, y_sample, st("gla_p"), st("hgrn_p"), st("conv_p"), st("mk_p"), st("mv_p"),
            st("gla_s"), st("hgrn_s"), st("conv_s"))
```

```python
import functools
import math

import jax
import jax.numpy as jnp
from jax import lax
from jax.experimental import pallas as pl
from jax.experimental.pallas import tpu as pltpu

F32 = jnp.float32
BF16 = jnp.bfloat16

LANE = 128
BF16_ROWS = 16
V7X_VMEM_LIMIT = 56 * 1024 * 1024

EPS = 1e-6
GLA_HEADS = 4
GLA_GATE_NORM = 16.0
HGRN_DK = 128
MEM_HEADS = 4
CONV_W = 3
REF_CHUNK = 16
PROMPT_BLOCK = 64
SAMPLE_SEQS = 2


def _cparams(n_axes):
    return pltpu.CompilerParams(dimension_semantics=("arbitrary",) * n_axes,
                                vmem_limit_bytes=V7X_VMEM_LIMIT)


def _sigmoid(x):
    return 0.5 * jnp.tanh(0.5 * x) + 0.5


def _pick_tile(n, candidates):
    for c in candidates:
        if n % c == 0:
            return c
    return n


def _side_cast_specs(arrays, n_steps, step_index):
    specs, shapes, chunks = [], [], []
    for a in arrays:
        rows = a.shape[0]
        c = max(c for c in range(1, n_steps + 1) if rows % c == 0 and (rows // c) % BF16_ROWS == 0)
        specs.append(pl.BlockSpec((rows // c, a.shape[1]),
                                  lambda *ids, c=c: (jnp.minimum(step_index(*ids), c - 1), 0)))
        shapes.append(jax.ShapeDtypeStruct(a.shape, BF16))
        chunks.append(c)
    return specs, shapes, tuple(chunks)


def _side_cast_run(step, src_refs, dst_refs, chunks):
    for src_ref, dst_ref, n_chunks in zip(src_refs, dst_refs, chunks):
        @pl.when(step < n_chunks)
        def _(src_ref=src_ref, dst_ref=dst_ref):
            dst_ref[...] = src_ref[...].astype(dst_ref.dtype)


def _dense_body(*refs, prologue, has_res, has_extra, has_final, row_chunk, w_transposed, cast_chunks, emit_w,
                also_bf16):
    it = iter(refs)
    x_ref = next(it)
    g_ref = next(it) if prologue == "norm" else None
    w_ref = next(it)
    we_ref = next(it) if has_extra else None
    r_ref = next(it) if has_res else None
    fg_ref = next(it) if has_final else None
    cast_in = [next(it) for _ in cast_chunks]
    o_ref = next(it)
    ob_ref = next(it) if also_bf16 else None
    oe_ref = next(it) if has_extra else None
    cast_out = [next(it) for _ in cast_chunks]
    wb_ref = next(it) if emit_w else None
    web_ref = next(it) if emit_w and has_extra else None
    h_ref = next(it) if prologue != "plain" else None
    j = pl.program_id(1)
    nt = (((1,), (1,)), ((), ()))

    _side_cast_run(pl.program_id(0) * pl.num_programs(1) + j, cast_in, cast_out, cast_chunks)

    if prologue != "plain":
        @pl.when(j == 0)
        def _():
            def body(c, carry):
                rows = pl.ds(pl.multiple_of(c * row_chunk, row_chunk), row_chunk)
                xf = x_ref[rows, :].astype(F32)
                if prologue == "norm":
                    ms = jnp.mean(xf * xf, axis=-1, keepdims=True)
                    xf = xf * lax.rsqrt(ms + EPS) * g_ref[...]
                h_ref[rows, :] = xf.astype(BF16)
                return carry
            lax.fori_loop(0, x_ref.shape[0] // row_chunk, body, 0)
            if has_extra:
                we = we_ref[...].astype(BF16)
                oe_ref[...] = lax.dot_general(h_ref[...], we, nt, preferred_element_type=F32)
                if emit_w:
                    web_ref[...] = we
        lhs = h_ref[...]
    else:
        lhs = x_ref[...]
    w = w_ref[...].astype(BF16)
    if emit_w:
        wb_ref[...] = w
    if w_transposed:
        acc = lax.dot_general(lhs, w, nt, preferred_element_type=F32)
    else:
        acc = jnp.dot(lhs, w, preferred_element_type=F32)
    if has_res:
        acc = acc + r_ref[...]
    if not has_final:
        o_ref[...] = acc.astype(o_ref.dtype)
        if also_bf16:
            ob_ref[...] = acc.astype(BF16)
    else:
        tn = acc.shape[1]
        o_ref[:, pl.ds(pl.multiple_of(j * tn, tn), tn)] = acc

        @pl.when(j == pl.num_programs(1) - 1)
        def _():
            def body(c, carry):
                rows = pl.ds(pl.multiple_of(c * row_chunk, row_chunk), row_chunk)
                xf = o_ref[rows, :]
                ms = jnp.mean(xf * xf, axis=-1, keepdims=True)
                o_ref[rows, :] = xf * lax.rsqrt(ms + EPS) * fg_ref[...]
                return carry
            lax.fori_loop(0, o_ref.shape[0] // row_chunk, body, 0)


def _dense(x, w, *, gain=None, res=None, prologue, out_dtype, name, layer=0, col_off=0, n_cols=None,
           w_transposed=False, extra_cols=None, final_gain=None, tm=None, tn=None, skip_cols=None,
           side_casts=(), emit_w_bf16=False, extra_w=None, also_bf16=False):
    m = x.shape[0]
    k = x.shape[1] if w_transposed else w.shape[-2]
    assert k == x.shape[1] or (k % LANE == 0 and k < x.shape[1] and prologue == "plain")
    n = n_cols if n_cols is not None else w.shape[-1]
    if tm is None:
        tm = _pick_tile(m, (1024, 512, 256, 128, 64, 32, 16))
    if tn is None:
        tn = _pick_tile(n, (1024, 512, 256, 128) if k <= 2048 else (512, 256, 128))
    assert m % tm == 0 and n % tn == 0
    row_chunk = min(tm, 128)
    in_specs = [pl.BlockSpec((tm, k), lambda i, j: (i, 0))]
    args = [x]
    if prologue == "norm":
        in_specs.append(pl.BlockSpec((1, k), lambda i, j: (0, 0)))
        args.append(gain)
    if w_transposed:
        assert col_off % BF16_ROWS == 0 and tn % BF16_ROWS == 0
        s_at, s_w = skip_cols if skip_cols is not None else (n, 0)
        assert s_at % tn == 0 and s_w % BF16_ROWS == 0
        w_spec = pl.BlockSpec(
            (pl.Element(tn), pl.Element(k)),
            lambda i, j: (pl.multiple_of(col_off + j * tn + jnp.where(j * tn >= s_at, s_w, 0), BF16_ROWS), 0))
        in_specs.append(w_spec)
    elif w.ndim == 3:
        assert col_off % tn == 0
        in_specs.append(pl.BlockSpec((None, k, tn), lambda i, j: (layer, 0, j + col_off // tn)))
    else:
        in_specs.append(pl.BlockSpec((k, tn), lambda i, j: (0, j)))
    args.append(w)
    out_specs = [pl.BlockSpec((tm, tn), lambda i, j: (i, j))]
    out_shape = [jax.ShapeDtypeStruct((m, n), out_dtype)]
    if also_bf16:
        assert final_gain is None and out_dtype == F32
        out_specs.append(out_specs[0])
        out_shape.append(jax.ShapeDtypeStruct((m, n), BF16))
    if extra_cols is not None:
        e_off, e_n = extra_cols
        assert w_transposed and prologue != "plain" and e_off % BF16_ROWS == 0 and e_n % BF16_ROWS == 0
        if extra_w is None:
            in_specs.append(pl.BlockSpec((pl.Element(e_n), pl.Element(k)), lambda i, j: (e_off, 0)))
            args.append(w)
        else:
            in_specs.append(pl.BlockSpec((e_n, k), lambda i, j: (0, 0)))
            args.append(extra_w)
        out_specs.append(pl.BlockSpec((tm, e_n), lambda i, j: (i, 0)))
        out_shape.append(jax.ShapeDtypeStruct((m, e_n), F32))
    if res is not None:
        in_specs.append(pl.BlockSpec((tm, tn), lambda i, j: (i, j)))
        args.append(res)
    if final_gain is not None:
        assert out_dtype == F32
        in_specs.append(pl.BlockSpec((1, n), lambda i, j: (0, 0)))
        args.append(final_gain)
        out_specs[0] = pl.BlockSpec((tm, n), lambda i, j: (i, 0))
    n_j = n // tn
    c_specs, c_shapes, cast_chunks = _side_cast_specs(side_casts, (m // tm) * n_j, lambda i, j: i * n_j + j)
    in_specs += c_specs
    args += list(side_casts)
    out_specs += c_specs
    out_shape += c_shapes
    if emit_w_bf16:
        assert w_transposed and m == tm
        out_specs.append(w_spec)
        out_shape.append(jax.ShapeDtypeStruct(w.shape, BF16))
        if extra_cols is not None:
            out_specs.append(pl.BlockSpec((extra_cols[1], k), lambda i, j: (0, 0)))
            out_shape.append(jax.ShapeDtypeStruct((extra_cols[1], k), BF16))
    scratch = [] if prologue == "plain" else [pltpu.VMEM((tm, k), BF16)]
    outs = pl.pallas_call(
        functools.partial(_dense_body, prologue=prologue, has_res=res is not None,
                          has_extra=extra_cols is not None, has_final=final_gain is not None,
                          row_chunk=row_chunk, w_transposed=w_transposed, cast_chunks=cast_chunks,
                          emit_w=emit_w_bf16, also_bf16=also_bf16),
        grid=(m // tm, n_j),
        in_specs=in_specs,
        out_specs=out_specs,
        out_shape=out_shape,
        scratch_shapes=scratch,
        compiler_params=_cparams(2),
        name=name,
    )(*args)
    return outs if (extra_cols is not None or side_casts or emit_w_bf16 or also_bf16) else outs[0]


def _cumsum_rows(x, group):
    rows = lax.broadcasted_iota(jnp.int32, x.shape, 0) % group
    shift = 1
    while shift < group:
        rolled = pltpu.roll(x, shift, axis=0)
        x = x + jnp.where(rows >= shift, rolled, 0.0)
        shift *= 2
    return x


def _cumsum_rows_mxu(x, group):
    n = x.shape[0]
    row = lax.broadcasted_iota(jnp.int32, (n, n), 0)
    col = lax.broadcasted_iota(jnp.int32, (n, n), 1)
    tri = jnp.where(jnp.logical_and(col <= row, col // group == row // group), 1.0, 0.0).astype(BF16)
    hi = x.astype(BF16)
    lo = (x - hi.astype(F32)).astype(BF16)
    return (jnp.dot(tri, hi, preferred_element_type=F32) + jnp.dot(tri, lo, preferred_element_type=F32))


def _lockstep(gens):
    gens = list(gens)
    results = [None] * len(gens)
    live = list(range(len(gens)))
    while live:
        still = []
        for n in live:
            try:
                next(gens[n])
                still.append(n)
            except StopIteration as stop:
                results[n] = stop.value
        live = still
    return results


def _recurrence_block(q, k, v, la, s_read, *, sub, mm_dtype):
    bt, kd = q.shape
    vd = v.shape[1]
    ns = bt // sub
    b_loc = _cumsum_rows_mxu(la, sub) if mm_dtype == BF16 else _cumsum_rows(la, sub)
    qd = q * jnp.exp(b_loc)
    ki = k * jnp.exp(-b_loc)
    tot = [b_loc[(i + 1) * sub - 1:(i + 1) * sub, :] for i in range(ns)]
    pre = [jnp.zeros((1, kd), F32)]
    for i in range(ns):
        pre.append(pre[-1] + tot[i])
    sl = [slice(i * sub, (i + 1) * sub) for i in range(ns)]
    vb = v.astype(mm_dtype)
    qdm = qd.astype(mm_dtype)
    q_state = jnp.concatenate([qd[sl[i]] * jnp.exp(pre[i]) for i in range(ns)], axis=0) if ns > 1 else qd
    q_state = q_state.astype(mm_dtype)
    ke = [ki[sl[i]] * jnp.exp(tot[i]) for i in range(ns)]
    kmats = []
    for i in range(ns):
        parts = [ke[j] if j == i - 1 else ke[j] * jnp.exp(pre[i] - pre[j + 1]) for j in range(i)]
        parts.append(ki[sl[i]])
        parts += [jnp.zeros((sub, kd), F32)] * (ns - 1 - i)
        kmats.append((jnp.concatenate(parts, axis=0) if ns > 1 else parts[0]).astype(mm_dtype))
    k_end = jnp.concatenate([ke[i] if i == ns - 1 else ke[i] * jnp.exp(pre[ns] - pre[i + 1])
                             for i in range(ns)], axis=0) if ns > 1 else ke[0]
    k_end = k_end.astype(mm_dtype)
    decay = jnp.broadcast_to(jnp.exp(pre[ns]), (LANE, kd)).T
    yield

    s_prev = s_read()
    o_state = jnp.dot(q_state, s_prev.astype(mm_dtype), preferred_element_type=F32)
    a_rows = [lax.dot_general(qdm[sl[i]], kmats[i], (((1,), (1,)), ((), ())), preferred_element_type=F32)
              for i in range(ns)]
    ds = lax.dot_general(k_end, vb, (((0,), (0,)), ((), ())), preferred_element_type=F32)
    yield

    a = jnp.concatenate(a_rows, axis=0) if ns > 1 else a_rows[0]
    row = lax.broadcasted_iota(jnp.int32, a.shape, 0)
    col = lax.broadcasted_iota(jnp.int32, a.shape, 1)
    a = jnp.where(col <= row, a, 0.0).astype(mm_dtype)
    o = jnp.dot(a, vb, preferred_element_type=F32) + o_state
    s_new = jnp.concatenate([decay * s_prev[:, c * LANE:(c + 1) * LANE] for c in range(vd // LANE)],
                            axis=1) + ds
    yield
    return o, s_new


def _head_gated_norm(o, gnorm, gate):
    ms = jnp.mean(o * o, axis=-1, keepdims=True)
    return o * lax.rsqrt(ms + EPS) * gnorm * (gate * _sigmoid(gate))


def _log_sigmoid(x):
    return jnp.minimum(x, 0.0) - jnp.log(1.0 + jnp.exp(-jnp.abs(x)))


N_MIXER_IN = 16


def _mixer_body(*refs, steps_per_seq, prompt_kw, sample_kw, cast_chunks):
    n_p, n_s, n_c = N_MIXER_IN, N_MIXER_IN + 2, len(cast_chunks)
    p_in, s_in, c_in = refs[:n_p], refs[n_p:n_p + n_s], refs[n_p + n_s:n_p + n_s + n_c]
    outs = refs[n_p + n_s + n_c:]
    (p_om, p_sg, p_sh), (s_om, s_sg, s_sh), c_out = outs[:3], outs[3:6], outs[6:]
    s_sg_in, s_sh_in = s_in[N_MIXER_IN:]
    step = pl.program_id(0)
    _side_cast_run(step, c_in, c_out, cast_chunks)

    @pl.when(step % steps_per_seq == 0)
    def _():
        p_sg[...] = jnp.zeros(p_sg.shape, F32)
        p_sh[...] = jnp.zeros(p_sh.shape, F32)

    p_heads = _mixer_heads(p_in, **prompt_kw)
    s_heads = _mixer_heads(s_in, **sample_kw)
    p_block, s_block = prompt_kw["block"], sample_kw["block"]
    n_p_blocks = p_in[0].shape[0] // p_block
    n_s_seqs = s_in[0].shape[0] // s_block
    gh, hk = prompt_kw["gla_heads"], prompt_kw["hgrn_dk"]
    s_mixes = []
    for c in range(max(n_p_blocks, n_s_seqs)):
        gens, n_pg = [], 0
        if c < n_p_blocks:
            rows = slice(c * p_block, (c + 1) * p_block)
            load = lambda ref, cols, rows=rows: ref[rows, cols].astype(F32)
            gens += p_heads(load, lambda h: p_sg[0, h], lambda h: p_sh[0, h])
            n_pg = len(gens)
        if c < n_s_seqs:
            load = lambda ref, cols, c=c: ref[:, cols].astype(F32)[c * s_block:(c + 1) * s_block]
            gens += s_heads(load, lambda h, c=c: s_sg_in[c, h], lambda h, c=c: s_sh_in[c, h])
        res = _lockstep(gens)
        if c < n_p_blocks:
            for h, s_new in enumerate(res[:gh]):
                p_sg[0, h] = s_new
            for h, (mix, s_new) in enumerate(res[gh:n_pg]):
                p_sh[0, h] = s_new
                p_om[rows, h * hk:(h + 1) * hk] = mix.astype(p_om.dtype)
        if c < n_s_seqs:
            s_res = res[n_pg:]
            for h, s_new in enumerate(s_res[:gh]):
                s_sg[c, h] = s_new
            for h, (mix, s_new) in enumerate(s_res[gh:]):
                s_sh[c, h] = s_new
            s_mixes.append([mix for mix, _ in s_res[gh:]])
    for h in range(len(s_mixes[0])):
        s_om[:, h * hk:(h + 1) * hk] = jnp.concatenate([m[h] for m in s_mixes], axis=0).astype(s_om.dtype)


def _mixer_heads(ins, *, layer, gla_heads, gla_dk, gla_dv, hgrn_heads, hgrn_dk, block, sub, mm_dtype):
    (q_ref, k_ref, v_ref, g_ref, ga_ref, wup_ref, ba_ref, gng_ref,
     hq_ref, hf_ref, hi_ref, hg_ref, ma_ref, mb_ref, lb_ref, gnh_ref) = ins[:N_MIXER_IN]

    p = lb_ref[...]
    e = jnp.exp(p - jnp.max(p, axis=0, keepdims=True))
    lb_all = jnp.sum(e[:layer + 1], axis=0, keepdims=True) / jnp.sum(e, axis=0, keepdims=True)

    def gla_head(h, load, s_read, shared):
        ks = slice(h * gla_dk, (h + 1) * gla_dk)
        vs = slice(h * gla_dv, (h + 1) * gla_dv)
        ga = load(ga_ref, slice(None)).astype(BF16)
        a_logit = jnp.dot(ga, wup_ref[:, ks], preferred_element_type=F32) + ba_ref[:, ks]
        yield
        la = _log_sigmoid(a_logit) * (1.0 / GLA_GATE_NORM)
        q = load(q_ref, ks) * (gla_dk ** -0.5)
        o, s_new = yield from _recurrence_block(q, load(k_ref, ks), load(v_ref, vs), la, s_read,
                                                sub=sub, mm_dtype=mm_dtype)
        shared[h] = _head_gated_norm(o, gng_ref[:, vs], load(g_ref, vs))
        return s_new

    def hgrn_head(h, load, s_read, shared):
        ks = slice(h * hgrn_dk, (h + 1) * hgrn_dk)
        lb = lb_all[:, ks]
        f = lb + (1.0 - lb) * _sigmoid(load(hf_ref, ks))
        hq = load(hq_ref, ks)
        q = hq * _sigmoid(hq) * (hgrn_dk ** -0.5)
        yield
        o, s_new = yield from _recurrence_block(q, 1.0 - f, load(hi_ref, ks), jnp.log(f), s_read,
                                                sub=sub, mm_dtype=mm_dtype)
        o_h = _head_gated_norm(o, gnh_ref[:, ks], load(hg_ref, ks))
        per = gla_dv // hgrn_dk
        o_gla = shared[h // per][:, (h % per) * hgrn_dk:(h % per + 1) * hgrn_dk]
        mix = _sigmoid(load(ma_ref, ks)) * o_gla + _sigmoid(load(mb_ref, ks)) * o_h
        return mix, s_new

    def make(load, sg_read, sh_read):
        shared = {}
        gens = [gla_head(h, load, functools.partial(sg_read, h), shared) for h in range(gla_heads)]
        gens += [hgrn_head(h, load, functools.partial(sh_read, h), shared) for h in range(hgrn_heads)]
        return gens

    return make


def _mixer(z_p, zga_p, z_s, zga_s, p, s_gla, s_hgrn, *, layer, n_prompt, prompt_len, n_sample, sample_len,
           d_model, name, side_casts=()):
    gh, hk = GLA_HEADS, HGRN_DK
    gk, gv = d_model // 2 // gh, d_model // gh
    hh = d_model // hk
    kw = gh * gk
    n_steps = n_sample // SAMPLE_SEQS
    p_step = n_prompt * prompt_len // n_steps
    steps_per_seq = prompt_len // p_step
    assert prompt_len % p_step == 0 and p_step % PROMPT_BLOCK == 0
    s_step = SAMPLE_SEQS * sample_len
    whole = lambda a: pl.BlockSpec(a.shape, lambda i: (0,) * a.ndim)
    params = [p["gla_w_a_up"], p["gla_b_a"], p["gla_norm_g"], p["hgrn_lower_bound"], p["hgrn_norm_g"]]

    def group(z, zga, rows):
        zs = lambda width, col: pl.BlockSpec((rows, width), lambda i: (i, col))
        specs = [zs(kw, 0), zs(kw, 1), zs(d_model, 1), zs(d_model, 2), zs(zga.shape[1], 0)]
        specs += [whole(a) for a in params[:3]] + [zs(d_model, 3 + c) for c in range(6)]
        specs += [whole(a) for a in params[3:]]
        return specs, [z, z, z, z, zga] + params[:3] + [z] * 6 + params[3:], zs(d_model, 0)

    p_specs, p_args, p_om = group(z_p, zga_p, p_step)
    s_specs, s_args, s_om = group(z_s, zga_s, s_step)
    s_specs += [pl.BlockSpec((None, SAMPLE_SEQS, gh, gk, gv), lambda i: (layer, i, 0, 0, 0)),
                pl.BlockSpec((None, SAMPLE_SEQS, hh, hk, hk), lambda i: (layer, i, 0, 0, 0))]
    s_args += [s_gla, s_hgrn]
    p_st = lambda *dims: pl.BlockSpec((1,) + dims, lambda i: (i // steps_per_seq, 0, 0, 0))
    s_st = lambda *dims: pl.BlockSpec((SAMPLE_SEQS,) + dims, lambda i: (i, 0, 0, 0))
    common = dict(layer=layer, gla_heads=gh, gla_dk=gk, gla_dv=gv, hgrn_heads=hh, hgrn_dk=hk)
    prompt_kw = dict(common, block=PROMPT_BLOCK, sub=REF_CHUNK, mm_dtype=BF16)
    sample_kw = dict(common, block=sample_len, sub=math.gcd(REF_CHUNK, sample_len), mm_dtype=F32)
    shapes = lambda t, n: [jax.ShapeDtypeStruct((t, d_model), BF16), jax.ShapeDtypeStruct((n, gh, gk, gv), F32),
                           jax.ShapeDtypeStruct((n, hh, hk, hk), F32)]
    c_specs, c_shapes, cast_chunks = _side_cast_specs(side_casts, n_steps, lambda i: i)
    outs = pl.pallas_call(
        functools.partial(_mixer_body, steps_per_seq=steps_per_seq, prompt_kw=prompt_kw, sample_kw=sample_kw,
                          cast_chunks=cast_chunks),
        grid=(n_steps,),
        in_specs=p_specs + s_specs + c_specs,
        out_specs=[p_om, p_st(gh, gk, gv), p_st(hh, hk, hk), s_om, s_st(gh, gk, gv), s_st(hh, hk, hk)] + c_specs,
        out_shape=shapes(z_p.shape[0], n_prompt) + shapes(z_s.shape[0], n_sample) + c_shapes,
        compiler_params=_cparams(1),
        name=name,
    )(*p_args, *s_args, *side_casts)
    return outs[:3], outs[3:6], outs[6:]


def _xattn_sample_body(q_ref, k_ref, v_ref, o_ref, *, heads, dh, seq_len):
    scale = dh ** -0.5
    n_mem = k_ref.shape[1]
    for s in range(k_ref.shape[0]):
        rows = slice(s * seq_len, (s + 1) * seq_len)
        q = q_ref[rows, :]
        q2 = jnp.concatenate([q[:, h * dh:(h + 1) * dh] for h in range(heads)], axis=0)
        k2 = k_ref[s].reshape(n_mem * heads, dh)
        v2 = v_ref[s].reshape(n_mem * heads, dh)
        sc = lax.dot_general(q2, k2, (((1,), (1,)), ((), ())), preferred_element_type=F32) * scale
        q_head = lax.broadcasted_iota(jnp.int32, sc.shape, 0) // seq_len
        k_head = lax.broadcasted_iota(jnp.int32, sc.shape, 1) % heads
        sc = jnp.where(q_head == k_head, sc, -jnp.inf)
        p = jnp.exp(sc - jnp.max(sc, axis=-1, keepdims=True))
        o2 = jnp.dot(p, v2, preferred_element_type=F32) / jnp.sum(p, axis=-1, keepdims=True)
        for h in range(heads):
            o_ref[rows, h * dh:(h + 1) * dh] = o2[h * seq_len:(h + 1) * seq_len].astype(o_ref.dtype)


def _mid_body(om_ref, x_ref, k_ref, v_ref, wo_ref, wq_ref, wm_ref, g_ref, o_ref, *, heads, dh):
    scale = dh ** -0.5
    x1 = x_ref[...] + jnp.dot(om_ref[...], wo_ref[...], preferred_element_type=F32)
    ms = jnp.mean(x1 * x1, axis=-1, keepdims=True)
    hx = (x1 * lax.rsqrt(ms + EPS) * g_ref[...]).astype(BF16)
    mq = jnp.dot(hx, wq_ref[...], preferred_element_type=F32).astype(BF16)
    heads_out = []
    for h in range(heads):
        hs = slice(h * dh, (h + 1) * dh)
        sc = lax.dot_general(mq[:, hs], k_ref[0, :, hs], (((1,), (1,)), ((), ())),
                             preferred_element_type=F32) * scale
        p = jnp.exp(sc - jnp.max(sc, axis=-1, keepdims=True))
        o = jnp.dot(p.astype(BF16), v_ref[0, :, hs], preferred_element_type=F32) / jnp.sum(p, axis=-1, keepdims=True)
        heads_out.append(o.astype(BF16))
    mo = jnp.concatenate(heads_out, axis=1)
    o_ref[...] = x1 + jnp.dot(mo, wm_ref[...], preferred_element_type=F32)


def _mid_prompt(o_mix, x, mem_k, mem_v, w_out, w_q, w_o, gain, *, n_seq, seq_len, name):
    t, d = x.shape
    n_mem = mem_k.shape[1]
    tm = 256
    n_t = seq_len // tm
    rows = lambda dtype_rows: pl.BlockSpec((tm, d), lambda b, i: (b * n_t + i, 0))
    kv = pl.BlockSpec((1, n_mem, d), lambda b, i: (b, 0, 0))
    wspec = pl.BlockSpec((d, d), lambda b, i: (0, 0), pipeline_mode=pl.Buffered(1))
    return pl.pallas_call(
        functools.partial(_mid_body, heads=MEM_HEADS, dh=d // MEM_HEADS),
        grid=(n_seq, n_t),
        in_specs=[rows(BF16), rows(F32), kv, kv, wspec, wspec, wspec, pl.BlockSpec((1, d), lambda b, i: (0, 0))],
        out_specs=rows(F32),
        out_shape=jax.ShapeDtypeStruct((t, d), F32),
        compiler_params=_cparams(2),
        name=name,
    )(o_mix, x, mem_k, mem_v, w_out, w_q, w_o, gain)


def _pre_xattn_body(om_ref, x_ref, wo_ref, wq_ref, g_ref, x1_ref, mq_ref):
    x1 = x_ref[...] + jnp.dot(om_ref[...], wo_ref[...], preferred_element_type=F32)
    ms = jnp.mean(x1 * x1, axis=-1, keepdims=True)
    hx = (x1 * lax.rsqrt(ms + EPS) * g_ref[...]).astype(BF16)
    x1_ref[...] = x1
    mq_ref[...] = jnp.dot(hx, wq_ref[...], preferred_element_type=F32)


def _pre_xattn_sample(o_mix, x, w_out, w_q, gain, *, name):
    t, d = x.shape
    tm = _pick_tile(t, (256, 128, 64, 32, 16))
    rows = pl.BlockSpec((tm, d), lambda i: (i, 0))
    wspec = pl.BlockSpec((d, d), lambda i: (0, 0), pipeline_mode=pl.Buffered(1))
    return pl.pallas_call(
        _pre_xattn_body,
        grid=(t // tm,),
        in_specs=[rows, rows, wspec, wspec, pl.BlockSpec((1, d), lambda i: (0, 0))],
        out_specs=[rows, rows],
        out_shape=[jax.ShapeDtypeStruct((t, d), F32), jax.ShapeDtypeStruct((t, d), F32)],
        compiler_params=_cparams(1),
        name=name,
    )(o_mix, x, w_out, w_q, gain)


def _ffn_up_body(*refs, mode, seq_len, tiles_per_seq, row_chunk, last_shift, side):
    if mode == "prompt" and side is not None:
        (x_ref, g_ref, wg_ref, wv_ref, cw_ref, cb_ref, sq_ref, sk_ref, sv_ref,
         act_ref, st_ref, so_ref, h_ref, tail_ref) = refs
    elif mode == "prompt":
        x_ref, g_ref, wg_ref, wv_ref, cw_ref, cb_ref, act_ref, st_ref, h_ref, tail_ref = refs
    else:
        x_ref, g_ref, wg_ref, wv_ref, cw_ref, cb_ref, buf_ref, act_ref, st_ref, h_ref = refs
        tail_ref = None
    i = pl.program_id(0)
    j = pl.program_id(1)

    if side is not None:
        @pl.when(i * pl.num_programs(1) + j < side["n_steps"])
        def _():
            _xattn_sample_body(sq_ref, sk_ref, sv_ref, so_ref, heads=MEM_HEADS, dh=side["dh"],
                               seq_len=side["seq_len"])

    @pl.when(j == 0)
    def _():
        def body(c, carry):
            rows = pl.ds(pl.multiple_of(c * row_chunk, row_chunk), row_chunk)
            xf = x_ref[rows, :]
            ms = jnp.mean(xf * xf, axis=-1, keepdims=True)
            h_ref[rows, :] = (xf * lax.rsqrt(ms + EPS) * g_ref[...]).astype(BF16)
            return carry
        lax.fori_loop(0, x_ref.shape[0] // row_chunk, body, 0)

    if mode == "prompt":
        @pl.when(i % tiles_per_seq == 0)
        def _():
            tail_ref[j] = jnp.zeros(tail_ref.shape[1:], F32)

    def step(shift):
        def place(a):
            if shift == 0:
                return a
            return jnp.concatenate([a[..., shift:], jnp.zeros(a.shape[:-1] + (shift,), a.dtype)], axis=-1)

        h = h_ref[...]
        ug = jnp.dot(h, wg_ref[...], preferred_element_type=F32)
        tm = ug.shape[0]
        row = lax.broadcasted_iota(jnp.int32, ug.shape, 0)
        roll1 = pltpu.roll(ug, 1, axis=0)
        roll2 = pltpu.roll(ug, 2, axis=0)
        if mode == "prompt":
            tail = tail_ref[j]
            prev1 = tail[7:8, :]
            prev2 = tail[6:7, :]
            sh1 = jnp.where(row >= 1, roll1, prev1)
            sh2 = jnp.where(row >= 2, roll2, jnp.where(row == 0, prev2, prev1))
            tail_ref[j] = ug[tm - 8:, :]
            st_ref[...] = place(ug[tm - 8:, :])
        else:
            buf = buf_ref[...]
            n_seq = buf.shape[0]
            spread = lambda r: jnp.broadcast_to(buf[:, r:r + 1, :],
                                                (n_seq, seq_len, buf.shape[2])).reshape(ug.shape)
            prev2, prev1 = spread(0), spread(1)
            pos = row % seq_len
            sh1 = jnp.where(pos >= 1, roll1, prev1)
            sh2 = jnp.where(pos >= 2, roll2, jnp.where(pos == 0, prev2, prev1))
            st_ref[...] = place(ug.reshape(n_seq, seq_len, ug.shape[1])[:, seq_len - (CONV_W - 1):, :])
        conv = cw_ref[0:1, :] * sh2 + cw_ref[1:2, :] * sh1 + cw_ref[2:3, :] * ug + cb_ref[...]
        gate = conv * _sigmoid(conv)
        uv = jnp.dot(h, wv_ref[...], preferred_element_type=F32)
        act_ref[...] = place((gate * uv).astype(act_ref.dtype))

    if last_shift == 0:
        step(0)
    else:
        last = pl.num_programs(1) - 1
        pl.when(j < last)(functools.partial(step, 0))
        pl.when(j == last)(functools.partial(step, last_shift))


def _ffn_up(x, gain, w_up, cw, cb, carry, *, mode, seq_len, name, side=None):
    m, k = x.shape
    d_ff = w_up.shape[1] // 2
    tm = _pick_tile(m, (1024,))
    tn = 4 * LANE
    nf = -(-d_ff // tn) * tn
    last_shift = nf - d_ff
    assert d_ff % LANE == 0 and d_ff >= tn
    col = lambda j, base=0: pl.multiple_of(base + jnp.minimum(j * tn, d_ff - tn), LANE)
    elem = lambda *dims: tuple(pl.Element(n) for n in dims)
    xs = pl.BlockSpec((tm, k), lambda i, j: (i, 0))
    gs = pl.BlockSpec((1, k), lambda i, j: (0, 0))
    wgs = pl.BlockSpec(elem(k, tn), lambda i, j: (0, col(j)))
    wvs = pl.BlockSpec(elem(k, tn), lambda i, j: (0, col(j, d_ff)))
    cws = pl.BlockSpec(elem(CONV_W, tn), lambda i, j: (0, col(j)))
    cbs = pl.BlockSpec(elem(1, tn), lambda i, j: (0, col(j)))
    ts = pl.BlockSpec((tm, tn), lambda i, j: (i, j))
    in_specs = [xs, gs, wgs, wvs, cws, cbs]
    args = [x, gain, w_up, w_up, cw, cb]
    scratch = [pltpu.VMEM((tm, k), BF16)]
    if mode == "prompt":
        assert seq_len % tm == 0
        scratch.append(pltpu.VMEM((nf // tn, 8, tn), F32))
        st_spec = pl.BlockSpec((None, 8, tn), lambda i, j: (i, 0, j))
        st_shape = jax.ShapeDtypeStruct((m // tm, 8, nf), F32)
    else:
        assert tm % seq_len == 0 and seq_len == 8
        n_blk = tm // seq_len
        in_specs.append(pl.BlockSpec(elem(n_blk, CONV_W - 1, tn), lambda i, j: (i * n_blk, 0, col(j))))
        args.append(carry)
        st_spec = pl.BlockSpec((n_blk, CONV_W - 1, tn), lambda i, j: (i, 0, j))
        st_shape = jax.ShapeDtypeStruct((m // seq_len, CONV_W - 1, nf), F32)
    n_j = nf // tn
    out_specs = [ts, st_spec]
    out_shape = [jax.ShapeDtypeStruct((m, nf), BF16), st_shape]
    side_kw = None
    if side is not None:
        assert mode == "prompt"
        mq, cache_k, cache_v, layer, s_len = side
        n_side = mq.shape[0] // (SAMPLE_SEQS * s_len)
        assert n_side <= (m // tm) * n_j
        grp = lambda i, j: jnp.minimum(i * n_j + j, n_side - 1)
        q_spec = pl.BlockSpec((SAMPLE_SEQS * s_len, mq.shape[1]), lambda i, j: (grp(i, j), 0))
        kv_spec = pl.BlockSpec((None, SAMPLE_SEQS) + cache_k.shape[2:], lambda i, j: (layer, grp(i, j), 0, 0, 0))
        in_specs += [q_spec, kv_spec, kv_spec]
        args += [mq, cache_k, cache_v]
        out_specs.append(q_spec)
        out_shape.append(jax.ShapeDtypeStruct(mq.shape, F32))
        side_kw = dict(n_steps=n_side, dh=cache_k.shape[-1], seq_len=s_len)
    return pl.pallas_call(
        functools.partial(_ffn_up_body, mode=mode, seq_len=seq_len, tiles_per_seq=max(seq_len // tm, 1),
                          row_chunk=128, last_shift=last_shift, side=side_kw),
        grid=(m // tm, n_j),
        in_specs=in_specs,
        out_specs=out_specs,
        out_shape=out_shape,
        scratch_shapes=scratch,
        compiler_params=_cparams(2),
        name=name,
    )(*args)


def _in_proj(x, w_in_t, p, name, *, tn, side_casts=(), emit_w_bf16=False, w_ga_t=None):
    ga_off, rank = p["ga_off"], p["rank"]
    return _dense(x, w_in_t, gain=p["norm_mix_g"], prologue="norm", w_transposed=True, tn=tn,
                  out_dtype=BF16, name=name, n_cols=w_in_t.shape[0] - rank,
                  skip_cols=(ga_off, rank), extra_cols=(ga_off, rank), side_casts=side_casts,
                  emit_w_bf16=emit_w_bf16, extra_w=w_ga_t)


def _conv_ffn(x2, carry, p, *, mode, n_seq, seq_len, tag, side=None):
    d_ff = p["d_ff"]
    act, ug_rows, *mo_side = _ffn_up(x2, p["norm_ffn_g"], p["w_up"], p["ffn_conv_w"], p["ffn_conv_b"], carry,
                                     mode=mode, seq_len=seq_len, name=f"ffn_up_{tag}", side=side)
    final = p["norm_final_g"] if p["is_last"] else None
    y = _dense(act, p["w_down"], res=x2, prologue="plain", out_dtype=F32, name=f"ffn_down_{tag}",
               final_gain=final, tm=512 if final is not None else None, tn=1024 if final is not None else None)
    if mode == "prompt":
        tiles_per_seq = ug_rows.shape[0] // n_seq
        conv_new = ug_rows[tiles_per_seq - 1::tiles_per_seq, 8 - (CONV_W - 1):, :d_ff]
    else:
        conv_new = ug_rows[:, :, :d_ff]
    return y, conv_new, (mo_side[0] if mo_side else None)


def kernel(x_prompt, x_sample, mem_prompt, state_gla, state_hgrn, state_ffn_conv, cache_mem_k, cache_mem_v,
           norm_mix_g, w_in, gla_w_a_up, gla_b_a, gla_norm_g, hgrn_lower_bound, hgrn_norm_g, w_out,
           norm_xattn_g, norm_mem_g, w_mem_q, w_mem_k, w_mem_v, w_mem_o,
           norm_ffn_g, w_ffn_up, ffn_conv_w, ffn_conv_b, w_ffn_down, norm_final_g):
    depth = w_in.shape[0]
    bp, lp, d = x_prompt.shape
    bs, ls, _ = x_sample.shape
    n_mem = mem_prompt.shape[1]
    d_ff = w_ffn_down.shape[1]
    rank = gla_w_a_up.shape[1]
    qk = gla_w_a_up.shape[2]
    ga_off = 2 * qk + 2 * d

    xp = x_prompt.reshape(bp * lp, d)
    xs = x_sample.reshape(bs * ls, d)
    mem = mem_prompt.reshape(bp * n_mem, d)
    row = lambda v: v.reshape(1, -1)

    outs = {k: [] for k in ("gla_p", "hgrn_p", "conv_p", "mk_p", "mv_p", "gla_s", "hgrn_s", "conv_s")}
    for l in range(depth):
        p = {
            "ga_off": ga_off, "rank": rank,
            "gla_w_a_up": gla_w_a_up[l].astype(BF16),
            "gla_b_a": row(gla_b_a[l]), "gla_norm_g": row(gla_norm_g[l]),
            "hgrn_lower_bound": hgrn_lower_bound, "hgrn_norm_g": row(hgrn_norm_g[l]),
            "norm_mix_g": row(norm_mix_g[l]), "norm_xattn_g": row(norm_xattn_g[l]),
            "norm_ffn_g": row(norm_ffn_g[l]), "norm_final_g": row(norm_final_g), "is_last": l == depth - 1,
            "ffn_conv_w": ffn_conv_w[l], "ffn_conv_b": row(ffn_conv_b[l]),
            "d_ff": d_ff,
        }
        g_mem = row(norm_mem_g[l])
        mem_proj = functools.partial(_dense, mem, gain=g_mem, prologue="norm", out_dtype=F32, layer=l,
                                     also_bf16=True)
        mk, mk_b = mem_proj(w_mem_k, name=f"mem_k_{l}")
        mv, mv_b = mem_proj(w_mem_v, name=f"mem_v_{l}")
        mk3, mv3 = mk.reshape(bp, n_mem, d), mv.reshape(bp, n_mem, d)
        outs["mk_p"].append(mk3.reshape(bp, n_mem, MEM_HEADS, d // MEM_HEADS))
        outs["mv_p"].append(mv3.reshape(bp, n_mem, MEM_HEADS, d // MEM_HEADS))
        zs, zga_s, w_in_t, w_ga_t = _in_proj(xs, jnp.swapaxes(w_in[l], 0, 1), p, f"in_proj_s{l}", tn=d // 2,
                                             emit_w_bf16=True)
        zp, zga_p, p["w_out"], p["w_mem_q"], p["w_mem_o"] = _in_proj(
            xp, w_in_t, p, f"in_proj_p{l}", tn=d, w_ga_t=w_ga_t, side_casts=(w_out[l], w_mem_q[l], w_mem_o[l]))
        (om_p, sg, sh), (om_s, sg2, sh2), (p["w_up"], p["w_down"]) = _mixer(
            zp, zga_p, zs, zga_s, p, state_gla, state_hgrn, layer=l, n_prompt=bp, prompt_len=lp,
            n_sample=bs, sample_len=ls, d_model=d, name=f"mixer_{l}", side_casts=(w_ffn_up[l], w_ffn_down[l]))
        outs["gla_p"].append(sg); outs["hgrn_p"].append(sh)
        outs["gla_s"].append(sg2); outs["hgrn_s"].append(sh2)
        x1_s, mq_s = _pre_xattn_sample(om_s, xs, p["w_out"], p["w_mem_q"], p["norm_xattn_g"],
                                       name=f"pre_xattn_s{l}")
        x2_p = _mid_prompt(om_p, xp, mk_b.reshape(bp, n_mem, d), mv_b.reshape(bp, n_mem, d),
                           p["w_out"], p["w_mem_q"], p["w_mem_o"],
                           p["norm_xattn_g"], n_seq=bp, seq_len=lp, name=f"mid_p{l}")
        xp, cb, mo_s = _conv_ffn(x2_p, None, p, mode="prompt", n_seq=bp, seq_len=lp, tag=f"p{l}",
                                 side=(mq_s, cache_mem_k, cache_mem_v, l, ls))
        x2_s = _dense(mo_s, p["w_mem_o"], res=x1_s, prologue="cast", out_dtype=F32, name=f"mem_o_s{l}")
        xs, cb2, _ = _conv_ffn(x2_s, state_ffn_conv[l], p, mode="sample", n_seq=bs, seq_len=ls, tag=f"s{l}")
        outs["conv_p"].append(cb); outs["conv_s"].append(cb2)

    y_prompt = xp.reshape(bp, lp, d)
    y_sample = xs.reshape(bs, ls, d)
    st = lambda k: jnp.stack(outs[k])
    return (y_prompt, y_sample, st("gla_p"), st("hgrn_p"), st("conv_p"), st("mk_p"), st("mv_p"),
            st("gla_s"), st("hgrn_s"), st("conv_s"))
```

```python
import functools
import math

import jax
import jax.numpy as jnp
from jax import lax
from jax.experimental import pallas as pl
from jax.experimental.pallas import tpu as pltpu

F32 = jnp.float32
BF16 = jnp.bfloat16

LANE = 128
BF16_ROWS = 16
V7X_VMEM_LIMIT = 56 * 1024 * 1024

EPS = 1e-6
GLA_HEADS = 4
GLA_GATE_NORM = 16.0
HGRN_DK = 128
MEM_HEADS = 4
CONV_W = 3
REF_CHUNK = 16
PROMPT_BLOCK = 64
SAMPLE_SEQS = 2


def _cparams(n_axes):
    return pltpu.CompilerParams(dimension_semantics=("arbitrary",) * n_axes,
                                vmem_limit_bytes=V7X_VMEM_LIMIT)


def _sigmoid(x):
    return 0.5 * jnp.tanh(0.5 * x) + 0.5


def _pick_tile(n, candidates):
    for c in candidates:
        if n % c == 0:
            return c
    return n


def _side_cast_specs(arrays, n_steps, step_index):
    specs, shapes, chunks = [], [], []
    for a in arrays:
        rows = a.shape[0]
        c = max(c for c in range(1, n_steps + 1) if rows % c == 0 and (rows // c) % BF16_ROWS == 0)
        specs.append(pl.BlockSpec((rows // c, a.shape[1]),
                                  lambda *ids, c=c: (jnp.minimum(step_index(*ids), c - 1), 0)))
        shapes.append(jax.ShapeDtypeStruct(a.shape, BF16))
        chunks.append(c)
    return specs, shapes, tuple(chunks)


def _side_cast_run(step, src_refs, dst_refs, chunks):
    for src_ref, dst_ref, n_chunks in zip(src_refs, dst_refs, chunks):
        @pl.when(step < n_chunks)
        def _(src_ref=src_ref, dst_ref=dst_ref):
            dst_ref[...] = src_ref[...].astype(dst_ref.dtype)


def _dense_body(*refs, prologue, has_res, has_extra, has_final, row_chunk, w_transposed, cast_chunks, emit_w,
                also_bf16):
    it = iter(refs)
    x_ref = next(it)
    g_ref = next(it) if prologue == "norm" else None
    w_ref = next(it)
    we_ref = next(it) if has_extra else None
    r_ref = next(it) if has_res else None
    fg_ref = next(it) if has_final else None
    cast_in = [next(it) for _ in cast_chunks]
    o_ref = next(it)
    ob_ref = next(it) if also_bf16 else None
    oe_ref = next(it) if has_extra else None
    cast_out = [next(it) for _ in cast_chunks]
    wb_ref = next(it) if emit_w else None
    web_ref = next(it) if emit_w and has_extra else None
    h_ref = next(it) if prologue != "plain" else None
    j = pl.program_id(1)
    nt = (((1,), (1,)), ((), ()))

    _side_cast_run(pl.program_id(0) * pl.num_programs(1) + j, cast_in, cast_out, cast_chunks)

    if prologue != "plain":
        @pl.when(j == 0)
        def _():
            def body(c, carry):
                rows = pl.ds(pl.multiple_of(c * row_chunk, row_chunk), row_chunk)
                xf = x_ref[rows, :].astype(F32)
                if prologue == "norm":
                    ms = jnp.mean(xf * xf, axis=-1, keepdims=True)
                    xf = xf * lax.rsqrt(ms + EPS) * g_ref[...]
                h_ref[rows, :] = xf.astype(BF16)
                return carry
            lax.fori_loop(0, x_ref.shape[0] // row_chunk, body, 0)
            if has_extra:
                we = we_ref[...].astype(BF16)
                oe_ref[...] = lax.dot_general(h_ref[...], we, nt, preferred_element_type=F32)
                if emit_w:
                    web_ref[...] = we
        lhs = h_ref[...]
    else:
        lhs = x_ref[...]
    w = w_ref[...].astype(BF16)
    if emit_w:
        wb_ref[...] = w
    if w_transposed:
        acc = lax.dot_general(lhs, w, nt, preferred_element_type=F32)
    else:
        acc = jnp.dot(lhs, w, preferred_element_type=F32)
    if has_res:
        acc = acc + r_ref[...]
    if not has_final:
        o_ref[...] = acc.astype(o_ref.dtype)
        if also_bf16:
            ob_ref[...] = acc.astype(BF16)
    else:
        tn = acc.shape[1]
        o_ref[:, pl.ds(pl.multiple_of(j * tn, tn), tn)] = acc

        @pl.when(j == pl.num_programs(1) - 1)
        def _():
            def body(c, carry):
                rows = pl.ds(pl.multiple_of(c * row_chunk, row_chunk), row_chunk)
                xf = o_ref[rows, :]
                ms = jnp.mean(xf * xf, axis=-1, keepdims=True)
                o_ref[rows, :] = xf * lax.rsqrt(ms + EPS) * fg_ref[...]
                return carry
            lax.fori_loop(0, o_ref.shape[0] // row_chunk, body, 0)


def _dense(x, w, *, gain=None, res=None, prologue, out_dtype, name, layer=0, col_off=0, n_cols=None,
           w_transposed=False, extra_cols=None, final_gain=None, tm=None, tn=None, skip_cols=None,
           side_casts=(), emit_w_bf16=False, extra_w=None, also_bf16=False):
    m = x.shape[0]
    k = x.shape[1] if w_transposed else w.shape[-2]
    assert k == x.shape[1] or (k % LANE == 0 and k < x.shape[1] and prologue == "plain")
    n = n_cols if n_cols is not None else w.shape[-1]
    if tm is None:
        tm = _pick_tile(m, (1024, 512, 256, 128, 64, 32, 16))
    if tn is None:
        tn = _pick_tile(n, (1024, 512, 256, 128) if k <= 2048 else (512, 256, 128))
    assert m % tm == 0 and n % tn == 0
    row_chunk = min(tm, 128)
    in_specs = [pl.BlockSpec((tm, k), lambda i, j: (i, 0))]
    args = [x]
    if prologue == "norm":
        in_specs.append(pl.BlockSpec((1, k), lambda i, j: (0, 0)))
        args.append(gain)
    if w_transposed:
        assert col_off % BF16_ROWS == 0 and tn % BF16_ROWS == 0
        s_at, s_w = skip_cols if skip_cols is not None else (n, 0)
        assert s_at % tn == 0 and s_w % BF16_ROWS == 0
        w_spec = pl.BlockSpec(
            (pl.Element(tn), pl.Element(k)),
            lambda i, j: (pl.multiple_of(col_off + j * tn + jnp.where(j * tn >= s_at, s_w, 0), BF16_ROWS), 0))
        in_specs.append(w_spec)
    elif w.ndim == 3:
        assert col_off % tn == 0
        in_specs.append(pl.BlockSpec((None, k, tn), lambda i, j: (layer, 0, j + col_off // tn)))
    else:
        in_specs.append(pl.BlockSpec((k, tn), lambda i, j: (0, j)))
    args.append(w)
    out_specs = [pl.BlockSpec((tm, tn), lambda i, j: (i, j))]
    out_shape = [jax.ShapeDtypeStruct((m, n), out_dtype)]
    if also_bf16:
        assert final_gain is None and out_dtype == F32
        out_specs.append(out_specs[0])
        out_shape.append(jax.ShapeDtypeStruct((m, n), BF16))
    if extra_cols is not None:
        e_off, e_n = extra_cols
        assert w_transposed and prologue != "plain" and e_off % BF16_ROWS == 0 and e_n % BF16_ROWS == 0
        if extra_w is None:
            in_specs.append(pl.BlockSpec((pl.Element(e_n), pl.Element(k)), lambda i, j: (e_off, 0)))
            args.append(w)
        else:
            in_specs.append(pl.BlockSpec((e_n, k), lambda i, j: (0, 0)))
            args.append(extra_w)
        out_specs.append(pl.BlockSpec((tm, e_n), lambda i, j: (i, 0)))
        out_shape.append(jax.ShapeDtypeStruct((m, e_n), F32))
    if res is not None:
        in_specs.append(pl.BlockSpec((tm, tn), lambda i, j: (i, j)))
        args.append(res)
    if final_gain is not None:
        assert out_dtype == F32
        in_specs.append(pl.BlockSpec((1, n), lambda i, j: (0, 0)))
        args.append(final_gain)
        out_specs[0] = pl.BlockSpec((tm, n), lambda i, j: (i, 0))
    n_j = n // tn
    c_specs, c_shapes, cast_chunks = _side_cast_specs(side_casts, (m // tm) * n_j, lambda i, j: i * n_j + j)
    in_specs += c_specs
    args += list(side_casts)
    out_specs += c_specs
    out_shape += c_shapes
    if emit_w_bf16:
        assert w_transposed and m == tm
        out_specs.append(pl.BlockSpec((tn, k), lambda i, j: (j, 0)))
        out_shape.append(jax.ShapeDtypeStruct((n, k), BF16))
        if extra_cols is not None:
            out_specs.append(pl.BlockSpec((extra_cols[1], k), lambda i, j: (0, 0)))
            out_shape.append(jax.ShapeDtypeStruct((extra_cols[1], k), BF16))
    scratch = [] if prologue == "plain" else [pltpu.VMEM((tm, k), BF16)]
    outs = pl.pallas_call(
        functools.partial(_dense_body, prologue=prologue, has_res=res is not None,
                          has_extra=extra_cols is not None, has_final=final_gain is not None,
                          row_chunk=row_chunk, w_transposed=w_transposed, cast_chunks=cast_chunks,
                          emit_w=emit_w_bf16, also_bf16=also_bf16),
        grid=(m // tm, n_j),
        in_specs=in_specs,
        out_specs=out_specs,
        out_shape=out_shape,
        scratch_shapes=scratch,
        compiler_params=_cparams(2),
        name=name,
    )(*args)
    return outs if (extra_cols is not None or side_casts or emit_w_bf16 or also_bf16) else outs[0]


def _cumsum_rows(x, group):
    rows = lax.broadcasted_iota(jnp.int32, x.shape, 0) % group
    shift = 1
    while shift < group:
        rolled = pltpu.roll(x, shift, axis=0)
        x = x + jnp.where(rows >= shift, rolled, 0.0)
        shift *= 2
    return x


def _cumsum_rows_mxu(x, group):
    n = x.shape[0]
    row = lax.broadcasted_iota(jnp.int32, (n, n), 0)
    col = lax.broadcasted_iota(jnp.int32, (n, n), 1)
    tri = jnp.where(jnp.logical_and(col <= row, col // group == row // group), 1.0, 0.0).astype(BF16)
    hi = x.astype(BF16)
    lo = (x - hi.astype(F32)).astype(BF16)
    return (jnp.dot(tri, hi, preferred_element_type=F32) + jnp.dot(tri, lo, preferred_element_type=F32))


def _lockstep(gens):
    gens = list(gens)
    results = [None] * len(gens)
    live = list(range(len(gens)))
    while live:
        still = []
        for n in live:
            try:
                next(gens[n])
                still.append(n)
            except StopIteration as stop:
                results[n] = stop.value
        live = still
    return results


def _recurrence_block(q, k, v, la, s_read, *, sub, mm_dtype):
    bt, kd = q.shape
    vd = v.shape[1]
    ns = bt // sub
    b_loc = _cumsum_rows_mxu(la, sub) if mm_dtype == BF16 else _cumsum_rows(la, sub)
    qd = q * jnp.exp(b_loc)
    ki = k * jnp.exp(-b_loc)
    tot = [b_loc[(i + 1) * sub - 1:(i + 1) * sub, :] for i in range(ns)]
    pre = [jnp.zeros((1, kd), F32)]
    for i in range(ns):
        pre.append(pre[-1] + tot[i])
    sl = [slice(i * sub, (i + 1) * sub) for i in range(ns)]
    vb = v.astype(mm_dtype)
    qdm = qd.astype(mm_dtype)
    q_state = jnp.concatenate([qd[sl[i]] * jnp.exp(pre[i]) for i in range(ns)], axis=0) if ns > 1 else qd
    q_state = q_state.astype(mm_dtype)
    ke = [ki[sl[i]] * jnp.exp(tot[i]) for i in range(ns)]
    kmats = []
    for i in range(ns):
        parts = [ke[j] if j == i - 1 else ke[j] * jnp.exp(pre[i] - pre[j + 1]) for j in range(i)]
        parts.append(ki[sl[i]])
        parts += [jnp.zeros((sub, kd), F32)] * (ns - 1 - i)
        kmats.append((jnp.concatenate(parts, axis=0) if ns > 1 else parts[0]).astype(mm_dtype))
    k_end = jnp.concatenate([ke[i] if i == ns - 1 else ke[i] * jnp.exp(pre[ns] - pre[i + 1])
                             for i in range(ns)], axis=0) if ns > 1 else ke[0]
    k_end = k_end.astype(mm_dtype)
    decay = jnp.broadcast_to(jnp.exp(pre[ns]), (LANE, kd)).T
    yield

    s_prev = s_read()
    o_state = jnp.dot(q_state, s_prev.astype(mm_dtype), preferred_element_type=F32)
    a_rows = [lax.dot_general(qdm[sl[i]], kmats[i], (((1,), (1,)), ((), ())), preferred_element_type=F32)
              for i in range(ns)]
    ds = lax.dot_general(k_end, vb, (((0,), (0,)), ((), ())), preferred_element_type=F32)
    yield

    a = jnp.concatenate(a_rows, axis=0) if ns > 1 else a_rows[0]
    row = lax.broadcasted_iota(jnp.int32, a.shape, 0)
    col = lax.broadcasted_iota(jnp.int32, a.shape, 1)
    a = jnp.where(col <= row, a, 0.0).astype(mm_dtype)
    o = jnp.dot(a, vb, preferred_element_type=F32) + o_state
    s_new = jnp.concatenate([decay * s_prev[:, c * LANE:(c + 1) * LANE] for c in range(vd // LANE)],
                            axis=1) + ds
    yield
    return o, s_new


def _head_gated_norm(o, gnorm, gate):
    ms = jnp.mean(o * o, axis=-1, keepdims=True)
    return o * lax.rsqrt(ms + EPS) * gnorm * (gate * _sigmoid(gate))


def _log_sigmoid(x):
    return jnp.minimum(x, 0.0) - jnp.log(1.0 + jnp.exp(-jnp.abs(x)))


N_MIXER_IN = 16


def _mixer_body(*refs, steps_per_seq, prompt_kw, sample_kw, cast_chunks):
    n_p, n_s, n_c = N_MIXER_IN, N_MIXER_IN + 2, len(cast_chunks)
    p_in, s_in, c_in = refs[:n_p], refs[n_p:n_p + n_s], refs[n_p + n_s:n_p + n_s + n_c]
    outs = refs[n_p + n_s + n_c:]
    (p_om, p_sg, p_sh), (s_om, s_sg, s_sh), c_out = outs[:3], outs[3:6], outs[6:]
    s_sg_in, s_sh_in = s_in[N_MIXER_IN:]
    step = pl.program_id(0)
    _side_cast_run(step, c_in, c_out, cast_chunks)

    @pl.when(step % steps_per_seq == 0)
    def _():
        p_sg[...] = jnp.zeros(p_sg.shape, F32)
        p_sh[...] = jnp.zeros(p_sh.shape, F32)

    p_heads = _mixer_heads(p_in, **prompt_kw)
    s_heads = _mixer_heads(s_in, **sample_kw)
    p_block, s_block = prompt_kw["block"], sample_kw["block"]
    n_p_blocks = p_in[0].shape[0] // p_block
    n_s_seqs = s_in[0].shape[0] // s_block
    gh, hk = prompt_kw["gla_heads"], prompt_kw["hgrn_dk"]
    s_mixes = []
    for c in range(max(n_p_blocks, n_s_seqs)):
        gens, n_pg = [], 0
        if c < n_p_blocks:
            rows = slice(c * p_block, (c + 1) * p_block)
            load = lambda ref, cols, rows=rows: ref[rows, cols].astype(F32)
            gens += p_heads(load, lambda h: p_sg[0, h], lambda h: p_sh[0, h])
            n_pg = len(gens)
        if c < n_s_seqs:
            load = lambda ref, cols, c=c: ref[:, cols].astype(F32)[c * s_block:(c + 1) * s_block]
            gens += s_heads(load, lambda h, c=c: s_sg_in[c, h], lambda h, c=c: s_sh_in[c, h])
        res = _lockstep(gens)
        if c < n_p_blocks:
            for h, s_new in enumerate(res[:gh]):
                p_sg[0, h] = s_new
            for h, (mix, s_new) in enumerate(res[gh:n_pg]):
                p_sh[0, h] = s_new
                p_om[rows, h * hk:(h + 1) * hk] = mix.astype(p_om.dtype)
        if c < n_s_seqs:
            s_res = res[n_pg:]
            for h, s_new in enumerate(s_res[:gh]):
                s_sg[c, h] = s_new
            for h, (mix, s_new) in enumerate(s_res[gh:]):
                s_sh[c, h] = s_new
            s_mixes.append([mix for mix, _ in s_res[gh:]])
    for h in range(len(s_mixes[0])):
        s_om[:, h * hk:(h + 1) * hk] = jnp.concatenate([m[h] for m in s_mixes], axis=0).astype(s_om.dtype)


def _mixer_heads(ins, *, layer, gla_heads, gla_dk, gla_dv, hgrn_heads, hgrn_dk, block, sub, mm_dtype):
    (q_ref, k_ref, v_ref, g_ref, ga_ref, wup_ref, ba_ref, gng_ref,
     hq_ref, hf_ref, hi_ref, hg_ref, ma_ref, mb_ref, lb_ref, gnh_ref) = ins[:N_MIXER_IN]

    p = lb_ref[...]
    e = jnp.exp(p - jnp.max(p, axis=0, keepdims=True))
    lb_all = jnp.sum(e[:layer + 1], axis=0, keepdims=True) / jnp.sum(e, axis=0, keepdims=True)

    def gla_head(h, load, s_read, shared):
        ks = slice(h * gla_dk, (h + 1) * gla_dk)
        vs = slice(h * gla_dv, (h + 1) * gla_dv)
        ga = load(ga_ref, slice(None)).astype(BF16)
        a_logit = jnp.dot(ga, wup_ref[:, ks], preferred_element_type=F32) + ba_ref[:, ks]
        yield
        la = _log_sigmoid(a_logit) * (1.0 / GLA_GATE_NORM)
        q = load(q_ref, ks) * (gla_dk ** -0.5)
        o, s_new = yield from _recurrence_block(q, load(k_ref, ks), load(v_ref, vs), la, s_read,
                                                sub=sub, mm_dtype=mm_dtype)
        shared[h] = _head_gated_norm(o, gng_ref[:, vs], load(g_ref, vs))
        return s_new

    def hgrn_head(h, load, s_read, shared):
        ks = slice(h * hgrn_dk, (h + 1) * hgrn_dk)
        lb = lb_all[:, ks]
        f = lb + (1.0 - lb) * _sigmoid(load(hf_ref, ks))
        hq = load(hq_ref, ks)
        q = hq * _sigmoid(hq) * (hgrn_dk ** -0.5)
        yield
        o, s_new = yield from _recurrence_block(q, 1.0 - f, load(hi_ref, ks), jnp.log(f), s_read,
                                                sub=sub, mm_dtype=mm_dtype)
        o_h = _head_gated_norm(o, gnh_ref[:, ks], load(hg_ref, ks))
        per = gla_dv // hgrn_dk
        o_gla = shared[h // per][:, (h % per) * hgrn_dk:(h % per + 1) * hgrn_dk]
        mix = _sigmoid(load(ma_ref, ks)) * o_gla + _sigmoid(load(mb_ref, ks)) * o_h
        return mix, s_new

    def make(load, sg_read, sh_read):
        shared = {}
        gens = [gla_head(h, load, functools.partial(sg_read, h), shared) for h in range(gla_heads)]
        gens += [hgrn_head(h, load, functools.partial(sh_read, h), shared) for h in range(hgrn_heads)]
        return gens

    return make


def _mixer(z_p, zga_p, z_s, zga_s, p, s_gla, s_hgrn, *, layer, n_prompt, prompt_len, n_sample, sample_len,
           d_model, name, side_casts=()):
    gh, hk = GLA_HEADS, HGRN_DK
    gk, gv = d_model // 2 // gh, d_model // gh
    hh = d_model // hk
    kw = gh * gk
    n_steps = n_sample // SAMPLE_SEQS
    p_step = n_prompt * prompt_len // n_steps
    steps_per_seq = prompt_len // p_step
    assert prompt_len % p_step == 0 and p_step % PROMPT_BLOCK == 0
    s_step = SAMPLE_SEQS * sample_len
    whole = lambda a: pl.BlockSpec(a.shape, lambda i: (0,) * a.ndim)
    params = [p["gla_w_a_up"], p["gla_b_a"], p["gla_norm_g"], p["hgrn_lower_bound"], p["hgrn_norm_g"]]

    def group(z, zga, rows):
        zs = lambda width, col: pl.BlockSpec((rows, width), lambda i: (i, col))
        specs = [zs(kw, 0), zs(kw, 1), zs(d_model, 1), zs(d_model, 2), zs(zga.shape[1], 0)]
        specs += [whole(a) for a in params[:3]] + [zs(d_model, 3 + c) for c in range(6)]
        specs += [whole(a) for a in params[3:]]
        return specs, [z, z, z, z, zga] + params[:3] + [z] * 6 + params[3:], zs(d_model, 0)

    p_specs, p_args, p_om = group(z_p, zga_p, p_step)
    s_specs, s_args, s_om = group(z_s, zga_s, s_step)
    s_specs += [pl.BlockSpec((None, SAMPLE_SEQS, gh, gk, gv), lambda i: (layer, i, 0, 0, 0)),
                pl.BlockSpec((None, SAMPLE_SEQS, hh, hk, hk), lambda i: (layer, i, 0, 0, 0))]
    s_args += [s_gla, s_hgrn]
    p_st = lambda *dims: pl.BlockSpec((1,) + dims, lambda i: (i // steps_per_seq, 0, 0, 0))
    s_st = lambda *dims: pl.BlockSpec((SAMPLE_SEQS,) + dims, lambda i: (i, 0, 0, 0))
    common = dict(layer=layer, gla_heads=gh, gla_dk=gk, gla_dv=gv, hgrn_heads=hh, hgrn_dk=hk)
    prompt_kw = dict(common, block=PROMPT_BLOCK, sub=REF_CHUNK, mm_dtype=BF16)
    sample_kw = dict(common, block=sample_len, sub=math.gcd(REF_CHUNK, sample_len), mm_dtype=F32)
    shapes = lambda t, n: [jax.ShapeDtypeStruct((t, d_model), BF16), jax.ShapeDtypeStruct((n, gh, gk, gv), F32),
                           jax.ShapeDtypeStruct((n, hh, hk, hk), F32)]
    c_specs, c_shapes, cast_chunks = _side_cast_specs(side_casts, n_steps, lambda i: i)
    outs = pl.pallas_call(
        functools.partial(_mixer_body, steps_per_seq=steps_per_seq, prompt_kw=prompt_kw, sample_kw=sample_kw,
                          cast_chunks=cast_chunks),
        grid=(n_steps,),
        in_specs=p_specs + s_specs + c_specs,
        out_specs=[p_om, p_st(gh, gk, gv), p_st(hh, hk, hk), s_om, s_st(gh, gk, gv), s_st(hh, hk, hk)] + c_specs,
        out_shape=shapes(z_p.shape[0], n_prompt) + shapes(z_s.shape[0], n_sample) + c_shapes,
        compiler_params=_cparams(1),
        name=name,
    )(*p_args, *s_args, *side_casts)
    return outs[:3], outs[3:6], outs[6:]


def _xattn_sample_body(q_ref, k_ref, v_ref, o_ref, *, heads, dh, seq_len):
    scale = dh ** -0.5
    n_mem = k_ref.shape[1]
    for s in range(k_ref.shape[0]):
        rows = slice(s * seq_len, (s + 1) * seq_len)
        q = q_ref[rows, :]
        q2 = jnp.concatenate([q[:, h * dh:(h + 1) * dh] for h in range(heads)], axis=0)
        k2 = k_ref[s].reshape(n_mem * heads, dh)
        v2 = v_ref[s].reshape(n_mem * heads, dh)
        sc = lax.dot_general(q2, k2, (((1,), (1,)), ((), ())), preferred_element_type=F32) * scale
        q_head = lax.broadcasted_iota(jnp.int32, sc.shape, 0) // seq_len
        k_head = lax.broadcasted_iota(jnp.int32, sc.shape, 1) % heads
        sc = jnp.where(q_head == k_head, sc, -jnp.inf)
        p = jnp.exp(sc - jnp.max(sc, axis=-1, keepdims=True))
        o2 = jnp.dot(p, v2, preferred_element_type=F32) / jnp.sum(p, axis=-1, keepdims=True)
        for h in range(heads):
            o_ref[rows, h * dh:(h + 1) * dh] = o2[h * seq_len:(h + 1) * seq_len].astype(o_ref.dtype)


def _mid_body(om_ref, x_ref, k_ref, v_ref, wo_ref, wq_ref, wm_ref, g_ref, o_ref, *, heads, dh):
    scale = dh ** -0.5
    x1 = x_ref[...] + jnp.dot(om_ref[...], wo_ref[...], preferred_element_type=F32)
    ms = jnp.mean(x1 * x1, axis=-1, keepdims=True)
    hx = (x1 * lax.rsqrt(ms + EPS) * g_ref[...]).astype(BF16)
    mq = jnp.dot(hx, wq_ref[...], preferred_element_type=F32).astype(BF16)
    heads_out = []
    for h in range(heads):
        hs = slice(h * dh, (h + 1) * dh)
        sc = lax.dot_general(mq[:, hs], k_ref[0, :, hs], (((1,), (1,)), ((), ())),
                             preferred_element_type=F32) * scale
        p = jnp.exp(sc - jnp.max(sc, axis=-1, keepdims=True))
        o = jnp.dot(p.astype(BF16), v_ref[0, :, hs], preferred_element_type=F32) / jnp.sum(p, axis=-1, keepdims=True)
        heads_out.append(o.astype(BF16))
    mo = jnp.concatenate(heads_out, axis=1)
    o_ref[...] = x1 + jnp.dot(mo, wm_ref[...], preferred_element_type=F32)


def _mid_prompt(o_mix, x, mem_k, mem_v, w_out, w_q, w_o, gain, *, n_seq, seq_len, name):
    t, d = x.shape
    n_mem = mem_k.shape[1]
    tm = 256
    n_t = seq_len // tm
    rows = lambda dtype_rows: pl.BlockSpec((tm, d), lambda b, i: (b * n_t + i, 0))
    kv = pl.BlockSpec((1, n_mem, d), lambda b, i: (b, 0, 0))
    wspec = pl.BlockSpec((d, d), lambda b, i: (0, 0), pipeline_mode=pl.Buffered(1))
    return pl.pallas_call(
        functools.partial(_mid_body, heads=MEM_HEADS, dh=d // MEM_HEADS),
        grid=(n_seq, n_t),
        in_specs=[rows(BF16), rows(F32), kv, kv, wspec, wspec, wspec, pl.BlockSpec((1, d), lambda b, i: (0, 0))],
        out_specs=rows(F32),
        out_shape=jax.ShapeDtypeStruct((t, d), F32),
        compiler_params=_cparams(2),
        name=name,
    )(o_mix, x, mem_k, mem_v, w_out, w_q, w_o, gain)


def _pre_xattn_body(om_ref, x_ref, wo_ref, wq_ref, g_ref, x1_ref, mq_ref):
    x1 = x_ref[...] + jnp.dot(om_ref[...], wo_ref[...], preferred_element_type=F32)
    ms = jnp.mean(x1 * x1, axis=-1, keepdims=True)
    hx = (x1 * lax.rsqrt(ms + EPS) * g_ref[...]).astype(BF16)
    x1_ref[...] = x1
    mq_ref[...] = jnp.dot(hx, wq_ref[...], preferred_element_type=F32)


def _pre_xattn_sample(o_mix, x, w_out, w_q, gain, *, name):
    t, d = x.shape
    tm = _pick_tile(t, (256, 128, 64, 32, 16))
    rows = pl.BlockSpec((tm, d), lambda i: (i, 0))
    wspec = pl.BlockSpec((d, d), lambda i: (0, 0), pipeline_mode=pl.Buffered(1))
    return pl.pallas_call(
        _pre_xattn_body,
        grid=(t // tm,),
        in_specs=[rows, rows, wspec, wspec, pl.BlockSpec((1, d), lambda i: (0, 0))],
        out_specs=[rows, rows],
        out_shape=[jax.ShapeDtypeStruct((t, d), F32), jax.ShapeDtypeStruct((t, d), F32)],
        compiler_params=_cparams(1),
        name=name,
    )(o_mix, x, w_out, w_q, gain)


def _ffn_up_body(*refs, mode, seq_len, tiles_per_seq, row_chunk, last_shift, side):
    if mode == "prompt" and side is not None:
        (x_ref, g_ref, wg_ref, wv_ref, cw_ref, cb_ref, sq_ref, sk_ref, sv_ref,
         act_ref, st_ref, so_ref, h_ref, tail_ref) = refs
    elif mode == "prompt":
        x_ref, g_ref, wg_ref, wv_ref, cw_ref, cb_ref, act_ref, st_ref, h_ref, tail_ref = refs
    else:
        x_ref, g_ref, wg_ref, wv_ref, cw_ref, cb_ref, buf_ref, act_ref, st_ref, h_ref = refs
        tail_ref = None
    i = pl.program_id(0)
    j = pl.program_id(1)

    if side is not None:
        @pl.when(i * pl.num_programs(1) + j < side["n_steps"])
        def _():
            _xattn_sample_body(sq_ref, sk_ref, sv_ref, so_ref, heads=MEM_HEADS, dh=side["dh"],
                               seq_len=side["seq_len"])

    @pl.when(j == 0)
    def _():
        def body(c, carry):
            rows = pl.ds(pl.multiple_of(c * row_chunk, row_chunk), row_chunk)
            xf = x_ref[rows, :]
            ms = jnp.mean(xf * xf, axis=-1, keepdims=True)
            h_ref[rows, :] = (xf * lax.rsqrt(ms + EPS) * g_ref[...]).astype(BF16)
            return carry
        lax.fori_loop(0, x_ref.shape[0] // row_chunk, body, 0)

    if mode == "prompt":
        @pl.when(i % tiles_per_seq == 0)
        def _():
            tail_ref[j] = jnp.zeros(tail_ref.shape[1:], F32)

    def step(shift):
        def place(a):
            if shift == 0:
                return a
            return jnp.concatenate([a[..., shift:], jnp.zeros(a.shape[:-1] + (shift,), a.dtype)], axis=-1)

        h = h_ref[...]
        ug = jnp.dot(h, wg_ref[...], preferred_element_type=F32)
        tm = ug.shape[0]
        row = lax.broadcasted_iota(jnp.int32, ug.shape, 0)
        roll1 = pltpu.roll(ug, 1, axis=0)
        roll2 = pltpu.roll(ug, 2, axis=0)
        if mode == "prompt":
            tail = tail_ref[j]
            prev1 = tail[7:8, :]
            prev2 = tail[6:7, :]
            sh1 = jnp.where(row >= 1, roll1, prev1)
            sh2 = jnp.where(row >= 2, roll2, jnp.where(row == 0, prev2, prev1))
            tail_ref[j] = ug[tm - 8:, :]
            st_ref[...] = place(ug[tm - 8:, :])
        else:
            buf = buf_ref[...]
            n_seq = buf.shape[0]
            spread = lambda r: jnp.broadcast_to(buf[:, r:r + 1, :],
                                                (n_seq, seq_len, buf.shape[2])).reshape(ug.shape)
            prev2, prev1 = spread(0), spread(1)
            pos = row % seq_len
            sh1 = jnp.where(pos >= 1, roll1, prev1)
            sh2 = jnp.where(pos >= 2, roll2, jnp.where(pos == 0, prev2, prev1))
            st_ref[...] = place(ug.reshape(n_seq, seq_len, ug.shape[1])[:, seq_len - (CONV_W - 1):, :])
        conv = cw_ref[0:1, :] * sh2 + cw_ref[1:2, :] * sh1 + cw_ref[2:3, :] * ug + cb_ref[...]
        gate = conv * _sigmoid(conv)
        uv = jnp.dot(h, wv_ref[...], preferred_element_type=F32)
        act_ref[...] = place((gate * uv).astype(act_ref.dtype))

    if last_shift == 0:
        step(0)
    else:
        last = pl.num_programs(1) - 1
        pl.when(j < last)(functools.partial(step, 0))
        pl.when(j == last)(functools.partial(step, last_shift))


def _ffn_up(x, gain, w_up, cw, cb, carry, *, mode, seq_len, name, side=None):
    m, k = x.shape
    d_ff = w_up.shape[1] // 2
    tm = _pick_tile(m, (1024,))
    tn = 4 * LANE
    nf = -(-d_ff // tn) * tn
    last_shift = nf - d_ff
    assert d_ff % LANE == 0 and d_ff >= tn
    col = lambda j, base=0: pl.multiple_of(base + jnp.minimum(j * tn, d_ff - tn), LANE)
    elem = lambda *dims: tuple(pl.Element(n) for n in dims)
    xs = pl.BlockSpec((tm, k), lambda i, j: (i, 0))
    gs = pl.BlockSpec((1, k), lambda i, j: (0, 0))
    wgs = pl.BlockSpec(elem(k, tn), lambda i, j: (0, col(j)))
    wvs = pl.BlockSpec(elem(k, tn), lambda i, j: (0, col(j, d_ff)))
    cws = pl.BlockSpec(elem(CONV_W, tn), lambda i, j: (0, col(j)))
    cbs = pl.BlockSpec(elem(1, tn), lambda i, j: (0, col(j)))
    ts = pl.BlockSpec((tm, tn), lambda i, j: (i, j))
    in_specs = [xs, gs, wgs, wvs, cws, cbs]
    args = [x, gain, w_up, w_up, cw, cb]
    scratch = [pltpu.VMEM((tm, k), BF16)]
    if mode == "prompt":
        assert seq_len % tm == 0
        scratch.append(pltpu.VMEM((nf // tn, 8, tn), F32))
        st_spec = pl.BlockSpec((None, 8, tn), lambda i, j: (i, 0, j))
        st_shape = jax.ShapeDtypeStruct((m // tm, 8, nf), F32)
    else:
        assert tm % seq_len == 0 and seq_len == 8
        n_blk = tm // seq_len
        in_specs.append(pl.BlockSpec(elem(n_blk, CONV_W - 1, tn), lambda i, j: (i * n_blk, 0, col(j))))
        args.append(carry)
        st_spec = pl.BlockSpec((n_blk, CONV_W - 1, tn), lambda i, j: (i, 0, j))
        st_shape = jax.ShapeDtypeStruct((m // seq_len, CONV_W - 1, nf), F32)
    n_j = nf // tn
    out_specs = [ts, st_spec]
    out_shape = [jax.ShapeDtypeStruct((m, nf), BF16), st_shape]
    side_kw = None
    if side is not None:
        assert mode == "prompt"
        mq, cache_k, cache_v, layer, s_len = side
        n_side = mq.shape[0] // (SAMPLE_SEQS * s_len)
        assert n_side <= (m // tm) * n_j
        grp = lambda i, j: jnp.minimum(i * n_j + j, n_side - 1)
        q_spec = pl.BlockSpec((SAMPLE_SEQS * s_len, mq.shape[1]), lambda i, j: (grp(i, j), 0))
        kv_spec = pl.BlockSpec((None, SAMPLE_SEQS) + cache_k.shape[2:], lambda i, j: (layer, grp(i, j), 0, 0, 0))
        in_specs += [q_spec, kv_spec, kv_spec]
        args += [mq, cache_k, cache_v]
        out_specs.append(q_spec)
        out_shape.append(jax.ShapeDtypeStruct(mq.shape, F32))
        side_kw = dict(n_steps=n_side, dh=cache_k.shape[-1], seq_len=s_len)
    return pl.pallas_call(
        functools.partial(_ffn_up_body, mode=mode, seq_len=seq_len, tiles_per_seq=max(seq_len // tm, 1),
                          row_chunk=128, last_shift=last_shift, side=side_kw),
        grid=(m // tm, n_j),
        in_specs=in_specs,
        out_specs=out_specs,
        out_shape=out_shape,
        scratch_shapes=scratch,
        compiler_params=_cparams(2),
        name=name,
    )(*args)


def _in_proj(x, w_in_t, p, name, *, tn, side_casts=(), emit_w_bf16=False, w_ga_t=None):
    ga_off, rank = p["ga_off"], p["rank"]
    n_cols, skip = (w_in_t.shape[0] - rank, (ga_off, rank)) if w_ga_t is None else (w_in_t.shape[0], None)
    return _dense(x, w_in_t, gain=p["norm_mix_g"], prologue="norm", w_transposed=True, tn=tn,
                  out_dtype=BF16, name=name, n_cols=n_cols, skip_cols=skip, extra_cols=(ga_off, rank),
                  side_casts=side_casts, emit_w_bf16=emit_w_bf16, extra_w=w_ga_t)


def _conv_ffn(x2, carry, p, *, mode, n_seq, seq_len, tag, side=None):
    d_ff = p["d_ff"]
    act, ug_rows, *mo_side = _ffn_up(x2, p["norm_ffn_g"], p["w_up"], p["ffn_conv_w"], p["ffn_conv_b"], carry,
                                     mode=mode, seq_len=seq_len, name=f"ffn_up_{tag}", side=side)
    final = p["norm_final_g"] if p["is_last"] else None
    y = _dense(act, p["w_down"], res=x2, prologue="plain", out_dtype=F32, name=f"ffn_down_{tag}",
               final_gain=final, tm=512 if final is not None else None, tn=1024 if final is not None else None)
    if mode == "prompt":
        tiles_per_seq = ug_rows.shape[0] // n_seq
        conv_new = ug_rows[tiles_per_seq - 1::tiles_per_seq, 8 - (CONV_W - 1):, :d_ff]
    else:
        conv_new = ug_rows[:, :, :d_ff]
    return y, conv_new, (mo_side[0] if mo_side else None)


def kernel(x_prompt, x_sample, mem_prompt, state_gla, state_hgrn, state_ffn_conv, cache_mem_k, cache_mem_v,
           norm_mix_g, w_in, gla_w_a_up, gla_b_a, gla_norm_g, hgrn_lower_bound, hgrn_norm_g, w_out,
           norm_xattn_g, norm_mem_g, w_mem_q, w_mem_k, w_mem_v, w_mem_o,
           norm_ffn_g, w_ffn_up, ffn_conv_w, ffn_conv_b, w_ffn_down, norm_final_g):
    depth = w_in.shape[0]
    bp, lp, d = x_prompt.shape
    bs, ls, _ = x_sample.shape
    n_mem = mem_prompt.shape[1]
    d_ff = w_ffn_down.shape[1]
    rank = gla_w_a_up.shape[1]
    qk = gla_w_a_up.shape[2]
    ga_off = 2 * qk + 2 * d

    xp = x_prompt.reshape(bp * lp, d)
    xs = x_sample.reshape(bs * ls, d)
    mem = mem_prompt.reshape(bp * n_mem, d)
    row = lambda v: v.reshape(1, -1)

    outs = {k: [] for k in ("gla_p", "hgrn_p", "conv_p", "mk_p", "mv_p", "gla_s", "hgrn_s", "conv_s")}
    for l in range(depth):
        p = {
            "ga_off": ga_off, "rank": rank,
            "gla_w_a_up": gla_w_a_up[l].astype(BF16),
            "gla_b_a": row(gla_b_a[l]), "gla_norm_g": row(gla_norm_g[l]),
            "hgrn_lower_bound": hgrn_lower_bound, "hgrn_norm_g": row(hgrn_norm_g[l]),
            "norm_mix_g": row(norm_mix_g[l]), "norm_xattn_g": row(norm_xattn_g[l]),
            "norm_ffn_g": row(norm_ffn_g[l]), "norm_final_g": row(norm_final_g), "is_last": l == depth - 1,
            "ffn_conv_w": ffn_conv_w[l], "ffn_conv_b": row(ffn_conv_b[l]),
            "d_ff": d_ff,
        }
        g_mem = row(norm_mem_g[l])
        mem_proj = functools.partial(_dense, mem, gain=g_mem, prologue="norm", out_dtype=F32, layer=l,
                                     also_bf16=True)
        mk, mk_b = mem_proj(w_mem_k, name=f"mem_k_{l}")
        mv, mv_b = mem_proj(w_mem_v, name=f"mem_v_{l}")
        mk3, mv3 = mk.reshape(bp, n_mem, d), mv.reshape(bp, n_mem, d)
        outs["mk_p"].append(mk3.reshape(bp, n_mem, MEM_HEADS, d // MEM_HEADS))
        outs["mv_p"].append(mv3.reshape(bp, n_mem, MEM_HEADS, d // MEM_HEADS))
        zs, zga_s, w_in_t, w_ga_t = _in_proj(xs, jnp.swapaxes(w_in[l], 0, 1), p, f"in_proj_s{l}", tn=d // 2,
                                             emit_w_bf16=True)
        zp, zga_p, p["w_out"], p["w_mem_q"], p["w_mem_o"] = _in_proj(
            xp, w_in_t, p, f"in_proj_p{l}", tn=d, w_ga_t=w_ga_t, side_casts=(w_out[l], w_mem_q[l], w_mem_o[l]))
        (om_p, sg, sh), (om_s, sg2, sh2), (p["w_up"], p["w_down"]) = _mixer(
            zp, zga_p, zs, zga_s, p, state_gla, state_hgrn, layer=l, n_prompt=bp, prompt_len=lp,
            n_sample=bs, sample_len=ls, d_model=d, name=f"mixer_{l}", side_casts=(w_ffn_up[l], w_ffn_down[l]))
        outs["gla_p"].append(sg); outs["hgrn_p"].append(sh)
        outs["gla_s"].append(sg2); outs["hgrn_s"].append(sh2)
        x1_s, mq_s = _pre_xattn_sample(om_s, xs, p["w_out"], p["w_mem_q"], p["norm_xattn_g"],
                                       name=f"pre_xattn_s{l}")
        x2_p = _mid_prompt(om_p, xp, mk_b.reshape(bp, n_mem, d), mv_b.reshape(bp, n_mem, d),
                           p["w_out"], p["w_mem_q"], p["w_mem_o"],
                           p["norm_xattn_g"], n_seq=bp, seq_len=lp, name=f"mid_p{l}")
        xp, cb, mo_s = _conv_ffn(x2_p, None, p, mode="prompt", n_seq=bp, seq_len=lp, tag=f"p{l}",
                                 side=(mq_s, cache_mem_k, cache_mem_v, l, ls))
        x2_s = _dense(mo_s, p["w_mem_o"], res=x1_s, prologue="cast", out_dtype=F32, name=f"mem_o_s{l}")
        xs, cb2, _ = _conv_ffn(x2_s, state_ffn_conv[l], p, mode="sample", n_seq=bs, seq_len=ls, tag=f"s{l}")
        outs["conv_p"].append(cb); outs["conv_s"].append(cb2)

    y_prompt = xp.reshape(bp, lp, d)
    y_sample = xs.reshape(bs, ls, d)
    st = lambda k: jnp.stack(outs[k])
    return (y_prompt, y_sample, st("gla_p"), st("hgrn_p"), st("conv_p"), st("mk_p"), st("mv_p"),
            st("gla_s"), st("hgrn_s"), st("conv_s"))
```
